```python
import math
import jax
import jax.numpy as jnp
from jax import lax
import numpy as np

D_MODEL = 1024
BATCH = 2
SEQ = 8192
DEPTH = 2

CHUNK = 64
N_EVEN = (DEPTH + 1) // 2
N_ODD = DEPTH // 2
EPS = 1e-6
ADA_SCALE = 0.5

MIX_WIDTH = D_MODEL
POOL_WIDTH = MIX_WIDTH // 2
POOL_WINDOWS = (2, 4, 8, 16)
N_POOL = len(POOL_WINDOWS)
POOL_GROUP = POOL_WIDTH // N_POOL
S5_WIDTH = MIX_WIDTH - POOL_WIDTH
S5_GROUP = 16
S5_GROUPS = S5_WIDTH // S5_GROUP
S5_STATE = 64
S5_DT_MIN = 1e-3
S5_DT_MAX = 1e-1

M2_INNER = 2 * D_MODEL
M2_HEADDIM = 64
M2_HEADS = M2_INNER // M2_HEADDIM
M2_GROUPS = 8
M2_HPG = M2_HEADS // M2_GROUPS
M2_STATE = 128
M2_CONV = 4
M2_BC = M2_GROUPS * M2_STATE
M2_CONV_DIM = M2_INNER + 2 * M2_BC
M2_PROJ = M2_INNER + M2_CONV_DIM + M2_HEADS
M2_DT_MIN = 1e-3
M2_DT_MAX = 1e-1

FFN_HIDDEN = 2816
N_EXPERTS = 8
TOP_K = 2
EXPERT_HIDDEN = 3584

kernel_name = "hybrid_pool_s5_ssd_moe_streaming_trunk"


def _rmsnorm(x, g):
    xf = x.astype(jnp.float32)
    y = xf * lax.rsqrt(jnp.mean(xf * xf, axis=-1, keepdims=True) + EPS)
    return (y * g.astype(jnp.float32)).astype(x.dtype)


def _modulate(x, g, shift, scale):
    return _rmsnorm(x, g) * (1.0 + scale) + shift


def _swiglu(h, w_gate, w_up, w_down):
    return (jax.nn.silu(h @ w_gate) * (h @ w_up)) @ w_down


def _pool_mixer(u, pool_w, pool_scale):
    s = u.shape[1]
    uf = u.astype(jnp.float32)
    cs = jnp.cumsum(uf, axis=1)
    t = jnp.arange(1, s + 1, dtype=jnp.float32)
    outs = []
    for gi, win in enumerate(POOL_WINDOWS):
        sl = slice(gi * POOL_GROUP, (gi + 1) * POOL_GROUP)
        cs_g = cs[..., sl]
        lag = jnp.pad(cs_g, ((0, 0), (win, 0), (0, 0)))[:, :s]
        count = jnp.minimum(t, float(win))[None, :, None]
        pooled = (cs_g - lag) / count - uf[..., sl]
        outs.append(pooled @ pool_w[gi].astype(jnp.float32))
    return jnp.concatenate(outs, axis=-1) * pool_scale.astype(jnp.float32)


def _complex_combine(left, right):
    a1r, a1i, b1r, b1i = left
    a2r, a2i, b2r, b2i = right
    return (a2r * a1r - a2i * a1i,
            a2r * a1i + a2i * a1r,
            a2r * b1r - a2i * b1i + b2r,
            a2r * b1i + a2i * b1r + b2i)


def _s5_mixer(u, lam_re, lam_im, log_step, b_re, b_im, c_re, c_im, d_skip, w_glu):
    f32 = jnp.float32
    bsz, s, _ = u.shape
    uf = u.astype(f32).reshape(bsz, s, S5_GROUPS, S5_GROUP)
    step = jnp.exp(log_step.astype(f32))[:, None]
    lr = jnp.minimum(lam_re.astype(f32), -1e-4)
    li = lam_im.astype(f32)
    mag = jnp.exp(lr * step)
    ar = mag * jnp.cos(li * step)
    ai = mag * jnp.sin(li * step)
    inv = 1.0 / (lr * lr + li * li)
    fr = ((ar - 1.0) * lr + ai * li) * inv
    fi = (ai * lr - (ar - 1.0) * li) * inv
    br, bi = b_re.astype(f32), b_im.astype(f32)
    bbr = fr[..., None] * br - fi[..., None] * bi
    bbi = fr[..., None] * bi + fi[..., None] * br
    bu_r = jnp.einsum("bsgh,gph->sbgp", uf, bbr)
    bu_i = jnp.einsum("bsgh,gph->sbgp", uf, bbi)
    a_r = jnp.broadcast_to(ar, (s, 1) + ar.shape)
    a_i = jnp.broadcast_to(ai, (s, 1) + ai.shape)
    _, _, xr, xi = lax.associative_scan(_complex_combine, (a_r, a_i, bu_r, bu_i), axis=0)
    y = (jnp.einsum("sbgp,ghp->bsgh", xr, c_re.astype(f32))
         - jnp.einsum("sbgp,ghp->bsgh", xi, c_im.astype(f32)))
    y = y + d_skip.astype(f32).reshape(S5_GROUPS, S5_GROUP) * uf
    y = jax.nn.gelu(y.reshape(bsz, s, S5_WIDTH))
    return y * jax.nn.sigmoid(y @ w_glu.astype(f32))


def _pool_s5_block(h, w_in, pool_w, pool_scale, lam_re, lam_im, log_step,
                   b_re, b_im, c_re, c_im, s5_d, w_glu, w_out):
    u = h @ w_in
    y_pool = _pool_mixer(u[..., :POOL_WIDTH], pool_w, pool_scale)
    y_s5 = _s5_mixer(u[..., POOL_WIDTH:], lam_re, lam_im, log_step,
                     b_re, b_im, c_re, c_im, s5_d, w_glu)
    y = jnp.concatenate([y_pool, y_s5], axis=-1).astype(h.dtype)
    return y @ w_out


def _ssd_scan(xs, dt, a, bm, cm):
    bsz, s = xs.shape[:2]
    nc = s // CHUNK
    xdt = (xs * dt[..., None]).reshape(bsz, nc, CHUNK, M2_GROUPS, M2_HPG, M2_HEADDIM)
    da = (dt * a).reshape(bsz, nc, CHUNK, M2_GROUPS, M2_HPG)
    bm = bm.reshape(bsz, nc, CHUNK, M2_GROUPS, M2_STATE)
    cm = cm.reshape(bsz, nc, CHUNK, M2_GROUPS, M2_STATE)
    cs = jnp.cumsum(da, axis=2)
    causal = jnp.tril(jnp.ones((CHUNK, CHUNK), dtype=bool))[:, :, None, None]
    seg = cs[:, :, :, None] - cs[:, :, None, :]
    decay = jnp.exp(jnp.where(causal, seg, -jnp.inf))
    cb = jnp.einsum("bclgn,bcsgn->bclsg", cm, bm)
    y_diag = jnp.einsum("bclsgj,bcsgjp->bclgjp", cb[..., None] * decay, xdt)
    to_end = jnp.exp(cs[:, :, -1:] - cs)
    states = jnp.einsum("bclgn,bclgjp->bcgjpn", bm, xdt * to_end[..., None])
    chunk_decay = jnp.exp(cs[:, :, -1])

    def step(carry, inp):
        st, dec = inp
        return carry * dec[..., None, None] + st, carry

    init = jnp.zeros((bsz, M2_GROUPS, M2_HPG, M2_HEADDIM, M2_STATE), jnp.float32)
    _, prev = lax.scan(step, init, (jnp.moveaxis(states, 1, 0), jnp.moveaxis(chunk_decay, 1, 0)))
    prev = jnp.moveaxis(prev, 0, 1)
    y_off = jnp.einsum("bclgn,bcgjpn->bclgjp", cm, prev) * jnp.exp(cs)[..., None]
    return (y_diag + y_off).reshape(bsz, s, M2_HEADS, M2_HEADDIM)


def _mamba2_mixer(h, w_in, conv_w, conv_b, dt_bias, a_log, d_skip, norm_g, w_out):
    f32 = jnp.float32
    bsz, s, _ = h.shape
    proj = h @ w_in
    z = proj[..., :M2_INNER]
    xbc = proj[..., M2_INNER:M2_INNER + M2_CONV_DIM]
    dt_raw = proj[..., M2_INNER + M2_CONV_DIM:]
    xbc = lax.conv_general_dilated(
        xbc, conv_w[:, None, :], window_strides=(1,), padding=[(M2_CONV - 1, 0)],
        dimension_numbers=("NWC", "WIO", "NWC"), feature_group_count=M2_CONV_DIM)
    xbc = jax.nn.silu(xbc + conv_b).astype(f32)
    xs = xbc[..., :M2_INNER].reshape(bsz, s, M2_HEADS, M2_HEADDIM)
    bm = xbc[..., M2_INNER:M2_INNER + M2_BC].reshape(bsz, s, M2_GROUPS, M2_STATE)
    cm = xbc[..., M2_INNER + M2_BC:].reshape(bsz, s, M2_GROUPS, M2_STATE)
    dt = jax.nn.softplus(dt_raw.astype(f32) + dt_bias.astype(f32))
    a = -jnp.exp(a_log.astype(f32))
    y = _ssd_scan(xs, dt, a, bm, cm) + d_skip.astype(f32)[:, None] * xs
    y = y.reshape(bsz, s, M2_INNER) * jax.nn.silu(z.astype(f32))
    y = _rmsnorm(y, norm_g).astype(h.dtype)
    return y @ w_out


def _moe_swiglu(h, w_router, b_router, w_gate, w_up, w_down):
    bsz, s, d = h.shape
    tok = h.reshape(bsz * s, d)
    logits = (tok @ w_router).astype(jnp.float32) + b_router.astype(jnp.float32)
    top_logit, top_idx = lax.top_k(logits, TOP_K)
    top_w = jax.nn.softmax(top_logit, axis=-1)
    gates = jnp.einsum("tk,tke->te", top_w,
                       jax.nn.one_hot(top_idx, N_EXPERTS, dtype=jnp.float32))
    out = jnp.zeros((bsz * s, d), jnp.float32)
    for e in range(N_EXPERTS):
        y_e = _swiglu(tok, w_gate[e], w_up[e], w_down[e]).astype(jnp.float32)
        out = out + gates[:, e:e + 1] * y_e
    return out.astype(h.dtype).reshape(bsz, s, d)


def setup_inputs(seed: int = 0) -> dict:
    key = jax.random.key(seed)
    ks = iter(jax.random.split(key, 64))
    f32 = jnp.float32

    def nrm(shape, scale):
        return jax.random.normal(next(ks), shape, f32) * scale

    def log_uniform(shape, lo, hi):
        return jax.random.uniform(next(ks), shape, f32, math.log(lo), math.log(hi))

    ne, no = N_EVEN, N_ODD
    x = nrm((BATCH, SEQ, D_MODEL), 1.0)
    c = nrm((BATCH, D_MODEL), 1.0)
    ada_w = nrm((DEPTH, D_MODEL, 6 * D_MODEL), ADA_SCALE * D_MODEL ** -0.5)
    ada_b = nrm((DEPTH, 6 * D_MODEL), 0.02)
    norm_g = 1.0 + nrm((DEPTH, 4, D_MODEL), 0.02)
    mix_w_in = nrm((ne, D_MODEL, MIX_WIDTH), D_MODEL ** -0.5)
    pool_w = nrm((ne, N_POOL, POOL_GROUP, POOL_GROUP), POOL_GROUP ** -0.5)
    pool_scale = 1.0 + nrm((ne, POOL_WIDTH), 0.05)
    s5_lam_re = -0.5 + nrm((ne, S5_GROUPS, S5_STATE), 0.01)
    s5_lam_im = math.pi * jnp.arange(S5_STATE, dtype=f32) + nrm((ne, S5_GROUPS, S5_STATE), 0.01)
    s5_log_step = log_uniform((ne, S5_GROUPS), S5_DT_MIN, S5_DT_MAX)
    s5_b_re = nrm((ne, S5_GROUPS, S5_STATE, S5_GROUP), (2.0 * S5_GROUP) ** -0.5)
    s5_b_im = nrm((ne, S5_GROUPS, S5_STATE, S5_GROUP), (2.0 * S5_GROUP) ** -0.5)
    s5_c_re = nrm((ne, S5_GROUPS, S5_GROUP, S5_STATE), 0.5 ** 0.5)
    s5_c_im = nrm((ne, S5_GROUPS, S5_GROUP, S5_STATE), 0.5 ** 0.5)
    s5_d = nrm((ne, S5_WIDTH), 0.5)
    s5_w_glu = nrm((ne, S5_WIDTH, S5_WIDTH), S5_WIDTH ** -0.5)
    mix_w_out = nrm((ne, MIX_WIDTH, D_MODEL), MIX_WIDTH ** -0.5)
    ffn_w_gate = nrm((ne, D_MODEL, FFN_HIDDEN), D_MODEL ** -0.5)
    ffn_w_up = nrm((ne, D_MODEL, FFN_HIDDEN), D_MODEL ** -0.5)
    ffn_w_down = nrm((ne, FFN_HIDDEN, D_MODEL), FFN_HIDDEN ** -0.5)
    m2_w_in = nrm((no, D_MODEL, M2_PROJ), D_MODEL ** -0.5)
    m2_conv_w = nrm((no, M2_CONV, M2_CONV_DIM), M2_CONV ** -0.5)
    m2_conv_b = nrm((no, M2_CONV_DIM), 0.02)
    dt0 = jnp.exp(log_uniform((no, M2_HEADS), M2_DT_MIN, M2_DT_MAX))
    m2_dt_bias = dt0 + jnp.log(-jnp.expm1(-dt0))
    m2_a_log = jnp.log(jax.random.uniform(next(ks), (no, M2_HEADS), f32, 1.0, 16.0))
    m2_d = 1.0 + nrm((no, M2_HEADS), 0.1)
    m2_norm_g = 1.0 + nrm((no, M2_INNER), 0.02)
    m2_w_out = nrm((no, M2_INNER, D_MODEL), M2_INNER ** -0.5)
    moe_w_router = nrm((no, D_MODEL, N_EXPERTS), D_MODEL ** -0.5)
    moe_b_router = nrm((no, N_EXPERTS), 0.01)
    moe_w_gate = nrm((no, N_EXPERTS, D_MODEL, EXPERT_HIDDEN), D_MODEL ** -0.5)
    moe_w_up = nrm((no, N_EXPERTS, D_MODEL, EXPERT_HIDDEN), D_MODEL ** -0.5)
    moe_w_down = nrm((no, N_EXPERTS, EXPERT_HIDDEN, D_MODEL), EXPERT_HIDDEN ** -0.5)
    return {
        "x": x, "c": c, "ada_w": ada_w, "ada_b": ada_b, "norm_g": norm_g,
        "mix_w_in": mix_w_in, "pool_w": pool_w, "pool_scale": pool_scale,
        "s5_lam_re": s5_lam_re, "s5_lam_im": s5_lam_im, "s5_log_step": s5_log_step,
        "s5_b_re": s5_b_re, "s5_b_im": s5_b_im, "s5_c_re": s5_c_re, "s5_c_im": s5_c_im,
        "s5_d": s5_d, "s5_w_glu": s5_w_glu, "mix_w_out": mix_w_out,
        "ffn_w_gate": ffn_w_gate, "ffn_w_up": ffn_w_up, "ffn_w_down": ffn_w_down,
        "m2_w_in": m2_w_in, "m2_conv_w": m2_conv_w, "m2_conv_b": m2_conv_b,
        "m2_dt_bias": m2_dt_bias, "m2_a_log": m2_a_log, "m2_d": m2_d,
        "m2_norm_g": m2_norm_g, "m2_w_out": m2_w_out,
        "moe_w_router": moe_w_router, "moe_b_router": moe_b_router,
        "moe_w_gate": moe_w_gate, "moe_w_up": moe_w_up, "moe_w_down": moe_w_down,
    }


def reference(x, c, ada_w, ada_b, norm_g, mix_w_in, pool_w, pool_scale,
              s5_lam_re, s5_lam_im, s5_log_step, s5_b_re, s5_b_im, s5_c_re, s5_c_im,
              s5_d, s5_w_glu, mix_w_out, ffn_w_gate, ffn_w_up, ffn_w_down,
              m2_w_in, m2_conv_w, m2_conv_b, m2_dt_bias, m2_a_log, m2_d,
              m2_norm_g, m2_w_out, moe_w_router, moe_b_router,
              moe_w_gate, moe_w_up, moe_w_down):
    for layer in range(DEPTH):
        i = layer // 2
        mod = (jax.nn.silu(c) @ ada_w[layer] + ada_b[layer])[:, None, :]
        sh_m, sc_m, g_m, sh_f, sc_f, g_f = jnp.split(mod, 6, axis=-1)
        h = _modulate(x, norm_g[layer, 0], sh_m, sc_m)
        if layer % 2 == 0:
            y = _pool_s5_block(h, mix_w_in[i], pool_w[i], pool_scale[i],
                               s5_lam_re[i], s5_lam_im[i], s5_log_step[i],
                               s5_b_re[i], s5_b_im[i], s5_c_re[i], s5_c_im[i],
                               s5_d[i], s5_w_glu[i], mix_w_out[i])
        else:
            y = _mamba2_mixer(h, m2_w_in[i], m2_conv_w[i], m2_conv_b[i], m2_dt_bias[i],
                              m2_a_log[i], m2_d[i], m2_norm_g[i], m2_w_out[i])
        x = x + g_m * _rmsnorm(y, norm_g[layer, 1])
        h = _modulate(x, norm_g[layer, 2], sh_f, sc_f)
        if layer % 2 == 0:
            y = _swiglu(h, ffn_w_gate[i], ffn_w_up[i], ffn_w_down[i])
        else:
            y = _moe_swiglu(h, moe_w_router[i], moe_b_router[i],
                            moe_w_gate[i], moe_w_up[i], moe_w_down[i])
        x = x + g_f * _rmsnorm(y, norm_g[layer, 3])
    return x
```

```python
import functools
import math

import jax
import jax.numpy as jnp
from jax import lax
from jax.experimental import pallas as pl
from jax.experimental.pallas import tpu as pltpu

F32 = jnp.float32
BF16 = jnp.bfloat16

D_MODEL = 1024
EPS = 1e-6
POOL_WIDTH = 512
POOL_WINDOWS = (2, 4, 8, 16)
POOL_GROUP = 128
S5_WIDTH = 512
S5_GROUP = 16
S5_GROUPS = 32
S5_STATE = 64
M2_INNER = 2048
M2_HEADDIM = 64
M2_HEADS = 32
M2_GROUPS = 8
M2_HPG = 4
M2_STATE = 128
M2_CONV = 4
M2_BC = 1024
M2_CONV_DIM = 4096
M2_PROJ = 6176
FFN_HIDDEN = 2816
N_EXPERTS = 8
EXPERT_HIDDEN = 3584

LANES = 128
SUBLANES = 8
VMEM_LIMIT_BYTES = 56 * 1024 * 1024

NEG_BIG = -1e30


def _cparams(sem):
    return pltpu.CompilerParams(dimension_semantics=sem, vmem_limit_bytes=VMEM_LIMIT_BYTES)


def _modulated_norm(x, g, scale, shift):
    ms = jnp.mean(x * x, axis=-1, keepdims=True)
    return (x * lax.rsqrt(ms + EPS) * g) * (1.0 + scale) + shift


def _post_norm_residual(x, y, ng, gate):
    ms = jnp.mean(y * y, axis=-1, keepdims=True)
    return x + gate * (y * lax.rsqrt(ms + EPS) * ng)


def _ada_kernel(c_ref, w_ref, b_ref, o_ref):
    c = c_ref[...]
    a = c * jax.nn.sigmoid(c)
    o_ref[0] = jnp.dot(a.astype(BF16), w_ref[0].astype(BF16),
                       preferred_element_type=F32) + b_ref[0]


def _ada_modulation(c, ada_w, ada_b):
    depth, d, n = ada_w.shape
    b = c.shape[0]
    c_pad = jnp.zeros((SUBLANES, d), F32).at[:b].set(c)
    tn = 1024
    out = pl.pallas_call(
        _ada_kernel,
        grid=(depth, n // tn),
        in_specs=[
            pl.BlockSpec((SUBLANES, d), lambda l, j: (0, 0)),
            pl.BlockSpec((1, d, tn), lambda l, j: (l, 0, j)),
            pl.BlockSpec((1, 1, tn), lambda l, j: (l, 0, j)),
        ],
        out_specs=pl.BlockSpec((1, SUBLANES, tn), lambda l, j: (l, 0, j)),
        out_shape=jax.ShapeDtypeStruct((depth, SUBLANES, n), F32),
        compiler_params=_cparams(("arbitrary", "arbitrary")),
        name="ada_modulation",
    )(c_pad, ada_w, ada_b.reshape(depth, 1, n))
    return out[:, :b]


def _norm_mm_kernel(x_ref, g_ref, sc_ref, sh_ref, w_ref, o_ref, h_scr):
    @pl.when(pl.program_id(1) == 0)
    def _():
        h = _modulated_norm(x_ref[...], g_ref[...], sc_ref[0], sh_ref[0])
        h_scr[...] = h.astype(BF16)

    o_ref[...] = jnp.dot(h_scr[...], w_ref[...], preferred_element_type=F32).astype(o_ref.dtype)


def _norm_matmul(x, g, scale, shift, w, *, tm, tn, out_dtype, name):
    t, d = x.shape
    n = w.shape[1]
    tiles_per_batch = t // scale.shape[0] // tm
    return pl.pallas_call(
        _norm_mm_kernel,
        grid=(t // tm, n // tn),
        in_specs=[
            pl.BlockSpec((tm, d), lambda i, j: (i, 0)),
            pl.BlockSpec((1, d), lambda i, j: (0, 0)),
            pl.BlockSpec((1, 1, d), lambda i, j: (i // tiles_per_batch, 0, 0)),
            pl.BlockSpec((1, 1, d), lambda i, j: (i // tiles_per_batch, 0, 0)),
            pl.BlockSpec((d, tn), lambda i, j: (0, j)),
        ],
        out_specs=pl.BlockSpec((tm, tn), lambda i, j: (i, j)),
        out_shape=jax.ShapeDtypeStruct((t, n), out_dtype),
        scratch_shapes=[pltpu.VMEM((tm, d), BF16)],
        compiler_params=_cparams(("arbitrary", "arbitrary")),
        name=name,
    )(x, g, scale, shift, w)


def _mm_post_kernel(y_ref, w_ref, x_ref, ng_ref, gate_ref, o_ref):
    y = jnp.dot(y_ref[...], w_ref[...], preferred_element_type=F32)
    o_ref[...] = _post_norm_residual(x_ref[...], y, ng_ref[...], gate_ref[0])


def _matmul_post(y, w, x, ng, gate, *, tm, name):
    t, k = y.shape
    d = w.shape[1]
    tiles_per_batch = t // gate.shape[0] // tm
    return pl.pallas_call(
        _mm_post_kernel,
        grid=(t // tm,),
        in_specs=[
            pl.BlockSpec((tm, k), lambda i: (i, 0)),
            pl.BlockSpec((k, d), lambda i: (0, 0)),
            pl.BlockSpec((tm, d), lambda i: (i, 0)),
            pl.BlockSpec((1, d), lambda i: (0, 0)),
            pl.BlockSpec((1, 1, d), lambda i: (i // tiles_per_batch, 0, 0)),
        ],
        out_specs=pl.BlockSpec((tm, d), lambda i: (i, 0)),
        out_shape=jax.ShapeDtypeStruct((t, d), F32),
        compiler_params=_cparams(("arbitrary",)),
        name=name,
    )(y, w, x, ng, gate)


def _ffn_kernel(x_ref, g_ref, sc_ref, sh_ref, wg_ref, wu_ref, wd_ref, ng_ref, gate_ref,
                o_ref, h_scr, acc_scr):
    j = pl.program_id(1)

    @pl.when(j == 0)
    def _():
        h = _modulated_norm(x_ref[...], g_ref[...], sc_ref[0], sh_ref[0])
        h_scr[...] = h.astype(BF16)
        acc_scr[...] = jnp.zeros_like(acc_scr)

    h = h_scr[...]
    gt = jnp.dot(h, wg_ref[...], preferred_element_type=F32)
    up = jnp.dot(h, wu_ref[...], preferred_element_type=F32)
    act = (gt * jax.nn.sigmoid(gt) * up).astype(BF16)
    acc_scr[...] += jnp.dot(act, wd_ref[...], preferred_element_type=F32)

    @pl.when(j == pl.num_programs(1) - 1)
    def _():
        o_ref[...] = _post_norm_residual(x_ref[...], acc_scr[...], ng_ref[...], gate_ref[0])


def _ffn_sublayer(x, g, scale, shift, wg, wu, wd, ng, gate, *, tm, th):
    t, d = x.shape
    hid = wg.shape[1]
    tiles_per_batch = t // scale.shape[0] // tm
    bvec = pl.BlockSpec((1, 1, d), lambda i, j: (i // tiles_per_batch, 0, 0))
    return pl.pallas_call(
        _ffn_kernel,
        grid=(t // tm, hid // th),
        in_specs=[
            pl.BlockSpec((tm, d), lambda i, j: (i, 0)),
            pl.BlockSpec((1, d), lambda i, j: (0, 0)),
            bvec, bvec,
            pl.BlockSpec((d, th), lambda i, j: (0, j)),
            pl.BlockSpec((d, th), lambda i, j: (0, j)),
            pl.BlockSpec((th, d), lambda i, j: (j, 0)),
            pl.BlockSpec((1, d), lambda i, j: (0, 0)),
            bvec,
        ],
        out_specs=pl.BlockSpec((tm, d), lambda i, j: (i, 0)),
        out_shape=jax.ShapeDtypeStruct((t, d), F32),
        scratch_shapes=[pltpu.VMEM((tm, d), BF16), pltpu.VMEM((tm, d), F32)],
        compiler_params=_cparams(("arbitrary", "arbitrary")),
        name="ffn_sublayer",
    )(x, g, scale, shift, wg, wu, wd, ng, gate)


def _router_kernel(x_ref, g_ref, sc_ref, sh_ref, wr_ref, br_ref, h3_ref, meta_ref, cnt_ref,
                   carry_scr):
    tm = x_ref.shape[0]

    @pl.when(pl.program_id(0) == 0)
    def _():
        carry_scr[...] = jnp.zeros_like(carry_scr)

    h = _modulated_norm(x_ref[...], g_ref[...], sc_ref[0], sh_ref[0])
    for s in range(SUBLANES):
        h3_ref[:, s, :] = h[:, s * LANES:(s + 1) * LANES]

    logits = jnp.dot(h, wr_ref[...], preferred_element_type=F32,
                     precision=lax.Precision.HIGHEST) + br_ref[...]
    lane = lax.broadcasted_iota(jnp.int32, (tm, LANES), 1).astype(F32)
    m1 = jnp.max(logits, axis=-1, keepdims=True)
    i1 = jnp.min(jnp.where(logits == m1, lane, float(LANES)), axis=-1, keepdims=True)
    oh1 = lane == i1
    rest = jnp.where(oh1, NEG_BIG * 2.0, logits)
    m2 = jnp.max(rest, axis=-1, keepdims=True)
    i2 = jnp.min(jnp.where(rest == m2, lane, float(LANES)), axis=-1, keepdims=True)
    oh2 = lane == i2
    e = jnp.exp(m2 - m1)
    w1 = 1.0 / (1.0 + e)
    w2 = e / (1.0 + e)

    picks = jnp.where(oh1, 1.0, 0.0) + jnp.where(oh2, 1.0, 0.0)
    row = lax.broadcasted_iota(jnp.int32, (tm, tm), 0)
    col = lax.broadcasted_iota(jnp.int32, (tm, tm), 1)
    lower = jnp.where(col < row, 1.0, 0.0).astype(BF16)
    before = jnp.dot(lower, picks.astype(BF16), preferred_element_type=F32) + carry_scr[...]
    r1 = jnp.sum(jnp.where(oh1, before, 0.0), axis=-1, keepdims=True)
    r2 = jnp.sum(jnp.where(oh2, before, 0.0), axis=-1, keepdims=True)
    carry_scr[...] += jnp.sum(picks, axis=0, keepdims=True)
    cnt_ref[...] = carry_scr[...]

    meta = jnp.where(lane == 0.0, i1, 0.0)
    meta = jnp.where(lane == 1.0, i2, meta)
    meta = jnp.where(lane == 2.0, w1, meta)
    meta = jnp.where(lane == 3.0, w2, meta)
    meta = jnp.where(lane == 4.0, r1, meta)
    meta = jnp.where(lane == 5.0, r2, meta)
    meta_ref[...] = meta


def _router(x, g, scale, shift, w_router, b_router, *, tm):
    t, d = x.shape
    tiles_per_batch = t // scale.shape[0] // tm
    wr = jnp.zeros((d, LANES), F32).at[:, :N_EXPERTS].set(w_router)
    br = jnp.full((1, LANES), NEG_BIG, F32).at[0, :N_EXPERTS].set(b_router)
    bvec = pl.BlockSpec((1, 1, d), lambda i: (i // tiles_per_batch, 0, 0))
    return pl.pallas_call(
        _router_kernel,
        grid=(t // tm,),
        in_specs=[
            pl.BlockSpec((tm, d), lambda i: (i, 0)),
            pl.BlockSpec((1, d), lambda i: (0, 0)),
            bvec, bvec,
            pl.BlockSpec((d, LANES), lambda i: (0, 0)),
            pl.BlockSpec((1, LANES), lambda i: (0, 0)),
        ],
        out_specs=[
            pl.BlockSpec((tm, SUBLANES, LANES), lambda i: (i, 0, 0)),
            pl.BlockSpec((tm, LANES), lambda i: (i, 0)),
            pl.BlockSpec((1, LANES), lambda i: (0, 0)),
        ],
        out_shape=[
            jax.ShapeDtypeStruct((t, SUBLANES, LANES), F32),
            jax.ShapeDtypeStruct((t, LANES), F32),
            jax.ShapeDtypeStruct((1, LANES), F32),
        ],
        scratch_shapes=[pltpu.VMEM((1, LANES), F32)],
        compiler_params=_cparams(("arbitrary",)),
        name="moe_router",
    )(x, g, scale, shift, wr, br)


def _dispatch_kernel(pos_ref, h3_ref, xs_in_ref, xs_ref, sem):
    del xs_in_ref
    ts = h3_ref.shape[0]

    def issue(r, carry):
        pltpu.make_async_copy(h3_ref.at[r], xs_ref.at[pos_ref[0, 0, r]], sem).start()
        pltpu.make_async_copy(h3_ref.at[r], xs_ref.at[pos_ref[0, 0, ts + r]], sem).start()
        return carry

    lax.fori_loop(0, ts, issue, 0)

    def drain(r, carry):
        pltpu.make_async_copy(h3_ref.at[0], xs_ref.at[0], sem).wait()
        pltpu.make_async_copy(h3_ref.at[0], xs_ref.at[0], sem).wait()
        return carry

    lax.fori_loop(0, ts, drain, 0)


def _dispatch(h3, pos, n_slots, *, ts):
    t = h3.shape[0]
    xs0 = jnp.zeros((n_slots, SUBLANES, LANES), F32)
    return pl.pallas_call(
        _dispatch_kernel,
        grid=(t // ts,),
        in_specs=[
            pl.BlockSpec((1, 1, 2 * ts), lambda i: (i, 0, 0), memory_space=pltpu.SMEM),
            pl.BlockSpec((ts, SUBLANES, LANES), lambda i: (i, 0, 0)),
            pl.BlockSpec(memory_space=pl.ANY),
        ],
        out_specs=pl.BlockSpec(memory_space=pl.ANY),
        out_shape=jax.ShapeDtypeStruct((n_slots, SUBLANES, LANES), F32),
        scratch_shapes=[pltpu.SemaphoreType.DMA(())],
        input_output_aliases={2: 0},
        compiler_params=_cparams(("arbitrary",)),
        name="moe_dispatch",
    )(pos, h3, xs0)


def _expert_kernel(te_ref, nu_ref, xs_ref, wg_ref, wu_ref, wd_ref, ys_ref, xb_scr, acc_scr):
    i = pl.program_id(0)
    j = pl.program_id(1)
    last = pl.num_programs(1) - 1
    used = i < nu_ref[0]

    @pl.when(used)
    def _():
        @pl.when(j == 0)
        def _():
            for s in range(SUBLANES):
                xb_scr[:, s * LANES:(s + 1) * LANES] = xs_ref[:, s, :].astype(BF16)
            acc_scr[...] = jnp.zeros_like(acc_scr)

        x = xb_scr[...]
        gt = jnp.dot(x, wg_ref[0], preferred_element_type=F32)
        up = jnp.dot(x, wu_ref[0], preferred_element_type=F32)
        act = (gt * jax.nn.sigmoid(gt) * up).astype(BF16)
        acc_scr[...] += jnp.dot(act, wd_ref[0], preferred_element_type=F32)

        @pl.when(j == last)
        def _():
            for s in range(SUBLANES):
                ys_ref[:, s, :] = acc_scr[:, s * LANES:(s + 1) * LANES]

    @pl.when(jnp.logical_and(jnp.logical_not(used), j == last))
    def _():
        ys_ref[...] = jnp.zeros_like(ys_ref)


def _experts(xs, tile_expert, n_used, wg, wu, wd, *, tm, th):
    n_slots = xs.shape[0]
    d = D_MODEL
    hid = wg.shape[2]
    nt = n_slots // tm

    def hidden_step(i, j, nu):
        return jnp.where(i < nu[0], j, 0)

    grid_spec = pltpu.PrefetchScalarGridSpec(
        num_scalar_prefetch=2,
        grid=(nt, hid // th),
        in_specs=[
            pl.BlockSpec((tm, SUBLANES, LANES), lambda i, j, te, nu: (i, 0, 0)),
            pl.BlockSpec((1, d, th), lambda i, j, te, nu: (te[i], 0, hidden_step(i, j, nu))),
            pl.BlockSpec((1, d, th), lambda i, j, te, nu: (te[i], 0, hidden_step(i, j, nu))),
            pl.BlockSpec((1, th, d), lambda i, j, te, nu: (te[i], hidden_step(i, j, nu), 0)),
        ],
        out_specs=pl.BlockSpec((tm, SUBLANES, LANES), lambda i, j, te, nu: (i, 0, 0)),
        scratch_shapes=[pltpu.VMEM((tm, d), BF16), pltpu.VMEM((tm, d), F32)],
    )
    return pl.pallas_call(
        _expert_kernel,
        grid_spec=grid_spec,
        out_shape=jax.ShapeDtypeStruct((n_slots, SUBLANES, LANES), F32),
        compiler_params=_cparams(("arbitrary", "arbitrary")),
        name="moe_experts",
    )(tile_expert, n_used, xs, wg, wu, wd)


def _combine_kernel(pos_ref, ys_ref, meta_ref, x_ref, ng_ref, gate_ref, o_ref, buf, sem):
    tc = x_ref.shape[0]

    def issue(r, carry):
        pltpu.make_async_copy(ys_ref.at[pos_ref[0, 0, r]], buf.at[r], sem).start()
        pltpu.make_async_copy(ys_ref.at[pos_ref[0, 0, tc + r]], buf.at[tc + r], sem).start()
        return carry

    lax.fori_loop(0, tc, issue, 0)

    def drain(r, carry):
        pltpu.make_async_copy(ys_ref.at[0], buf.at[0], sem).wait()
        pltpu.make_async_copy(ys_ref.at[0], buf.at[0], sem).wait()
        return carry

    lax.fori_loop(0, tc, drain, 0)

    meta = meta_ref[...]
    w1 = meta[:, 2:3]
    w2 = meta[:, 3:4]
    first = jnp.concatenate([buf[0:tc, s, :] for s in range(SUBLANES)], axis=-1)
    second = jnp.concatenate([buf[tc:2 * tc, s, :] for s in range(SUBLANES)], axis=-1)
    y = w1 * first + w2 * second
    o_ref[...] = _post_norm_residual(x_ref[...], y, ng_ref[...], gate_ref[0])


def _combine(pos, ys, meta, x, ng, gate, *, tc):
    t, d = x.shape
    tiles_per_batch = t // gate.shape[0] // tc
    return pl.pallas_call(
        _combine_kernel,
        grid=(t // tc,),
        in_specs=[
            pl.BlockSpec((1, 1, 2 * tc), lambda i: (i, 0, 0), memory_space=pltpu.SMEM),
            pl.BlockSpec(memory_space=pl.ANY),
            pl.BlockSpec((tc, LANES), lambda i: (i, 0)),
            pl.BlockSpec((tc, d), lambda i: (i, 0)),
            pl.BlockSpec((1, d), lambda i: (0, 0)),
            pl.BlockSpec((1, 1, d), lambda i: (i // tiles_per_batch, 0, 0)),
        ],
        out_specs=pl.BlockSpec((tc, d), lambda i: (i, 0)),
        out_shape=jax.ShapeDtypeStruct((t, d), F32),
        scratch_shapes=[pltpu.VMEM((2 * tc, SUBLANES, LANES), F32), pltpu.SemaphoreType.DMA(())],
        compiler_params=_cparams(("arbitrary",)),
        name="moe_combine",
    )(pos, ys, meta, x, ng, gate)


def _moe_sublayer(x, g, scale, shift, w_router, b_router, wg, wu, wd, ng, gate):
    t = x.shape[0]
    tm_r, ts, tm_e, th, tc = 512, 256, 512, 512, 256
    h3, meta, counts = _router(x, g, scale, shift, w_router, b_router, tm=tm_r)

    cnt = counts[0, :N_EXPERTS].astype(jnp.int32)
    padded = ((cnt + tm_e - 1) // tm_e) * tm_e
    ends = jnp.cumsum(padded)
    offs = ends - padded
    i1 = meta[:, 0].astype(jnp.int32)
    i2 = meta[:, 1].astype(jnp.int32)
    experts = jnp.arange(N_EXPERTS, dtype=jnp.int32)
    off1 = jnp.sum(jnp.where(i1[:, None] == experts[None, :], offs[None, :], 0), axis=1)
    off2 = jnp.sum(jnp.where(i2[:, None] == experts[None, :], offs[None, :], 0), axis=1)
    pos1 = off1 + meta[:, 4].astype(jnp.int32)
    pos2 = off2 + meta[:, 5].astype(jnp.int32)
    nt = 2 * t // tm_e + N_EXPERTS
    starts = jnp.arange(nt, dtype=jnp.int32) * tm_e
    tile_expert = jnp.minimum(jnp.sum(starts[:, None] >= ends[None, :], axis=1), N_EXPERTS - 1)
    tile_expert = tile_expert.astype(jnp.int32)
    n_used = (ends[-1:] // tm_e).astype(jnp.int32)

    def tiled_pos(tile):
        return jnp.concatenate([pos1.reshape(t // tile, 1, tile), pos2.reshape(t // tile, 1, tile)],
                               axis=-1)

    xs = _dispatch(h3, tiled_pos(ts), nt * tm_e, ts=ts)
    ys = _experts(xs, tile_expert, n_used, wg, wu, wd, tm=tm_e, th=th)
    return _combine(tiled_pos(tc), ys, meta, x, ng, gate, tc=tc)


def _tmp_pool_mixer(u, pool_w, pool_scale):
    s = u.shape[1]
    cs = jnp.cumsum(u, axis=1)
    tt = jnp.arange(1, s + 1, dtype=F32)
    outs = []
    for gi, win in enumerate(POOL_WINDOWS):
        sl = slice(gi * POOL_GROUP, (gi + 1) * POOL_GROUP)
        cs_g = cs[..., sl]
        lag = jnp.pad(cs_g, ((0, 0), (win, 0), (0, 0)))[:, :s]
        count = jnp.minimum(tt, float(win))[None, :, None]
        pooled = (cs_g - lag) / count - u[..., sl]
        outs.append(pooled @ pool_w[gi])
    return jnp.concatenate(outs, axis=-1) * pool_scale


def _tmp_combine(left, right):
    a1r, a1i, b1r, b1i = left
    a2r, a2i, b2r, b2i = right
    return (a2r * a1r - a2i * a1i, a2r * a1i + a2i * a1r,
            a2r * b1r - a2i * b1i + b2r, a2r * b1i + a2i * b1r + b2i)


def _tmp_s5(u, lam_re, lam_im, log_step, b_re, b_im, c_re, c_im, d_skip, w_glu):
    bsz, s, _ = u.shape
    uf = u.reshape(bsz, s, S5_GROUPS, S5_GROUP)
    step = jnp.exp(log_step)[:, None]
    lr = jnp.minimum(lam_re, -1e-4)
    li = lam_im
    mag = jnp.exp(lr * step)
    ar = mag * jnp.cos(li * step)
    ai = mag * jnp.sin(li * step)
    inv = 1.0 / (lr * lr + li * li)
    fr = ((ar - 1.0) * lr + ai * li) * inv
    fi = (ai * lr - (ar - 1.0) * li) * inv
    bbr = fr[..., None] * b_re - fi[..., None] * b_im
    bbi = fr[..., None] * b_im + fi[..., None] * b_re
    bu_r = jnp.einsum("bsgh,gph->sbgp", uf, bbr)
    bu_i = jnp.einsum("bsgh,gph->sbgp", uf, bbi)
    a_r = jnp.broadcast_to(ar, (s, 1) + ar.shape)
    a_i = jnp.broadcast_to(ai, (s, 1) + ai.shape)
    _, _, xr, xi = lax.associative_scan(_tmp_combine, (a_r, a_i, bu_r, bu_i), axis=0)
    y = (jnp.einsum("sbgp,ghp->bsgh", xr, c_re) - jnp.einsum("sbgp,ghp->bsgh", xi, c_im))
    y = y + d_skip.reshape(S5_GROUPS, S5_GROUP) * uf
    y = jax.nn.gelu(y.reshape(bsz, s, S5_WIDTH))
    return y * jax.nn.sigmoid(y @ w_glu)


def _tmp_ssd(xs, dt, a, bm, cm):
    chunk = 64
    bsz, s = xs.shape[:2]
    nc = s // chunk
    xdt = (xs * dt[..., None]).reshape(bsz, nc, chunk, M2_GROUPS, M2_HPG, M2_HEADDIM)
    da = (dt * a).reshape(bsz, nc, chunk, M2_GROUPS, M2_HPG)
    bm = bm.reshape(bsz, nc, chunk, M2_GROUPS, M2_STATE)
    cm = cm.reshape(bsz, nc, chunk, M2_GROUPS, M2_STATE)
    cs = jnp.cumsum(da, axis=2)
    causal = jnp.tril(jnp.ones((chunk, chunk), dtype=bool))[:, :, None, None]
    seg = cs[:, :, :, None] - cs[:, :, None, :]
    decay = jnp.exp(jnp.where(causal, seg, -jnp.inf))
    cb = jnp.einsum("bclgn,bcsgn->bclsg", cm, bm)
    y_diag = jnp.einsum("bclsgj,bcsgjp->bclgjp", cb[..., None] * decay, xdt)
    to_end = jnp.exp(cs[:, :, -1:] - cs)
    states = jnp.einsum("bclgn,bclgjp->bcgjpn", bm, xdt * to_end[..., None])
    chunk_decay = jnp.exp(cs[:, :, -1])

    def step(carry, inp):
        st, dec = inp
        return carry * dec[..., None, None] + st, carry

    init = jnp.zeros((bsz, M2_GROUPS, M2_HPG, M2_HEADDIM, M2_STATE), F32)
    _, prev = lax.scan(step, init, (jnp.moveaxis(states, 1, 0), jnp.moveaxis(chunk_decay, 1, 0)))
    prev = jnp.moveaxis(prev, 0, 1)
    y_off = jnp.einsum("bclgn,bcgjpn->bclgjp", cm, prev) * jnp.exp(cs)[..., None]
    return (y_diag + y_off).reshape(bsz, s, M2_HEADS, M2_HEADDIM)


def _tmp_mamba_inner(proj, conv_w, conv_b, dt_bias, a_log, d_skip, norm_g):
    bsz, s, _ = proj.shape
    z = proj[..., :M2_INNER]
    xbc = proj[..., M2_INNER:M2_INNER + M2_CONV_DIM]
    dt_raw = proj[..., M2_INNER + M2_CONV_DIM:M2_PROJ]
    xbc = lax.conv_general_dilated(
        xbc, conv_w[:, None, :], window_strides=(1,), padding=[(M2_CONV - 1, 0)],
        dimension_numbers=("NWC", "WIO", "NWC"), feature_group_count=M2_CONV_DIM)
    xbc = jax.nn.silu(xbc + conv_b)
    xs = xbc[..., :M2_INNER].reshape(bsz, s, M2_HEADS, M2_HEADDIM)
    bm = xbc[..., M2_INNER:M2_INNER + M2_BC].reshape(bsz, s, M2_GROUPS, M2_STATE)
    cm = xbc[..., M2_INNER + M2_BC:].reshape(bsz, s, M2_GROUPS, M2_STATE)
    dt = jax.nn.softplus(dt_raw + dt_bias)
    a = -jnp.exp(a_log)
    y = _tmp_ssd(xs, dt, a, bm, cm) + d_skip[:, None] * xs
    y = y.reshape(bsz, s, M2_INNER) * jax.nn.silu(z)
    ms = jnp.mean(y * y, axis=-1, keepdims=True)
    return y * lax.rsqrt(ms + EPS) * norm_g


def kernel(x, c, ada_w, ada_b, norm_g, mix_w_in, pool_w, pool_scale, s5_lam_re, s5_lam_im, s5_log_step, s5_b_re, s5_b_im, s5_c_re, s5_c_im, s5_d, s5_w_glu, mix_w_out, ffn_w_gate, ffn_w_up, ffn_w_down, m2_w_in, m2_conv_w, m2_conv_b, m2_dt_bias, m2_a_log, m2_d, m2_norm_g, m2_w_out, moe_w_router, moe_b_router, moe_w_gate, moe_w_up, moe_w_down):
    bsz, seq, d = x.shape
    t = bsz * seq
    xt = x.reshape(t, d)

    mod = _ada_modulation(c, ada_w, ada_b)

    def mod_vecs(layer):
        return [mod[layer, :, k * d:(k + 1) * d].reshape(bsz, 1, d) for k in range(6)]

    def gvec(layer, k):
        return norm_g[layer, k].reshape(1, d)

    sh_m, sc_m, g_m, sh_f, sc_f, g_f = mod_vecs(0)
    u = _norm_matmul(xt, gvec(0, 0), sc_m, sh_m, mix_w_in[0].astype(BF16),
                     tm=1024, tn=1024, out_dtype=F32, name="mix_in_proj")
    u = u.reshape(bsz, seq, d)
    y_pool = _tmp_pool_mixer(u[..., :POOL_WIDTH], pool_w[0], pool_scale[0])
    y_s5 = _tmp_s5(u[..., POOL_WIDTH:], s5_lam_re[0], s5_lam_im[0], s5_log_step[0],
                   s5_b_re[0], s5_b_im[0], s5_c_re[0], s5_c_im[0], s5_d[0], s5_w_glu[0])
    ycat = jnp.concatenate([y_pool, y_s5], axis=-1).reshape(t, d).astype(BF16)
    xt = _matmul_post(ycat, mix_w_out[0].astype(BF16), xt, gvec(0, 1), g_m,
                      tm=1024, name="mix_out_proj")
    xt = _ffn_sublayer(xt, gvec(0, 2), sc_f, sh_f, ffn_w_gate[0].astype(BF16),
                       ffn_w_up[0].astype(BF16), ffn_w_down[0].astype(BF16), gvec(0, 3), g_f,
                       tm=1024, th=1408)

    sh_m, sc_m, g_m, sh_f, sc_f, g_f = mod_vecs(1)
    proj_pad = 6272
    w_in = jnp.zeros((d, proj_pad), BF16).at[:, :M2_PROJ].set(m2_w_in[0].astype(BF16))
    proj = _norm_matmul(xt, gvec(1, 0), sc_m, sh_m, w_in,
                        tm=1024, tn=896, out_dtype=F32, name="m2_in_proj")
    y = _tmp_mamba_inner(proj.reshape(bsz, seq, proj_pad), m2_conv_w[0], m2_conv_b[0],
                         m2_dt_bias[0], m2_a_log[0], m2_d[0], m2_norm_g[0])
    xt = _matmul_post(y.reshape(t, M2_INNER).astype(BF16), m2_w_out[0].astype(BF16), xt,
                      gvec(1, 1), g_m, tm=1024, name="m2_out_proj")
    xt = _moe_sublayer(xt, gvec(1, 2), sc_f, sh_f, moe_w_router[0], moe_b_router[0],
                       moe_w_gate[0].astype(BF16), moe_w_up[0].astype(BF16),
                       moe_w_down[0].astype(BF16), gvec(1, 3), g_f)
    return xt.reshape(bsz, seq, d)
```

```python
import functools
import math

import jax
import jax.numpy as jnp
from jax import lax
from jax.experimental import pallas as pl
from jax.experimental.pallas import tpu as pltpu

F32 = jnp.float32
BF16 = jnp.bfloat16

D_MODEL = 1024
EPS = 1e-6
POOL_WIDTH = 512
POOL_WINDOWS = (2, 4, 8, 16)
POOL_GROUP = 128
S5_WIDTH = 512
S5_GROUP = 16
S5_GROUPS = 32
S5_STATE = 64
M2_INNER = 2048
M2_HEADDIM = 64
M2_HEADS = 32
M2_GROUPS = 8
M2_HPG = 4
M2_STATE = 128
M2_CONV = 4
M2_BC = 1024
M2_CONV_DIM = 4096
M2_PROJ = 6176
FFN_HIDDEN = 2816
N_EXPERTS = 8
EXPERT_HIDDEN = 3584

LANES = 128
SUBLANES = 8
VMEM_LIMIT_BYTES = 56 * 1024 * 1024

NEG_BIG = -1e30


def _cparams(sem):
    return pltpu.CompilerParams(dimension_semantics=sem, vmem_limit_bytes=VMEM_LIMIT_BYTES)


def _modulated_norm(x, g, scale, shift):
    ms = jnp.mean(x * x, axis=-1, keepdims=True)
    return (x * lax.rsqrt(ms + EPS) * g) * (1.0 + scale) + shift


def _post_norm_residual(x, y, ng, gate):
    ms = jnp.mean(y * y, axis=-1, keepdims=True)
    return x + gate * (y * lax.rsqrt(ms + EPS) * ng)


def _ada_kernel(c_ref, w_ref, b_ref, o_ref):
    c = c_ref[...]
    a = c * jax.nn.sigmoid(c)
    o_ref[0] = jnp.dot(a.astype(BF16), w_ref[0].astype(BF16),
                       preferred_element_type=F32) + b_ref[0]


def _ada_modulation(c, ada_w, ada_b):
    depth, d, n = ada_w.shape
    b = c.shape[0]
    c_pad = jnp.zeros((SUBLANES, d), F32).at[:b].set(c)
    tn = 1024
    out = pl.pallas_call(
        _ada_kernel,
        grid=(depth, n // tn),
        in_specs=[
            pl.BlockSpec((SUBLANES, d), lambda l, j: (0, 0)),
            pl.BlockSpec((1, d, tn), lambda l, j: (l, 0, j)),
            pl.BlockSpec((1, 1, tn), lambda l, j: (l, 0, j)),
        ],
        out_specs=pl.BlockSpec((1, SUBLANES, tn), lambda l, j: (l, 0, j)),
        out_shape=jax.ShapeDtypeStruct((depth, SUBLANES, n), F32),
        compiler_params=_cparams(("arbitrary", "arbitrary")),
        name="ada_modulation",
    )(c_pad, ada_w, ada_b.reshape(depth, 1, n))
    return out[:, :b]


def _norm_mm_kernel(x_ref, g_ref, sc_ref, sh_ref, w_ref, o_ref, h_scr):
    @pl.when(pl.program_id(1) == 0)
    def _():
        h = _modulated_norm(x_ref[...], g_ref[...], sc_ref[0], sh_ref[0])
        h_scr[...] = h.astype(BF16)

    o_ref[...] = jnp.dot(h_scr[...], w_ref[...], preferred_element_type=F32).astype(o_ref.dtype)


def _norm_matmul(x, g, scale, shift, w, *, tm, tn, out_dtype, name):
    t, d = x.shape
    n = w.shape[1]
    tiles_per_batch = t // scale.shape[0] // tm
    return pl.pallas_call(
        _norm_mm_kernel,
        grid=(t // tm, n // tn),
        in_specs=[
            pl.BlockSpec((tm, d), lambda i, j: (i, 0)),
            pl.BlockSpec((1, d), lambda i, j: (0, 0)),
            pl.BlockSpec((1, 1, d), lambda i, j: (i // tiles_per_batch, 0, 0)),
            pl.BlockSpec((1, 1, d), lambda i, j: (i // tiles_per_batch, 0, 0)),
            pl.BlockSpec((d, tn), lambda i, j: (0, j)),
        ],
        out_specs=pl.BlockSpec((tm, tn), lambda i, j: (i, j)),
        out_shape=jax.ShapeDtypeStruct((t, n), out_dtype),
        scratch_shapes=[pltpu.VMEM((tm, d), BF16)],
        compiler_params=_cparams(("arbitrary", "arbitrary")),
        name=name,
    )(x, g, scale, shift, w)


def _mm_post_kernel(y_ref, w_ref, x_ref, ng_ref, gate_ref, o_ref):
    y = jnp.dot(y_ref[...], w_ref[...], preferred_element_type=F32)
    o_ref[...] = _post_norm_residual(x_ref[...], y, ng_ref[...], gate_ref[0])


def _matmul_post(y, w, x, ng, gate, *, tm, name):
    t, k = y.shape
    d = w.shape[1]
    tiles_per_batch = t // gate.shape[0] // tm
    return pl.pallas_call(
        _mm_post_kernel,
        grid=(t // tm,),
        in_specs=[
            pl.BlockSpec((tm, k), lambda i: (i, 0)),
            pl.BlockSpec((k, d), lambda i: (0, 0)),
            pl.BlockSpec((tm, d), lambda i: (i, 0)),
            pl.BlockSpec((1, d), lambda i: (0, 0)),
            pl.BlockSpec((1, 1, d), lambda i: (i // tiles_per_batch, 0, 0)),
        ],
        out_specs=pl.BlockSpec((tm, d), lambda i: (i, 0)),
        out_shape=jax.ShapeDtypeStruct((t, d), F32),
        compiler_params=_cparams(("arbitrary",)),
        name=name,
    )(y, w, x, ng, gate)


def _ffn_kernel(x_ref, g_ref, sc_ref, sh_ref, wg_ref, wu_ref, wd_ref, ng_ref, gate_ref,
                o_ref, h_scr, acc_scr):
    j = pl.program_id(1)

    @pl.when(j == 0)
    def _():
        h = _modulated_norm(x_ref[...], g_ref[...], sc_ref[0], sh_ref[0])
        h_scr[...] = h.astype(BF16)
        acc_scr[...] = jnp.zeros_like(acc_scr)

    h = h_scr[...]
    gt = jnp.dot(h, wg_ref[...], preferred_element_type=F32)
    up = jnp.dot(h, wu_ref[...], preferred_element_type=F32)
    act = (gt * jax.nn.sigmoid(gt) * up).astype(BF16)
    acc_scr[...] += jnp.dot(act, wd_ref[...], preferred_element_type=F32)

    @pl.when(j == pl.num_programs(1) - 1)
    def _():
        o_ref[...] = _post_norm_residual(x_ref[...], acc_scr[...], ng_ref[...], gate_ref[0])


def _ffn_sublayer(x, g, scale, shift, wg, wu, wd, ng, gate, *, tm, th):
    t, d = x.shape
    hid = wg.shape[1]
    tiles_per_batch = t // scale.shape[0] // tm
    bvec = pl.BlockSpec((1, 1, d), lambda i, j: (i // tiles_per_batch, 0, 0))
    return pl.pallas_call(
        _ffn_kernel,
        grid=(t // tm, hid // th),
        in_specs=[
            pl.BlockSpec((tm, d), lambda i, j: (i, 0)),
            pl.BlockSpec((1, d), lambda i, j: (0, 0)),
            bvec, bvec,
            pl.BlockSpec((d, th), lambda i, j: (0, j)),
            pl.BlockSpec((d, th), lambda i, j: (0, j)),
            pl.BlockSpec((th, d), lambda i, j: (j, 0)),
            pl.BlockSpec((1, d), lambda i, j: (0, 0)),
            bvec,
        ],
        out_specs=pl.BlockSpec((tm, d), lambda i, j: (i, 0)),
        out_shape=jax.ShapeDtypeStruct((t, d), F32),
        scratch_shapes=[pltpu.VMEM((tm, d), BF16), pltpu.VMEM((tm, d), F32)],
        compiler_params=_cparams(("arbitrary", "arbitrary")),
        name="ffn_sublayer",
    )(x, g, scale, shift, wg, wu, wd, ng, gate)


def _router_kernel(x_ref, g_ref, sc_ref, sh_ref, wr_ref, br_ref, h3_ref, meta_ref, cnt_ref,
                   carry_scr):
    tm = x_ref.shape[0]

    @pl.when(pl.program_id(0) == 0)
    def _():
        carry_scr[...] = jnp.zeros_like(carry_scr)

    h = _modulated_norm(x_ref[...], g_ref[...], sc_ref[0], sh_ref[0])
    for s in range(SUBLANES):
        h3_ref[:, s, :] = h[:, s * LANES:(s + 1) * LANES]

    logits = jnp.dot(h, wr_ref[...], preferred_element_type=F32,
                     precision=lax.Precision.HIGHEST) + br_ref[...]
    lane = lax.broadcasted_iota(jnp.int32, (tm, LANES), 1).astype(F32)
    m1 = jnp.max(logits, axis=-1, keepdims=True)
    i1 = jnp.min(jnp.where(logits == m1, lane, float(LANES)), axis=-1, keepdims=True)
    oh1 = lane == i1
    rest = jnp.where(oh1, NEG_BIG * 2.0, logits)
    m2 = jnp.max(rest, axis=-1, keepdims=True)
    i2 = jnp.min(jnp.where(rest == m2, lane, float(LANES)), axis=-1, keepdims=True)
    oh2 = lane == i2
    e = jnp.exp(m2 - m1)
    w1 = 1.0 / (1.0 + e)
    w2 = e / (1.0 + e)

    picks = jnp.where(oh1, 1.0, 0.0) + jnp.where(oh2, 1.0, 0.0)
    row = lax.broadcasted_iota(jnp.int32, (tm, tm), 0)
    col = lax.broadcasted_iota(jnp.int32, (tm, tm), 1)
    lower = jnp.where(col < row, 1.0, 0.0).astype(BF16)
    before = jnp.dot(lower, picks.astype(BF16), preferred_element_type=F32) + carry_scr[...]
    r1 = jnp.sum(jnp.where(oh1, before, 0.0), axis=-1, keepdims=True)
    r2 = jnp.sum(jnp.where(oh2, before, 0.0), axis=-1, keepdims=True)
    carry_scr[...] += jnp.sum(picks, axis=0, keepdims=True)
    cnt_ref[...] = carry_scr[...]

    meta = jnp.where(lane == 0.0, i1, 0.0)
    meta = jnp.where(lane == 1.0, i2, meta)
    meta = jnp.where(lane == 2.0, w1, meta)
    meta = jnp.where(lane == 3.0, w2, meta)
    meta = jnp.where(lane == 4.0, r1, meta)
    meta = jnp.where(lane == 5.0, r2, meta)
    meta_ref[...] = meta


def _router(x, g, scale, shift, w_router, b_router, *, tm):
    t, d = x.shape
    tiles_per_batch = t // scale.shape[0] // tm
    wr = jnp.zeros((d, LANES), F32).at[:, :N_EXPERTS].set(w_router)
    br = jnp.full((1, LANES), NEG_BIG, F32).at[0, :N_EXPERTS].set(b_router)
    bvec = pl.BlockSpec((1, 1, d), lambda i: (i // tiles_per_batch, 0, 0))
    return pl.pallas_call(
        _router_kernel,
        grid=(t // tm,),
        in_specs=[
            pl.BlockSpec((tm, d), lambda i: (i, 0)),
            pl.BlockSpec((1, d), lambda i: (0, 0)),
            bvec, bvec,
            pl.BlockSpec((d, LANES), lambda i: (0, 0)),
            pl.BlockSpec((1, LANES), lambda i: (0, 0)),
        ],
        out_specs=[
            pl.BlockSpec((tm, SUBLANES, LANES), lambda i: (i, 0, 0)),
            pl.BlockSpec((tm, LANES), lambda i: (i, 0)),
            pl.BlockSpec((1, LANES), lambda i: (0, 0)),
        ],
        out_shape=[
            jax.ShapeDtypeStruct((t, SUBLANES, LANES), F32),
            jax.ShapeDtypeStruct((t, LANES), F32),
            jax.ShapeDtypeStruct((1, LANES), F32),
        ],
        scratch_shapes=[pltpu.VMEM((1, LANES), F32)],
        compiler_params=_cparams(("arbitrary",)),
        name="moe_router",
    )(x, g, scale, shift, wr, br)


def _dispatch_kernel(pos_ref, h3_ref, xs_in_ref, xs_ref, sem):
    del xs_in_ref
    ts = h3_ref.shape[0]

    def issue(r, carry):
        pltpu.make_async_copy(h3_ref.at[r], xs_ref.at[pos_ref[0, 0, r]], sem).start()
        pltpu.make_async_copy(h3_ref.at[r], xs_ref.at[pos_ref[0, 0, ts + r]], sem).start()
        return carry

    lax.fori_loop(0, ts, issue, 0)

    def drain(r, carry):
        pltpu.make_async_copy(h3_ref.at[0], xs_ref.at[0], sem).wait()
        pltpu.make_async_copy(h3_ref.at[0], xs_ref.at[0], sem).wait()
        return carry

    lax.fori_loop(0, ts, drain, 0)


def _dispatch(h3, pos, n_slots, *, ts):
    t = h3.shape[0]
    xs0 = jnp.zeros((n_slots, SUBLANES, LANES), F32)
    return pl.pallas_call(
        _dispatch_kernel,
        grid=(t // ts,),
        in_specs=[
            pl.BlockSpec((1, 1, 2 * ts), lambda i: (i, 0, 0), memory_space=pltpu.SMEM),
            pl.BlockSpec((ts, SUBLANES, LANES), lambda i: (i, 0, 0)),
            pl.BlockSpec(memory_space=pl.ANY),
        ],
        out_specs=pl.BlockSpec(memory_space=pl.ANY),
        out_shape=jax.ShapeDtypeStruct((n_slots, SUBLANES, LANES), F32),
        scratch_shapes=[pltpu.SemaphoreType.DMA(())],
        input_output_aliases={2: 0},
        compiler_params=_cparams(("arbitrary",)),
        name="moe_dispatch",
    )(pos, h3, xs0)


def _expert_kernel(te_ref, nu_ref, xs_ref, wg_ref, wu_ref, wd_ref, ys_ref, xb_scr, acc_scr):
    i = pl.program_id(0)
    j = pl.program_id(1)
    last = pl.num_programs(1) - 1
    used = i < nu_ref[0]

    @pl.when(used)
    def _():
        @pl.when(j == 0)
        def _():
            for s in range(SUBLANES):
                xb_scr[:, s * LANES:(s + 1) * LANES] = xs_ref[:, s, :].astype(BF16)
            acc_scr[...] = jnp.zeros_like(acc_scr)

        x = xb_scr[...]
        gt = jnp.dot(x, wg_ref[0], preferred_element_type=F32)
        up = jnp.dot(x, wu_ref[0], preferred_element_type=F32)
        act = (gt * jax.nn.sigmoid(gt) * up).astype(BF16)
        acc_scr[...] += jnp.dot(act, wd_ref[0], preferred_element_type=F32)

        @pl.when(j == last)
        def _():
            for s in range(SUBLANES):
                ys_ref[:, s, :] = acc_scr[:, s * LANES:(s + 1) * LANES]

    @pl.when(jnp.logical_and(jnp.logical_not(used), j == last))
    def _():
        ys_ref[...] = jnp.zeros_like(ys_ref)


def _experts(xs, tile_expert, n_used, wg, wu, wd, *, tm, th):
    n_slots = xs.shape[0]
    d = D_MODEL
    hid = wg.shape[2]
    nt = n_slots // tm

    def hidden_step(i, j, nu):
        return jnp.where(i < nu[0], j, 0)

    grid_spec = pltpu.PrefetchScalarGridSpec(
        num_scalar_prefetch=2,
        grid=(nt, hid // th),
        in_specs=[
            pl.BlockSpec((tm, SUBLANES, LANES), lambda i, j, te, nu: (i, 0, 0)),
            pl.BlockSpec((1, d, th), lambda i, j, te, nu: (te[i], 0, hidden_step(i, j, nu))),
            pl.BlockSpec((1, d, th), lambda i, j, te, nu: (te[i], 0, hidden_step(i, j, nu))),
            pl.BlockSpec((1, th, d), lambda i, j, te, nu: (te[i], hidden_step(i, j, nu), 0)),
        ],
        out_specs=pl.BlockSpec((tm, SUBLANES, LANES), lambda i, j, te, nu: (i, 0, 0)),
        scratch_shapes=[pltpu.VMEM((tm, d), BF16), pltpu.VMEM((tm, d), F32)],
    )
    return pl.pallas_call(
        _expert_kernel,
        grid_spec=grid_spec,
        out_shape=jax.ShapeDtypeStruct((n_slots, SUBLANES, LANES), F32),
        compiler_params=_cparams(("arbitrary", "arbitrary")),
        name="moe_experts",
    )(tile_expert, n_used, xs, wg, wu, wd)


def _combine_kernel(pos_ref, ys_ref, meta_ref, x_ref, ng_ref, gate_ref, o_ref, buf, sem):
    tc = x_ref.shape[0]

    def issue(r, carry):
        pltpu.make_async_copy(ys_ref.at[pos_ref[0, 0, r]], buf.at[r], sem).start()
        pltpu.make_async_copy(ys_ref.at[pos_ref[0, 0, tc + r]], buf.at[tc + r], sem).start()
        return carry

    lax.fori_loop(0, tc, issue, 0)

    def drain(r, carry):
        pltpu.make_async_copy(ys_ref.at[0], buf.at[0], sem).wait()
        pltpu.make_async_copy(ys_ref.at[0], buf.at[0], sem).wait()
        return carry

    lax.fori_loop(0, tc, drain, 0)

    meta = meta_ref[...]
    w1 = meta[:, 2:3]
    w2 = meta[:, 3:4]
    first = jnp.concatenate([buf[0:tc, s, :] for s in range(SUBLANES)], axis=-1)
    second = jnp.concatenate([buf[tc:2 * tc, s, :] for s in range(SUBLANES)], axis=-1)
    y = w1 * first + w2 * second
    o_ref[...] = _post_norm_residual(x_ref[...], y, ng_ref[...], gate_ref[0])


def _combine(pos, ys, meta, x, ng, gate, *, tc):
    t, d = x.shape
    tiles_per_batch = t // gate.shape[0] // tc
    return pl.pallas_call(
        _combine_kernel,
        grid=(t // tc,),
        in_specs=[
            pl.BlockSpec((1, 1, 2 * tc), lambda i: (i, 0, 0), memory_space=pltpu.SMEM),
            pl.BlockSpec(memory_space=pl.ANY),
            pl.BlockSpec((tc, LANES), lambda i: (i, 0)),
            pl.BlockSpec((tc, d), lambda i: (i, 0)),
            pl.BlockSpec((1, d), lambda i: (0, 0)),
            pl.BlockSpec((1, 1, d), lambda i: (i // tiles_per_batch, 0, 0)),
        ],
        out_specs=pl.BlockSpec((tc, d), lambda i: (i, 0)),
        out_shape=jax.ShapeDtypeStruct((t, d), F32),
        scratch_shapes=[pltpu.VMEM((2 * tc, SUBLANES, LANES), F32), pltpu.SemaphoreType.DMA(())],
        compiler_params=_cparams(("arbitrary",)),
        name="moe_combine",
    )(pos, ys, meta, x, ng, gate)


def _moe_sublayer(x, g, scale, shift, w_router, b_router, wg, wu, wd, ng, gate):
    t = x.shape[0]
    tm_r, ts, tm_e, th, tc = 512, 256, 512, 512, 256
    h3, meta, counts = _router(x, g, scale, shift, w_router, b_router, tm=tm_r)

    cnt = counts[0, :N_EXPERTS].astype(jnp.int32)
    padded = ((cnt + tm_e - 1) // tm_e) * tm_e
    ends = jnp.cumsum(padded)
    offs = ends - padded
    i1 = meta[:, 0].astype(jnp.int32)
    i2 = meta[:, 1].astype(jnp.int32)
    experts = jnp.arange(N_EXPERTS, dtype=jnp.int32)
    off1 = jnp.sum(jnp.where(i1[:, None] == experts[None, :], offs[None, :], 0), axis=1)
    off2 = jnp.sum(jnp.where(i2[:, None] == experts[None, :], offs[None, :], 0), axis=1)
    pos1 = off1 + meta[:, 4].astype(jnp.int32)
    pos2 = off2 + meta[:, 5].astype(jnp.int32)
    nt = 2 * t // tm_e + N_EXPERTS
    starts = jnp.arange(nt, dtype=jnp.int32) * tm_e
    tile_expert = jnp.minimum(jnp.sum(starts[:, None] >= ends[None, :], axis=1), N_EXPERTS - 1)
    tile_expert = tile_expert.astype(jnp.int32)
    n_used = (ends[-1:] // tm_e).astype(jnp.int32)

    def tiled_pos(tile):
        return jnp.concatenate([pos1.reshape(t // tile, 1, tile), pos2.reshape(t // tile, 1, tile)],
                               axis=-1)

    xs = _dispatch(h3, tiled_pos(ts), nt * tm_e, ts=ts)
    ys = _experts(xs, tile_expert, n_used, wg, wu, wd, tm=tm_e, th=th)
    return _combine(tiled_pos(tc), ys, meta, x, ng, gate, tc=tc)


def _s5_disc_kernel(lam_re_ref, lam_im_ref, step_ref, pow_ref, b_re_ref, b_im_ref,
                    apr_ref, api_ref, bbr_ref, bbi_ref):
    lr = jnp.minimum(lam_re_ref[...], -1e-4)
    li = lam_im_ref[...]
    step = step_ref[...]
    m = pow_ref[...]
    mag = jnp.exp(lr * step * m)
    ang = li * step * m
    apr_ref[...] = mag * jnp.cos(ang)
    api_ref[...] = mag * jnp.sin(ang)
    mag1 = jnp.exp(lr * step)
    ar = mag1 * jnp.cos(li * step)
    ai = mag1 * jnp.sin(li * step)
    inv = 1.0 / (lr * lr + li * li)
    fr = ((ar - 1.0) * lr + ai * li) * inv
    fi = (ai * lr - (ar - 1.0) * li) * inv
    br = b_re_ref[...]
    bi = b_im_ref[...]
    bbr_ref[...] = fr * br - fi * bi
    bbi_ref[...] = fr * bi + fi * br


def _s5_discretise(lam_re, lam_im, log_step, b_re, b_im, n_pow):
    gp = S5_GROUPS * S5_STATE
    step = jnp.repeat(jnp.exp(log_step), S5_STATE).reshape(1, gp)
    pows = (2.0 ** jnp.arange(n_pow, dtype=F32)).reshape(n_pow, 1)
    b_re_t = jnp.transpose(b_re, (2, 0, 1)).reshape(S5_GROUP, gp)
    b_im_t = jnp.transpose(b_im, (2, 0, 1)).reshape(S5_GROUP, gp)
    return pl.pallas_call(
        _s5_disc_kernel,
        out_shape=[jax.ShapeDtypeStruct((n_pow, gp), F32), jax.ShapeDtypeStruct((n_pow, gp), F32),
                   jax.ShapeDtypeStruct((S5_GROUP, gp), F32), jax.ShapeDtypeStruct((S5_GROUP, gp), F32)],
        name="s5_discretise",
    )(lam_re.reshape(1, gp), lam_im.reshape(1, gp), step, pows, b_re_t, b_im_t)


POOL_HALO = 16
S5_HALF = 256
S5_HALF_STATES = 1024


def _complex_axpy(xr, xi, cr, ci, sr, si):
    return xr + (cr * sr - ci * si), xi + (cr * si + ci * sr)


def _mix0_kernel(u_ref, pw_ref, ps_ref, bw_ref, cw_ref, apr_ref, api_ref, dsk_ref, wglu_ref,
                 o_ref, halo_scr, ext_scr, sr_scr, si_scr, xr_scr, xi_scr):
    tt = u_ref.shape[0]
    n_states = xr_scr.shape[1]
    blk = pl.program_id(1)

    @pl.when(blk == 0)
    def _():
        halo_scr[...] = jnp.zeros_like(halo_scr)
        sr_scr[...] = jnp.zeros_like(sr_scr)
        si_scr[...] = jnp.zeros_like(si_scr)

    up = u_ref[:, :POOL_WIDTH]
    ext_scr[0:POOL_HALO, :] = halo_scr[...]
    ext_scr[POOL_HALO:, :] = up
    halo_scr[...] = up[tt - POOL_HALO:, :]
    ext = ext_scr[...].astype(BF16)
    row = lax.broadcasted_iota(jnp.int32, (tt, tt + POOL_HALO), 0)
    col = lax.broadcasted_iota(jnp.int32, (tt, tt + POOL_HALO), 1)
    lag = row + POOL_HALO - col
    t_glob = (blk * tt + row + 1).astype(F32)
    pooled_out = []
    for gi, win in enumerate(POOL_WINDOWS):
        inv_count = 1.0 / jnp.minimum(t_glob, float(win))
        band = jnp.where((lag >= 0) & (lag < win), inv_count, 0.0) - jnp.where(lag == 0, 1.0, 0.0)
        pooled = jnp.dot(band.astype(BF16), ext[:, gi * POOL_GROUP:(gi + 1) * POOL_GROUP],
                         preferred_element_type=F32)
        pooled_out.append(jnp.dot(pooled.astype(BF16), pw_ref[gi], preferred_element_type=F32))
    y_pool = jnp.concatenate(pooled_out, axis=-1) * ps_ref[...]
    o_ref[:, :POOL_WIDTH] = y_pool.astype(o_ref.dtype)

    us = u_ref[:, POOL_WIDTH:]
    usb = us.astype(BF16)
    for hf in range(2):
        bu = jnp.dot(usb[:, hf * S5_HALF:(hf + 1) * S5_HALF], bw_ref[hf],
                     preferred_element_type=F32)
        xr_scr[:, hf * S5_HALF_STATES:(hf + 1) * S5_HALF_STATES] = bu[:, :S5_HALF_STATES]
        xi_scr[:, hf * S5_HALF_STATES:(hf + 1) * S5_HALF_STATES] = bu[:, S5_HALF_STATES:]
    r0, i0 = _complex_axpy(xr_scr[0:1, :], xi_scr[0:1, :], apr_ref[0:1, :], api_ref[0:1, :],
                           sr_scr[...], si_scr[...])
    xr_scr[0:1, :] = r0
    xi_scr[0:1, :] = i0
    rows = lax.broadcasted_iota(jnp.int32, (tt, n_states), 0)
    k = 0
    shift = 1
    while shift < tt:
        cr = apr_ref[k:k + 1, :]
        ci = api_ref[k:k + 1, :]
        if shift < SUBLANES:
            xr = xr_scr[...]
            xi = xi_scr[...]
            pr = jnp.where(rows < shift, 0.0, pltpu.roll(xr, shift, 0))
            pi = jnp.where(rows < shift, 0.0, pltpu.roll(xi, shift, 0))
            nr, ni = _complex_axpy(xr, xi, cr, ci, pr, pi)
            xr_scr[...] = nr
            xi_scr[...] = ni
        else:
            pr = xr_scr[0:tt - shift, :]
            pi = xi_scr[0:tt - shift, :]
            nr, ni = _complex_axpy(xr_scr[shift:, :], xi_scr[shift:, :], cr, ci, pr, pi)
            xr_scr[shift:, :] = nr
            xi_scr[shift:, :] = ni
        shift *= 2
        k += 1
    sr_scr[...] = xr_scr[tt - 1:tt, :]
    si_scr[...] = xi_scr[tt - 1:tt, :]

    ys = []
    for hf in range(2):
        sl = slice(hf * S5_HALF_STATES, (hf + 1) * S5_HALF_STATES)
        xcat = jnp.concatenate([xr_scr[:, sl], xi_scr[:, sl]], axis=-1).astype(BF16)
        ys.append(jnp.dot(xcat, cw_ref[hf], preferred_element_type=F32))
    y = jnp.concatenate(ys, axis=-1) + dsk_ref[...] * us
    y = jax.nn.gelu(y)
    gate = jnp.dot(y.astype(BF16), wglu_ref[...], preferred_element_type=F32)
    o_ref[:, POOL_WIDTH:] = (y * jax.nn.sigmoid(gate)).astype(o_ref.dtype)


def _pool_s5_mixer(u, bsz, pool_w, pool_scale, lam_re, lam_im, log_step, b_re, b_im, c_re, c_im,
                   d_skip, w_glu, *, tt):
    t = u.shape[0]
    seq = t // bsz
    n_pow = int(math.log2(tt))
    gp = S5_GROUPS * S5_STATE
    apr, api, bbr, bbi = _s5_discretise(lam_re, lam_im, log_step, b_re, b_im, n_pow)

    gh = S5_GROUPS // 2
    eye = jnp.eye(gh, dtype=F32)

    def in_map(bb):
        bb = bb.reshape(S5_GROUP, 2, gh, S5_STATE)
        return jnp.einsum("hxgp,gk->xghkp", bb, eye).reshape(2, gh * S5_GROUP, gh * S5_STATE)

    bw = jnp.concatenate([in_map(bbr), in_map(bbi)], axis=-1).astype(BF16)

    def out_map(cc):
        cc = cc.reshape(2, gh, S5_GROUP, S5_STATE)
        return jnp.einsum("xghp,gk->xgpkh", cc, eye).reshape(2, gh * S5_STATE, gh * S5_GROUP)

    cw = jnp.concatenate([out_map(c_re), -out_map(c_im)], axis=1).astype(BF16)

    const2 = lambda b, i: (0, 0)
    const3 = lambda b, i: (0, 0, 0)
    nblk = seq // tt
    return pl.pallas_call(
        _mix0_kernel,
        grid=(bsz, nblk),
        in_specs=[
            pl.BlockSpec((tt, D_MODEL), lambda b, i: (b * nblk + i, 0)),
            pl.BlockSpec((len(POOL_WINDOWS), POOL_GROUP, POOL_GROUP), const3),
            pl.BlockSpec((1, POOL_WIDTH), const2),
            pl.BlockSpec((2, S5_HALF, 2 * S5_HALF_STATES), const3),
            pl.BlockSpec((2, 2 * S5_HALF_STATES, S5_HALF), const3),
            pl.BlockSpec((n_pow, gp), const2),
            pl.BlockSpec((n_pow, gp), const2),
            pl.BlockSpec((1, S5_WIDTH), const2),
            pl.BlockSpec((S5_WIDTH, S5_WIDTH), const2),
        ],
        out_specs=pl.BlockSpec((tt, D_MODEL), lambda b, i: (b * nblk + i, 0)),
        out_shape=jax.ShapeDtypeStruct((t, D_MODEL), BF16),
        scratch_shapes=[
            pltpu.VMEM((POOL_HALO, POOL_WIDTH), F32),
            pltpu.VMEM((tt + POOL_HALO, POOL_WIDTH), F32),
            pltpu.VMEM((1, gp), F32), pltpu.VMEM((1, gp), F32),
            pltpu.VMEM((tt, gp), F32), pltpu.VMEM((tt, gp), F32),
        ],
        compiler_params=_cparams(("arbitrary", "arbitrary")),
        name="pool_s5_mixer",
    )(u, pool_w.astype(BF16), pool_scale.reshape(1, POOL_WIDTH), bw, cw, apr, api,
      d_skip.reshape(1, S5_WIDTH), w_glu.astype(BF16))


CONV_HALO = SUBLANES
M2_GROUP_WIDTH = M2_HPG * M2_HEADDIM


def _causal_conv_silu(x_ref, halo_scr, full_scr, w_ref, b_ref, col0):
    lc = x_ref.shape[0]
    width = x_ref.shape[1]
    x = x_ref[...].astype(F32)
    full_scr[0:CONV_HALO, :] = halo_scr[...]
    full_scr[CONV_HALO:, :] = x
    halo_scr[...] = x[lc - CONV_HALO:, :]
    cols = slice(col0, col0 + width)
    acc = b_ref[:, cols] + w_ref[M2_CONV - 1:M2_CONV, cols] * x
    for k in range(M2_CONV - 1):
        off = CONV_HALO - (M2_CONV - 1) + k
        acc = acc + w_ref[k:k + 1, cols] * full_scr[off:off + lc, :]
    return acc * jax.nn.sigmoid(acc)


def _split_dot(v, e):
    hi = v.astype(BF16)
    lo = (v - hi.astype(F32)).astype(BF16)
    return (jnp.dot(hi, e, preferred_element_type=F32) + jnp.dot(lo, e, preferred_element_type=F32))


def _ssd_kernel(z_ref, xs_ref, bc_ref, dtr_ref, cw_ref, cb_ref, dtb_ref, alog_ref, dx_ref, ng_ref,
                e_ref, o_ref, halo_x, halo_bc, full_x, full_bc, st_scr, y_scr):
    lc = z_ref.shape[0]

    @pl.when(pl.program_id(1) == 0)
    def _():
        halo_x[...] = jnp.zeros_like(halo_x)
        halo_bc[...] = jnp.zeros_like(halo_bc)
        st_scr[...] = jnp.zeros_like(st_scr)

    xs = _causal_conv_silu(xs_ref, halo_x, full_x, cw_ref, cb_ref, 0)
    bc = _causal_conv_silu(bc_ref, halo_bc, full_bc, cw_ref, cb_ref, M2_INNER)

    lane = lax.broadcasted_iota(jnp.int32, (1, LANES), 1)
    x_dt = dtr_ref[...].astype(F32) + dtb_ref[...]
    dt = jnp.maximum(x_dt, 0.0) + jnp.log(1.0 + jnp.exp(-jnp.abs(x_dt)))
    a = jnp.where(lane < M2_HEADS, -jnp.exp(alog_ref[...]), 0.0)
    da = dt * a
    row = lax.broadcasted_iota(jnp.int32, (lc, lc), 0)
    col = lax.broadcasted_iota(jnp.int32, (lc, lc), 1)
    causal = col <= row
    cs = jnp.dot(jnp.where(causal, 1.0, 0.0), da, preferred_element_type=F32,
                 precision=lax.Precision.HIGHEST)
    cs_last = cs[lc - 1:lc, :]
    ecs = jnp.exp(cs)
    w_in = dt * jnp.exp(cs_last - cs)
    cs_t = cs.T
    dt_t = dt.T
    e = e_ref[...]
    ecs_x = _split_dot(ecs, e)
    w_x = _split_dot(w_in, e)
    xsb = xs.astype(BF16)
    xw = (xs * w_x).astype(BF16)

    for g in range(M2_GROUPS):
        bm = bc[:, g * M2_STATE:(g + 1) * M2_STATE].astype(BF16)
        cm = bc[:, M2_BC + g * M2_STATE:M2_BC + (g + 1) * M2_STATE].astype(BF16)
        cbm = lax.dot_general(cm, bm, (((1,), (1,)), ((), ())), preferred_element_type=F32)
        gcols = slice(g * M2_GROUP_WIDTH, (g + 1) * M2_GROUP_WIDTH)
        st = st_scr[g]
        y_off = jnp.dot(cm, st.astype(BF16), preferred_element_type=F32) * ecs_x[:, gcols]
        y_heads = []
        for jj in range(M2_HPG):
            j = g * M2_HPG + jj
            seg = cs[:, j:j + 1] - cs_t[j:j + 1, :]
            dec = jnp.exp(jnp.where(causal, seg, NEG_BIG))
            m = (cbm * dec * dt_t[j:j + 1, :]).astype(BF16)
            y_heads.append(jnp.dot(m, xsb[:, j * M2_HEADDIM:(j + 1) * M2_HEADDIM],
                                   preferred_element_type=F32))
        y_g = jnp.concatenate(y_heads, axis=-1) + y_off
        st_scr[g] = st * ecs_x[lc - 1:lc, gcols] + lax.dot_general(
            bm, xw[:, gcols], (((0,), (0,)), ((), ())), preferred_element_type=F32)
        zg = z_ref[:, gcols].astype(F32)
        y_g = (y_g + dx_ref[:, gcols] * xs[:, gcols]) * (zg * jax.nn.sigmoid(zg))
        y_scr[:, gcols] = y_g

    y = y_scr[...]
    ms = jnp.mean(y * y, axis=-1, keepdims=True)
    o_ref[...] = (y * lax.rsqrt(ms + EPS) * ng_ref[...]).astype(o_ref.dtype)


def _ssd_mixer(proj, bsz, conv_w, conv_b, dt_bias, a_log, d_skip, norm_g, *, lc):
    t = proj.shape[0]
    seq = t // bsz
    nblk = seq // lc
    half = M2_INNER
    pad = LANES - M2_HEADS
    dtb = jnp.pad(dt_bias, (0, pad)).reshape(1, LANES)
    alog = jnp.pad(a_log, (0, pad)).reshape(1, LANES)
    dx = jnp.repeat(d_skip, M2_HEADDIM).reshape(1, M2_INNER)
    heads = jnp.arange(LANES, dtype=jnp.int32)[:, None]
    chans = jnp.arange(M2_INNER, dtype=jnp.int32)[None, :] // M2_HEADDIM
    expand = (heads == chans).astype(BF16)
    const2 = lambda b, i: (0, 0)
    return pl.pallas_call(
        _ssd_kernel,
        grid=(bsz, nblk),
        in_specs=[
            pl.BlockSpec((lc, half), lambda b, i: (b * nblk + i, 0)),
            pl.BlockSpec((lc, half), lambda b, i: (b * nblk + i, 1)),
            pl.BlockSpec((lc, half), lambda b, i: (b * nblk + i, 2)),
            pl.BlockSpec((lc, LANES), lambda b, i: (b * nblk + i, 3 * half // LANES)),
            pl.BlockSpec((M2_CONV, M2_CONV_DIM), const2),
            pl.BlockSpec((1, M2_CONV_DIM), const2),
            pl.BlockSpec((1, LANES), const2),
            pl.BlockSpec((1, LANES), const2),
            pl.BlockSpec((1, M2_INNER), const2),
            pl.BlockSpec((1, M2_INNER), const2),
            pl.BlockSpec((LANES, M2_INNER), const2),
        ],
        out_specs=pl.BlockSpec((lc, M2_INNER), lambda b, i: (b * nblk + i, 0)),
        out_shape=jax.ShapeDtypeStruct((t, M2_INNER), BF16),
        scratch_shapes=[
            pltpu.VMEM((CONV_HALO, half), F32), pltpu.VMEM((CONV_HALO, half), F32),
            pltpu.VMEM((lc + CONV_HALO, half), F32), pltpu.VMEM((lc + CONV_HALO, half), F32),
            pltpu.VMEM((M2_GROUPS, M2_STATE, M2_GROUP_WIDTH), F32),
            pltpu.VMEM((lc, M2_INNER), F32),
        ],
        compiler_params=_cparams(("arbitrary", "arbitrary")),
        name="ssd_mixer",
    )(proj, proj, proj, proj, conv_w, conv_b.reshape(1, M2_CONV_DIM), dtb, alog, dx,
      norm_g.reshape(1, M2_INNER), expand)


def kernel(x, c, ada_w, ada_b, norm_g, mix_w_in, pool_w, pool_scale, s5_lam_re, s5_lam_im, s5_log_step, s5_b_re, s5_b_im, s5_c_re, s5_c_im, s5_d, s5_w_glu, mix_w_out, ffn_w_gate, ffn_w_up, ffn_w_down, m2_w_in, m2_conv_w, m2_conv_b, m2_dt_bias, m2_a_log, m2_d, m2_norm_g, m2_w_out, moe_w_router, moe_b_router, moe_w_gate, moe_w_up, moe_w_down):
    bsz, seq, d = x.shape
    t = bsz * seq
    xt = x.reshape(t, d)

    mod = _ada_modulation(c, ada_w, ada_b)

    def mod_vecs(layer):
        return [mod[layer, :, k * d:(k + 1) * d].reshape(bsz, 1, d) for k in range(6)]

    def gvec(layer, k):
        return norm_g[layer, k].reshape(1, d)

    sh_m, sc_m, g_m, sh_f, sc_f, g_f = mod_vecs(0)
    u = _norm_matmul(xt, gvec(0, 0), sc_m, sh_m, mix_w_in[0].astype(BF16),
                     tm=1024, tn=1024, out_dtype=F32, name="mix_in_proj")
    ycat = _pool_s5_mixer(u, bsz, pool_w[0], pool_scale[0], s5_lam_re[0], s5_lam_im[0],
                          s5_log_step[0], s5_b_re[0], s5_b_im[0], s5_c_re[0], s5_c_im[0],
                          s5_d[0], s5_w_glu[0], tt=256)
    xt = _matmul_post(ycat, mix_w_out[0].astype(BF16), xt, gvec(0, 1), g_m,
                      tm=1024, name="mix_out_proj")
    xt = _ffn_sublayer(xt, gvec(0, 2), sc_f, sh_f, ffn_w_gate[0].astype(BF16),
                       ffn_w_up[0].astype(BF16), ffn_w_down[0].astype(BF16), gvec(0, 3), g_f,
                       tm=1024, th=1408)

    sh_m, sc_m, g_m, sh_f, sc_f, g_f = mod_vecs(1)
    proj_pad = 6272
    w_in = jnp.zeros((d, proj_pad), BF16).at[:, :M2_PROJ].set(m2_w_in[0].astype(BF16))
    proj = _norm_matmul(xt, gvec(1, 0), sc_m, sh_m, w_in,
                        tm=1024, tn=896, out_dtype=BF16, name="m2_in_proj")
    y = _ssd_mixer(proj, bsz, m2_conv_w[0], m2_conv_b[0], m2_dt_bias[0], m2_a_log[0], m2_d[0],
                   m2_norm_g[0], lc=128)
    xt = _matmul_post(y, m2_w_out[0].astype(BF16), xt, gvec(1, 1), g_m, tm=1024,
                      name="m2_out_proj")
    xt = _moe_sublayer(xt, gvec(1, 2), sc_f, sh_f, moe_w_router[0], moe_b_router[0],
                       moe_w_gate[0].astype(BF16), moe_w_up[0].astype(BF16),
                       moe_w_down[0].astype(BF16), gvec(1, 3), g_f)
    return xt.reshape(bsz, seq, d)
```

```python
import functools
import math

import jax
import jax.numpy as jnp
from jax import lax
from jax.experimental import pallas as pl
from jax.experimental.pallas import tpu as pltpu

F32 = jnp.float32
BF16 = jnp.bfloat16

D_MODEL = 1024
EPS = 1e-6
POOL_WIDTH = 512
POOL_WINDOWS = (2, 4, 8, 16)
POOL_GROUP = 128
S5_WIDTH = 512
S5_GROUP = 16
S5_GROUPS = 32
S5_STATE = 64
M2_INNER = 2048
M2_HEADDIM = 64
M2_HEADS = 32
M2_GROUPS = 8
M2_HPG = 4
M2_STATE = 128
M2_CONV = 4
M2_BC = 1024
M2_CONV_DIM = 4096
M2_PROJ = 6176
FFN_HIDDEN = 2816
N_EXPERTS = 8
EXPERT_HIDDEN = 3584

LANES = 128
SUBLANES = 8
VMEM_LIMIT_BYTES = 56 * 1024 * 1024

NEG_BIG = -1e30


def _cparams(sem):
    return pltpu.CompilerParams(dimension_semantics=sem, vmem_limit_bytes=VMEM_LIMIT_BYTES)


def _modulated_norm(x, g, scale, shift):
    ms = jnp.mean(x * x, axis=-1, keepdims=True)
    return (x * lax.rsqrt(ms + EPS) * g) * (1.0 + scale) + shift


def _post_norm_residual(x, y, ng, gate):
    ms = jnp.mean(y * y, axis=-1, keepdims=True)
    return x + gate * (y * lax.rsqrt(ms + EPS) * ng)


def _ada_kernel(c_ref, w_ref, b_ref, o_ref):
    c = c_ref[...]
    a = c * jax.nn.sigmoid(c)
    o_ref[0] = jnp.dot(a.astype(BF16), w_ref[0].astype(BF16),
                       preferred_element_type=F32) + b_ref[0]


def _ada_modulation(c, ada_w, ada_b):
    depth, d, n = ada_w.shape
    b = c.shape[0]
    c_pad = jnp.zeros((SUBLANES, d), F32).at[:b].set(c)
    tn = 1024
    out = pl.pallas_call(
        _ada_kernel,
        grid=(depth, n // tn),
        in_specs=[
            pl.BlockSpec((SUBLANES, d), lambda l, j: (0, 0)),
            pl.BlockSpec((1, d, tn), lambda l, j: (l, 0, j)),
            pl.BlockSpec((1, 1, tn), lambda l, j: (l, 0, j)),
        ],
        out_specs=pl.BlockSpec((1, SUBLANES, tn), lambda l, j: (l, 0, j)),
        out_shape=jax.ShapeDtypeStruct((depth, SUBLANES, n), F32),
        compiler_params=_cparams(("arbitrary", "arbitrary")),
        name="ada_modulation",
    )(c_pad, ada_w, ada_b.reshape(depth, 1, n))
    return out[:, :b]


def _norm_mm_kernel(x_ref, g_ref, sc_ref, sh_ref, w_ref, o_ref, h_scr):
    @pl.when(pl.program_id(1) == 0)
    def _():
        h = _modulated_norm(x_ref[...], g_ref[...], sc_ref[0], sh_ref[0])
        h_scr[...] = h.astype(BF16)

    o_ref[...] = jnp.dot(h_scr[...], w_ref[...], preferred_element_type=F32).astype(o_ref.dtype)


def _norm_matmul(x, g, scale, shift, w, *, tm, tn, out_dtype, name):
    t, d = x.shape
    n = w.shape[1]
    tiles_per_batch = t // scale.shape[0] // tm
    return pl.pallas_call(
        _norm_mm_kernel,
        grid=(t // tm, n // tn),
        in_specs=[
            pl.BlockSpec((tm, d), lambda i, j: (i, 0)),
            pl.BlockSpec((1, d), lambda i, j: (0, 0)),
            pl.BlockSpec((1, 1, d), lambda i, j: (i // tiles_per_batch, 0, 0)),
            pl.BlockSpec((1, 1, d), lambda i, j: (i // tiles_per_batch, 0, 0)),
            pl.BlockSpec((d, tn), lambda i, j: (0, j)),
        ],
        out_specs=pl.BlockSpec((tm, tn), lambda i, j: (i, j)),
        out_shape=jax.ShapeDtypeStruct((t, n), out_dtype),
        scratch_shapes=[pltpu.VMEM((tm, d), BF16)],
        compiler_params=_cparams(("arbitrary", "arbitrary")),
        name=name,
    )(x, g, scale, shift, w)


def _mm_post_kernel(y_ref, w_ref, x_ref, ng_ref, gate_ref, o_ref):
    y = jnp.dot(y_ref[...], w_ref[...], preferred_element_type=F32)
    o_ref[...] = _post_norm_residual(x_ref[...], y, ng_ref[...], gate_ref[0])


def _matmul_post(y, w, x, ng, gate, *, tm, name):
    t, k = y.shape
    d = w.shape[1]
    tiles_per_batch = t // gate.shape[0] // tm
    return pl.pallas_call(
        _mm_post_kernel,
        grid=(t // tm,),
        in_specs=[
            pl.BlockSpec((tm, k), lambda i: (i, 0)),
            pl.BlockSpec((k, d), lambda i: (0, 0)),
            pl.BlockSpec((tm, d), lambda i: (i, 0)),
            pl.BlockSpec((1, d), lambda i: (0, 0)),
            pl.BlockSpec((1, 1, d), lambda i: (i // tiles_per_batch, 0, 0)),
        ],
        out_specs=pl.BlockSpec((tm, d), lambda i: (i, 0)),
        out_shape=jax.ShapeDtypeStruct((t, d), F32),
        compiler_params=_cparams(("arbitrary",)),
        name=name,
    )(y, w, x, ng, gate)


def _ffn_kernel(x_ref, g_ref, sc_ref, sh_ref, wg_ref, wu_ref, wd_ref, ng_ref, gate_ref,
                o_ref, h_scr, acc_scr):
    j = pl.program_id(1)

    @pl.when(j == 0)
    def _():
        h = _modulated_norm(x_ref[...], g_ref[...], sc_ref[0], sh_ref[0])
        h_scr[...] = h.astype(BF16)
        acc_scr[...] = jnp.zeros_like(acc_scr)

    h = h_scr[...]
    gt = jnp.dot(h, wg_ref[...], preferred_element_type=F32)
    up = jnp.dot(h, wu_ref[...], preferred_element_type=F32)
    act = (gt * jax.nn.sigmoid(gt) * up).astype(BF16)
    acc_scr[...] += jnp.dot(act, wd_ref[...], preferred_element_type=F32)

    @pl.when(j == pl.num_programs(1) - 1)
    def _():
        o_ref[...] = _post_norm_residual(x_ref[...], acc_scr[...], ng_ref[...], gate_ref[0])


def _ffn_sublayer(x, g, scale, shift, wg, wu, wd, ng, gate, *, tm, th):
    t, d = x.shape
    hid = wg.shape[1]
    tiles_per_batch = t // scale.shape[0] // tm
    bvec = pl.BlockSpec((1, 1, d), lambda i, j: (i // tiles_per_batch, 0, 0))
    return pl.pallas_call(
        _ffn_kernel,
        grid=(t // tm, hid // th),
        in_specs=[
            pl.BlockSpec((tm, d), lambda i, j: (i, 0)),
            pl.BlockSpec((1, d), lambda i, j: (0, 0)),
            bvec, bvec,
            pl.BlockSpec((d, th), lambda i, j: (0, j)),
            pl.BlockSpec((d, th), lambda i, j: (0, j)),
            pl.BlockSpec((th, d), lambda i, j: (j, 0)),
            pl.BlockSpec((1, d), lambda i, j: (0, 0)),
            bvec,
        ],
        out_specs=pl.BlockSpec((tm, d), lambda i, j: (i, 0)),
        out_shape=jax.ShapeDtypeStruct((t, d), F32),
        scratch_shapes=[pltpu.VMEM((tm, d), BF16), pltpu.VMEM((tm, d), F32)],
        compiler_params=_cparams(("arbitrary", "arbitrary")),
        name="ffn_sublayer",
    )(x, g, scale, shift, wg, wu, wd, ng, gate)


def _router_kernel(x_ref, g_ref, sc_ref, sh_ref, wr_ref, br_ref, h_ref, meta_ref, cnt_ref,
                   carry_scr):
    tm = x_ref.shape[0]

    @pl.when(pl.program_id(0) == 0)
    def _():
        carry_scr[...] = jnp.zeros_like(carry_scr)

    h = _modulated_norm(x_ref[...], g_ref[...], sc_ref[0], sh_ref[0])
    h_ref[...] = h

    logits = jnp.dot(h, wr_ref[...], preferred_element_type=F32,
                     precision=lax.Precision.HIGHEST) + br_ref[...]
    lane = lax.broadcasted_iota(jnp.int32, (tm, LANES), 1).astype(F32)
    m1 = jnp.max(logits, axis=-1, keepdims=True)
    i1 = jnp.min(jnp.where(logits == m1, lane, float(LANES)), axis=-1, keepdims=True)
    oh1 = lane == i1
    rest = jnp.where(oh1, NEG_BIG * 2.0, logits)
    m2 = jnp.max(rest, axis=-1, keepdims=True)
    i2 = jnp.min(jnp.where(rest == m2, lane, float(LANES)), axis=-1, keepdims=True)
    oh2 = lane == i2
    e = jnp.exp(m2 - m1)
    w1 = 1.0 / (1.0 + e)
    w2 = e / (1.0 + e)

    picks = jnp.where(oh1, 1.0, 0.0) + jnp.where(oh2, 1.0, 0.0)
    row = lax.broadcasted_iota(jnp.int32, (tm, tm), 0)
    col = lax.broadcasted_iota(jnp.int32, (tm, tm), 1)
    lower = jnp.where(col < row, 1.0, 0.0).astype(BF16)
    before = jnp.dot(lower, picks.astype(BF16), preferred_element_type=F32) + carry_scr[...]
    r1 = jnp.sum(jnp.where(oh1, before, 0.0), axis=-1, keepdims=True)
    r2 = jnp.sum(jnp.where(oh2, before, 0.0), axis=-1, keepdims=True)
    carry_scr[...] += jnp.sum(picks, axis=0, keepdims=True)
    cnt_ref[...] = carry_scr[...]

    meta = jnp.where(lane == 0.0, i1, 0.0)
    meta = jnp.where(lane == 1.0, i2, meta)
    meta = jnp.where(lane == 2.0, w1, meta)
    meta = jnp.where(lane == 3.0, w2, meta)
    meta = jnp.where(lane == 4.0, r1, meta)
    meta = jnp.where(lane == 5.0, r2, meta)
    meta_ref[...] = meta


def _router(x, g, scale, shift, w_router, b_router, *, tm):
    t, d = x.shape
    tiles_per_batch = t // scale.shape[0] // tm
    wr = jnp.zeros((d, LANES), F32).at[:, :N_EXPERTS].set(w_router)
    br = jnp.full((1, LANES), NEG_BIG, F32).at[0, :N_EXPERTS].set(b_router)
    bvec = pl.BlockSpec((1, 1, d), lambda i: (i // tiles_per_batch, 0, 0))
    return pl.pallas_call(
        _router_kernel,
        grid=(t // tm,),
        in_specs=[
            pl.BlockSpec((tm, d), lambda i: (i, 0)),
            pl.BlockSpec((1, d), lambda i: (0, 0)),
            bvec, bvec,
            pl.BlockSpec((d, LANES), lambda i: (0, 0)),
            pl.BlockSpec((1, LANES), lambda i: (0, 0)),
        ],
        out_specs=[
            pl.BlockSpec((tm, d), lambda i: (i, 0)),
            pl.BlockSpec((tm, LANES), lambda i: (i, 0)),
            pl.BlockSpec((1, LANES), lambda i: (0, 0)),
        ],
        out_shape=[
            jax.ShapeDtypeStruct((t, d), F32),
            jax.ShapeDtypeStruct((t, LANES), F32),
            jax.ShapeDtypeStruct((1, LANES), F32),
        ],
        scratch_shapes=[pltpu.VMEM((1, LANES), F32)],
        compiler_params=_cparams(("arbitrary",)),
        name="moe_router",
    )(x, g, scale, shift, wr, br)


ROW_DMA_UNROLL = 8


def _dispatch_kernel(pos_ref, h_ref, xs_in_ref, xs_ref, sem):
    del xs_in_ref
    ts = h_ref.shape[0]

    def issue(r, carry):
        src = h_ref.at[pl.ds(r, 1), :]
        pltpu.make_async_copy(src, xs_ref.at[pl.ds(pos_ref[0, 0, r], 1), :], sem).start()
        pltpu.make_async_copy(src, xs_ref.at[pl.ds(pos_ref[0, 0, ts + r], 1), :], sem).start()
        return carry

    lax.fori_loop(0, ts, issue, 0, unroll=ROW_DMA_UNROLL)
    for _ in range(2):
        pltpu.make_async_copy(h_ref, xs_ref.at[pl.ds(0, ts), :], sem).wait()


def _dispatch(h, pos, n_slots, *, ts):
    t, d = h.shape
    xs0 = jnp.zeros((n_slots, d), F32)
    return pl.pallas_call(
        _dispatch_kernel,
        grid=(t // ts,),
        in_specs=[
            pl.BlockSpec((1, 1, 2 * ts), lambda i: (i, 0, 0), memory_space=pltpu.SMEM),
            pl.BlockSpec((ts, d), lambda i: (i, 0)),
            pl.BlockSpec(memory_space=pl.ANY),
        ],
        out_specs=pl.BlockSpec(memory_space=pl.ANY),
        out_shape=jax.ShapeDtypeStruct((n_slots, d), F32),
        scratch_shapes=[pltpu.SemaphoreType.DMA(())],
        input_output_aliases={2: 0},
        compiler_params=_cparams(("arbitrary",)),
        name="moe_dispatch",
    )(pos, h, xs0)


def _expert_kernel(te_ref, nu_ref, xs_ref, wg_ref, wu_ref, wd_ref, ys_ref, xb_scr, acc_scr):
    i = pl.program_id(0)
    j = pl.program_id(1)
    last = pl.num_programs(1) - 1
    used = i < nu_ref[0]

    @pl.when(used)
    def _():
        @pl.when(j == 0)
        def _():
            xb_scr[...] = xs_ref[...].astype(BF16)
            acc_scr[...] = jnp.zeros_like(acc_scr)

        x = xb_scr[...]
        gt = jnp.dot(x, wg_ref[0], preferred_element_type=F32)
        up = jnp.dot(x, wu_ref[0], preferred_element_type=F32)
        act = (gt * jax.nn.sigmoid(gt) * up).astype(BF16)
        acc_scr[...] += jnp.dot(act, wd_ref[0], preferred_element_type=F32)

        @pl.when(j == last)
        def _():
            ys_ref[...] = acc_scr[...]

    @pl.when(jnp.logical_and(jnp.logical_not(used), j == last))
    def _():
        ys_ref[...] = jnp.zeros_like(ys_ref)


def _experts(xs, tile_expert, n_used, wg, wu, wd, *, tm, th):
    n_slots = xs.shape[0]
    d = D_MODEL
    hid = wg.shape[2]
    nt = n_slots // tm

    def hidden_step(i, j, nu):
        return jnp.where(i < nu[0], j, 0)

    grid_spec = pltpu.PrefetchScalarGridSpec(
        num_scalar_prefetch=2,
        grid=(nt, hid // th),
        in_specs=[
            pl.BlockSpec((tm, d), lambda i, j, te, nu: (i, 0)),
            pl.BlockSpec((1, d, th), lambda i, j, te, nu: (te[i], 0, hidden_step(i, j, nu))),
            pl.BlockSpec((1, d, th), lambda i, j, te, nu: (te[i], 0, hidden_step(i, j, nu))),
            pl.BlockSpec((1, th, d), lambda i, j, te, nu: (te[i], hidden_step(i, j, nu), 0)),
        ],
        out_specs=pl.BlockSpec((tm, d), lambda i, j, te, nu: (i, 0)),
        scratch_shapes=[pltpu.VMEM((tm, d), BF16), pltpu.VMEM((tm, d), F32)],
    )
    return pl.pallas_call(
        _expert_kernel,
        grid_spec=grid_spec,
        out_shape=jax.ShapeDtypeStruct((n_slots, d), F32),
        compiler_params=_cparams(("arbitrary", "arbitrary")),
        name="moe_experts",
    )(tile_expert, n_used, xs, wg, wu, wd)


def _combine_kernel(pos_ref, ys_ref, meta_ref, x_ref, ng_ref, gate_ref, o_ref, buf, sem):
    tc = x_ref.shape[0]

    def issue(r, carry):
        pltpu.make_async_copy(ys_ref.at[pl.ds(pos_ref[0, 0, r], 1), :],
                              buf.at[pl.ds(r, 1), :], sem).start()
        pltpu.make_async_copy(ys_ref.at[pl.ds(pos_ref[0, 0, tc + r], 1), :],
                              buf.at[pl.ds(tc + r, 1), :], sem).start()
        return carry

    lax.fori_loop(0, tc, issue, 0, unroll=ROW_DMA_UNROLL)
    pltpu.make_async_copy(ys_ref.at[pl.ds(0, 2 * tc), :], buf, sem).wait()

    meta = meta_ref[...]
    w1 = meta[:, 2:3]
    w2 = meta[:, 3:4]
    y = w1 * buf[0:tc, :] + w2 * buf[tc:2 * tc, :]
    o_ref[...] = _post_norm_residual(x_ref[...], y, ng_ref[...], gate_ref[0])


def _combine(pos, ys, meta, x, ng, gate, *, tc):
    t, d = x.shape
    tiles_per_batch = t // gate.shape[0] // tc
    return pl.pallas_call(
        _combine_kernel,
        grid=(t // tc,),
        in_specs=[
            pl.BlockSpec((1, 1, 2 * tc), lambda i: (i, 0, 0), memory_space=pltpu.SMEM),
            pl.BlockSpec(memory_space=pl.ANY),
            pl.BlockSpec((tc, LANES), lambda i: (i, 0)),
            pl.BlockSpec((tc, d), lambda i: (i, 0)),
            pl.BlockSpec((1, d), lambda i: (0, 0)),
            pl.BlockSpec((1, 1, d), lambda i: (i // tiles_per_batch, 0, 0)),
        ],
        out_specs=pl.BlockSpec((tc, d), lambda i: (i, 0)),
        out_shape=jax.ShapeDtypeStruct((t, d), F32),
        scratch_shapes=[pltpu.VMEM((2 * tc, d), F32), pltpu.SemaphoreType.DMA(())],
        compiler_params=_cparams(("arbitrary",)),
        name="moe_combine",
    )(pos, ys, meta, x, ng, gate)


def _moe_sublayer(x, g, scale, shift, w_router, b_router, wg, wu, wd, ng, gate):
    t = x.shape[0]
    tm_r, ts, tm_e, th, tc = 512, 256, 512, 896, 256
    h, meta, counts = _router(x, g, scale, shift, w_router, b_router, tm=tm_r)

    cnt = counts[0, :N_EXPERTS].astype(jnp.int32)
    padded = ((cnt + tm_e - 1) // tm_e) * tm_e
    ends = jnp.cumsum(padded)
    offs = ends - padded
    i1 = meta[:, 0].astype(jnp.int32)
    i2 = meta[:, 1].astype(jnp.int32)
    experts = jnp.arange(N_EXPERTS, dtype=jnp.int32)
    off1 = jnp.sum(jnp.where(i1[:, None] == experts[None, :], offs[None, :], 0), axis=1)
    off2 = jnp.sum(jnp.where(i2[:, None] == experts[None, :], offs[None, :], 0), axis=1)
    pos1 = off1 + meta[:, 4].astype(jnp.int32)
    pos2 = off2 + meta[:, 5].astype(jnp.int32)
    nt = 2 * t // tm_e + N_EXPERTS
    starts = jnp.arange(nt, dtype=jnp.int32) * tm_e
    tile_expert = jnp.minimum(jnp.sum(starts[:, None] >= ends[None, :], axis=1), N_EXPERTS - 1)
    tile_expert = tile_expert.astype(jnp.int32)
    n_used = (ends[-1:] // tm_e).astype(jnp.int32)

    def tiled_pos(tile):
        return jnp.concatenate([pos1.reshape(t // tile, 1, tile), pos2.reshape(t // tile, 1, tile)],
                               axis=-1)

    xs = _dispatch(h, tiled_pos(ts), nt * tm_e, ts=ts)
    ys = _experts(xs, tile_expert, n_used, wg, wu, wd, tm=tm_e, th=th)
    return _combine(tiled_pos(tc), ys, meta, x, ng, gate, tc=tc)


def _s5_disc_kernel(lam_re_ref, lam_im_ref, step_ref, pow_ref, keep_ref, b_re_ref, b_im_ref,
                    apr_ref, api_ref, bbr_ref, bbi_ref):
    lr = jnp.minimum(lam_re_ref[...], -1e-4)
    li = lam_im_ref[...]
    step = step_ref[...]
    m = pow_ref[...]
    mag = jnp.exp(lr * step * m) * keep_ref[...]
    ang = li * step * m
    apr_ref[...] = mag * jnp.cos(ang)
    api_ref[...] = mag * jnp.sin(ang)
    mag1 = jnp.exp(lr * step)
    ar = mag1 * jnp.cos(li * step)
    ai = mag1 * jnp.sin(li * step)
    inv = 1.0 / (lr * lr + li * li)
    fr = ((ar - 1.0) * lr + ai * li) * inv
    fi = (ai * lr - (ar - 1.0) * li) * inv
    br = b_re_ref[...]
    bi = b_im_ref[...]
    bbr_ref[...] = fr * br - fi * bi
    bbi_ref[...] = fr * bi + fi * br


S5_LOCAL_STEPS = 3
S5_ROW_POW0 = S5_LOCAL_STEPS * SUBLANES
S5_TILE_POW0 = S5_ROW_POW0 + SUBLANES


def _s5_power_rows(n_tile_steps):
    exps, keep = [], []
    for k in range(S5_LOCAL_STEPS):
        for tau in range(SUBLANES):
            exps.append(float(1 << k))
            keep.append(1.0 if tau >= (1 << k) else 0.0)
    for tau in range(SUBLANES):
        exps.append(float(tau + 1))
        keep.append(1.0)
    for k in range(n_tile_steps):
        exps.append(float(SUBLANES << k))
        keep.append(1.0)
    while len(exps) % SUBLANES:
        exps.append(0.0)
        keep.append(0.0)
    return exps, keep


def _s5_discretise(lam_re, lam_im, log_step, b_re, b_im, n_tile_steps):
    gp = S5_GROUPS * S5_STATE
    step = jnp.repeat(jnp.exp(log_step), S5_STATE).reshape(1, gp)
    exps, keep = _s5_power_rows(n_tile_steps)
    rows = len(exps)
    b_re_t = jnp.transpose(b_re, (2, 0, 1)).reshape(S5_GROUP, gp)
    b_im_t = jnp.transpose(b_im, (2, 0, 1)).reshape(S5_GROUP, gp)
    return pl.pallas_call(
        _s5_disc_kernel,
        out_shape=[jax.ShapeDtypeStruct((rows, gp), F32), jax.ShapeDtypeStruct((rows, gp), F32),
                   jax.ShapeDtypeStruct((S5_GROUP, gp), F32), jax.ShapeDtypeStruct((S5_GROUP, gp), F32)],
        name="s5_discretise",
    )(lam_re.reshape(1, gp), lam_im.reshape(1, gp), step,
      jnp.asarray(exps, F32).reshape(rows, 1), jnp.asarray(keep, F32).reshape(rows, 1),
      b_re_t, b_im_t)


POOL_HALO = 16
S5_HALF = 256
S5_HALF_STATES = 1024


def _complex_axpy(xr, xi, cr, ci, sr, si):
    return xr + (cr * sr - ci * si), xi + (cr * si + ci * sr)


def _mix0_kernel(u_ref, pw_ref, ps_ref, bw_ref, cw_ref, apr_ref, api_ref, dsk_ref, wglu_ref,
                 o_ref, halo_scr, ext_scr, sr_scr, si_scr, xr_scr, xi_scr):
    tt = u_ref.shape[0]
    n_states = sr_scr.shape[1]
    blk = pl.program_id(1)

    @pl.when(blk == 0)
    def _():
        halo_scr[...] = jnp.zeros_like(halo_scr)
        sr_scr[...] = jnp.zeros_like(sr_scr)
        si_scr[...] = jnp.zeros_like(si_scr)

    up = u_ref[:, :POOL_WIDTH]
    ext_scr[0:POOL_HALO, :] = halo_scr[...]
    ext_scr[POOL_HALO:, :] = up
    halo_scr[...] = up[tt - POOL_HALO:, :]
    ext = ext_scr[...].astype(BF16)
    row = lax.broadcasted_iota(jnp.int32, (tt, tt + POOL_HALO), 0)
    col = lax.broadcasted_iota(jnp.int32, (tt, tt + POOL_HALO), 1)
    lag = row + POOL_HALO - col
    t_glob = (blk * tt + row + 1).astype(F32)
    pooled_out = []
    for gi, win in enumerate(POOL_WINDOWS):
        inv_count = 1.0 / jnp.minimum(t_glob, float(win))
        band = jnp.where((lag >= 0) & (lag < win), inv_count, 0.0) - jnp.where(lag == 0, 1.0, 0.0)
        pooled = jnp.dot(band.astype(BF16), ext[:, gi * POOL_GROUP:(gi + 1) * POOL_GROUP],
                         preferred_element_type=F32)
        pooled_out.append(jnp.dot(pooled.astype(BF16), pw_ref[gi], preferred_element_type=F32))
    y_pool = jnp.concatenate(pooled_out, axis=-1) * ps_ref[...]
    o_ref[:, :POOL_WIDTH] = y_pool.astype(o_ref.dtype)

    us = u_ref[:, POOL_WIDTH:]
    usb = us.astype(BF16)
    bu = [jnp.dot(usb[:, hf * S5_HALF:(hf + 1) * S5_HALF], bw_ref[hf], preferred_element_type=F32)
          for hf in range(2)]
    nt = tt // SUBLANES
    xr = jnp.concatenate([b[:, :S5_HALF_STATES] for b in bu], axis=-1).reshape(nt, SUBLANES, n_states)
    xi = jnp.concatenate([b[:, S5_HALF_STATES:] for b in bu], axis=-1).reshape(nt, SUBLANES, n_states)
    for k in range(S5_LOCAL_STEPS):
        cr = apr_ref[k * SUBLANES:(k + 1) * SUBLANES, :][None]
        ci = api_ref[k * SUBLANES:(k + 1) * SUBLANES, :][None]
        xr, xi = _complex_axpy(xr, xi, cr, ci, pltpu.roll(xr, 1 << k, 1), pltpu.roll(xi, 1 << k, 1))

    xr2 = xr.reshape(tt, n_states)
    xi2 = xi.reshape(tt, n_states)
    n_cb = n_states // LANES
    for cb in range(n_cb):
        xr_scr[cb] = xr2[:, cb * LANES:(cb + 1) * LANES]
        xi_scr[cb] = xi2[:, cb * LANES:(cb + 1) * LANES]
    tile_end = pl.ds(SUBLANES - 1, nt, stride=SUBLANES)
    er = jnp.concatenate([xr_scr[cb, tile_end, :] for cb in range(n_cb)], axis=-1)
    ei = jnp.concatenate([xi_scr[cb, tile_end, :] for cb in range(n_cb)], axis=-1)
    prev_r = sr_scr[...]
    prev_i = si_scr[...]
    tile_row = lax.broadcasted_iota(jnp.int32, (nt, n_states), 0)
    a8r = apr_ref[S5_TILE_POW0:S5_TILE_POW0 + 1, :]
    a8i = api_ref[S5_TILE_POW0:S5_TILE_POW0 + 1, :]
    er = er + jnp.where(tile_row == 0, a8r * prev_r - a8i * prev_i, 0.0)
    ei = ei + jnp.where(tile_row == 0, a8r * prev_i + a8i * prev_r, 0.0)
    k = 0
    while (1 << k) < nt:
        cr = apr_ref[S5_TILE_POW0 + k:S5_TILE_POW0 + k + 1, :]
        ci = api_ref[S5_TILE_POW0 + k:S5_TILE_POW0 + k + 1, :]
        pr = jnp.where(tile_row < (1 << k), 0.0, pltpu.roll(er, 1 << k, 0))
        pi = jnp.where(tile_row < (1 << k), 0.0, pltpu.roll(ei, 1 << k, 0))
        er, ei = _complex_axpy(er, ei, cr, ci, pr, pi)
        k += 1
    sr_scr[...] = er[nt - 1:nt, :]
    si_scr[...] = ei[nt - 1:nt, :]
    in_r = jnp.where(tile_row == 0, prev_r, pltpu.roll(er, 1, 0))
    in_i = jnp.where(tile_row == 0, prev_i, pltpu.roll(ei, 1, 0))

    rep_row = lax.broadcasted_iota(jnp.int32, (tt, nt), 0)
    rep_col = lax.broadcasted_iota(jnp.int32, (tt, nt), 1)
    rep = jnp.where(rep_row // SUBLANES == rep_col, 1.0, 0.0).astype(BF16)
    entering = jnp.concatenate([in_r, in_i], axis=-1)
    ent_hi = entering.astype(BF16)
    ent_lo = (entering - ent_hi.astype(F32)).astype(BF16)
    ent = (jnp.dot(rep, ent_hi, preferred_element_type=F32)
           + jnp.dot(rep, ent_lo, preferred_element_type=F32))
    cbr = ent[:, :n_states].reshape(nt, SUBLANES, n_states)
    cbi = ent[:, n_states:].reshape(nt, SUBLANES, n_states)
    pwr = apr_ref[S5_ROW_POW0:S5_ROW_POW0 + SUBLANES, :][None]
    pwi = api_ref[S5_ROW_POW0:S5_ROW_POW0 + SUBLANES, :][None]
    xr, xi = _complex_axpy(xr, xi, pwr, pwi, cbr, cbi)
    xr2 = xr.reshape(tt, n_states).astype(BF16)
    xi2 = xi.reshape(tt, n_states).astype(BF16)

    ys = []
    for hf in range(2):
        sl = slice(hf * S5_HALF_STATES, (hf + 1) * S5_HALF_STATES)
        xcat = jnp.concatenate([xr2[:, sl], xi2[:, sl]], axis=-1)
        ys.append(jnp.dot(xcat, cw_ref[hf], preferred_element_type=F32))
    y = jnp.concatenate(ys, axis=-1) + dsk_ref[...] * us
    y = jax.nn.gelu(y)
    gate = jnp.dot(y.astype(BF16), wglu_ref[...], preferred_element_type=F32)
    o_ref[:, POOL_WIDTH:] = (y * jax.nn.sigmoid(gate)).astype(o_ref.dtype)


def _pool_s5_mixer(u, bsz, pool_w, pool_scale, lam_re, lam_im, log_step, b_re, b_im, c_re, c_im,
                   d_skip, w_glu, *, tt):
    t = u.shape[0]
    seq = t // bsz
    gp = S5_GROUPS * S5_STATE
    apr, api, bbr, bbi = _s5_discretise(lam_re, lam_im, log_step, b_re, b_im,
                                        int(math.log2(tt // SUBLANES)))
    n_pow = apr.shape[0]

    gh = S5_GROUPS // 2
    eye = jnp.eye(gh, dtype=F32)

    def in_map(bb):
        bb = bb.reshape(S5_GROUP, 2, gh, S5_STATE)
        return jnp.einsum("hxgp,gk->xghkp", bb, eye).reshape(2, gh * S5_GROUP, gh * S5_STATE)

    bw = jnp.concatenate([in_map(bbr), in_map(bbi)], axis=-1).astype(BF16)

    def out_map(cc):
        cc = cc.reshape(2, gh, S5_GROUP, S5_STATE)
        return jnp.einsum("xghp,gk->xgpkh", cc, eye).reshape(2, gh * S5_STATE, gh * S5_GROUP)

    cw = jnp.concatenate([out_map(c_re), -out_map(c_im)], axis=1).astype(BF16)

    const2 = lambda b, i: (0, 0)
    const3 = lambda b, i: (0, 0, 0)
    nblk = seq // tt
    return pl.pallas_call(
        _mix0_kernel,
        grid=(bsz, nblk),
        in_specs=[
            pl.BlockSpec((tt, D_MODEL), lambda b, i: (b * nblk + i, 0)),
            pl.BlockSpec((len(POOL_WINDOWS), POOL_GROUP, POOL_GROUP), const3),
            pl.BlockSpec((1, POOL_WIDTH), const2),
            pl.BlockSpec((2, S5_HALF, 2 * S5_HALF_STATES), const3),
            pl.BlockSpec((2, 2 * S5_HALF_STATES, S5_HALF), const3),
            pl.BlockSpec((n_pow, gp), const2),
            pl.BlockSpec((n_pow, gp), const2),
            pl.BlockSpec((1, S5_WIDTH), const2),
            pl.BlockSpec((S5_WIDTH, S5_WIDTH), const2),
        ],
        out_specs=pl.BlockSpec((tt, D_MODEL), lambda b, i: (b * nblk + i, 0)),
        out_shape=jax.ShapeDtypeStruct((t, D_MODEL), BF16),
        scratch_shapes=[
            pltpu.VMEM((POOL_HALO, POOL_WIDTH), F32),
            pltpu.VMEM((tt + POOL_HALO, POOL_WIDTH), F32),
            pltpu.VMEM((1, gp), F32), pltpu.VMEM((1, gp), F32),
            pltpu.VMEM((gp // LANES, tt, LANES), F32), pltpu.VMEM((gp // LANES, tt, LANES), F32),
        ],
        compiler_params=_cparams(("arbitrary", "arbitrary")),
        name="pool_s5_mixer",
    )(u, pool_w.astype(BF16), pool_scale.reshape(1, POOL_WIDTH), bw, cw, apr, api,
      d_skip.reshape(1, S5_WIDTH), w_glu.astype(BF16))


CONV_HALO = SUBLANES
M2_GROUP_WIDTH = M2_HPG * M2_HEADDIM


def _causal_conv_silu(x_ref, halo_scr, full_scr, w_ref, b_ref, col0):
    lc = x_ref.shape[0]
    width = x_ref.shape[1]
    x = x_ref[...].astype(F32)
    full_scr[0:CONV_HALO, :] = halo_scr[...]
    full_scr[CONV_HALO:, :] = x
    halo_scr[...] = x[lc - CONV_HALO:, :]
    cols = slice(col0, col0 + width)
    acc = b_ref[:, cols] + w_ref[M2_CONV - 1:M2_CONV, cols] * x
    for k in range(M2_CONV - 1):
        off = CONV_HALO - (M2_CONV - 1) + k
        acc = acc + w_ref[k:k + 1, cols] * full_scr[off:off + lc, :]
    return acc * jax.nn.sigmoid(acc)


def _split_dot(v, e):
    hi = v.astype(BF16)
    lo = (v - hi.astype(F32)).astype(BF16)
    return (jnp.dot(hi, e, preferred_element_type=F32) + jnp.dot(lo, e, preferred_element_type=F32))


def _ssd_kernel(z_ref, xs_ref, bc_ref, dtr_ref, cw_ref, cb_ref, dtb_ref, alog_ref, dx_ref, ng_ref,
                e_ref, o_ref, halo_x, halo_bc, full_x, full_bc, st_scr, y_scr):
    lc = z_ref.shape[0]

    @pl.when(pl.program_id(1) == 0)
    def _():
        halo_x[...] = jnp.zeros_like(halo_x)
        halo_bc[...] = jnp.zeros_like(halo_bc)
        st_scr[...] = jnp.zeros_like(st_scr)

    xs = _causal_conv_silu(xs_ref, halo_x, full_x, cw_ref, cb_ref, 0)
    bc = _causal_conv_silu(bc_ref, halo_bc, full_bc, cw_ref, cb_ref, M2_INNER)

    lane = lax.broadcasted_iota(jnp.int32, (1, LANES), 1)
    x_dt = dtr_ref[...].astype(F32) + dtb_ref[...]
    dt = jnp.maximum(x_dt, 0.0) + jnp.log(1.0 + jnp.exp(-jnp.abs(x_dt)))
    a = jnp.where(lane < M2_HEADS, -jnp.exp(alog_ref[...]), 0.0)
    da = dt * a
    row = lax.broadcasted_iota(jnp.int32, (lc, lc), 0)
    col = lax.broadcasted_iota(jnp.int32, (lc, lc), 1)
    causal = col <= row
    cs = jnp.dot(jnp.where(causal, 1.0, 0.0), da, preferred_element_type=F32,
                 precision=lax.Precision.HIGHEST)
    cs_last = cs[lc - 1:lc, :]
    ecs = jnp.exp(cs)
    w_in = dt * jnp.exp(cs_last - cs)
    cs_t = cs.T
    dt_t = dt.T
    e = e_ref[...]
    ecs_x = _split_dot(ecs, e)
    w_x = _split_dot(w_in, e)
    xsb = xs.astype(BF16)
    xw = (xs * w_x).astype(BF16)

    for g in range(M2_GROUPS):
        bm = bc[:, g * M2_STATE:(g + 1) * M2_STATE].astype(BF16)
        cm = bc[:, M2_BC + g * M2_STATE:M2_BC + (g + 1) * M2_STATE].astype(BF16)
        cbm = lax.dot_general(cm, bm, (((1,), (1,)), ((), ())), preferred_element_type=F32)
        gcols = slice(g * M2_GROUP_WIDTH, (g + 1) * M2_GROUP_WIDTH)
        st = st_scr[g]
        y_off = jnp.dot(cm, st.astype(BF16), preferred_element_type=F32) * ecs_x[:, gcols]
        y_heads = []
        for jj in range(M2_HPG):
            j = g * M2_HPG + jj
            seg = cs[:, j:j + 1] - cs_t[j:j + 1, :]
            dec = jnp.exp(jnp.where(causal, seg, NEG_BIG))
            m = (cbm * dec * dt_t[j:j + 1, :]).astype(BF16)
            y_heads.append(jnp.dot(m, xsb[:, j * M2_HEADDIM:(j + 1) * M2_HEADDIM],
                                   preferred_element_type=F32))
        y_g = jnp.concatenate(y_heads, axis=-1) + y_off
        st_scr[g] = st * ecs_x[lc - 1:lc, gcols] + lax.dot_general(
            bm, xw[:, gcols], (((0,), (0,)), ((), ())), preferred_element_type=F32)
        zg = z_ref[:, gcols].astype(F32)
        y_g = (y_g + dx_ref[:, gcols] * xs[:, gcols]) * (zg * jax.nn.sigmoid(zg))
        y_scr[:, gcols] = y_g

    y = y_scr[...]
    ms = jnp.mean(y * y, axis=-1, keepdims=True)
    o_ref[...] = (y * lax.rsqrt(ms + EPS) * ng_ref[...]).astype(o_ref.dtype)


def _ssd_mixer(proj, bsz, conv_w, conv_b, dt_bias, a_log, d_skip, norm_g, *, lc):
    t = proj.shape[0]
    seq = t // bsz
    nblk = seq // lc
    half = M2_INNER
    pad = LANES - M2_HEADS
    dtb = jnp.pad(dt_bias, (0, pad)).reshape(1, LANES)
    alog = jnp.pad(a_log, (0, pad)).reshape(1, LANES)
    dx = jnp.repeat(d_skip, M2_HEADDIM).reshape(1, M2_INNER)
    heads = jnp.arange(LANES, dtype=jnp.int32)[:, None]
    chans = jnp.arange(M2_INNER, dtype=jnp.int32)[None, :] // M2_HEADDIM
    expand = (heads == chans).astype(BF16)
    const2 = lambda b, i: (0, 0)
    return pl.pallas_call(
        _ssd_kernel,
        grid=(bsz, nblk),
        in_specs=[
            pl.BlockSpec((lc, half), lambda b, i: (b * nblk + i, 0)),
            pl.BlockSpec((lc, half), lambda b, i: (b * nblk + i, 1)),
            pl.BlockSpec((lc, half), lambda b, i: (b * nblk + i, 2)),
            pl.BlockSpec((lc, LANES), lambda b, i: (b * nblk + i, 3 * half // LANES)),
            pl.BlockSpec((M2_CONV, M2_CONV_DIM), const2),
            pl.BlockSpec((1, M2_CONV_DIM), const2),
            pl.BlockSpec((1, LANES), const2),
            pl.BlockSpec((1, LANES), const2),
            pl.BlockSpec((1, M2_INNER), const2),
            pl.BlockSpec((1, M2_INNER), const2),
            pl.BlockSpec((LANES, M2_INNER), const2),
        ],
        out_specs=pl.BlockSpec((lc, M2_INNER), lambda b, i: (b * nblk + i, 0)),
        out_shape=jax.ShapeDtypeStruct((t, M2_INNER), BF16),
        scratch_shapes=[
            pltpu.VMEM((CONV_HALO, half), F32), pltpu.VMEM((CONV_HALO, half), F32),
            pltpu.VMEM((lc + CONV_HALO, half), F32), pltpu.VMEM((lc + CONV_HALO, half), F32),
            pltpu.VMEM((M2_GROUPS, M2_STATE, M2_GROUP_WIDTH), F32),
            pltpu.VMEM((lc, M2_INNER), F32),
        ],
        compiler_params=_cparams(("arbitrary", "arbitrary")),
        name="ssd_mixer",
    )(proj, proj, proj, proj, conv_w, conv_b.reshape(1, M2_CONV_DIM), dtb, alog, dx,
      norm_g.reshape(1, M2_INNER), expand)


def kernel(x, c, ada_w, ada_b, norm_g, mix_w_in, pool_w, pool_scale, s5_lam_re, s5_lam_im, s5_log_step, s5_b_re, s5_b_im, s5_c_re, s5_c_im, s5_d, s5_w_glu, mix_w_out, ffn_w_gate, ffn_w_up, ffn_w_down, m2_w_in, m2_conv_w, m2_conv_b, m2_dt_bias, m2_a_log, m2_d, m2_norm_g, m2_w_out, moe_w_router, moe_b_router, moe_w_gate, moe_w_up, moe_w_down):
    bsz, seq, d = x.shape
    t = bsz * seq
    xt = x.reshape(t, d)

    mod = _ada_modulation(c, ada_w, ada_b)

    def mod_vecs(layer):
        return [mod[layer, :, k * d:(k + 1) * d].reshape(bsz, 1, d) for k in range(6)]

    def gvec(layer, k):
        return norm_g[layer, k].reshape(1, d)

    sh_m, sc_m, g_m, sh_f, sc_f, g_f = mod_vecs(0)
    u = _norm_matmul(xt, gvec(0, 0), sc_m, sh_m, mix_w_in[0].astype(BF16),
                     tm=1024, tn=1024, out_dtype=F32, name="mix_in_proj")
    ycat = _pool_s5_mixer(u, bsz, pool_w[0], pool_scale[0], s5_lam_re[0], s5_lam_im[0],
                          s5_log_step[0], s5_b_re[0], s5_b_im[0], s5_c_re[0], s5_c_im[0],
                          s5_d[0], s5_w_glu[0], tt=256)
    xt = _matmul_post(ycat, mix_w_out[0].astype(BF16), xt, gvec(0, 1), g_m,
                      tm=1024, name="mix_out_proj")
    xt = _ffn_sublayer(xt, gvec(0, 2), sc_f, sh_f, ffn_w_gate[0].astype(BF16),
                       ffn_w_up[0].astype(BF16), ffn_w_down[0].astype(BF16), gvec(0, 3), g_f,
                       tm=1024, th=1408)

    sh_m, sc_m, g_m, sh_f, sc_f, g_f = mod_vecs(1)
    proj_pad = 6272
    w_in = jnp.zeros((d, proj_pad), BF16).at[:, :M2_PROJ].set(m2_w_in[0].astype(BF16))
    proj = _norm_matmul(xt, gvec(1, 0), sc_m, sh_m, w_in,
                        tm=1024, tn=896, out_dtype=BF16, name="m2_in_proj")
    y = _ssd_mixer(proj, bsz, m2_conv_w[0], m2_conv_b[0], m2_dt_bias[0], m2_a_log[0], m2_d[0],
                   m2_norm_g[0], lc=128)
    xt = _matmul_post(y, m2_w_out[0].astype(BF16), xt, gvec(1, 1), g_m, tm=1024,
                      name="m2_out_proj")
    xt = _moe_sublayer(xt, gvec(1, 2), sc_f, sh_f, moe_w_router[0], moe_b_router[0],
                       moe_w_gate[0].astype(BF16), moe_w_up[0].astype(BF16),
                       moe_w_down[0].astype(BF16), gvec(1, 3), g_f)
    return xt.reshape(bsz, seq, d)
```

```python
import functools
import math

import jax
import jax.numpy as jnp
from jax import lax
from jax.experimental import pallas as pl
from jax.experimental.pallas import tpu as pltpu

F32 = jnp.float32
BF16 = jnp.bfloat16

D_MODEL = 1024
EPS = 1e-6
POOL_WIDTH = 512
POOL_WINDOWS = (2, 4, 8, 16)
POOL_GROUP = 128
S5_WIDTH = 512
S5_GROUP = 16
S5_GROUPS = 32
S5_STATE = 64
M2_INNER = 2048
M2_HEADDIM = 64
M2_HEADS = 32
M2_GROUPS = 8
M2_HPG = 4
M2_STATE = 128
M2_CONV = 4
M2_BC = 1024
M2_CONV_DIM = 4096
M2_PROJ = 6176
FFN_HIDDEN = 2816
N_EXPERTS = 8
EXPERT_HIDDEN = 3584

LANES = 128
SUBLANES = 8
VMEM_LIMIT_BYTES = 56 * 1024 * 1024

NEG_BIG = -1e30


def _cparams(sem):
    return pltpu.CompilerParams(dimension_semantics=sem, vmem_limit_bytes=VMEM_LIMIT_BYTES)


def _modulated_norm(x, g, scale, shift):
    ms = jnp.mean(x * x, axis=-1, keepdims=True)
    return (x * lax.rsqrt(ms + EPS) * g) * (1.0 + scale) + shift


def _post_norm_residual(x, y, ng, gate):
    ms = jnp.mean(y * y, axis=-1, keepdims=True)
    return x + gate * (y * lax.rsqrt(ms + EPS) * ng)


def _ada_kernel(c_ref, w_ref, b_ref, o_ref):
    c = c_ref[...]
    a = c * jax.nn.sigmoid(c)
    o_ref[0] = jnp.dot(a.astype(BF16), w_ref[0].astype(BF16),
                       preferred_element_type=F32) + b_ref[0]


def _ada_modulation(c, ada_w, ada_b):
    depth, d, n = ada_w.shape
    b = c.shape[0]
    c_pad = jnp.zeros((SUBLANES, d), F32).at[:b].set(c)
    tn = 1024
    out = pl.pallas_call(
        _ada_kernel,
        grid=(depth, n // tn),
        in_specs=[
            pl.BlockSpec((SUBLANES, d), lambda l, j: (0, 0)),
            pl.BlockSpec((1, d, tn), lambda l, j: (l, 0, j)),
            pl.BlockSpec((1, 1, tn), lambda l, j: (l, 0, j)),
        ],
        out_specs=pl.BlockSpec((1, SUBLANES, tn), lambda l, j: (l, 0, j)),
        out_shape=jax.ShapeDtypeStruct((depth, SUBLANES, n), F32),
        compiler_params=_cparams(("arbitrary", "arbitrary")),
        name="ada_modulation",
    )(c_pad, ada_w, ada_b.reshape(depth, 1, n))
    return out[:, :b]


def _norm_mm_kernel(x_ref, g_ref, sc_ref, sh_ref, w_ref, o_ref, h_scr):
    @pl.when(pl.program_id(1) == 0)
    def _():
        h = _modulated_norm(x_ref[...], g_ref[...], sc_ref[0], sh_ref[0])
        h_scr[...] = h.astype(BF16)

    o_ref[...] = jnp.dot(h_scr[...], w_ref[...], preferred_element_type=F32).astype(o_ref.dtype)


def _norm_matmul(x, g, scale, shift, w, *, tm, tn, out_dtype, name):
    t, d = x.shape
    n = w.shape[1]
    tiles_per_batch = t // scale.shape[0] // tm
    return pl.pallas_call(
        _norm_mm_kernel,
        grid=(t // tm, n // tn),
        in_specs=[
            pl.BlockSpec((tm, d), lambda i, j: (i, 0)),
            pl.BlockSpec((1, d), lambda i, j: (0, 0)),
            pl.BlockSpec((1, 1, d), lambda i, j: (i // tiles_per_batch, 0, 0)),
            pl.BlockSpec((1, 1, d), lambda i, j: (i // tiles_per_batch, 0, 0)),
            pl.BlockSpec((d, tn), lambda i, j: (0, j)),
        ],
        out_specs=pl.BlockSpec((tm, tn), lambda i, j: (i, j)),
        out_shape=jax.ShapeDtypeStruct((t, n), out_dtype),
        scratch_shapes=[pltpu.VMEM((tm, d), BF16)],
        compiler_params=_cparams(("arbitrary", "arbitrary")),
        name=name,
    )(x, g, scale, shift, w)


def _mm_post_kernel(y_ref, w_ref, x_ref, ng_ref, gate_ref, o_ref):
    y = jnp.dot(y_ref[...], w_ref[...], preferred_element_type=F32)
    o_ref[...] = _post_norm_residual(x_ref[...], y, ng_ref[...], gate_ref[0])


def _matmul_post(y, w, x, ng, gate, *, tm, name):
    t, k = y.shape
    d = w.shape[1]
    tiles_per_batch = t // gate.shape[0] // tm
    return pl.pallas_call(
        _mm_post_kernel,
        grid=(t // tm,),
        in_specs=[
            pl.BlockSpec((tm, k), lambda i: (i, 0)),
            pl.BlockSpec((k, d), lambda i: (0, 0)),
            pl.BlockSpec((tm, d), lambda i: (i, 0)),
            pl.BlockSpec((1, d), lambda i: (0, 0)),
            pl.BlockSpec((1, 1, d), lambda i: (i // tiles_per_batch, 0, 0)),
        ],
        out_specs=pl.BlockSpec((tm, d), lambda i: (i, 0)),
        out_shape=jax.ShapeDtypeStruct((t, d), F32),
        compiler_params=_cparams(("arbitrary",)),
        name=name,
    )(y, w, x, ng, gate)


def _ffn_kernel(x_ref, g_ref, sc_ref, sh_ref, wg_ref, wu_ref, wd_ref, ng_ref, gate_ref,
                o_ref, h_scr, acc_scr):
    j = pl.program_id(1)

    @pl.when(j == 0)
    def _():
        h = _modulated_norm(x_ref[...], g_ref[...], sc_ref[0], sh_ref[0])
        h_scr[...] = h.astype(BF16)
        acc_scr[...] = jnp.zeros_like(acc_scr)

    h = h_scr[...]
    gt = jnp.dot(h, wg_ref[...], preferred_element_type=F32)
    up = jnp.dot(h, wu_ref[...], preferred_element_type=F32)
    act = (gt * jax.nn.sigmoid(gt) * up).astype(BF16)
    acc_scr[...] += jnp.dot(act, wd_ref[...], preferred_element_type=F32)

    @pl.when(j == pl.num_programs(1) - 1)
    def _():
        o_ref[...] = _post_norm_residual(x_ref[...], acc_scr[...], ng_ref[...], gate_ref[0])


def _ffn_sublayer(x, g, scale, shift, wg, wu, wd, ng, gate, *, tm, th):
    t, d = x.shape
    hid = wg.shape[1]
    tiles_per_batch = t // scale.shape[0] // tm
    bvec = pl.BlockSpec((1, 1, d), lambda i, j: (i // tiles_per_batch, 0, 0))
    return pl.pallas_call(
        _ffn_kernel,
        grid=(t // tm, hid // th),
        in_specs=[
            pl.BlockSpec((tm, d), lambda i, j: (i, 0)),
            pl.BlockSpec((1, d), lambda i, j: (0, 0)),
            bvec, bvec,
            pl.BlockSpec((d, th), lambda i, j: (0, j)),
            pl.BlockSpec((d, th), lambda i, j: (0, j)),
            pl.BlockSpec((th, d), lambda i, j: (j, 0)),
            pl.BlockSpec((1, d), lambda i, j: (0, 0)),
            bvec,
        ],
        out_specs=pl.BlockSpec((tm, d), lambda i, j: (i, 0)),
        out_shape=jax.ShapeDtypeStruct((t, d), F32),
        scratch_shapes=[pltpu.VMEM((tm, d), BF16), pltpu.VMEM((tm, d), F32)],
        compiler_params=_cparams(("arbitrary", "arbitrary")),
        name="ffn_sublayer",
    )(x, g, scale, shift, wg, wu, wd, ng, gate)


def _router_kernel(x_ref, g_ref, sc_ref, sh_ref, wr_ref, br_ref, h_ref, meta_ref, cnt_ref,
                   carry_scr):
    tm = x_ref.shape[0]

    @pl.when(pl.program_id(0) == 0)
    def _():
        carry_scr[...] = jnp.zeros_like(carry_scr)

    h = _modulated_norm(x_ref[...], g_ref[...], sc_ref[0], sh_ref[0])
    h_ref[...] = h

    logits = jnp.dot(h, wr_ref[...], preferred_element_type=F32,
                     precision=lax.Precision.HIGHEST) + br_ref[...]
    lane = lax.broadcasted_iota(jnp.int32, (tm, LANES), 1).astype(F32)
    m1 = jnp.max(logits, axis=-1, keepdims=True)
    i1 = jnp.min(jnp.where(logits == m1, lane, float(LANES)), axis=-1, keepdims=True)
    oh1 = lane == i1
    rest = jnp.where(oh1, NEG_BIG * 2.0, logits)
    m2 = jnp.max(rest, axis=-1, keepdims=True)
    i2 = jnp.min(jnp.where(rest == m2, lane, float(LANES)), axis=-1, keepdims=True)
    oh2 = lane == i2
    e = jnp.exp(m2 - m1)
    w1 = 1.0 / (1.0 + e)
    w2 = e / (1.0 + e)

    picks = jnp.where(oh1, 1.0, 0.0) + jnp.where(oh2, 1.0, 0.0)
    row = lax.broadcasted_iota(jnp.int32, (tm, tm), 0)
    col = lax.broadcasted_iota(jnp.int32, (tm, tm), 1)
    lower = jnp.where(col < row, 1.0, 0.0).astype(BF16)
    before = jnp.dot(lower, picks.astype(BF16), preferred_element_type=F32) + carry_scr[...]
    r1 = jnp.sum(jnp.where(oh1, before, 0.0), axis=-1, keepdims=True)
    r2 = jnp.sum(jnp.where(oh2, before, 0.0), axis=-1, keepdims=True)
    carry_scr[...] += jnp.sum(picks, axis=0, keepdims=True)
    cnt_ref[...] = carry_scr[...]

    meta = jnp.where(lane == 0.0, i1, 0.0)
    meta = jnp.where(lane == 1.0, i2, meta)
    meta = jnp.where(lane == 2.0, w1, meta)
    meta = jnp.where(lane == 3.0, w2, meta)
    meta = jnp.where(lane == 4.0, r1, meta)
    meta = jnp.where(lane == 5.0, r2, meta)
    meta_ref[...] = meta


def _router(x, g, scale, shift, w_router, b_router, *, tm):
    t, d = x.shape
    tiles_per_batch = t // scale.shape[0] // tm
    wr = jnp.zeros((d, LANES), F32).at[:, :N_EXPERTS].set(w_router)
    br = jnp.full((1, LANES), NEG_BIG, F32).at[0, :N_EXPERTS].set(b_router)
    bvec = pl.BlockSpec((1, 1, d), lambda i: (i // tiles_per_batch, 0, 0))
    return pl.pallas_call(
        _router_kernel,
        grid=(t // tm,),
        in_specs=[
            pl.BlockSpec((tm, d), lambda i: (i, 0)),
            pl.BlockSpec((1, d), lambda i: (0, 0)),
            bvec, bvec,
            pl.BlockSpec((d, LANES), lambda i: (0, 0)),
            pl.BlockSpec((1, LANES), lambda i: (0, 0)),
        ],
        out_specs=[
            pl.BlockSpec((tm, d), lambda i: (i, 0)),
            pl.BlockSpec((tm, LANES), lambda i: (i, 0)),
            pl.BlockSpec((1, LANES), lambda i: (0, 0)),
        ],
        out_shape=[
            jax.ShapeDtypeStruct((t, d), F32),
            jax.ShapeDtypeStruct((t, LANES), F32),
            jax.ShapeDtypeStruct((1, LANES), F32),
        ],
        scratch_shapes=[pltpu.VMEM((1, LANES), F32)],
        compiler_params=_cparams(("arbitrary",)),
        name="moe_router",
    )(x, g, scale, shift, wr, br)


ROW_DMA_UNROLL = 8


def _expert_kernel(te_ref, nu_ref, code_ref, code_next_ref, h_ref, wg_ref, wu_ref, wd_ref,
                   out2_ref, xg_scr, xb_scr, acc_scr, stage_scr, gsem, ssem):
    i = pl.program_id(0)
    j = pl.program_id(1)
    last = pl.num_programs(1) - 1
    n_used = nu_ref[0]
    used = i < n_used
    par = i % 2
    tm = xb_scr.shape[0]
    token_mask = h_ref.shape[0] - 1

    def start_gather(codes, buf):
        def body(r, carry):
            tok = codes[0, 0, r] & token_mask
            pltpu.make_async_copy(h_ref.at[pl.ds(tok, 1), :], xg_scr.at[buf, pl.ds(r, 1), :],
                                  gsem.at[buf]).start()
            return carry
        lax.fori_loop(0, tm, body, 0, unroll=ROW_DMA_UNROLL)

    def wait_gather(buf):
        pltpu.make_async_copy(h_ref.at[pl.ds(0, tm), :], xg_scr.at[buf], gsem.at[buf]).wait()

    def wait_scatter(buf):
        pltpu.make_async_copy(stage_scr.at[buf], out2_ref.at[pl.ds(0, tm), :], ssem.at[buf]).wait()

    @pl.when(jnp.logical_and(i == 0, j == 0))
    def _():
        stage_scr[...] = jnp.zeros_like(stage_scr)
        dump0 = out2_ref.shape[0] - 2 * tm
        for buf in range(2):
            zero_fill = pltpu.make_async_copy(
                stage_scr.at[buf], out2_ref.at[pl.ds(dump0 + buf * tm, tm), :], ssem.at[buf])
            zero_fill.start()
            zero_fill.wait()

    @pl.when(used)
    def _():
        @pl.when(j == 0)
        def _():
            @pl.when(i == 0)
            def _():
                start_gather(code_ref, 0)

            wait_gather(par)
            xb_scr[...] = xg_scr[par].astype(BF16)
            acc_scr[...] = jnp.zeros_like(acc_scr)

            @pl.when(i + 1 < n_used)
            def _():
                start_gather(code_next_ref, 1 - par)

        x = xb_scr[...]
        gt = jnp.dot(x, wg_ref[0], preferred_element_type=F32)
        up = jnp.dot(x, wu_ref[0], preferred_element_type=F32)
        act = (gt * jax.nn.sigmoid(gt) * up).astype(BF16)
        acc_scr[...] += jnp.dot(act, wd_ref[0], preferred_element_type=F32)

        @pl.when(j == last)
        def _():
            @pl.when(i >= 2)
            def _():
                wait_scatter(par)

            stage_scr[par] = acc_scr[...]

            def body(r, carry):
                pltpu.make_async_copy(stage_scr.at[par, pl.ds(r, 1), :],
                                      out2_ref.at[pl.ds(code_ref[0, 0, r], 1), :],
                                      ssem.at[par]).start()
                return carry
            lax.fori_loop(0, tm, body, 0, unroll=ROW_DMA_UNROLL)

    @pl.when(jnp.logical_and(i == pl.num_programs(0) - 1, j == last))
    def _():
        @pl.when(n_used >= 1)
        def _():
            wait_scatter((n_used - 1) % 2)

        @pl.when(n_used >= 2)
        def _():
            wait_scatter(n_used % 2)


def _experts(h, codes, tile_expert, n_used, wg, wu, wd, *, tm, th, out_rows):
    t, d = h.shape
    hid = wg.shape[2]
    nt = codes.shape[0]

    def hidden_step(i, j, nu):
        return jnp.where(i < nu[0], j, 0)

    grid_spec = pltpu.PrefetchScalarGridSpec(
        num_scalar_prefetch=2,
        grid=(nt, hid // th),
        in_specs=[
            pl.BlockSpec((1, 1, tm), lambda i, j, te, nu: (i, 0, 0), memory_space=pltpu.SMEM),
            pl.BlockSpec((1, 1, tm), lambda i, j, te, nu: (jnp.minimum(i + 1, nt - 1), 0, 0),
                         memory_space=pltpu.SMEM),
            pl.BlockSpec(memory_space=pl.ANY),
            pl.BlockSpec((1, d, th), lambda i, j, te, nu: (te[i], 0, hidden_step(i, j, nu))),
            pl.BlockSpec((1, d, th), lambda i, j, te, nu: (te[i], 0, hidden_step(i, j, nu))),
            pl.BlockSpec((1, th, d), lambda i, j, te, nu: (te[i], hidden_step(i, j, nu), 0)),
        ],
        out_specs=pl.BlockSpec(memory_space=pl.ANY),
        scratch_shapes=[
            pltpu.VMEM((2, tm, d), F32), pltpu.VMEM((tm, d), BF16), pltpu.VMEM((tm, d), F32),
            pltpu.VMEM((2, tm, d), F32),
            pltpu.SemaphoreType.DMA((2,)), pltpu.SemaphoreType.DMA((2,)),
        ],
    )
    return pl.pallas_call(
        _expert_kernel,
        grid_spec=grid_spec,
        out_shape=jax.ShapeDtypeStruct((out_rows, d), F32),
        compiler_params=_cparams(("arbitrary", "arbitrary")),
        name="moe_experts",
    )(tile_expert, n_used, codes, codes, h, wg, wu, wd)


def _combine_kernel(y1_ref, y2_ref, meta_ref, x_ref, ng_ref, gate_ref, o_ref):
    meta = meta_ref[...]
    y = meta[:, 2:3] * y1_ref[...] + meta[:, 3:4] * y2_ref[...]
    o_ref[...] = _post_norm_residual(x_ref[...], y, ng_ref[...], gate_ref[0])


def _combine(out2, meta, x, ng, gate, *, tc):
    t, d = x.shape
    tiles_per_batch = t // gate.shape[0] // tc
    nblk = t // tc
    return pl.pallas_call(
        _combine_kernel,
        grid=(nblk,),
        in_specs=[
            pl.BlockSpec((tc, d), lambda i: (i, 0)),
            pl.BlockSpec((tc, d), lambda i: (nblk + i, 0)),
            pl.BlockSpec((tc, LANES), lambda i: (i, 0)),
            pl.BlockSpec((tc, d), lambda i: (i, 0)),
            pl.BlockSpec((1, d), lambda i: (0, 0)),
            pl.BlockSpec((1, 1, d), lambda i: (i // tiles_per_batch, 0, 0)),
        ],
        out_specs=pl.BlockSpec((tc, d), lambda i: (i, 0)),
        out_shape=jax.ShapeDtypeStruct((t, d), F32),
        compiler_params=_cparams(("arbitrary",)),
        name="moe_combine",
    )(out2, out2, meta, x, ng, gate)


def _moe_sublayer(x, g, scale, shift, w_router, b_router, wg, wu, wd, ng, gate):
    t = x.shape[0]
    tm_r, tm_e, th, tc = 512, 512, 896, 1024
    h, meta, counts = _router(x, g, scale, shift, w_router, b_router, tm=tm_r)

    cnt = counts[0, :N_EXPERTS].astype(jnp.int32)
    padded = ((cnt + tm_e - 1) // tm_e) * tm_e
    ends = jnp.cumsum(padded)
    offs = ends - padded
    i1 = meta[:, 0].astype(jnp.int32)
    i2 = meta[:, 1].astype(jnp.int32)
    experts = jnp.arange(N_EXPERTS, dtype=jnp.int32)
    off1 = jnp.sum(jnp.where(i1[:, None] == experts[None, :], offs[None, :], 0), axis=1)
    off2 = jnp.sum(jnp.where(i2[:, None] == experts[None, :], offs[None, :], 0), axis=1)
    pos1 = off1 + meta[:, 4].astype(jnp.int32)
    pos2 = off2 + meta[:, 5].astype(jnp.int32)
    nt = 2 * t // tm_e + N_EXPERTS
    starts = jnp.arange(nt, dtype=jnp.int32) * tm_e
    tile_expert = jnp.minimum(jnp.sum(starts[:, None] >= ends[None, :], axis=1), N_EXPERTS - 1)
    tile_expert = tile_expert.astype(jnp.int32)
    n_used = (ends[-1:] // tm_e).astype(jnp.int32)

    slot = jnp.arange(nt * tm_e, dtype=jnp.int32)
    dump = 2 * t + ((slot // tm_e) % 2) * tm_e + slot % tm_e
    codes = dump.at[jnp.concatenate([pos1, pos2])].set(
        jnp.arange(2 * t, dtype=jnp.int32), unique_indices=True, mode="promise_in_bounds")
    out2 = _experts(h, codes.reshape(nt, 1, tm_e), tile_expert, n_used, wg, wu, wd,
                    tm=tm_e, th=th, out_rows=2 * t + 2 * tm_e)
    return _combine(out2, meta, x, ng, gate, tc=tc)


def _s5_disc_kernel(lam_re_ref, lam_im_ref, step_ref, pow_ref, keep_ref, b_re_ref, b_im_ref,
                    apr_ref, api_ref, bbr_ref, bbi_ref):
    lr = jnp.minimum(lam_re_ref[...], -1e-4)
    li = lam_im_ref[...]
    step = step_ref[...]
    m = pow_ref[...]
    mag = jnp.exp(lr * step * m) * keep_ref[...]
    ang = li * step * m
    apr_ref[...] = mag * jnp.cos(ang)
    api_ref[...] = mag * jnp.sin(ang)
    mag1 = jnp.exp(lr * step)
    ar = mag1 * jnp.cos(li * step)
    ai = mag1 * jnp.sin(li * step)
    inv = 1.0 / (lr * lr + li * li)
    fr = ((ar - 1.0) * lr + ai * li) * inv
    fi = (ai * lr - (ar - 1.0) * li) * inv
    br = b_re_ref[...]
    bi = b_im_ref[...]
    bbr_ref[...] = fr * br - fi * bi
    bbi_ref[...] = fr * bi + fi * br


S5_LOCAL_STEPS = 3
S5_ROW_POW0 = S5_LOCAL_STEPS * SUBLANES
S5_TILE_POW0 = S5_ROW_POW0 + SUBLANES


def _s5_power_rows(n_tile_steps):
    exps, keep = [], []
    for k in range(S5_LOCAL_STEPS):
        for tau in range(SUBLANES):
            exps.append(float(1 << k))
            keep.append(1.0 if tau >= (1 << k) else 0.0)
    for tau in range(SUBLANES):
        exps.append(float(tau + 1))
        keep.append(1.0)
    for k in range(n_tile_steps):
        exps.append(float(SUBLANES << k))
        keep.append(1.0)
    while len(exps) % SUBLANES:
        exps.append(0.0)
        keep.append(0.0)
    return exps, keep


def _s5_discretise(lam_re, lam_im, log_step, b_re, b_im, n_tile_steps):
    gp = S5_GROUPS * S5_STATE
    step = jnp.repeat(jnp.exp(log_step), S5_STATE).reshape(1, gp)
    exps, keep = _s5_power_rows(n_tile_steps)
    rows = len(exps)
    b_re_t = jnp.transpose(b_re, (2, 0, 1)).reshape(S5_GROUP, gp)
    b_im_t = jnp.transpose(b_im, (2, 0, 1)).reshape(S5_GROUP, gp)
    return pl.pallas_call(
        _s5_disc_kernel,
        out_shape=[jax.ShapeDtypeStruct((rows, gp), F32), jax.ShapeDtypeStruct((rows, gp), F32),
                   jax.ShapeDtypeStruct((S5_GROUP, gp), F32), jax.ShapeDtypeStruct((S5_GROUP, gp), F32)],
        name="s5_discretise",
    )(lam_re.reshape(1, gp), lam_im.reshape(1, gp), step,
      jnp.asarray(exps, F32).reshape(rows, 1), jnp.asarray(keep, F32).reshape(rows, 1),
      b_re_t, b_im_t)


POOL_HALO = 16
S5_HALF = 256
S5_HALF_STATES = 1024


def _complex_axpy(xr, xi, cr, ci, sr, si):
    return xr + (cr * sr - ci * si), xi + (cr * si + ci * sr)


def _mix0_kernel(u_ref, pw_ref, ps_ref, bw_ref, cw_ref, apr_ref, api_ref, dsk_ref, wglu_ref,
                 o_ref, halo_scr, ext_scr, sr_scr, si_scr, xr_scr, xi_scr):
    tt = u_ref.shape[0]
    n_states = sr_scr.shape[1]
    blk = pl.program_id(1)

    @pl.when(blk == 0)
    def _():
        halo_scr[...] = jnp.zeros_like(halo_scr)
        sr_scr[...] = jnp.zeros_like(sr_scr)
        si_scr[...] = jnp.zeros_like(si_scr)

    up = u_ref[:, :POOL_WIDTH]
    ext_scr[0:POOL_HALO, :] = halo_scr[...]
    ext_scr[POOL_HALO:, :] = up
    halo_scr[...] = up[tt - POOL_HALO:, :]
    ext = ext_scr[...].astype(BF16)
    row = lax.broadcasted_iota(jnp.int32, (tt, tt + POOL_HALO), 0)
    col = lax.broadcasted_iota(jnp.int32, (tt, tt + POOL_HALO), 1)
    lag = row + POOL_HALO - col
    t_glob = (blk * tt + row + 1).astype(F32)
    pooled_out = []
    for gi, win in enumerate(POOL_WINDOWS):
        inv_count = 1.0 / jnp.minimum(t_glob, float(win))
        band = jnp.where((lag >= 0) & (lag < win), inv_count, 0.0) - jnp.where(lag == 0, 1.0, 0.0)
        pooled = jnp.dot(band.astype(BF16), ext[:, gi * POOL_GROUP:(gi + 1) * POOL_GROUP],
                         preferred_element_type=F32)
        pooled_out.append(jnp.dot(pooled.astype(BF16), pw_ref[gi], preferred_element_type=F32))
    y_pool = jnp.concatenate(pooled_out, axis=-1) * ps_ref[...]
    o_ref[:, :POOL_WIDTH] = y_pool.astype(o_ref.dtype)

    us = u_ref[:, POOL_WIDTH:]
    usb = us.astype(BF16)
    bu = [jnp.dot(usb[:, hf * S5_HALF:(hf + 1) * S5_HALF], bw_ref[hf], preferred_element_type=F32)
          for hf in range(2)]
    nt = tt // SUBLANES
    xr = jnp.concatenate([b[:, :S5_HALF_STATES] for b in bu], axis=-1).reshape(nt, SUBLANES, n_states)
    xi = jnp.concatenate([b[:, S5_HALF_STATES:] for b in bu], axis=-1).reshape(nt, SUBLANES, n_states)
    for k in range(S5_LOCAL_STEPS):
        cr = apr_ref[k * SUBLANES:(k + 1) * SUBLANES, :][None]
        ci = api_ref[k * SUBLANES:(k + 1) * SUBLANES, :][None]
        xr, xi = _complex_axpy(xr, xi, cr, ci, pltpu.roll(xr, 1 << k, 1), pltpu.roll(xi, 1 << k, 1))

    xr2 = xr.reshape(tt, n_states)
    xi2 = xi.reshape(tt, n_states)
    n_cb = n_states // LANES
    for cb in range(n_cb):
        xr_scr[cb] = xr2[:, cb * LANES:(cb + 1) * LANES]
        xi_scr[cb] = xi2[:, cb * LANES:(cb + 1) * LANES]
    tile_end = pl.ds(SUBLANES - 1, nt, stride=SUBLANES)
    er = jnp.concatenate([xr_scr[cb, tile_end, :] for cb in range(n_cb)], axis=-1)
    ei = jnp.concatenate([xi_scr[cb, tile_end, :] for cb in range(n_cb)], axis=-1)
    prev_r = sr_scr[...]
    prev_i = si_scr[...]
    tile_row = lax.broadcasted_iota(jnp.int32, (nt, n_states), 0)
    a8r = apr_ref[S5_TILE_POW0:S5_TILE_POW0 + 1, :]
    a8i = api_ref[S5_TILE_POW0:S5_TILE_POW0 + 1, :]
    er = er + jnp.where(tile_row == 0, a8r * prev_r - a8i * prev_i, 0.0)
    ei = ei + jnp.where(tile_row == 0, a8r * prev_i + a8i * prev_r, 0.0)
    k = 0
    while (1 << k) < nt:
        cr = apr_ref[S5_TILE_POW0 + k:S5_TILE_POW0 + k + 1, :]
        ci = api_ref[S5_TILE_POW0 + k:S5_TILE_POW0 + k + 1, :]
        pr = jnp.where(tile_row < (1 << k), 0.0, pltpu.roll(er, 1 << k, 0))
        pi = jnp.where(tile_row < (1 << k), 0.0, pltpu.roll(ei, 1 << k, 0))
        er, ei = _complex_axpy(er, ei, cr, ci, pr, pi)
        k += 1
    sr_scr[...] = er[nt - 1:nt, :]
    si_scr[...] = ei[nt - 1:nt, :]
    in_r = jnp.where(tile_row == 0, prev_r, pltpu.roll(er, 1, 0))
    in_i = jnp.where(tile_row == 0, prev_i, pltpu.roll(ei, 1, 0))

    rep_row = lax.broadcasted_iota(jnp.int32, (tt, nt), 0)
    rep_col = lax.broadcasted_iota(jnp.int32, (tt, nt), 1)
    rep = jnp.where(rep_row // SUBLANES == rep_col, 1.0, 0.0).astype(BF16)
    entering = jnp.concatenate([in_r, in_i], axis=-1)
    ent_hi = entering.astype(BF16)
    ent_lo = (entering - ent_hi.astype(F32)).astype(BF16)
    ent = (jnp.dot(rep, ent_hi, preferred_element_type=F32)
           + jnp.dot(rep, ent_lo, preferred_element_type=F32))
    cbr = ent[:, :n_states].reshape(nt, SUBLANES, n_states)
    cbi = ent[:, n_states:].reshape(nt, SUBLANES, n_states)
    pwr = apr_ref[S5_ROW_POW0:S5_ROW_POW0 + SUBLANES, :][None]
    pwi = api_ref[S5_ROW_POW0:S5_ROW_POW0 + SUBLANES, :][None]
    xr, xi = _complex_axpy(xr, xi, pwr, pwi, cbr, cbi)
    xr2 = xr.reshape(tt, n_states).astype(BF16)
    xi2 = xi.reshape(tt, n_states).astype(BF16)

    ys = []
    for hf in range(2):
        sl = slice(hf * S5_HALF_STATES, (hf + 1) * S5_HALF_STATES)
        xcat = jnp.concatenate([xr2[:, sl], xi2[:, sl]], axis=-1)
        ys.append(jnp.dot(xcat, cw_ref[hf], preferred_element_type=F32))
    y = jnp.concatenate(ys, axis=-1) + dsk_ref[...] * us
    y = jax.nn.gelu(y)
    gate = jnp.dot(y.astype(BF16), wglu_ref[...], preferred_element_type=F32)
    o_ref[:, POOL_WIDTH:] = (y * jax.nn.sigmoid(gate)).astype(o_ref.dtype)


def _pool_s5_mixer(u, bsz, pool_w, pool_scale, lam_re, lam_im, log_step, b_re, b_im, c_re, c_im,
                   d_skip, w_glu, *, tt):
    t = u.shape[0]
    seq = t // bsz
    gp = S5_GROUPS * S5_STATE
    apr, api, bbr, bbi = _s5_discretise(lam_re, lam_im, log_step, b_re, b_im,
                                        int(math.log2(tt // SUBLANES)))
    n_pow = apr.shape[0]

    gh = S5_GROUPS // 2
    eye = jnp.eye(gh, dtype=F32)

    def in_map(bb):
        bb = bb.reshape(S5_GROUP, 2, gh, S5_STATE)
        return jnp.einsum("hxgp,gk->xghkp", bb, eye).reshape(2, gh * S5_GROUP, gh * S5_STATE)

    bw = jnp.concatenate([in_map(bbr), in_map(bbi)], axis=-1).astype(BF16)

    def out_map(cc):
        cc = cc.reshape(2, gh, S5_GROUP, S5_STATE)
        return jnp.einsum("xghp,gk->xgpkh", cc, eye).reshape(2, gh * S5_STATE, gh * S5_GROUP)

    cw = jnp.concatenate([out_map(c_re), -out_map(c_im)], axis=1).astype(BF16)

    const2 = lambda b, i: (0, 0)
    const3 = lambda b, i: (0, 0, 0)
    nblk = seq // tt
    return pl.pallas_call(
        _mix0_kernel,
        grid=(bsz, nblk),
        in_specs=[
            pl.BlockSpec((tt, D_MODEL), lambda b, i: (b * nblk + i, 0)),
            pl.BlockSpec((len(POOL_WINDOWS), POOL_GROUP, POOL_GROUP), const3),
            pl.BlockSpec((1, POOL_WIDTH), const2),
            pl.BlockSpec((2, S5_HALF, 2 * S5_HALF_STATES), const3),
            pl.BlockSpec((2, 2 * S5_HALF_STATES, S5_HALF), const3),
            pl.BlockSpec((n_pow, gp), const2),
            pl.BlockSpec((n_pow, gp), const2),
            pl.BlockSpec((1, S5_WIDTH), const2),
            pl.BlockSpec((S5_WIDTH, S5_WIDTH), const2),
        ],
        out_specs=pl.BlockSpec((tt, D_MODEL), lambda b, i: (b * nblk + i, 0)),
        out_shape=jax.ShapeDtypeStruct((t, D_MODEL), BF16),
        scratch_shapes=[
            pltpu.VMEM((POOL_HALO, POOL_WIDTH), F32),
            pltpu.VMEM((tt + POOL_HALO, POOL_WIDTH), F32),
            pltpu.VMEM((1, gp), F32), pltpu.VMEM((1, gp), F32),
            pltpu.VMEM((gp // LANES, tt, LANES), F32), pltpu.VMEM((gp // LANES, tt, LANES), F32),
        ],
        compiler_params=_cparams(("arbitrary", "arbitrary")),
        name="pool_s5_mixer",
    )(u, pool_w.astype(BF16), pool_scale.reshape(1, POOL_WIDTH), bw, cw, apr, api,
      d_skip.reshape(1, S5_WIDTH), w_glu.astype(BF16))


CONV_HALO = SUBLANES
M2_GROUP_WIDTH = M2_HPG * M2_HEADDIM


def _causal_conv_silu(x_ref, halo_scr, full_scr, w_ref, b_ref, col0):
    lc = x_ref.shape[0]
    width = x_ref.shape[1]
    x = x_ref[...].astype(F32)
    full_scr[0:CONV_HALO, :] = halo_scr[...]
    full_scr[CONV_HALO:, :] = x
    halo_scr[...] = x[lc - CONV_HALO:, :]
    cols = slice(col0, col0 + width)
    acc = b_ref[:, cols] + w_ref[M2_CONV - 1:M2_CONV, cols] * x
    for k in range(M2_CONV - 1):
        off = CONV_HALO - (M2_CONV - 1) + k
        acc = acc + w_ref[k:k + 1, cols] * full_scr[off:off + lc, :]
    return acc * jax.nn.sigmoid(acc)


def _split_dot(v, e):
    hi = v.astype(BF16)
    lo = (v - hi.astype(F32)).astype(BF16)
    return (jnp.dot(hi, e, preferred_element_type=F32) + jnp.dot(lo, e, preferred_element_type=F32))


def _ssd_kernel(z_ref, xs_ref, bc_ref, dtr_ref, cw_ref, cb_ref, dtb_ref, alog_ref, dx_ref, ng_ref,
                e_ref, o_ref, halo_x, halo_bc, full_x, full_bc, st_scr, y_scr):
    lc = z_ref.shape[0]

    @pl.when(pl.program_id(1) == 0)
    def _():
        halo_x[...] = jnp.zeros_like(halo_x)
        halo_bc[...] = jnp.zeros_like(halo_bc)
        st_scr[...] = jnp.zeros_like(st_scr)

    xs = _causal_conv_silu(xs_ref, halo_x, full_x, cw_ref, cb_ref, 0)
    bc = _causal_conv_silu(bc_ref, halo_bc, full_bc, cw_ref, cb_ref, M2_INNER)

    lane = lax.broadcasted_iota(jnp.int32, (1, LANES), 1)
    x_dt = dtr_ref[...].astype(F32) + dtb_ref[...]
    dt = jnp.maximum(x_dt, 0.0) + jnp.log(1.0 + jnp.exp(-jnp.abs(x_dt)))
    a = jnp.where(lane < M2_HEADS, -jnp.exp(alog_ref[...]), 0.0)
    da = dt * a
    row = lax.broadcasted_iota(jnp.int32, (lc, lc), 0)
    col = lax.broadcasted_iota(jnp.int32, (lc, lc), 1)
    causal = col <= row
    cs = jnp.dot(jnp.where(causal, 1.0, 0.0), da, preferred_element_type=F32,
                 precision=lax.Precision.HIGHEST)
    cs_last = cs[lc - 1:lc, :]
    ecs = jnp.exp(cs)
    w_in = dt * jnp.exp(cs_last - cs)
    cs_t = cs.T
    dt_t = dt.T
    e = e_ref[...]
    ecs_x = _split_dot(ecs, e)
    w_x = _split_dot(w_in, e)
    xsb = xs.astype(BF16)
    xw = (xs * w_x).astype(BF16)

    for g in range(M2_GROUPS):
        bm = bc[:, g * M2_STATE:(g + 1) * M2_STATE].astype(BF16)
        cm = bc[:, M2_BC + g * M2_STATE:M2_BC + (g + 1) * M2_STATE].astype(BF16)
        cbm = lax.dot_general(cm, bm, (((1,), (1,)), ((), ())), preferred_element_type=F32)
        gcols = slice(g * M2_GROUP_WIDTH, (g + 1) * M2_GROUP_WIDTH)
        st = st_scr[g]
        y_off = jnp.dot(cm, st.astype(BF16), preferred_element_type=F32) * ecs_x[:, gcols]
        y_heads = []
        for jj in range(M2_HPG):
            j = g * M2_HPG + jj
            seg = cs[:, j:j + 1] - cs_t[j:j + 1, :]
            dec = jnp.exp(jnp.where(causal, seg, NEG_BIG))
            m = (cbm * dec * dt_t[j:j + 1, :]).astype(BF16)
            y_heads.append(jnp.dot(m, xsb[:, j * M2_HEADDIM:(j + 1) * M2_HEADDIM],
                                   preferred_element_type=F32))
        y_g = jnp.concatenate(y_heads, axis=-1) + y_off
        st_scr[g] = st * ecs_x[lc - 1:lc, gcols] + lax.dot_general(
            bm, xw[:, gcols], (((0,), (0,)), ((), ())), preferred_element_type=F32)
        zg = z_ref[:, gcols].astype(F32)
        y_g = (y_g + dx_ref[:, gcols] * xs[:, gcols]) * (zg * jax.nn.sigmoid(zg))
        y_scr[:, gcols] = y_g

    y = y_scr[...]
    ms = jnp.mean(y * y, axis=-1, keepdims=True)
    o_ref[...] = (y * lax.rsqrt(ms + EPS) * ng_ref[...]).astype(o_ref.dtype)


def _ssd_mixer(proj, bsz, conv_w, conv_b, dt_bias, a_log, d_skip, norm_g, *, lc):
    t = proj.shape[0]
    seq = t // bsz
    nblk = seq // lc
    half = M2_INNER
    pad = LANES - M2_HEADS
    dtb = jnp.pad(dt_bias, (0, pad)).reshape(1, LANES)
    alog = jnp.pad(a_log, (0, pad)).reshape(1, LANES)
    dx = jnp.repeat(d_skip, M2_HEADDIM).reshape(1, M2_INNER)
    heads = jnp.arange(LANES, dtype=jnp.int32)[:, None]
    chans = jnp.arange(M2_INNER, dtype=jnp.int32)[None, :] // M2_HEADDIM
    expand = (heads == chans).astype(BF16)
    const2 = lambda b, i: (0, 0)
    return pl.pallas_call(
        _ssd_kernel,
        grid=(bsz, nblk),
        in_specs=[
            pl.BlockSpec((lc, half), lambda b, i: (b * nblk + i, 0)),
            pl.BlockSpec((lc, half), lambda b, i: (b * nblk + i, 1)),
            pl.BlockSpec((lc, half), lambda b, i: (b * nblk + i, 2)),
            pl.BlockSpec((lc, LANES), lambda b, i: (b * nblk + i, 3 * half // LANES)),
            pl.BlockSpec((M2_CONV, M2_CONV_DIM), const2),
            pl.BlockSpec((1, M2_CONV_DIM), const2),
            pl.BlockSpec((1, LANES), const2),
            pl.BlockSpec((1, LANES), const2),
            pl.BlockSpec((1, M2_INNER), const2),
            pl.BlockSpec((1, M2_INNER), const2),
            pl.BlockSpec((LANES, M2_INNER), const2),
        ],
        out_specs=pl.BlockSpec((lc, M2_INNER), lambda b, i: (b * nblk + i, 0)),
        out_shape=jax.ShapeDtypeStruct((t, M2_INNER), BF16),
        scratch_shapes=[
            pltpu.VMEM((CONV_HALO, half), F32), pltpu.VMEM((CONV_HALO, half), F32),
            pltpu.VMEM((lc + CONV_HALO, half), F32), pltpu.VMEM((lc + CONV_HALO, half), F32),
            pltpu.VMEM((M2_GROUPS, M2_STATE, M2_GROUP_WIDTH), F32),
            pltpu.VMEM((lc, M2_INNER), F32),
        ],
        compiler_params=_cparams(("arbitrary", "arbitrary")),
        name="ssd_mixer",
    )(proj, proj, proj, proj, conv_w, conv_b.reshape(1, M2_CONV_DIM), dtb, alog, dx,
      norm_g.reshape(1, M2_INNER), expand)


def kernel(x, c, ada_w, ada_b, norm_g, mix_w_in, pool_w, pool_scale, s5_lam_re, s5_lam_im, s5_log_step, s5_b_re, s5_b_im, s5_c_re, s5_c_im, s5_d, s5_w_glu, mix_w_out, ffn_w_gate, ffn_w_up, ffn_w_down, m2_w_in, m2_conv_w, m2_conv_b, m2_dt_bias, m2_a_log, m2_d, m2_norm_g, m2_w_out, moe_w_router, moe_b_router, moe_w_gate, moe_w_up, moe_w_down):
    bsz, seq, d = x.shape
    t = bsz * seq
    xt = x.reshape(t, d)

    mod = _ada_modulation(c, ada_w, ada_b)

    def mod_vecs(layer):
        return [mod[layer, :, k * d:(k + 1) * d].reshape(bsz, 1, d) for k in range(6)]

    def gvec(layer, k):
        return norm_g[layer, k].reshape(1, d)

    sh_m, sc_m, g_m, sh_f, sc_f, g_f = mod_vecs(0)
    u = _norm_matmul(xt, gvec(0, 0), sc_m, sh_m, mix_w_in[0].astype(BF16),
                     tm=1024, tn=1024, out_dtype=F32, name="mix_in_proj")
    ycat = _pool_s5_mixer(u, bsz, pool_w[0], pool_scale[0], s5_lam_re[0], s5_lam_im[0],
                          s5_log_step[0], s5_b_re[0], s5_b_im[0], s5_c_re[0], s5_c_im[0],
                          s5_d[0], s5_w_glu[0], tt=256)
    xt = _matmul_post(ycat, mix_w_out[0].astype(BF16), xt, gvec(0, 1), g_m,
                      tm=1024, name="mix_out_proj")
    xt = _ffn_sublayer(xt, gvec(0, 2), sc_f, sh_f, ffn_w_gate[0].astype(BF16),
                       ffn_w_up[0].astype(BF16), ffn_w_down[0].astype(BF16), gvec(0, 3), g_f,
                       tm=1024, th=1408)

    sh_m, sc_m, g_m, sh_f, sc_f, g_f = mod_vecs(1)
    proj_pad = 6272
    w_in = jnp.zeros((d, proj_pad), BF16).at[:, :M2_PROJ].set(m2_w_in[0].astype(BF16))
    proj = _norm_matmul(xt, gvec(1, 0), sc_m, sh_m, w_in,
                        tm=1024, tn=896, out_dtype=BF16, name="m2_in_proj")
    y = _ssd_mixer(proj, bsz, m2_conv_w[0], m2_conv_b[0], m2_dt_bias[0], m2_a_log[0], m2_d[0],
                   m2_norm_g[0], lc=128)
    xt = _matmul_post(y, m2_w_out[0].astype(BF16), xt, gvec(1, 1), g_m, tm=1024,
                      name="m2_out_proj")
    xt = _moe_sublayer(xt, gvec(1, 2), sc_f, sh_f, moe_w_router[0], moe_b_router[0],
                       moe_w_gate[0].astype(BF16), moe_w_up[0].astype(BF16),
                       moe_w_down[0].astype(BF16), gvec(1, 3), g_f)
    return xt.reshape(bsz, seq, d)
```

```python
import functools
import math

import jax
import jax.numpy as jnp
from jax import lax
from jax.experimental import pallas as pl
from jax.experimental.pallas import tpu as pltpu

F32 = jnp.float32
BF16 = jnp.bfloat16

D_MODEL = 1024
EPS = 1e-6
POOL_WIDTH = 512
POOL_WINDOWS = (2, 4, 8, 16)
POOL_GROUP = 128
S5_WIDTH = 512
S5_GROUP = 16
S5_GROUPS = 32
S5_STATE = 64
M2_INNER = 2048
M2_HEADDIM = 64
M2_HEADS = 32
M2_GROUPS = 8
M2_HPG = 4
M2_STATE = 128
M2_CONV = 4
M2_BC = 1024
M2_CONV_DIM = 4096
M2_PROJ = 6176
FFN_HIDDEN = 2816
N_EXPERTS = 8
EXPERT_HIDDEN = 3584

LANES = 128
SUBLANES = 8
VMEM_LIMIT_BYTES = 56 * 1024 * 1024

NEG_BIG = -1e30


def _cparams(sem):
    return pltpu.CompilerParams(dimension_semantics=sem, vmem_limit_bytes=VMEM_LIMIT_BYTES)


def _modulated_norm(x, g, scale, shift):
    ms = jnp.mean(x * x, axis=-1, keepdims=True)
    return (x * lax.rsqrt(ms + EPS) * g) * (1.0 + scale) + shift


def _post_norm_residual(x, y, ng, gate):
    ms = jnp.mean(y * y, axis=-1, keepdims=True)
    return x + gate * (y * lax.rsqrt(ms + EPS) * ng)


def _ada_kernel(c_ref, w_ref, b_ref, o_ref):
    c = c_ref[...]
    a = c * jax.nn.sigmoid(c)
    o_ref[0] = jnp.dot(a.astype(BF16), w_ref[0].astype(BF16),
                       preferred_element_type=F32) + b_ref[0]


def _ada_modulation(c, ada_w, ada_b):
    depth, d, n = ada_w.shape
    b = c.shape[0]
    c_pad = jnp.zeros((SUBLANES, d), F32).at[:b].set(c)
    tn = 1024
    out = pl.pallas_call(
        _ada_kernel,
        grid=(depth, n // tn),
        in_specs=[
            pl.BlockSpec((SUBLANES, d), lambda l, j: (0, 0)),
            pl.BlockSpec((1, d, tn), lambda l, j: (l, 0, j)),
            pl.BlockSpec((1, 1, tn), lambda l, j: (l, 0, j)),
        ],
        out_specs=pl.BlockSpec((1, SUBLANES, tn), lambda l, j: (l, 0, j)),
        out_shape=jax.ShapeDtypeStruct((depth, SUBLANES, n), F32),
        compiler_params=_cparams(("arbitrary", "arbitrary")),
        name="ada_modulation",
    )(c_pad, ada_w, ada_b.reshape(depth, 1, n))
    return out[:, :b]


def _norm_mm_kernel(x_ref, g_ref, sc_ref, sh_ref, w_ref, o_ref, h_scr):
    @pl.when(pl.program_id(1) == 0)
    def _():
        h = _modulated_norm(x_ref[...], g_ref[...], sc_ref[0], sh_ref[0])
        h_scr[...] = h.astype(BF16)

    o_ref[...] = jnp.dot(h_scr[...], w_ref[...], preferred_element_type=F32).astype(o_ref.dtype)


def _norm_matmul(x, g, scale, shift, w, *, tm, tn, out_dtype, name):
    t, d = x.shape
    n = w.shape[1]
    tiles_per_batch = t // scale.shape[0] // tm
    return pl.pallas_call(
        _norm_mm_kernel,
        grid=(t // tm, n // tn),
        in_specs=[
            pl.BlockSpec((tm, d), lambda i, j: (i, 0)),
            pl.BlockSpec((1, d), lambda i, j: (0, 0)),
            pl.BlockSpec((1, 1, d), lambda i, j: (i // tiles_per_batch, 0, 0)),
            pl.BlockSpec((1, 1, d), lambda i, j: (i // tiles_per_batch, 0, 0)),
            pl.BlockSpec((d, tn), lambda i, j: (0, j)),
        ],
        out_specs=pl.BlockSpec((tm, tn), lambda i, j: (i, j)),
        out_shape=jax.ShapeDtypeStruct((t, n), out_dtype),
        scratch_shapes=[pltpu.VMEM((tm, d), BF16)],
        compiler_params=_cparams(("arbitrary", "arbitrary")),
        name=name,
    )(x, g, scale, shift, w)


def _mm_post_kernel(y_ref, w_ref, x_ref, ng_ref, gate_ref, o_ref):
    y = jnp.dot(y_ref[...], w_ref[...], preferred_element_type=F32)
    o_ref[...] = _post_norm_residual(x_ref[...], y, ng_ref[...], gate_ref[0])


def _matmul_post(y, w, x, ng, gate, *, tm, name):
    t, k = y.shape
    d = w.shape[1]
    tiles_per_batch = t // gate.shape[0] // tm
    return pl.pallas_call(
        _mm_post_kernel,
        grid=(t // tm,),
        in_specs=[
            pl.BlockSpec((tm, k), lambda i: (i, 0)),
            pl.BlockSpec((k, d), lambda i: (0, 0)),
            pl.BlockSpec((tm, d), lambda i: (i, 0)),
            pl.BlockSpec((1, d), lambda i: (0, 0)),
            pl.BlockSpec((1, 1, d), lambda i: (i // tiles_per_batch, 0, 0)),
        ],
        out_specs=pl.BlockSpec((tm, d), lambda i: (i, 0)),
        out_shape=jax.ShapeDtypeStruct((t, d), F32),
        compiler_params=_cparams(("arbitrary",)),
        name=name,
    )(y, w, x, ng, gate)


def _ffn_kernel(x_ref, g_ref, sc_ref, sh_ref, wg_ref, wu_ref, wd_ref, ng_ref, gate_ref,
                o_ref, h_scr, acc_scr):
    j = pl.program_id(1)

    @pl.when(j == 0)
    def _():
        h = _modulated_norm(x_ref[...], g_ref[...], sc_ref[0], sh_ref[0])
        h_scr[...] = h.astype(BF16)
        acc_scr[...] = jnp.zeros_like(acc_scr)

    h = h_scr[...]
    gt = jnp.dot(h, wg_ref[...], preferred_element_type=F32)
    up = jnp.dot(h, wu_ref[...], preferred_element_type=F32)
    act = (gt * jax.nn.sigmoid(gt) * up).astype(BF16)
    acc_scr[...] += jnp.dot(act, wd_ref[...], preferred_element_type=F32)

    @pl.when(j == pl.num_programs(1) - 1)
    def _():
        o_ref[...] = _post_norm_residual(x_ref[...], acc_scr[...], ng_ref[...], gate_ref[0])


def _ffn_sublayer(x, g, scale, shift, wg, wu, wd, ng, gate, *, tm, th):
    t, d = x.shape
    hid = wg.shape[1]
    tiles_per_batch = t // scale.shape[0] // tm
    bvec = pl.BlockSpec((1, 1, d), lambda i, j: (i // tiles_per_batch, 0, 0))
    return pl.pallas_call(
        _ffn_kernel,
        grid=(t // tm, hid // th),
        in_specs=[
            pl.BlockSpec((tm, d), lambda i, j: (i, 0)),
            pl.BlockSpec((1, d), lambda i, j: (0, 0)),
            bvec, bvec,
            pl.BlockSpec((d, th), lambda i, j: (0, j)),
            pl.BlockSpec((d, th), lambda i, j: (0, j)),
            pl.BlockSpec((th, d), lambda i, j: (j, 0)),
            pl.BlockSpec((1, d), lambda i, j: (0, 0)),
            bvec,
        ],
        out_specs=pl.BlockSpec((tm, d), lambda i, j: (i, 0)),
        out_shape=jax.ShapeDtypeStruct((t, d), F32),
        scratch_shapes=[pltpu.VMEM((tm, d), BF16), pltpu.VMEM((tm, d), F32)],
        compiler_params=_cparams(("arbitrary", "arbitrary")),
        name="ffn_sublayer",
    )(x, g, scale, shift, wg, wu, wd, ng, gate)


def _router_kernel(x_ref, g_ref, sc_ref, sh_ref, wr_ref, br_ref, h_ref, meta_ref, cnt_ref):
    tm = x_ref.shape[0]

    h = _modulated_norm(x_ref[...], g_ref[...], sc_ref[0], sh_ref[0])
    h_ref[...] = h.astype(h_ref.dtype)

    logits = jnp.dot(h, wr_ref[...], preferred_element_type=F32,
                     precision=lax.Precision.HIGHEST) + br_ref[...]
    lane = lax.broadcasted_iota(jnp.int32, (tm, LANES), 1).astype(F32)
    m1 = jnp.max(logits, axis=-1, keepdims=True)
    i1 = jnp.min(jnp.where(logits == m1, lane, float(LANES)), axis=-1, keepdims=True)
    oh1 = lane == i1
    rest = jnp.where(oh1, NEG_BIG * 2.0, logits)
    m2 = jnp.max(rest, axis=-1, keepdims=True)
    i2 = jnp.min(jnp.where(rest == m2, lane, float(LANES)), axis=-1, keepdims=True)
    oh2 = lane == i2
    e = jnp.exp(m2 - m1)
    w1 = 1.0 / (1.0 + e)
    w2 = e / (1.0 + e)

    picks = jnp.where(oh1, 1.0, 0.0) + jnp.where(oh2, 1.0, 0.0)
    row = lax.broadcasted_iota(jnp.int32, (tm, tm), 0)
    col = lax.broadcasted_iota(jnp.int32, (tm, tm), 1)
    lower = jnp.where(col < row, 1.0, 0.0).astype(BF16)
    before = jnp.dot(lower, picks.astype(BF16), preferred_element_type=F32)
    tile_cnt = jnp.sum(picks, axis=0, keepdims=True)
    ea = lax.broadcasted_iota(jnp.int32, (LANES, LANES), 0)
    eb = lax.broadcasted_iota(jnp.int32, (LANES, LANES), 1)
    seg_rows = jnp.floor((tile_cnt + (SUBLANES - 1.0)) * (1.0 / SUBLANES)) * SUBLANES
    seg_off = jnp.dot(jnp.broadcast_to(seg_rows, (SUBLANES, LANES)), jnp.where(ea < eb, 1.0, 0.0),
                      preferred_element_type=F32, precision=lax.Precision.HIGHEST)[0:1, :]
    local = before + seg_off
    loc1 = jnp.sum(jnp.where(oh1, local, 0.0), axis=-1, keepdims=True)
    loc2 = jnp.sum(jnp.where(oh2, local, 0.0), axis=-1, keepdims=True)
    cnt_ref[0] = tile_cnt

    meta = jnp.where(lane == 0.0, i1, 0.0)
    meta = jnp.where(lane == 1.0, i2, meta)
    meta = jnp.where(lane == 2.0, w1, meta)
    meta = jnp.where(lane == 3.0, w2, meta)
    meta = jnp.where(lane == 4.0, loc1, meta)
    meta = jnp.where(lane == 5.0, loc2, meta)
    meta_ref[...] = meta


def _router(x, g, scale, shift, w_router, b_router, *, tm):
    t, d = x.shape
    tiles_per_batch = t // scale.shape[0] // tm
    wr = jnp.zeros((d, LANES), F32).at[:, :N_EXPERTS].set(w_router)
    br = jnp.full((1, LANES), NEG_BIG, F32).at[0, :N_EXPERTS].set(b_router)
    bvec = pl.BlockSpec((1, 1, d), lambda i: (i // tiles_per_batch, 0, 0))
    return pl.pallas_call(
        _router_kernel,
        grid=(t // tm,),
        in_specs=[
            pl.BlockSpec((tm, d), lambda i: (i, 0)),
            pl.BlockSpec((1, d), lambda i: (0, 0)),
            bvec, bvec,
            pl.BlockSpec((d, LANES), lambda i: (0, 0)),
            pl.BlockSpec((1, LANES), lambda i: (0, 0)),
        ],
        out_specs=[
            pl.BlockSpec((tm, d), lambda i: (i, 0)),
            pl.BlockSpec((tm, LANES), lambda i: (i, 0)),
            pl.BlockSpec((1, 1, LANES), lambda i: (i, 0, 0)),
        ],
        out_shape=[
            jax.ShapeDtypeStruct((t, d), BF16),
            jax.ShapeDtypeStruct((t, LANES), F32),
            jax.ShapeDtypeStruct((t // tm, 1, LANES), F32),
        ],
        compiler_params=_cparams(("arbitrary",)),
        name="moe_router",
    )(x, g, scale, shift, wr, br)


SEG_FIELDS = 3
SEG_PAD_ROWS = N_EXPERTS * SUBLANES


def _segment_copies(scal_ref, tile, tm, per_copy):
    base = tile * (SEG_FIELDS * N_EXPERTS)
    for e in range(N_EXPERTS):
        n = scal_ref[base + e]
        g0 = scal_ref[base + N_EXPERTS + e]
        l0 = scal_ref[base + 2 * N_EXPERTS + e]
        done = 0
        b = tm
        while b >= SUBLANES:
            take = n & b

            @pl.when(take != 0)
            def _(b=b, done=done, g0=g0, l0=l0):
                per_copy(pl.multiple_of(l0 + done, SUBLANES), pl.multiple_of(g0 + done, SUBLANES), b)

            done = done + take
            b //= 2


def _dispatch_kernel(scal_ref, tail_ref, h_ref, meta_ref, xs_ref, ws_ref,
                     seg_scr, wseg_scr, zero_scr, sem, zsem, *, n_tiles, max_unused):
    tile = pl.program_id(0)
    tm = h_ref.shape[0]
    ns = seg_scr.shape[1]
    par = tile % 2

    def segment_dmas(which, buf):
        def descriptors(local, glob, n):
            return (pltpu.make_async_copy(wseg_scr.at[buf, pl.ds(local, n), :],
                                          ws_ref.at[pl.ds(glob, n), :], sem.at[buf]),
                    pltpu.make_async_copy(seg_scr.at[buf, pl.ds(local, n), :],
                                          xs_ref.at[pl.ds(glob, n), :], sem.at[buf]))
        return descriptors

    def wait_segments(which, buf):
        descriptors = segment_dmas(which, buf)
        _segment_copies(scal_ref, which, tm, lambda l, g, n: [c.wait() for c in descriptors(l, g, n)])

    @pl.when(tile == 0)
    def _():
        zero_scr[...] = jnp.zeros_like(zero_scr)
        fills = []
        n_slot_tiles = xs_ref.shape[0] // tm
        tails = [(tail_ref[N_EXPERTS + e] > 0, pl.multiple_of(tail_ref[e], tm))
                 for e in range(N_EXPERTS)]
        tails += [(k >= tail_ref[2 * N_EXPERTS], k * tm)
                  for k in range(n_slot_tiles - max_unused, n_slot_tiles)]
        for nonempty, tail in tails:
            fills.append((nonempty,
                          pltpu.make_async_copy(zero_scr, xs_ref.at[pl.ds(tail, tm), :], zsem),
                          pltpu.make_async_copy(zero_scr.at[:, 0:LANES],
                                                ws_ref.at[pl.ds(tail, tm), :], zsem)))
        for nonempty, fill_x, fill_w in fills:
            @pl.when(nonempty)
            def _(fill_x=fill_x, fill_w=fill_w):
                fill_x.start()
                fill_w.start()
        for nonempty, fill_x, fill_w in fills:
            @pl.when(nonempty)
            def _(fill_x=fill_x, fill_w=fill_w):
                fill_x.wait()
                fill_w.wait()

    meta = meta_ref[...]
    meta_t = meta.T
    slot = lax.broadcasted_iota(jnp.int32, (ns, tm), 0).astype(F32)
    g1 = jnp.where(slot == meta_t[4:5, :], 1.0, 0.0).astype(BF16)
    g2 = jnp.where(slot == meta_t[5:6, :], 1.0, 0.0).astype(BF16)
    rows = jnp.dot(g1 + g2, h_ref[...], preferred_element_type=F32)
    m_hi = meta.astype(BF16)
    m_lo = (meta - m_hi.astype(F32)).astype(BF16)
    first = jnp.dot(g1, m_hi, preferred_element_type=F32) + jnp.dot(g1, m_lo, preferred_element_type=F32)
    second = jnp.dot(g2, m_hi, preferred_element_type=F32) + jnp.dot(g2, m_lo, preferred_element_type=F32)
    weight = first[:, 2:3] + second[:, 3:4]

    @pl.when(tile >= 2)
    def _():
        wait_segments(tile - 2, par)

    seg_scr[par] = rows
    wseg_scr[par] = jnp.broadcast_to(weight, (ns, LANES))
    descriptors = segment_dmas(tile, par)
    _segment_copies(scal_ref, tile, tm, lambda l, g, n: [c.start() for c in descriptors(l, g, n)])

    @pl.when(tile == n_tiles - 1)
    def _():
        wait_segments(tile, par)
        if n_tiles > 1:
            wait_segments(tile - 1, 1 - par)


def _dispatch(h, meta, seg_table, tail_table, n_slots, *, tm):
    t, d = h.shape
    grid_spec = pltpu.PrefetchScalarGridSpec(
        num_scalar_prefetch=2,
        grid=(t // tm,),
        in_specs=[
            pl.BlockSpec((tm, d), lambda i, sc, tl: (i, 0)),
            pl.BlockSpec((tm, LANES), lambda i, sc, tl: (i, 0)),
        ],
        out_specs=[pl.BlockSpec(memory_space=pl.ANY), pl.BlockSpec(memory_space=pl.ANY)],
        scratch_shapes=[
            pltpu.VMEM((2, 2 * tm + SEG_PAD_ROWS, d), F32),
            pltpu.VMEM((2, 2 * tm + SEG_PAD_ROWS, LANES), F32),
            pltpu.VMEM((tm, d), F32),
            pltpu.SemaphoreType.DMA((2,)), pltpu.SemaphoreType.DMA(()),
        ],
    )
    return pl.pallas_call(
        functools.partial(_dispatch_kernel, n_tiles=t // tm,
                          max_unused=n_slots // tm - 2 * t // tm),
        grid_spec=grid_spec,
        out_shape=[jax.ShapeDtypeStruct((n_slots, d), F32),
                   jax.ShapeDtypeStruct((n_slots, LANES), F32)],
        compiler_params=_cparams(("arbitrary",)),
        name="moe_dispatch",
    )(seg_table, tail_table, h, meta)


def _expert_kernel(te_ref, nu_ref, xs_ref, ws_ref, wg_ref, wu_ref, wd_ref, ys_ref,
                   xb_scr, acc_scr):
    i = pl.program_id(0)
    j = pl.program_id(1)
    last = pl.num_programs(1) - 1
    used = i < nu_ref[0]

    @pl.when(used)
    def _():
        @pl.when(j == 0)
        def _():
            xb_scr[...] = xs_ref[...].astype(BF16)
            acc_scr[...] = jnp.zeros_like(acc_scr)

        x = xb_scr[...]
        gt = jnp.dot(x, wg_ref[0], preferred_element_type=F32)
        up = jnp.dot(x, wu_ref[0], preferred_element_type=F32)
        act = (gt * jax.nn.sigmoid(gt) * up).astype(BF16)
        acc_scr[...] += jnp.dot(act, wd_ref[0], preferred_element_type=F32)

        @pl.when(j == last)
        def _():
            ys_ref[...] = (acc_scr[...] * ws_ref[:, 0:1]).astype(ys_ref.dtype)

    @pl.when(jnp.logical_and(jnp.logical_not(used), j == last))
    def _():
        ys_ref[...] = jnp.zeros_like(ys_ref)


def _experts(xs, ws, tile_expert, n_used, wg, wu, wd, *, tm, th):
    n_slots, d = xs.shape
    hid = wg.shape[2]
    nt = n_slots // tm

    def hidden_step(i, j, nu):
        return jnp.where(i < nu[0], j, 0)

    def slot_tile(i, nu):
        return jnp.where(i < nu[0], i, 0)

    grid_spec = pltpu.PrefetchScalarGridSpec(
        num_scalar_prefetch=2,
        grid=(nt, hid // th),
        in_specs=[
            pl.BlockSpec((tm, d), lambda i, j, te, nu: (slot_tile(i, nu), 0)),
            pl.BlockSpec((tm, LANES), lambda i, j, te, nu: (slot_tile(i, nu), 0)),
            pl.BlockSpec((1, d, th), lambda i, j, te, nu: (te[i], 0, hidden_step(i, j, nu))),
            pl.BlockSpec((1, d, th), lambda i, j, te, nu: (te[i], 0, hidden_step(i, j, nu))),
            pl.BlockSpec((1, th, d), lambda i, j, te, nu: (te[i], hidden_step(i, j, nu), 0)),
        ],
        out_specs=pl.BlockSpec((tm, d), lambda i, j, te, nu: (i, 0)),
        scratch_shapes=[pltpu.VMEM((tm, d), BF16), pltpu.VMEM((tm, d), F32)],
    )
    return pl.pallas_call(
        _expert_kernel,
        grid_spec=grid_spec,
        out_shape=jax.ShapeDtypeStruct((n_slots, d), F32),
        compiler_params=_cparams(("arbitrary", "arbitrary")),
        name="moe_experts",
    )(tile_expert, n_used, xs, ws, wg, wu, wd)


def _combine_kernel(scal_ref, ys_ref, meta_ref, x_ref, ng_ref, gate_ref, o_ref, buf, sem,
                    *, n_tiles):
    tile = pl.program_id(0)
    tm = x_ref.shape[0]
    ns = buf.shape[1]
    par = tile % 2

    def fetch(which, slot, wait):
        def copy_in(local, glob, n):
            dma = pltpu.make_async_copy(ys_ref.at[pl.ds(glob, n), :],
                                        buf.at[slot, pl.ds(local, n), :], sem.at[slot])
            dma.wait() if wait else dma.start()
        _segment_copies(scal_ref, which, tm, copy_in)

    @pl.when(tile == 0)
    def _():
        buf[...] = jnp.zeros_like(buf)
        fetch(tile, par, False)

    @pl.when(tile + 1 < n_tiles)
    def _():
        fetch(tile + 1, 1 - par, False)

    fetch(tile, par, True)
    meta = meta_ref[...]
    slot_id = lax.broadcasted_iota(jnp.int32, (tm, ns), 1).astype(F32)
    pick = jnp.where((slot_id == meta[:, 4:5]) | (slot_id == meta[:, 5:6]), 1.0, 0.0).astype(BF16)
    y = jnp.dot(pick, buf[par].astype(BF16), preferred_element_type=F32)
    o_ref[...] = _post_norm_residual(x_ref[...], y, ng_ref[...], gate_ref[0])


def _combine(ys, meta, seg_table, x, ng, gate, *, tm):
    t, d = x.shape
    tiles_per_batch = t // gate.shape[0] // tm
    grid_spec = pltpu.PrefetchScalarGridSpec(
        num_scalar_prefetch=1,
        grid=(t // tm,),
        in_specs=[
            pl.BlockSpec(memory_space=pl.ANY),
            pl.BlockSpec((tm, LANES), lambda i, sc: (i, 0)),
            pl.BlockSpec((tm, d), lambda i, sc: (i, 0)),
            pl.BlockSpec((1, d), lambda i, sc: (0, 0)),
            pl.BlockSpec((1, 1, d), lambda i, sc: (i // tiles_per_batch, 0, 0)),
        ],
        out_specs=pl.BlockSpec((tm, d), lambda i, sc: (i, 0)),
        scratch_shapes=[pltpu.VMEM((2, 2 * tm + SEG_PAD_ROWS, d), F32),
                        pltpu.SemaphoreType.DMA((2,))],
    )
    return pl.pallas_call(
        functools.partial(_combine_kernel, n_tiles=t // tm),
        grid_spec=grid_spec,
        out_shape=jax.ShapeDtypeStruct((t, d), F32),
        compiler_params=_cparams(("arbitrary",)),
        name="moe_combine",
    )(seg_table, ys, meta, x, ng, gate)


def _moe_sublayer(x, g, scale, shift, w_router, b_router, wg, wu, wd, ng, gate):
    t = x.shape[0]
    tm, tm_e, th = 512, 512, 896
    h, meta, counts = _router(x, g, scale, shift, w_router, b_router, tm=tm)

    assert tm == tm_e
    tile_cnt = counts[:, 0, :N_EXPERTS].astype(jnp.int32)
    seg_rows = ((tile_cnt + SUBLANES - 1) // SUBLANES) * SUBLANES
    cnt = jnp.sum(seg_rows, axis=0)
    padded = ((cnt + tm_e - 1) // tm_e) * tm_e
    ends = jnp.cumsum(padded)
    offs = ends - padded
    glob_start = offs[None, :] + jnp.cumsum(seg_rows, axis=0) - seg_rows
    local_start = jnp.cumsum(seg_rows, axis=1) - seg_rows
    seg_table = jnp.concatenate([seg_rows, glob_start, local_start], axis=1).reshape(-1)
    tail_table = jnp.concatenate([ends - tm_e, padded, ends[-1:] // tm_e]).astype(jnp.int32)
    n_tok_tiles = t // tm
    nt = -(-(2 * t + n_tok_tiles * SEG_PAD_ROWS) // tm_e) + N_EXPERTS
    starts = jnp.arange(nt, dtype=jnp.int32) * tm_e
    tile_expert = jnp.minimum(jnp.sum(starts[:, None] >= ends[None, :], axis=1), N_EXPERTS - 1)
    tile_expert = tile_expert.astype(jnp.int32)
    n_used = (ends[-1:] // tm_e).astype(jnp.int32)

    xs, ws = _dispatch(h, meta, seg_table, tail_table, nt * tm_e, tm=tm)
    ys = _experts(xs, ws, tile_expert, n_used, wg, wu, wd, tm=tm_e, th=th)
    return _combine(ys, meta, seg_table, x, ng, gate, tm=tm)


def _s5_disc_kernel(lam_re_ref, lam_im_ref, step_ref, pow_ref, keep_ref, b_re_ref, b_im_ref,
                    apr_ref, api_ref, bbr_ref, bbi_ref):
    lr = jnp.minimum(lam_re_ref[...], -1e-4)
    li = lam_im_ref[...]
    step = step_ref[...]
    m = pow_ref[...]
    mag = jnp.exp(lr * step * m) * keep_ref[...]
    ang = li * step * m
    apr_ref[...] = mag * jnp.cos(ang)
    api_ref[...] = mag * jnp.sin(ang)
    mag1 = jnp.exp(lr * step)
    ar = mag1 * jnp.cos(li * step)
    ai = mag1 * jnp.sin(li * step)
    inv = 1.0 / (lr * lr + li * li)
    fr = ((ar - 1.0) * lr + ai * li) * inv
    fi = (ai * lr - (ar - 1.0) * li) * inv
    br = b_re_ref[...]
    bi = b_im_ref[...]
    bbr_ref[...] = fr * br - fi * bi
    bbi_ref[...] = fr * bi + fi * br


S5_LOCAL_STEPS = 3
S5_ROW_POW0 = S5_LOCAL_STEPS * SUBLANES
S5_TILE_POW0 = S5_ROW_POW0 + SUBLANES


def _s5_power_rows(n_tile_steps):
    exps, keep = [], []
    for k in range(S5_LOCAL_STEPS):
        for tau in range(SUBLANES):
            exps.append(float(1 << k))
            keep.append(1.0 if tau >= (1 << k) else 0.0)
    for tau in range(SUBLANES):
        exps.append(float(tau + 1))
        keep.append(1.0)
    for k in range(n_tile_steps):
        exps.append(float(SUBLANES << k))
        keep.append(1.0)
    while len(exps) % SUBLANES:
        exps.append(0.0)
        keep.append(0.0)
    return exps, keep


def _s5_discretise(lam_re, lam_im, log_step, b_re, b_im, n_tile_steps):
    gp = S5_GROUPS * S5_STATE
    step = jnp.repeat(jnp.exp(log_step), S5_STATE).reshape(1, gp)
    exps, keep = _s5_power_rows(n_tile_steps)
    rows = len(exps)
    b_re_t = jnp.transpose(b_re, (2, 0, 1)).reshape(S5_GROUP, gp)
    b_im_t = jnp.transpose(b_im, (2, 0, 1)).reshape(S5_GROUP, gp)
    return pl.pallas_call(
        _s5_disc_kernel,
        out_shape=[jax.ShapeDtypeStruct((rows, gp), F32), jax.ShapeDtypeStruct((rows, gp), F32),
                   jax.ShapeDtypeStruct((S5_GROUP, gp), F32), jax.ShapeDtypeStruct((S5_GROUP, gp), F32)],
        name="s5_discretise",
    )(lam_re.reshape(1, gp), lam_im.reshape(1, gp), step,
      jnp.asarray(exps, F32).reshape(rows, 1), jnp.asarray(keep, F32).reshape(rows, 1),
      b_re_t, b_im_t)


POOL_HALO = 16
S5_HALF = 256
S5_HALF_STATES = 1024


def _complex_axpy(xr, xi, cr, ci, sr, si):
    return xr + (cr * sr - ci * si), xi + (cr * si + ci * sr)


def _mix0_kernel(u_ref, pw_ref, ps_ref, bw_ref, cw_ref, apr_ref, api_ref, dsk_ref, wglu_ref,
                 o_ref, halo_scr, ext_scr, sr_scr, si_scr, xr_scr, xi_scr):
    tt = u_ref.shape[0]
    n_states = sr_scr.shape[1]
    blk = pl.program_id(1)

    @pl.when(blk == 0)
    def _():
        halo_scr[...] = jnp.zeros_like(halo_scr)
        sr_scr[...] = jnp.zeros_like(sr_scr)
        si_scr[...] = jnp.zeros_like(si_scr)

    up = u_ref[:, :POOL_WIDTH]
    ext_scr[0:POOL_HALO, :] = halo_scr[...]
    ext_scr[POOL_HALO:, :] = up
    halo_scr[...] = up[tt - POOL_HALO:, :]
    ext = ext_scr[...].astype(BF16)
    row = lax.broadcasted_iota(jnp.int32, (tt, tt + POOL_HALO), 0)
    col = lax.broadcasted_iota(jnp.int32, (tt, tt + POOL_HALO), 1)
    lag = row + POOL_HALO - col
    t_glob = (blk * tt + row + 1).astype(F32)
    pooled_out = []
    for gi, win in enumerate(POOL_WINDOWS):
        inv_count = 1.0 / jnp.minimum(t_glob, float(win))
        band = jnp.where((lag >= 0) & (lag < win), inv_count, 0.0) - jnp.where(lag == 0, 1.0, 0.0)
        pooled = jnp.dot(band.astype(BF16), ext[:, gi * POOL_GROUP:(gi + 1) * POOL_GROUP],
                         preferred_element_type=F32)
        pooled_out.append(jnp.dot(pooled.astype(BF16), pw_ref[gi], preferred_element_type=F32))
    y_pool = jnp.concatenate(pooled_out, axis=-1) * ps_ref[...]
    o_ref[:, :POOL_WIDTH] = y_pool.astype(o_ref.dtype)

    us = u_ref[:, POOL_WIDTH:]
    usb = us.astype(BF16)
    bu = [jnp.dot(usb[:, hf * S5_HALF:(hf + 1) * S5_HALF], bw_ref[hf], preferred_element_type=F32)
          for hf in range(2)]
    nt = tt // SUBLANES
    xr = jnp.concatenate([b[:, :S5_HALF_STATES] for b in bu], axis=-1).reshape(nt, SUBLANES, n_states)
    xi = jnp.concatenate([b[:, S5_HALF_STATES:] for b in bu], axis=-1).reshape(nt, SUBLANES, n_states)
    for k in range(S5_LOCAL_STEPS):
        cr = apr_ref[k * SUBLANES:(k + 1) * SUBLANES, :][None]
        ci = api_ref[k * SUBLANES:(k + 1) * SUBLANES, :][None]
        xr, xi = _complex_axpy(xr, xi, cr, ci, pltpu.roll(xr, 1 << k, 1), pltpu.roll(xi, 1 << k, 1))

    xr2 = xr.reshape(tt, n_states)
    xi2 = xi.reshape(tt, n_states)
    n_cb = n_states // LANES
    for cb in range(n_cb):
        xr_scr[cb] = xr2[:, cb * LANES:(cb + 1) * LANES]
        xi_scr[cb] = xi2[:, cb * LANES:(cb + 1) * LANES]
    tile_end = pl.ds(SUBLANES - 1, nt, stride=SUBLANES)
    er = jnp.concatenate([xr_scr[cb, tile_end, :] for cb in range(n_cb)], axis=-1)
    ei = jnp.concatenate([xi_scr[cb, tile_end, :] for cb in range(n_cb)], axis=-1)
    prev_r = sr_scr[...]
    prev_i = si_scr[...]
    tile_row = lax.broadcasted_iota(jnp.int32, (nt, n_states), 0)
    a8r = apr_ref[S5_TILE_POW0:S5_TILE_POW0 + 1, :]
    a8i = api_ref[S5_TILE_POW0:S5_TILE_POW0 + 1, :]
    er = er + jnp.where(tile_row == 0, a8r * prev_r - a8i * prev_i, 0.0)
    ei = ei + jnp.where(tile_row == 0, a8r * prev_i + a8i * prev_r, 0.0)
    k = 0
    while (1 << k) < nt:
        cr = apr_ref[S5_TILE_POW0 + k:S5_TILE_POW0 + k + 1, :]
        ci = api_ref[S5_TILE_POW0 + k:S5_TILE_POW0 + k + 1, :]
        pr = jnp.where(tile_row < (1 << k), 0.0, pltpu.roll(er, 1 << k, 0))
        pi = jnp.where(tile_row < (1 << k), 0.0, pltpu.roll(ei, 1 << k, 0))
        er, ei = _complex_axpy(er, ei, cr, ci, pr, pi)
        k += 1
    sr_scr[...] = er[nt - 1:nt, :]
    si_scr[...] = ei[nt - 1:nt, :]
    in_r = jnp.where(tile_row == 0, prev_r, pltpu.roll(er, 1, 0))
    in_i = jnp.where(tile_row == 0, prev_i, pltpu.roll(ei, 1, 0))

    rep_row = lax.broadcasted_iota(jnp.int32, (tt, nt), 0)
    rep_col = lax.broadcasted_iota(jnp.int32, (tt, nt), 1)
    rep = jnp.where(rep_row // SUBLANES == rep_col, 1.0, 0.0).astype(BF16)
    entering = jnp.concatenate([in_r, in_i], axis=-1)
    ent_hi = entering.astype(BF16)
    ent_lo = (entering - ent_hi.astype(F32)).astype(BF16)
    ent = (jnp.dot(rep, ent_hi, preferred_element_type=F32)
           + jnp.dot(rep, ent_lo, preferred_element_type=F32))
    cbr = ent[:, :n_states].reshape(nt, SUBLANES, n_states)
    cbi = ent[:, n_states:].reshape(nt, SUBLANES, n_states)
    pwr = apr_ref[S5_ROW_POW0:S5_ROW_POW0 + SUBLANES, :][None]
    pwi = api_ref[S5_ROW_POW0:S5_ROW_POW0 + SUBLANES, :][None]
    xr, xi = _complex_axpy(xr, xi, pwr, pwi, cbr, cbi)
    xr2 = xr.reshape(tt, n_states).astype(BF16)
    xi2 = xi.reshape(tt, n_states).astype(BF16)

    ys = []
    for hf in range(2):
        sl = slice(hf * S5_HALF_STATES, (hf + 1) * S5_HALF_STATES)
        xcat = jnp.concatenate([xr2[:, sl], xi2[:, sl]], axis=-1)
        ys.append(jnp.dot(xcat, cw_ref[hf], preferred_element_type=F32))
    y = jnp.concatenate(ys, axis=-1) + dsk_ref[...] * us
    y = jax.nn.gelu(y)
    gate = jnp.dot(y.astype(BF16), wglu_ref[...], preferred_element_type=F32)
    o_ref[:, POOL_WIDTH:] = (y * jax.nn.sigmoid(gate)).astype(o_ref.dtype)


def _pool_s5_mixer(u, bsz, pool_w, pool_scale, lam_re, lam_im, log_step, b_re, b_im, c_re, c_im,
                   d_skip, w_glu, *, tt):
    t = u.shape[0]
    seq = t // bsz
    gp = S5_GROUPS * S5_STATE
    apr, api, bbr, bbi = _s5_discretise(lam_re, lam_im, log_step, b_re, b_im,
                                        int(math.log2(tt // SUBLANES)))
    n_pow = apr.shape[0]

    gh = S5_GROUPS // 2
    eye = jnp.eye(gh, dtype=F32)

    def in_map(bb):
        bb = bb.reshape(S5_GROUP, 2, gh, S5_STATE)
        return jnp.einsum("hxgp,gk->xghkp", bb, eye).reshape(2, gh * S5_GROUP, gh * S5_STATE)

    bw = jnp.concatenate([in_map(bbr), in_map(bbi)], axis=-1).astype(BF16)

    def out_map(cc):
        cc = cc.reshape(2, gh, S5_GROUP, S5_STATE)
        return jnp.einsum("xghp,gk->xgpkh", cc, eye).reshape(2, gh * S5_STATE, gh * S5_GROUP)

    cw = jnp.concatenate([out_map(c_re), -out_map(c_im)], axis=1).astype(BF16)

    const2 = lambda b, i: (0, 0)
    const3 = lambda b, i: (0, 0, 0)
    nblk = seq // tt
    return pl.pallas_call(
        _mix0_kernel,
        grid=(bsz, nblk),
        in_specs=[
            pl.BlockSpec((tt, D_MODEL), lambda b, i: (b * nblk + i, 0)),
            pl.BlockSpec((len(POOL_WINDOWS), POOL_GROUP, POOL_GROUP), const3),
            pl.BlockSpec((1, POOL_WIDTH), const2),
            pl.BlockSpec((2, S5_HALF, 2 * S5_HALF_STATES), const3),
            pl.BlockSpec((2, 2 * S5_HALF_STATES, S5_HALF), const3),
            pl.BlockSpec((n_pow, gp), const2),
            pl.BlockSpec((n_pow, gp), const2),
            pl.BlockSpec((1, S5_WIDTH), const2),
            pl.BlockSpec((S5_WIDTH, S5_WIDTH), const2),
        ],
        out_specs=pl.BlockSpec((tt, D_MODEL), lambda b, i: (b * nblk + i, 0)),
        out_shape=jax.ShapeDtypeStruct((t, D_MODEL), BF16),
        scratch_shapes=[
            pltpu.VMEM((POOL_HALO, POOL_WIDTH), F32),
            pltpu.VMEM((tt + POOL_HALO, POOL_WIDTH), F32),
            pltpu.VMEM((1, gp), F32), pltpu.VMEM((1, gp), F32),
            pltpu.VMEM((gp // LANES, tt, LANES), F32), pltpu.VMEM((gp // LANES, tt, LANES), F32),
        ],
        compiler_params=_cparams(("arbitrary", "arbitrary")),
        name="pool_s5_mixer",
    )(u, pool_w.astype(BF16), pool_scale.reshape(1, POOL_WIDTH), bw, cw, apr, api,
      d_skip.reshape(1, S5_WIDTH), w_glu.astype(BF16))


CONV_HALO = SUBLANES
M2_GROUP_WIDTH = M2_HPG * M2_HEADDIM


def _causal_conv_silu(x_ref, halo_scr, full_scr, w_ref, b_ref, col0):
    lc = x_ref.shape[0]
    width = x_ref.shape[1]
    x = x_ref[...].astype(F32)
    full_scr[0:CONV_HALO, :] = halo_scr[...]
    full_scr[CONV_HALO:, :] = x
    halo_scr[...] = x[lc - CONV_HALO:, :]
    cols = slice(col0, col0 + width)
    acc = b_ref[:, cols] + w_ref[M2_CONV - 1:M2_CONV, cols] * x
    for k in range(M2_CONV - 1):
        off = CONV_HALO - (M2_CONV - 1) + k
        acc = acc + w_ref[k:k + 1, cols] * full_scr[off:off + lc, :]
    return acc * jax.nn.sigmoid(acc)


def _split_dot(v, e):
    hi = v.astype(BF16)
    lo = (v - hi.astype(F32)).astype(BF16)
    return (jnp.dot(hi, e, preferred_element_type=F32) + jnp.dot(lo, e, preferred_element_type=F32))


def _ssd_kernel(z_ref, xs_ref, bc_ref, dtr_ref, cw_ref, cb_ref, dtb_ref, alog_ref, dx_ref, ng_ref,
                e_ref, o_ref, halo_x, halo_bc, full_x, full_bc, st_scr, y_scr):
    lc = z_ref.shape[0]

    @pl.when(pl.program_id(1) == 0)
    def _():
        halo_x[...] = jnp.zeros_like(halo_x)
        halo_bc[...] = jnp.zeros_like(halo_bc)
        st_scr[...] = jnp.zeros_like(st_scr)

    xs = _causal_conv_silu(xs_ref, halo_x, full_x, cw_ref, cb_ref, 0)
    bc = _causal_conv_silu(bc_ref, halo_bc, full_bc, cw_ref, cb_ref, M2_INNER)

    lane = lax.broadcasted_iota(jnp.int32, (1, LANES), 1)
    x_dt = dtr_ref[...].astype(F32) + dtb_ref[...]
    dt = jnp.maximum(x_dt, 0.0) + jnp.log(1.0 + jnp.exp(-jnp.abs(x_dt)))
    a = jnp.where(lane < M2_HEADS, -jnp.exp(alog_ref[...]), 0.0)
    da = dt * a
    row = lax.broadcasted_iota(jnp.int32, (lc, lc), 0)
    col = lax.broadcasted_iota(jnp.int32, (lc, lc), 1)
    causal = col <= row
    cs = jnp.dot(jnp.where(causal, 1.0, 0.0), da, preferred_element_type=F32,
                 precision=lax.Precision.HIGHEST)
    cs_last = cs[lc - 1:lc, :]
    ecs = jnp.exp(cs)
    w_in = dt * jnp.exp(cs_last - cs)
    cs_t = cs.T
    dt_t = dt.T
    e = e_ref[...]
    ecs_x = _split_dot(ecs, e)
    w_x = _split_dot(w_in, e)
    xsb = xs.astype(BF16)
    xw = (xs * w_x).astype(BF16)

    for g in range(M2_GROUPS):
        bm = bc[:, g * M2_STATE:(g + 1) * M2_STATE].astype(BF16)
        cm = bc[:, M2_BC + g * M2_STATE:M2_BC + (g + 1) * M2_STATE].astype(BF16)
        cbm = lax.dot_general(cm, bm, (((1,), (1,)), ((), ())), preferred_element_type=F32)
        gcols = slice(g * M2_GROUP_WIDTH, (g + 1) * M2_GROUP_WIDTH)
        st = st_scr[g]
        y_off = jnp.dot(cm, st.astype(BF16), preferred_element_type=F32) * ecs_x[:, gcols]
        y_heads = []
        for jj in range(M2_HPG):
            j = g * M2_HPG + jj
            seg = cs[:, j:j + 1] - cs_t[j:j + 1, :]
            dec = jnp.exp(jnp.where(causal, seg, NEG_BIG))
            m = (cbm * dec * dt_t[j:j + 1, :]).astype(BF16)
            y_heads.append(jnp.dot(m, xsb[:, j * M2_HEADDIM:(j + 1) * M2_HEADDIM],
                                   preferred_element_type=F32))
        y_g = jnp.concatenate(y_heads, axis=-1) + y_off
        st_scr[g] = st * ecs_x[lc - 1:lc, gcols] + lax.dot_general(
            bm, xw[:, gcols], (((0,), (0,)), ((), ())), preferred_element_type=F32)
        zg = z_ref[:, gcols].astype(F32)
        y_g = (y_g + dx_ref[:, gcols] * xs[:, gcols]) * (zg * jax.nn.sigmoid(zg))
        y_scr[:, gcols] = y_g

    y = y_scr[...]
    ms = jnp.mean(y * y, axis=-1, keepdims=True)
    o_ref[...] = (y * lax.rsqrt(ms + EPS) * ng_ref[...]).astype(o_ref.dtype)


def _ssd_mixer(proj, bsz, conv_w, conv_b, dt_bias, a_log, d_skip, norm_g, *, lc):
    t = proj.shape[0]
    seq = t // bsz
    nblk = seq // lc
    half = M2_INNER
    pad = LANES - M2_HEADS
    dtb = jnp.pad(dt_bias, (0, pad)).reshape(1, LANES)
    alog = jnp.pad(a_log, (0, pad)).reshape(1, LANES)
    dx = jnp.repeat(d_skip, M2_HEADDIM).reshape(1, M2_INNER)
    heads = jnp.arange(LANES, dtype=jnp.int32)[:, None]
    chans = jnp.arange(M2_INNER, dtype=jnp.int32)[None, :] // M2_HEADDIM
    expand = (heads == chans).astype(BF16)
    const2 = lambda b, i: (0, 0)
    return pl.pallas_call(
        _ssd_kernel,
        grid=(bsz, nblk),
        in_specs=[
            pl.BlockSpec((lc, half), lambda b, i: (b * nblk + i, 0)),
            pl.BlockSpec((lc, half), lambda b, i: (b * nblk + i, 1)),
            pl.BlockSpec((lc, half), lambda b, i: (b * nblk + i, 2)),
            pl.BlockSpec((lc, LANES), lambda b, i: (b * nblk + i, 3 * half // LANES)),
            pl.BlockSpec((M2_CONV, M2_CONV_DIM), const2),
            pl.BlockSpec((1, M2_CONV_DIM), const2),
            pl.BlockSpec((1, LANES), const2),
            pl.BlockSpec((1, LANES), const2),
            pl.BlockSpec((1, M2_INNER), const2),
            pl.BlockSpec((1, M2_INNER), const2),
            pl.BlockSpec((LANES, M2_INNER), const2),
        ],
        out_specs=pl.BlockSpec((lc, M2_INNER), lambda b, i: (b * nblk + i, 0)),
        out_shape=jax.ShapeDtypeStruct((t, M2_INNER), BF16),
        scratch_shapes=[
            pltpu.VMEM((CONV_HALO, half), F32), pltpu.VMEM((CONV_HALO, half), F32),
            pltpu.VMEM((lc + CONV_HALO, half), F32), pltpu.VMEM((lc + CONV_HALO, half), F32),
            pltpu.VMEM((M2_GROUPS, M2_STATE, M2_GROUP_WIDTH), F32),
            pltpu.VMEM((lc, M2_INNER), F32),
        ],
        compiler_params=_cparams(("arbitrary", "arbitrary")),
        name="ssd_mixer",
    )(proj, proj, proj, proj, conv_w, conv_b.reshape(1, M2_CONV_DIM), dtb, alog, dx,
      norm_g.reshape(1, M2_INNER), expand)


def kernel(x, c, ada_w, ada_b, norm_g, mix_w_in, pool_w, pool_scale, s5_lam_re, s5_lam_im, s5_log_step, s5_b_re, s5_b_im, s5_c_re, s5_c_im, s5_d, s5_w_glu, mix_w_out, ffn_w_gate, ffn_w_up, ffn_w_down, m2_w_in, m2_conv_w, m2_conv_b, m2_dt_bias, m2_a_log, m2_d, m2_norm_g, m2_w_out, moe_w_router, moe_b_router, moe_w_gate, moe_w_up, moe_w_down):
    bsz, seq, d = x.shape
    t = bsz * seq
    xt = x.reshape(t, d)

    mod = _ada_modulation(c, ada_w, ada_b)

    def mod_vecs(layer):
        return [mod[layer, :, k * d:(k + 1) * d].reshape(bsz, 1, d) for k in range(6)]

    def gvec(layer, k):
        return norm_g[layer, k].reshape(1, d)

    sh_m, sc_m, g_m, sh_f, sc_f, g_f = mod_vecs(0)
    u = _norm_matmul(xt, gvec(0, 0), sc_m, sh_m, mix_w_in[0].astype(BF16),
                     tm=1024, tn=1024, out_dtype=F32, name="mix_in_proj")
    ycat = _pool_s5_mixer(u, bsz, pool_w[0], pool_scale[0], s5_lam_re[0], s5_lam_im[0],
                          s5_log_step[0], s5_b_re[0], s5_b_im[0], s5_c_re[0], s5_c_im[0],
                          s5_d[0], s5_w_glu[0], tt=256)
    xt = _matmul_post(ycat, mix_w_out[0].astype(BF16), xt, gvec(0, 1), g_m,
                      tm=1024, name="mix_out_proj")
    xt = _ffn_sublayer(xt, gvec(0, 2), sc_f, sh_f, ffn_w_gate[0].astype(BF16),
                       ffn_w_up[0].astype(BF16), ffn_w_down[0].astype(BF16), gvec(0, 3), g_f,
                       tm=1024, th=1408)

    sh_m, sc_m, g_m, sh_f, sc_f, g_f = mod_vecs(1)
    proj_pad = 6272
    w_in = jnp.zeros((d, proj_pad), BF16).at[:, :M2_PROJ].set(m2_w_in[0].astype(BF16))
    proj = _norm_matmul(xt, gvec(1, 0), sc_m, sh_m, w_in,
                        tm=1024, tn=896, out_dtype=BF16, name="m2_in_proj")
    y = _ssd_mixer(proj, bsz, m2_conv_w[0], m2_conv_b[0], m2_dt_bias[0], m2_a_log[0], m2_d[0],
                   m2_norm_g[0], lc=128)
    xt = _matmul_post(y, m2_w_out[0].astype(BF16), xt, gvec(1, 1), g_m, tm=1024,
                      name="m2_out_proj")
    xt = _moe_sublayer(xt, gvec(1, 2), sc_f, sh_f, moe_w_router[0], moe_b_router[0],
                       moe_w_gate[0].astype(BF16), moe_w_up[0].astype(BF16),
                       moe_w_down[0].astype(BF16), gvec(1, 3), g_f)
    return xt.reshape(bsz, seq, d)
```

```python
import functools
import math

import jax
import jax.numpy as jnp
from jax import lax
from jax.experimental import pallas as pl
from jax.experimental.pallas import tpu as pltpu

F32 = jnp.float32
BF16 = jnp.bfloat16

D_MODEL = 1024
EPS = 1e-6
POOL_WIDTH = 512
POOL_WINDOWS = (2, 4, 8, 16)
POOL_GROUP = 128
S5_WIDTH = 512
S5_GROUP = 16
S5_GROUPS = 32
S5_STATE = 64
M2_INNER = 2048
M2_HEADDIM = 64
M2_HEADS = 32
M2_GROUPS = 8
M2_HPG = 4
M2_STATE = 128
M2_CONV = 4
M2_BC = 1024
M2_CONV_DIM = 4096
M2_PROJ = 6176
FFN_HIDDEN = 2816
N_EXPERTS = 8
EXPERT_HIDDEN = 3584

LANES = 128
SUBLANES = 8
VMEM_LIMIT_BYTES = 56 * 1024 * 1024

NEG_BIG = -1e30


def _cparams(sem):
    return pltpu.CompilerParams(dimension_semantics=sem, vmem_limit_bytes=VMEM_LIMIT_BYTES)


def _modulated_norm(x, g, scale, shift):
    ms = jnp.mean(x * x, axis=-1, keepdims=True)
    return (x * lax.rsqrt(ms + EPS) * g) * (1.0 + scale) + shift


def _post_norm_residual(x, y, ng, gate):
    ms = jnp.mean(y * y, axis=-1, keepdims=True)
    return x + gate * (y * lax.rsqrt(ms + EPS) * ng)


def _ada_kernel(c_ref, w_ref, b_ref, o_ref):
    c = c_ref[...]
    a = c * jax.nn.sigmoid(c)
    o_ref[0] = jnp.dot(a.astype(BF16), w_ref[0].astype(BF16),
                       preferred_element_type=F32) + b_ref[0]


def _ada_modulation(c, ada_w, ada_b):
    depth, d, n = ada_w.shape
    b = c.shape[0]
    c_pad = jnp.zeros((SUBLANES, d), F32).at[:b].set(c)
    tn = 1024
    out = pl.pallas_call(
        _ada_kernel,
        grid=(depth, n // tn),
        in_specs=[
            pl.BlockSpec((SUBLANES, d), lambda l, j: (0, 0)),
            pl.BlockSpec((1, d, tn), lambda l, j: (l, 0, j)),
            pl.BlockSpec((1, 1, tn), lambda l, j: (l, 0, j)),
        ],
        out_specs=pl.BlockSpec((1, SUBLANES, tn), lambda l, j: (l, 0, j)),
        out_shape=jax.ShapeDtypeStruct((depth, SUBLANES, n), F32),
        compiler_params=_cparams(("arbitrary", "arbitrary")),
        name="ada_modulation",
    )(c_pad, ada_w, ada_b.reshape(depth, 1, n))
    return out[:, :b]


def _norm_mm_kernel(x_ref, g_ref, sc_ref, sh_ref, w_ref, *rest, n_side, side_steps):
    side_in = rest[:n_side]
    o_ref = rest[n_side]
    side_out = rest[n_side + 1:2 * n_side + 1]
    h_scr = rest[2 * n_side + 1]

    @pl.when(pl.program_id(1) == 0)
    def _():
        h = _modulated_norm(x_ref[...], g_ref[...], sc_ref[0], sh_ref[0])
        h_scr[...] = h.astype(BF16)

    o_ref[...] = jnp.dot(h_scr[...], w_ref[...], preferred_element_type=F32).astype(o_ref.dtype)

    step = pl.program_id(0) * pl.num_programs(1) + pl.program_id(1)
    for k in range(n_side):
        @pl.when(jnp.logical_and(step >= k * side_steps, step < (k + 1) * side_steps))
        def _(k=k):
            side_out[k][...] = side_in[k][...].astype(BF16)


def _norm_matmul(x, g, scale, shift, w, *, tm, tn, out_dtype, name, side=(), side_steps=1):
    t, d = x.shape
    n = w.shape[1]
    n_j = n // tn
    tiles_per_batch = t // scale.shape[0] // tm
    assert len(side) * side_steps <= (t // tm) * n_j

    def side_spec(k, arr):
        rows = arr.shape[0] // side_steps
        return pl.BlockSpec(
            (rows, arr.shape[1]),
            lambda i, j: (jnp.clip(i * n_j + j - k * side_steps, 0, side_steps - 1), 0))

    side_specs = [side_spec(k, a) for k, a in enumerate(side)]
    outs = pl.pallas_call(
        functools.partial(_norm_mm_kernel, n_side=len(side), side_steps=side_steps),
        grid=(t // tm, n_j),
        in_specs=[
            pl.BlockSpec((tm, d), lambda i, j: (i, 0)),
            pl.BlockSpec((1, d), lambda i, j: (0, 0)),
            pl.BlockSpec((1, 1, d), lambda i, j: (i // tiles_per_batch, 0, 0)),
            pl.BlockSpec((1, 1, d), lambda i, j: (i // tiles_per_batch, 0, 0)),
            pl.BlockSpec((d, tn), lambda i, j: (0, j)),
        ] + side_specs,
        out_specs=[pl.BlockSpec((tm, tn), lambda i, j: (i, j))] + side_specs,
        out_shape=[jax.ShapeDtypeStruct((t, n), out_dtype)]
                  + [jax.ShapeDtypeStruct(a.shape, BF16) for a in side],
        scratch_shapes=[pltpu.VMEM((tm, d), BF16)],
        compiler_params=_cparams(("arbitrary", "arbitrary")),
        name=name,
    )(x, g, scale, shift, w, *side)
    return (outs[0], outs[1:]) if side else outs[0]


def _mm_post_kernel(y_ref, w_ref, x_ref, ng_ref, gate_ref, o_ref):
    y = jnp.dot(y_ref[...], w_ref[...], preferred_element_type=F32)
    o_ref[...] = _post_norm_residual(x_ref[...], y, ng_ref[...], gate_ref[0])


def _matmul_post(y, w, x, ng, gate, *, tm, name):
    t, k = y.shape
    d = w.shape[1]
    tiles_per_batch = t // gate.shape[0] // tm
    return pl.pallas_call(
        _mm_post_kernel,
        grid=(t // tm,),
        in_specs=[
            pl.BlockSpec((tm, k), lambda i: (i, 0)),
            pl.BlockSpec((k, d), lambda i: (0, 0)),
            pl.BlockSpec((tm, d), lambda i: (i, 0)),
            pl.BlockSpec((1, d), lambda i: (0, 0)),
            pl.BlockSpec((1, 1, d), lambda i: (i // tiles_per_batch, 0, 0)),
        ],
        out_specs=pl.BlockSpec((tm, d), lambda i: (i, 0)),
        out_shape=jax.ShapeDtypeStruct((t, d), F32),
        compiler_params=_cparams(("arbitrary",)),
        name=name,
    )(y, w, x, ng, gate)


def _ffn_kernel(x_ref, g_ref, sc_ref, sh_ref, wg_ref, wu_ref, wd_ref, ng_ref, gate_ref,
                o_ref, h_scr, acc_scr):
    j = pl.program_id(1)

    @pl.when(j == 0)
    def _():
        h = _modulated_norm(x_ref[...], g_ref[...], sc_ref[0], sh_ref[0])
        h_scr[...] = h.astype(BF16)
        acc_scr[...] = jnp.zeros_like(acc_scr)

    h = h_scr[...]
    gt = jnp.dot(h, wg_ref[...], preferred_element_type=F32)
    up = jnp.dot(h, wu_ref[...], preferred_element_type=F32)
    act = (gt * jax.nn.sigmoid(gt) * up).astype(BF16)
    acc_scr[...] += jnp.dot(act, wd_ref[...], preferred_element_type=F32)

    @pl.when(j == pl.num_programs(1) - 1)
    def _():
        o_ref[...] = _post_norm_residual(x_ref[...], acc_scr[...], ng_ref[...], gate_ref[0])


def _ffn_sublayer(x, g, scale, shift, wg, wu, wd, ng, gate, *, tm, th):
    t, d = x.shape
    hid = wg.shape[1]
    tiles_per_batch = t // scale.shape[0] // tm
    bvec = pl.BlockSpec((1, 1, d), lambda i, j: (i // tiles_per_batch, 0, 0))
    return pl.pallas_call(
        _ffn_kernel,
        grid=(t // tm, hid // th),
        in_specs=[
            pl.BlockSpec((tm, d), lambda i, j: (i, 0)),
            pl.BlockSpec((1, d), lambda i, j: (0, 0)),
            bvec, bvec,
            pl.BlockSpec((d, th), lambda i, j: (0, j)),
            pl.BlockSpec((d, th), lambda i, j: (0, j)),
            pl.BlockSpec((th, d), lambda i, j: (j, 0)),
            pl.BlockSpec((1, d), lambda i, j: (0, 0)),
            bvec,
        ],
        out_specs=pl.BlockSpec((tm, d), lambda i, j: (i, 0)),
        out_shape=jax.ShapeDtypeStruct((t, d), F32),
        scratch_shapes=[pltpu.VMEM((tm, d), BF16), pltpu.VMEM((tm, d), F32)],
        compiler_params=_cparams(("arbitrary", "arbitrary")),
        name="ffn_sublayer",
    )(x, g, scale, shift, wg, wu, wd, ng, gate)


def _router_kernel(x_ref, g_ref, sc_ref, sh_ref, wr_ref, br_ref, h_ref, meta_ref, cnt_ref):
    tm = x_ref.shape[0]

    h = _modulated_norm(x_ref[...], g_ref[...], sc_ref[0], sh_ref[0])
    h_ref[...] = h.astype(h_ref.dtype)

    logits = jnp.dot(h, wr_ref[...], preferred_element_type=F32,
                     precision=lax.Precision.HIGHEST) + br_ref[...]
    lane = lax.broadcasted_iota(jnp.int32, (tm, LANES), 1).astype(F32)
    m1 = jnp.max(logits, axis=-1, keepdims=True)
    i1 = jnp.min(jnp.where(logits == m1, lane, float(LANES)), axis=-1, keepdims=True)
    oh1 = lane == i1
    rest = jnp.where(oh1, NEG_BIG * 2.0, logits)
    m2 = jnp.max(rest, axis=-1, keepdims=True)
    i2 = jnp.min(jnp.where(rest == m2, lane, float(LANES)), axis=-1, keepdims=True)
    oh2 = lane == i2
    e = jnp.exp(m2 - m1)
    w1 = 1.0 / (1.0 + e)
    w2 = e / (1.0 + e)

    picks = jnp.where(oh1, 1.0, 0.0) + jnp.where(oh2, 1.0, 0.0)
    row = lax.broadcasted_iota(jnp.int32, (tm, tm), 0)
    col = lax.broadcasted_iota(jnp.int32, (tm, tm), 1)
    lower = jnp.where(col < row, 1.0, 0.0).astype(BF16)
    before = jnp.dot(lower, picks.astype(BF16), preferred_element_type=F32)
    tile_cnt = jnp.sum(picks, axis=0, keepdims=True)
    ea = lax.broadcasted_iota(jnp.int32, (LANES, LANES), 0)
    eb = lax.broadcasted_iota(jnp.int32, (LANES, LANES), 1)
    seg_rows = jnp.floor((tile_cnt + (SUBLANES - 1.0)) * (1.0 / SUBLANES)) * SUBLANES
    seg_off = jnp.dot(jnp.broadcast_to(seg_rows, (SUBLANES, LANES)), jnp.where(ea < eb, 1.0, 0.0),
                      preferred_element_type=F32, precision=lax.Precision.HIGHEST)[0:1, :]
    local = before + seg_off
    loc1 = jnp.sum(jnp.where(oh1, local, 0.0), axis=-1, keepdims=True)
    loc2 = jnp.sum(jnp.where(oh2, local, 0.0), axis=-1, keepdims=True)
    cnt_ref[0] = tile_cnt

    meta = jnp.where(lane == 0.0, i1, 0.0)
    meta = jnp.where(lane == 1.0, i2, meta)
    meta = jnp.where(lane == 2.0, w1, meta)
    meta = jnp.where(lane == 3.0, w2, meta)
    meta = jnp.where(lane == 4.0, loc1, meta)
    meta = jnp.where(lane == 5.0, loc2, meta)
    meta_ref[...] = meta


def _router(x, g, scale, shift, w_router, b_router, *, tm):
    t, d = x.shape
    tiles_per_batch = t // scale.shape[0] // tm
    wr = jnp.zeros((d, LANES), F32).at[:, :N_EXPERTS].set(w_router)
    br = jnp.full((1, LANES), NEG_BIG, F32).at[0, :N_EXPERTS].set(b_router)
    bvec = pl.BlockSpec((1, 1, d), lambda i: (i // tiles_per_batch, 0, 0))
    return pl.pallas_call(
        _router_kernel,
        grid=(t // tm,),
        in_specs=[
            pl.BlockSpec((tm, d), lambda i: (i, 0)),
            pl.BlockSpec((1, d), lambda i: (0, 0)),
            bvec, bvec,
            pl.BlockSpec((d, LANES), lambda i: (0, 0)),
            pl.BlockSpec((1, LANES), lambda i: (0, 0)),
        ],
        out_specs=[
            pl.BlockSpec((tm, d), lambda i: (i, 0)),
            pl.BlockSpec((tm, LANES), lambda i: (i, 0)),
            pl.BlockSpec((1, 1, LANES), lambda i: (i, 0, 0)),
        ],
        out_shape=[
            jax.ShapeDtypeStruct((t, d), BF16),
            jax.ShapeDtypeStruct((t, LANES), F32),
            jax.ShapeDtypeStruct((t // tm, 1, LANES), F32),
        ],
        compiler_params=_cparams(("arbitrary",)),
        name="moe_router",
    )(x, g, scale, shift, wr, br)


SEG_FIELDS = 3
SEG_PAD_ROWS = N_EXPERTS * SUBLANES


def _segment_copies(scal_ref, tile, tm, per_copy):
    base = tile * (SEG_FIELDS * N_EXPERTS)
    for e in range(N_EXPERTS):
        n = scal_ref[base + e]
        g0 = scal_ref[base + N_EXPERTS + e]
        l0 = scal_ref[base + 2 * N_EXPERTS + e]
        done = 0
        b = tm
        while b >= SUBLANES:
            take = n & b

            @pl.when(take != 0)
            def _(b=b, done=done, g0=g0, l0=l0):
                per_copy(pl.multiple_of(l0 + done, SUBLANES), pl.multiple_of(g0 + done, SUBLANES), b)

            done = done + take
            b //= 2


def _dispatch_kernel(scal_ref, tail_ref, h_ref, meta_ref, xs_ref, ws_ref,
                     seg_scr, wseg_scr, zero_scr, sem, zsem, *, n_tiles, max_unused):
    tile = pl.program_id(0)
    tm = h_ref.shape[0]
    ns = seg_scr.shape[1]
    par = tile % 2

    def segment_dmas(which, buf):
        def descriptors(local, glob, n):
            return (pltpu.make_async_copy(wseg_scr.at[buf, pl.ds(local, n), :],
                                          ws_ref.at[pl.ds(glob, n), :], sem.at[buf]),
                    pltpu.make_async_copy(seg_scr.at[buf, pl.ds(local, n), :],
                                          xs_ref.at[pl.ds(glob, n), :], sem.at[buf]))
        return descriptors

    def wait_segments(which, buf):
        descriptors = segment_dmas(which, buf)
        _segment_copies(scal_ref, which, tm, lambda l, g, n: [c.wait() for c in descriptors(l, g, n)])

    @pl.when(tile == 0)
    def _():
        zero_scr[...] = jnp.zeros_like(zero_scr)
        fills = []
        n_slot_tiles = xs_ref.shape[0] // tm
        tails = [(tail_ref[N_EXPERTS + e] > 0, pl.multiple_of(tail_ref[e], tm))
                 for e in range(N_EXPERTS)]
        tails += [(k >= tail_ref[2 * N_EXPERTS], k * tm)
                  for k in range(n_slot_tiles - max_unused, n_slot_tiles)]
        for nonempty, tail in tails:
            fills.append((nonempty,
                          pltpu.make_async_copy(zero_scr, xs_ref.at[pl.ds(tail, tm), :], zsem),
                          pltpu.make_async_copy(zero_scr.at[:, 0:LANES],
                                                ws_ref.at[pl.ds(tail, tm), :], zsem)))
        for nonempty, fill_x, fill_w in fills:
            @pl.when(nonempty)
            def _(fill_x=fill_x, fill_w=fill_w):
                fill_x.start()
                fill_w.start()
        for nonempty, fill_x, fill_w in fills:
            @pl.when(nonempty)
            def _(fill_x=fill_x, fill_w=fill_w):
                fill_x.wait()
                fill_w.wait()

    meta = meta_ref[...]
    meta_t = meta.T
    slot = lax.broadcasted_iota(jnp.int32, (ns, tm), 0).astype(F32)
    g1 = jnp.where(slot == meta_t[4:5, :], 1.0, 0.0).astype(BF16)
    g2 = jnp.where(slot == meta_t[5:6, :], 1.0, 0.0).astype(BF16)
    rows = jnp.dot(g1 + g2, h_ref[...], preferred_element_type=F32)
    m_hi = meta.astype(BF16)
    m_lo = (meta - m_hi.astype(F32)).astype(BF16)
    first = jnp.dot(g1, m_hi, preferred_element_type=F32) + jnp.dot(g1, m_lo, preferred_element_type=F32)
    second = jnp.dot(g2, m_hi, preferred_element_type=F32) + jnp.dot(g2, m_lo, preferred_element_type=F32)
    weight = first[:, 2:3] + second[:, 3:4]

    @pl.when(tile >= 2)
    def _():
        wait_segments(tile - 2, par)

    seg_scr[par] = rows
    wseg_scr[par] = jnp.broadcast_to(weight, (ns, LANES))
    descriptors = segment_dmas(tile, par)
    _segment_copies(scal_ref, tile, tm, lambda l, g, n: [c.start() for c in descriptors(l, g, n)])

    @pl.when(tile == n_tiles - 1)
    def _():
        wait_segments(tile, par)
        if n_tiles > 1:
            wait_segments(tile - 1, 1 - par)


def _dispatch(h, meta, seg_table, tail_table, n_slots, *, tm):
    t, d = h.shape
    grid_spec = pltpu.PrefetchScalarGridSpec(
        num_scalar_prefetch=2,
        grid=(t // tm,),
        in_specs=[
            pl.BlockSpec((tm, d), lambda i, sc, tl: (i, 0)),
            pl.BlockSpec((tm, LANES), lambda i, sc, tl: (i, 0)),
        ],
        out_specs=[pl.BlockSpec(memory_space=pl.ANY), pl.BlockSpec(memory_space=pl.ANY)],
        scratch_shapes=[
            pltpu.VMEM((2, 2 * tm + SEG_PAD_ROWS, d), F32),
            pltpu.VMEM((2, 2 * tm + SEG_PAD_ROWS, LANES), F32),
            pltpu.VMEM((tm, d), F32),
            pltpu.SemaphoreType.DMA((2,)), pltpu.SemaphoreType.DMA(()),
        ],
    )
    return pl.pallas_call(
        functools.partial(_dispatch_kernel, n_tiles=t // tm,
                          max_unused=n_slots // tm - 2 * t // tm),
        grid_spec=grid_spec,
        out_shape=[jax.ShapeDtypeStruct((n_slots, d), F32),
                   jax.ShapeDtypeStruct((n_slots, LANES), F32)],
        compiler_params=_cparams(("arbitrary",)),
        name="moe_dispatch",
    )(seg_table, tail_table, h, meta)


def _expert_kernel(te_ref, nu_ref, xs_ref, ws_ref, wg_ref, wu_ref, wd_ref, ys_ref,
                   xb_scr, acc_scr):
    i = pl.program_id(0)
    j = pl.program_id(1)
    last = pl.num_programs(1) - 1
    used = i < nu_ref[0]

    @pl.when(used)
    def _():
        @pl.when(j == 0)
        def _():
            xb_scr[...] = xs_ref[...].astype(BF16)
            acc_scr[...] = jnp.zeros_like(acc_scr)

        x = xb_scr[...]
        gt = jnp.dot(x, wg_ref[0], preferred_element_type=F32)
        up = jnp.dot(x, wu_ref[0], preferred_element_type=F32)
        act = (gt * jax.nn.sigmoid(gt) * up).astype(BF16)
        acc_scr[...] += jnp.dot(act, wd_ref[0], preferred_element_type=F32)

        @pl.when(j == last)
        def _():
            ys_ref[...] = (acc_scr[...] * ws_ref[:, 0:1]).astype(ys_ref.dtype)

    @pl.when(jnp.logical_and(jnp.logical_not(used), j == last))
    def _():
        ys_ref[...] = jnp.zeros_like(ys_ref)


def _experts(xs, ws, tile_expert, n_used, wg, wu, wd, *, tm, th):
    n_slots, d = xs.shape
    hid = wg.shape[2]
    nt = n_slots // tm

    def hidden_step(i, j, nu):
        return jnp.where(i < nu[0], j, 0)

    def slot_tile(i, nu):
        return jnp.where(i < nu[0], i, 0)

    grid_spec = pltpu.PrefetchScalarGridSpec(
        num_scalar_prefetch=2,
        grid=(nt, hid // th),
        in_specs=[
            pl.BlockSpec((tm, d), lambda i, j, te, nu: (slot_tile(i, nu), 0)),
            pl.BlockSpec((tm, LANES), lambda i, j, te, nu: (slot_tile(i, nu), 0)),
            pl.BlockSpec((1, d, th), lambda i, j, te, nu: (te[i], 0, hidden_step(i, j, nu))),
            pl.BlockSpec((1, d, th), lambda i, j, te, nu: (te[i], 0, hidden_step(i, j, nu))),
            pl.BlockSpec((1, th, d), lambda i, j, te, nu: (te[i], hidden_step(i, j, nu), 0)),
        ],
        out_specs=pl.BlockSpec((tm, d), lambda i, j, te, nu: (i, 0)),
        scratch_shapes=[pltpu.VMEM((tm, d), BF16), pltpu.VMEM((tm, d), F32)],
    )
    return pl.pallas_call(
        _expert_kernel,
        grid_spec=grid_spec,
        out_shape=jax.ShapeDtypeStruct((n_slots, d), F32),
        compiler_params=_cparams(("arbitrary", "arbitrary")),
        name="moe_experts",
    )(tile_expert, n_used, xs, ws, wg, wu, wd)


def _combine_kernel(scal_ref, ys_ref, meta_ref, x_ref, ng_ref, gate_ref, o_ref, buf, sem,
                    *, n_tiles):
    tile = pl.program_id(0)
    tm = x_ref.shape[0]
    ns = buf.shape[1]
    par = tile % 2

    def fetch(which, slot, wait):
        def copy_in(local, glob, n):
            dma = pltpu.make_async_copy(ys_ref.at[pl.ds(glob, n), :],
                                        buf.at[slot, pl.ds(local, n), :], sem.at[slot])
            dma.wait() if wait else dma.start()
        _segment_copies(scal_ref, which, tm, copy_in)

    @pl.when(tile == 0)
    def _():
        buf[...] = jnp.zeros_like(buf)
        fetch(tile, par, False)

    @pl.when(tile + 1 < n_tiles)
    def _():
        fetch(tile + 1, 1 - par, False)

    fetch(tile, par, True)
    meta = meta_ref[...]
    slot_id = lax.broadcasted_iota(jnp.int32, (tm, ns), 1).astype(F32)
    pick = jnp.where((slot_id == meta[:, 4:5]) | (slot_id == meta[:, 5:6]), 1.0, 0.0).astype(BF16)
    y = jnp.dot(pick, buf[par].astype(BF16), preferred_element_type=F32)
    o_ref[...] = _post_norm_residual(x_ref[...], y, ng_ref[...], gate_ref[0])


def _combine(ys, meta, seg_table, x, ng, gate, *, tm):
    t, d = x.shape
    tiles_per_batch = t // gate.shape[0] // tm
    grid_spec = pltpu.PrefetchScalarGridSpec(
        num_scalar_prefetch=1,
        grid=(t // tm,),
        in_specs=[
            pl.BlockSpec(memory_space=pl.ANY),
            pl.BlockSpec((tm, LANES), lambda i, sc: (i, 0)),
            pl.BlockSpec((tm, d), lambda i, sc: (i, 0)),
            pl.BlockSpec((1, d), lambda i, sc: (0, 0)),
            pl.BlockSpec((1, 1, d), lambda i, sc: (i // tiles_per_batch, 0, 0)),
        ],
        out_specs=pl.BlockSpec((tm, d), lambda i, sc: (i, 0)),
        scratch_shapes=[pltpu.VMEM((2, 2 * tm + SEG_PAD_ROWS, d), F32),
                        pltpu.SemaphoreType.DMA((2,))],
    )
    return pl.pallas_call(
        functools.partial(_combine_kernel, n_tiles=t // tm),
        grid_spec=grid_spec,
        out_shape=jax.ShapeDtypeStruct((t, d), F32),
        compiler_params=_cparams(("arbitrary",)),
        name="moe_combine",
    )(seg_table, ys, meta, x, ng, gate)


def _moe_sublayer(x, g, scale, shift, w_router, b_router, wg, wu, wd, ng, gate):
    t = x.shape[0]
    tm, tm_e, th = 512, 512, 1792
    h, meta, counts = _router(x, g, scale, shift, w_router, b_router, tm=tm)

    assert tm == tm_e
    tile_cnt = counts[:, 0, :N_EXPERTS].astype(jnp.int32)
    seg_rows = ((tile_cnt + SUBLANES - 1) // SUBLANES) * SUBLANES
    cnt = jnp.sum(seg_rows, axis=0)
    padded = ((cnt + tm_e - 1) // tm_e) * tm_e
    ends = jnp.cumsum(padded)
    offs = ends - padded
    glob_start = offs[None, :] + jnp.cumsum(seg_rows, axis=0) - seg_rows
    local_start = jnp.cumsum(seg_rows, axis=1) - seg_rows
    seg_table = jnp.concatenate([seg_rows, glob_start, local_start], axis=1).reshape(-1)
    tail_table = jnp.concatenate([ends - tm_e, padded, ends[-1:] // tm_e]).astype(jnp.int32)
    n_tok_tiles = t // tm
    nt = -(-(2 * t + n_tok_tiles * SEG_PAD_ROWS) // tm_e) + N_EXPERTS
    starts = jnp.arange(nt, dtype=jnp.int32) * tm_e
    tile_expert = jnp.minimum(jnp.sum(starts[:, None] >= ends[None, :], axis=1), N_EXPERTS - 1)
    tile_expert = tile_expert.astype(jnp.int32)
    n_used = (ends[-1:] // tm_e).astype(jnp.int32)

    xs, ws = _dispatch(h, meta, seg_table, tail_table, nt * tm_e, tm=tm)
    ys = _experts(xs, ws, tile_expert, n_used, wg, wu, wd, tm=tm_e, th=th)
    return _combine(ys, meta, seg_table, x, ng, gate, tm=tm)


def _s5_disc_kernel(lam_re_ref, lam_im_ref, step_ref, pow_ref, keep_ref, b_re_ref, b_im_ref,
                    apr_ref, api_ref, bbr_ref, bbi_ref):
    lr = jnp.minimum(lam_re_ref[...], -1e-4)
    li = lam_im_ref[...]
    step = step_ref[...]
    m = pow_ref[...]
    mag = jnp.exp(lr * step * m) * keep_ref[...]
    ang = li * step * m
    apr_ref[...] = mag * jnp.cos(ang)
    api_ref[...] = mag * jnp.sin(ang)
    mag1 = jnp.exp(lr * step)
    ar = mag1 * jnp.cos(li * step)
    ai = mag1 * jnp.sin(li * step)
    inv = 1.0 / (lr * lr + li * li)
    fr = ((ar - 1.0) * lr + ai * li) * inv
    fi = (ai * lr - (ar - 1.0) * li) * inv
    br = b_re_ref[...]
    bi = b_im_ref[...]
    bbr_ref[...] = fr * br - fi * bi
    bbi_ref[...] = fr * bi + fi * br


S5_LOCAL_STEPS = 3
S5_ROW_POW0 = S5_LOCAL_STEPS * SUBLANES
S5_TILE_POW0 = S5_ROW_POW0 + SUBLANES


def _s5_power_rows(n_tile_steps):
    exps, keep = [], []
    for k in range(S5_LOCAL_STEPS):
        for tau in range(SUBLANES):
            exps.append(float(1 << k))
            keep.append(1.0 if tau >= (1 << k) else 0.0)
    for tau in range(SUBLANES):
        exps.append(float(tau + 1))
        keep.append(1.0)
    for k in range(n_tile_steps):
        exps.append(float(SUBLANES << k))
        keep.append(1.0)
    while len(exps) % SUBLANES:
        exps.append(0.0)
        keep.append(0.0)
    return exps, keep


def _s5_discretise(lam_re, lam_im, log_step, b_re, b_im, n_tile_steps):
    gp = S5_GROUPS * S5_STATE
    step = jnp.repeat(jnp.exp(log_step), S5_STATE).reshape(1, gp)
    exps, keep = _s5_power_rows(n_tile_steps)
    rows = len(exps)
    b_re_t = jnp.transpose(b_re, (2, 0, 1)).reshape(S5_GROUP, gp)
    b_im_t = jnp.transpose(b_im, (2, 0, 1)).reshape(S5_GROUP, gp)
    return pl.pallas_call(
        _s5_disc_kernel,
        out_shape=[jax.ShapeDtypeStruct((rows, gp), F32), jax.ShapeDtypeStruct((rows, gp), F32),
                   jax.ShapeDtypeStruct((S5_GROUP, gp), F32), jax.ShapeDtypeStruct((S5_GROUP, gp), F32)],
        name="s5_discretise",
    )(lam_re.reshape(1, gp), lam_im.reshape(1, gp), step,
      jnp.asarray(exps, F32).reshape(rows, 1), jnp.asarray(keep, F32).reshape(rows, 1),
      b_re_t, b_im_t)


POOL_HALO = 16
S5_HALF = 256
S5_HALF_STATES = 1024


def _complex_axpy(xr, xi, cr, ci, sr, si):
    return xr + (cr * sr - ci * si), xi + (cr * si + ci * sr)


def _mix0_kernel(u_ref, pw_ref, ps_ref, bw_ref, cw_ref, apr_ref, api_ref, dsk_ref, wglu_ref,
                 o_ref, halo_scr, ext_scr, sr_scr, si_scr, xr_scr, xi_scr):
    tt = u_ref.shape[0]
    n_states = sr_scr.shape[1]
    blk = pl.program_id(1)

    @pl.when(blk == 0)
    def _():
        halo_scr[...] = jnp.zeros_like(halo_scr)
        sr_scr[...] = jnp.zeros_like(sr_scr)
        si_scr[...] = jnp.zeros_like(si_scr)

    up = u_ref[:, :POOL_WIDTH]
    ext_scr[0:POOL_HALO, :] = halo_scr[...]
    ext_scr[POOL_HALO:, :] = up
    halo_scr[...] = up[tt - POOL_HALO:, :]
    ext = ext_scr[...].astype(BF16)
    row = lax.broadcasted_iota(jnp.int32, (tt, tt + POOL_HALO), 0)
    col = lax.broadcasted_iota(jnp.int32, (tt, tt + POOL_HALO), 1)
    lag = row + POOL_HALO - col
    t_glob = (blk * tt + row + 1).astype(F32)
    pooled_out = []
    for gi, win in enumerate(POOL_WINDOWS):
        inv_count = 1.0 / jnp.minimum(t_glob, float(win))
        band = jnp.where((lag >= 0) & (lag < win), inv_count, 0.0) - jnp.where(lag == 0, 1.0, 0.0)
        pooled = jnp.dot(band.astype(BF16), ext[:, gi * POOL_GROUP:(gi + 1) * POOL_GROUP],
                         preferred_element_type=F32)
        pooled_out.append(jnp.dot(pooled.astype(BF16), pw_ref[gi], preferred_element_type=F32))
    y_pool = jnp.concatenate(pooled_out, axis=-1) * ps_ref[...]
    o_ref[:, :POOL_WIDTH] = y_pool.astype(o_ref.dtype)

    us = u_ref[:, POOL_WIDTH:]
    usb = us.astype(BF16)
    bu = [jnp.dot(usb[:, hf * S5_HALF:(hf + 1) * S5_HALF], bw_ref[hf], preferred_element_type=F32)
          for hf in range(2)]
    nt = tt // SUBLANES
    xr = jnp.concatenate([b[:, :S5_HALF_STATES] for b in bu], axis=-1).reshape(nt, SUBLANES, n_states)
    xi = jnp.concatenate([b[:, S5_HALF_STATES:] for b in bu], axis=-1).reshape(nt, SUBLANES, n_states)
    for k in range(S5_LOCAL_STEPS):
        cr = apr_ref[k * SUBLANES:(k + 1) * SUBLANES, :][None]
        ci = api_ref[k * SUBLANES:(k + 1) * SUBLANES, :][None]
        xr, xi = _complex_axpy(xr, xi, cr, ci, pltpu.roll(xr, 1 << k, 1), pltpu.roll(xi, 1 << k, 1))

    xr2 = xr.reshape(tt, n_states)
    xi2 = xi.reshape(tt, n_states)
    n_cb = n_states // LANES
    for cb in range(n_cb):
        xr_scr[cb] = xr2[:, cb * LANES:(cb + 1) * LANES]
        xi_scr[cb] = xi2[:, cb * LANES:(cb + 1) * LANES]
    tile_end = pl.ds(SUBLANES - 1, nt, stride=SUBLANES)
    er = jnp.concatenate([xr_scr[cb, tile_end, :] for cb in range(n_cb)], axis=-1)
    ei = jnp.concatenate([xi_scr[cb, tile_end, :] for cb in range(n_cb)], axis=-1)
    prev_r = sr_scr[...]
    prev_i = si_scr[...]
    tile_row = lax.broadcasted_iota(jnp.int32, (nt, n_states), 0)
    a8r = apr_ref[S5_TILE_POW0:S5_TILE_POW0 + 1, :]
    a8i = api_ref[S5_TILE_POW0:S5_TILE_POW0 + 1, :]
    er = er + jnp.where(tile_row == 0, a8r * prev_r - a8i * prev_i, 0.0)
    ei = ei + jnp.where(tile_row == 0, a8r * prev_i + a8i * prev_r, 0.0)
    k = 0
    while (1 << k) < nt:
        cr = apr_ref[S5_TILE_POW0 + k:S5_TILE_POW0 + k + 1, :]
        ci = api_ref[S5_TILE_POW0 + k:S5_TILE_POW0 + k + 1, :]
        pr = jnp.where(tile_row < (1 << k), 0.0, pltpu.roll(er, 1 << k, 0))
        pi = jnp.where(tile_row < (1 << k), 0.0, pltpu.roll(ei, 1 << k, 0))
        er, ei = _complex_axpy(er, ei, cr, ci, pr, pi)
        k += 1
    sr_scr[...] = er[nt - 1:nt, :]
    si_scr[...] = ei[nt - 1:nt, :]
    in_r = jnp.where(tile_row == 0, prev_r, pltpu.roll(er, 1, 0))
    in_i = jnp.where(tile_row == 0, prev_i, pltpu.roll(ei, 1, 0))

    rep_row = lax.broadcasted_iota(jnp.int32, (tt, nt), 0)
    rep_col = lax.broadcasted_iota(jnp.int32, (tt, nt), 1)
    rep = jnp.where(rep_row // SUBLANES == rep_col, 1.0, 0.0).astype(BF16)
    entering = jnp.concatenate([in_r, in_i], axis=-1)
    ent_hi = entering.astype(BF16)
    ent_lo = (entering - ent_hi.astype(F32)).astype(BF16)
    ent = (jnp.dot(rep, ent_hi, preferred_element_type=F32)
           + jnp.dot(rep, ent_lo, preferred_element_type=F32))
    cbr = ent[:, :n_states].reshape(nt, SUBLANES, n_states)
    cbi = ent[:, n_states:].reshape(nt, SUBLANES, n_states)
    pwr = apr_ref[S5_ROW_POW0:S5_ROW_POW0 + SUBLANES, :][None]
    pwi = api_ref[S5_ROW_POW0:S5_ROW_POW0 + SUBLANES, :][None]
    xr, xi = _complex_axpy(xr, xi, pwr, pwi, cbr, cbi)
    xr2 = xr.reshape(tt, n_states).astype(BF16)
    xi2 = xi.reshape(tt, n_states).astype(BF16)

    ys = []
    for hf in range(2):
        sl = slice(hf * S5_HALF_STATES, (hf + 1) * S5_HALF_STATES)
        xcat = jnp.concatenate([xr2[:, sl], xi2[:, sl]], axis=-1)
        ys.append(jnp.dot(xcat, cw_ref[hf], preferred_element_type=F32))
    y = jnp.concatenate(ys, axis=-1) + dsk_ref[...] * us
    y = jax.nn.gelu(y)
    gate = jnp.dot(y.astype(BF16), wglu_ref[...], preferred_element_type=F32)
    o_ref[:, POOL_WIDTH:] = (y * jax.nn.sigmoid(gate)).astype(o_ref.dtype)


def _pool_s5_mixer(u, bsz, pool_w, pool_scale, lam_re, lam_im, log_step, b_re, b_im, c_re, c_im,
                   d_skip, w_glu, *, tt):
    t = u.shape[0]
    seq = t // bsz
    gp = S5_GROUPS * S5_STATE
    apr, api, bbr, bbi = _s5_discretise(lam_re, lam_im, log_step, b_re, b_im,
                                        int(math.log2(tt // SUBLANES)))
    n_pow = apr.shape[0]

    gh = S5_GROUPS // 2
    eye = jnp.eye(gh, dtype=F32)

    def in_map(bb):
        bb = bb.reshape(S5_GROUP, 2, gh, S5_STATE)
        return jnp.einsum("hxgp,gk->xghkp", bb, eye).reshape(2, gh * S5_GROUP, gh * S5_STATE)

    bw = jnp.concatenate([in_map(bbr), in_map(bbi)], axis=-1).astype(BF16)

    def out_map(cc):
        cc = cc.reshape(2, gh, S5_GROUP, S5_STATE)
        return jnp.einsum("xghp,gk->xgpkh", cc, eye).reshape(2, gh * S5_STATE, gh * S5_GROUP)

    cw = jnp.concatenate([out_map(c_re), -out_map(c_im)], axis=1).astype(BF16)

    const2 = lambda b, i: (0, 0)
    const3 = lambda b, i: (0, 0, 0)
    nblk = seq // tt
    return pl.pallas_call(
        _mix0_kernel,
        grid=(bsz, nblk),
        in_specs=[
            pl.BlockSpec((tt, D_MODEL), lambda b, i: (b * nblk + i, 0)),
            pl.BlockSpec((len(POOL_WINDOWS), POOL_GROUP, POOL_GROUP), const3),
            pl.BlockSpec((1, POOL_WIDTH), const2),
            pl.BlockSpec((2, S5_HALF, 2 * S5_HALF_STATES), const3),
            pl.BlockSpec((2, 2 * S5_HALF_STATES, S5_HALF), const3),
            pl.BlockSpec((n_pow, gp), const2),
            pl.BlockSpec((n_pow, gp), const2),
            pl.BlockSpec((1, S5_WIDTH), const2),
            pl.BlockSpec((S5_WIDTH, S5_WIDTH), const2),
        ],
        out_specs=pl.BlockSpec((tt, D_MODEL), lambda b, i: (b * nblk + i, 0)),
        out_shape=jax.ShapeDtypeStruct((t, D_MODEL), BF16),
        scratch_shapes=[
            pltpu.VMEM((POOL_HALO, POOL_WIDTH), F32),
            pltpu.VMEM((tt + POOL_HALO, POOL_WIDTH), F32),
            pltpu.VMEM((1, gp), F32), pltpu.VMEM((1, gp), F32),
            pltpu.VMEM((gp // LANES, tt, LANES), F32), pltpu.VMEM((gp // LANES, tt, LANES), F32),
        ],
        compiler_params=_cparams(("arbitrary", "arbitrary")),
        name="pool_s5_mixer",
    )(u, pool_w.astype(BF16), pool_scale.reshape(1, POOL_WIDTH), bw, cw, apr, api,
      d_skip.reshape(1, S5_WIDTH), w_glu.astype(BF16))


CONV_HALO = SUBLANES
M2_GROUP_WIDTH = M2_HPG * M2_HEADDIM


def _causal_conv_silu(x_ref, halo_scr, full_scr, w_ref, b_ref, col0):
    lc = x_ref.shape[0]
    width = x_ref.shape[1]
    x = x_ref[...].astype(F32)
    full_scr[0:CONV_HALO, :] = halo_scr[...]
    full_scr[CONV_HALO:, :] = x
    halo_scr[...] = x[lc - CONV_HALO:, :]
    cols = slice(col0, col0 + width)
    acc = b_ref[:, cols] + w_ref[M2_CONV - 1:M2_CONV, cols] * x
    for k in range(M2_CONV - 1):
        off = CONV_HALO - (M2_CONV - 1) + k
        acc = acc + w_ref[k:k + 1, cols] * full_scr[off:off + lc, :]
    return acc * jax.nn.sigmoid(acc)


def _split_dot(v, e):
    hi = v.astype(BF16)
    lo = (v - hi.astype(F32)).astype(BF16)
    return (jnp.dot(hi, e, preferred_element_type=F32) + jnp.dot(lo, e, preferred_element_type=F32))


def _ssd_kernel(z_ref, xs_ref, bc_ref, dtr_ref, cw_ref, cb_ref, dtb_ref, alog_ref, dx_ref, ng_ref,
                e_ref, o_ref, halo_x, halo_bc, full_x, full_bc, st_scr, y_scr):
    lc = z_ref.shape[0]

    @pl.when(pl.program_id(1) == 0)
    def _():
        halo_x[...] = jnp.zeros_like(halo_x)
        halo_bc[...] = jnp.zeros_like(halo_bc)
        st_scr[...] = jnp.zeros_like(st_scr)

    xs = _causal_conv_silu(xs_ref, halo_x, full_x, cw_ref, cb_ref, 0)
    bc = _causal_conv_silu(bc_ref, halo_bc, full_bc, cw_ref, cb_ref, M2_INNER)

    lane = lax.broadcasted_iota(jnp.int32, (1, LANES), 1)
    x_dt = dtr_ref[...].astype(F32) + dtb_ref[...]
    dt = jnp.maximum(x_dt, 0.0) + jnp.log(1.0 + jnp.exp(-jnp.abs(x_dt)))
    a = jnp.where(lane < M2_HEADS, -jnp.exp(alog_ref[...]), 0.0)
    da = dt * a
    row = lax.broadcasted_iota(jnp.int32, (lc, lc), 0)
    col = lax.broadcasted_iota(jnp.int32, (lc, lc), 1)
    causal = col <= row
    cs = jnp.dot(jnp.where(causal, 1.0, 0.0), da, preferred_element_type=F32,
                 precision=lax.Precision.HIGHEST)
    cs_last = cs[lc - 1:lc, :]
    ecs = jnp.exp(cs)
    w_in = dt * jnp.exp(cs_last - cs)
    cs_t = cs.T
    dt_t = dt.T
    e = e_ref[...]
    ecs_x = _split_dot(ecs, e)
    w_x = _split_dot(w_in, e)
    xsb = xs.astype(BF16)
    xw = (xs * w_x).astype(BF16)

    for g in range(M2_GROUPS):
        bm = bc[:, g * M2_STATE:(g + 1) * M2_STATE].astype(BF16)
        cm = bc[:, M2_BC + g * M2_STATE:M2_BC + (g + 1) * M2_STATE].astype(BF16)
        cbm = lax.dot_general(cm, bm, (((1,), (1,)), ((), ())), preferred_element_type=F32)
        gcols = slice(g * M2_GROUP_WIDTH, (g + 1) * M2_GROUP_WIDTH)
        st = st_scr[g]
        y_off = jnp.dot(cm, st.astype(BF16), preferred_element_type=F32) * ecs_x[:, gcols]
        y_heads = []
        for jj in range(M2_HPG):
            j = g * M2_HPG + jj
            seg = cs[:, j:j + 1] - cs_t[j:j + 1, :]
            dec = jnp.exp(jnp.where(causal, seg, NEG_BIG))
            m = (cbm * dec * dt_t[j:j + 1, :]).astype(BF16)
            y_heads.append(jnp.dot(m, xsb[:, j * M2_HEADDIM:(j + 1) * M2_HEADDIM],
                                   preferred_element_type=F32))
        y_g = jnp.concatenate(y_heads, axis=-1) + y_off
        st_scr[g] = st * ecs_x[lc - 1:lc, gcols] + lax.dot_general(
            bm, xw[:, gcols], (((0,), (0,)), ((), ())), preferred_element_type=F32)
        zg = z_ref[:, gcols].astype(F32)
        y_g = (y_g + dx_ref[:, gcols] * xs[:, gcols]) * (zg * jax.nn.sigmoid(zg))
        y_scr[:, gcols] = y_g

    y = y_scr[...]
    ms = jnp.mean(y * y, axis=-1, keepdims=True)
    o_ref[...] = (y * lax.rsqrt(ms + EPS) * ng_ref[...]).astype(o_ref.dtype)


def _ssd_mixer(proj, bsz, conv_w, conv_b, dt_bias, a_log, d_skip, norm_g, *, lc):
    t = proj.shape[0]
    seq = t // bsz
    nblk = seq // lc
    half = M2_INNER
    pad = LANES - M2_HEADS
    dtb = jnp.pad(dt_bias, (0, pad)).reshape(1, LANES)
    alog = jnp.pad(a_log, (0, pad)).reshape(1, LANES)
    dx = jnp.repeat(d_skip, M2_HEADDIM).reshape(1, M2_INNER)
    heads = jnp.arange(LANES, dtype=jnp.int32)[:, None]
    chans = jnp.arange(M2_INNER, dtype=jnp.int32)[None, :] // M2_HEADDIM
    expand = (heads == chans).astype(BF16)
    const2 = lambda b, i: (0, 0)
    return pl.pallas_call(
        _ssd_kernel,
        grid=(bsz, nblk),
        in_specs=[
            pl.BlockSpec((lc, half), lambda b, i: (b * nblk + i, 0)),
            pl.BlockSpec((lc, half), lambda b, i: (b * nblk + i, 1)),
            pl.BlockSpec((lc, half), lambda b, i: (b * nblk + i, 2)),
            pl.BlockSpec((lc, LANES), lambda b, i: (b * nblk + i, 3 * half // LANES)),
            pl.BlockSpec((M2_CONV, M2_CONV_DIM), const2),
            pl.BlockSpec((1, M2_CONV_DIM), const2),
            pl.BlockSpec((1, LANES), const2),
            pl.BlockSpec((1, LANES), const2),
            pl.BlockSpec((1, M2_INNER), const2),
            pl.BlockSpec((1, M2_INNER), const2),
            pl.BlockSpec((LANES, M2_INNER), const2),
        ],
        out_specs=pl.BlockSpec((lc, M2_INNER), lambda b, i: (b * nblk + i, 0)),
        out_shape=jax.ShapeDtypeStruct((t, M2_INNER), BF16),
        scratch_shapes=[
            pltpu.VMEM((CONV_HALO, half), F32), pltpu.VMEM((CONV_HALO, half), F32),
            pltpu.VMEM((lc + CONV_HALO, half), F32), pltpu.VMEM((lc + CONV_HALO, half), F32),
            pltpu.VMEM((M2_GROUPS, M2_STATE, M2_GROUP_WIDTH), F32),
            pltpu.VMEM((lc, M2_INNER), F32),
        ],
        compiler_params=_cparams(("arbitrary", "arbitrary")),
        name="ssd_mixer",
    )(proj, proj, proj, proj, conv_w, conv_b.reshape(1, M2_CONV_DIM), dtb, alog, dx,
      norm_g.reshape(1, M2_INNER), expand)


def kernel(x, c, ada_w, ada_b, norm_g, mix_w_in, pool_w, pool_scale, s5_lam_re, s5_lam_im, s5_log_step, s5_b_re, s5_b_im, s5_c_re, s5_c_im, s5_d, s5_w_glu, mix_w_out, ffn_w_gate, ffn_w_up, ffn_w_down, m2_w_in, m2_conv_w, m2_conv_b, m2_dt_bias, m2_a_log, m2_d, m2_norm_g, m2_w_out, moe_w_router, moe_b_router, moe_w_gate, moe_w_up, moe_w_down):
    bsz, seq, d = x.shape
    t = bsz * seq
    xt = x.reshape(t, d)

    mod = _ada_modulation(c, ada_w, ada_b)

    def mod_vecs(layer):
        return [mod[layer, :, k * d:(k + 1) * d].reshape(bsz, 1, d) for k in range(6)]

    def gvec(layer, k):
        return norm_g[layer, k].reshape(1, d)

    sh_m, sc_m, g_m, sh_f, sc_f, g_f = mod_vecs(0)
    u = _norm_matmul(xt, gvec(0, 0), sc_m, sh_m, mix_w_in[0].astype(BF16),
                     tm=1024, tn=1024, out_dtype=F32, name="mix_in_proj")
    ycat = _pool_s5_mixer(u, bsz, pool_w[0], pool_scale[0], s5_lam_re[0], s5_lam_im[0],
                          s5_log_step[0], s5_b_re[0], s5_b_im[0], s5_c_re[0], s5_c_im[0],
                          s5_d[0], s5_w_glu[0], tt=256)
    xt = _matmul_post(ycat, mix_w_out[0].astype(BF16), xt, gvec(0, 1), g_m,
                      tm=1024, name="mix_out_proj")
    xt = _ffn_sublayer(xt, gvec(0, 2), sc_f, sh_f, ffn_w_gate[0].astype(BF16),
                       ffn_w_up[0].astype(BF16), ffn_w_down[0].astype(BF16), gvec(0, 3), g_f,
                       tm=1024, th=1408)

    sh_m, sc_m, g_m, sh_f, sc_f, g_f = mod_vecs(1)
    proj_pad = 6272
    w_in = jnp.zeros((d, proj_pad), BF16).at[:, :M2_PROJ].set(m2_w_in[0].astype(BF16))
    ne, dh = N_EXPERTS, EXPERT_HIDDEN
    proj, (wg_b, wu_b, wd_b) = _norm_matmul(
        xt, gvec(1, 0), sc_m, sh_m, w_in, tm=1024, tn=896, out_dtype=BF16, name="m2_in_proj",
        side=(moe_w_gate[0].reshape(ne * d, dh), moe_w_up[0].reshape(ne * d, dh),
              moe_w_down[0].reshape(ne * dh, d)), side_steps=32)
    y = _ssd_mixer(proj, bsz, m2_conv_w[0], m2_conv_b[0], m2_dt_bias[0], m2_a_log[0], m2_d[0],
                   m2_norm_g[0], lc=128)
    xt = _matmul_post(y, m2_w_out[0].astype(BF16), xt, gvec(1, 1), g_m, tm=1024,
                      name="m2_out_proj")
    xt = _moe_sublayer(xt, gvec(1, 2), sc_f, sh_f, moe_w_router[0], moe_b_router[0],
                       wg_b.reshape(ne, d, dh), wu_b.reshape(ne, d, dh), wd_b.reshape(ne, dh, d),
                       gvec(1, 3), g_f)
    return xt.reshape(bsz, seq, d)
```

```python
import functools
import math

import jax
import jax.numpy as jnp
from jax import lax
from jax.experimental import pallas as pl
from jax.experimental.pallas import tpu as pltpu

F32 = jnp.float32
BF16 = jnp.bfloat16

D_MODEL = 1024
EPS = 1e-6
POOL_WIDTH = 512
POOL_WINDOWS = (2, 4, 8, 16)
POOL_GROUP = 128
S5_WIDTH = 512
S5_GROUP = 16
S5_GROUPS = 32
S5_STATE = 64
M2_INNER = 2048
M2_HEADDIM = 64
M2_HEADS = 32
M2_GROUPS = 8
M2_HPG = 4
M2_STATE = 128
M2_CONV = 4
M2_BC = 1024
M2_CONV_DIM = 4096
M2_PROJ = 6176
FFN_HIDDEN = 2816
N_EXPERTS = 8
EXPERT_HIDDEN = 3584

LANES = 128
SUBLANES = 8
VMEM_LIMIT_BYTES = 56 * 1024 * 1024

NEG_BIG = -1e30


def _cparams(sem):
    return pltpu.CompilerParams(dimension_semantics=sem, vmem_limit_bytes=VMEM_LIMIT_BYTES)


def _modulated_norm(x, g, scale, shift):
    ms = jnp.mean(x * x, axis=-1, keepdims=True)
    return (x * lax.rsqrt(ms + EPS) * g) * (1.0 + scale) + shift


def _post_norm_residual(x, y, ng, gate):
    ms = jnp.mean(y * y, axis=-1, keepdims=True)
    return x + gate * (y * lax.rsqrt(ms + EPS) * ng)


def _ada_kernel(c_ref, w_ref, b_ref, o_ref):
    c = c_ref[...]
    a = c * jax.nn.sigmoid(c)
    o_ref[0] = jnp.dot(a.astype(BF16), w_ref[0].astype(BF16),
                       preferred_element_type=F32) + b_ref[0]


def _ada_modulation(c, ada_w, ada_b):
    depth, d, n = ada_w.shape
    b = c.shape[0]
    c_pad = jnp.zeros((SUBLANES, d), F32).at[:b].set(c)
    tn = 1024
    out = pl.pallas_call(
        _ada_kernel,
        grid=(depth, n // tn),
        in_specs=[
            pl.BlockSpec((SUBLANES, d), lambda l, j: (0, 0)),
            pl.BlockSpec((1, d, tn), lambda l, j: (l, 0, j)),
            pl.BlockSpec((1, 1, tn), lambda l, j: (l, 0, j)),
        ],
        out_specs=pl.BlockSpec((1, SUBLANES, tn), lambda l, j: (l, 0, j)),
        out_shape=jax.ShapeDtypeStruct((depth, SUBLANES, n), F32),
        compiler_params=_cparams(("arbitrary", "arbitrary")),
        name="ada_modulation",
    )(c_pad, ada_w, ada_b.reshape(depth, 1, n))
    return out[:, :b]


def _norm_mm_kernel(x_ref, g_ref, sc_ref, sh_ref, w_ref, *rest, n_side, side_steps):
    side_in = rest[:n_side]
    o_ref = rest[n_side]
    side_out = rest[n_side + 1:2 * n_side + 1]
    h_scr = rest[2 * n_side + 1]

    @pl.when(pl.program_id(1) == 0)
    def _():
        h = _modulated_norm(x_ref[...], g_ref[...], sc_ref[0], sh_ref[0])
        h_scr[...] = h.astype(BF16)

    o_ref[...] = jnp.dot(h_scr[...], w_ref[...], preferred_element_type=F32).astype(o_ref.dtype)

    step = pl.program_id(0) * pl.num_programs(1) + pl.program_id(1)
    for k in range(n_side):
        @pl.when(jnp.logical_and(step >= k * side_steps, step < (k + 1) * side_steps))
        def _(k=k):
            side_out[k][...] = side_in[k][...].astype(BF16)


def _norm_matmul(x, g, scale, shift, w, *, tm, tn, out_dtype, name, side=(), side_steps=1):
    t, d = x.shape
    n = w.shape[1]
    n_j = n // tn
    tiles_per_batch = t // scale.shape[0] // tm
    assert len(side) * side_steps <= (t // tm) * n_j

    def side_spec(k, arr):
        rows = arr.shape[0] // side_steps
        return pl.BlockSpec(
            (rows, arr.shape[1]),
            lambda i, j: (jnp.clip(i * n_j + j - k * side_steps, 0, side_steps - 1), 0))

    side_specs = [side_spec(k, a) for k, a in enumerate(side)]
    outs = pl.pallas_call(
        functools.partial(_norm_mm_kernel, n_side=len(side), side_steps=side_steps),
        grid=(t // tm, n_j),
        in_specs=[
            pl.BlockSpec((tm, d), lambda i, j: (i, 0)),
            pl.BlockSpec((1, d), lambda i, j: (0, 0)),
            pl.BlockSpec((1, 1, d), lambda i, j: (i // tiles_per_batch, 0, 0)),
            pl.BlockSpec((1, 1, d), lambda i, j: (i // tiles_per_batch, 0, 0)),
            pl.BlockSpec((d, tn), lambda i, j: (0, j)),
        ] + side_specs,
        out_specs=[pl.BlockSpec((tm, tn), lambda i, j: (i, j))] + side_specs,
        out_shape=[jax.ShapeDtypeStruct((t, n), out_dtype)]
                  + [jax.ShapeDtypeStruct(a.shape, BF16) for a in side],
        scratch_shapes=[pltpu.VMEM((tm, d), BF16)],
        compiler_params=_cparams(("arbitrary", "arbitrary")),
        name=name,
    )(x, g, scale, shift, w, *side)
    return (outs[0], outs[1:]) if side else outs[0]


def _mm_post_kernel(y_ref, w_ref, x_ref, ng_ref, gate_ref, o_ref):
    y = jnp.dot(y_ref[...], w_ref[...], preferred_element_type=F32)
    o_ref[...] = _post_norm_residual(x_ref[...], y, ng_ref[...], gate_ref[0])


def _matmul_post(y, w, x, ng, gate, *, tm, name):
    t, k = y.shape
    d = w.shape[1]
    tiles_per_batch = t // gate.shape[0] // tm
    return pl.pallas_call(
        _mm_post_kernel,
        grid=(t // tm,),
        in_specs=[
            pl.BlockSpec((tm, k), lambda i: (i, 0)),
            pl.BlockSpec((k, d), lambda i: (0, 0)),
            pl.BlockSpec((tm, d), lambda i: (i, 0)),
            pl.BlockSpec((1, d), lambda i: (0, 0)),
            pl.BlockSpec((1, 1, d), lambda i: (i // tiles_per_batch, 0, 0)),
        ],
        out_specs=pl.BlockSpec((tm, d), lambda i: (i, 0)),
        out_shape=jax.ShapeDtypeStruct((t, d), F32),
        compiler_params=_cparams(("arbitrary",)),
        name=name,
    )(y, w, x, ng, gate)


def _ffn_kernel(x_ref, g_ref, sc_ref, sh_ref, wg_ref, wu_ref, wd_ref, ng_ref, gate_ref,
                o_ref, h_scr, acc_scr):
    j = pl.program_id(1)

    @pl.when(j == 0)
    def _():
        h = _modulated_norm(x_ref[...], g_ref[...], sc_ref[0], sh_ref[0])
        h_scr[...] = h.astype(BF16)
        acc_scr[...] = jnp.zeros_like(acc_scr)

    h = h_scr[...]
    gt = jnp.dot(h, wg_ref[...], preferred_element_type=F32)
    up = jnp.dot(h, wu_ref[...], preferred_element_type=F32)
    act = (gt * jax.nn.sigmoid(gt) * up).astype(BF16)
    acc_scr[...] += jnp.dot(act, wd_ref[...], preferred_element_type=F32)

    @pl.when(j == pl.num_programs(1) - 1)
    def _():
        o_ref[...] = _post_norm_residual(x_ref[...], acc_scr[...], ng_ref[...], gate_ref[0])


def _ffn_sublayer(x, g, scale, shift, wg, wu, wd, ng, gate, *, tm, th):
    t, d = x.shape
    hid = wg.shape[1]
    tiles_per_batch = t // scale.shape[0] // tm
    bvec = pl.BlockSpec((1, 1, d), lambda i, j: (i // tiles_per_batch, 0, 0))
    return pl.pallas_call(
        _ffn_kernel,
        grid=(t // tm, hid // th),
        in_specs=[
            pl.BlockSpec((tm, d), lambda i, j: (i, 0)),
            pl.BlockSpec((1, d), lambda i, j: (0, 0)),
            bvec, bvec,
            pl.BlockSpec((d, th), lambda i, j: (0, j)),
            pl.BlockSpec((d, th), lambda i, j: (0, j)),
            pl.BlockSpec((th, d), lambda i, j: (j, 0)),
            pl.BlockSpec((1, d), lambda i, j: (0, 0)),
            bvec,
        ],
        out_specs=pl.BlockSpec((tm, d), lambda i, j: (i, 0)),
        out_shape=jax.ShapeDtypeStruct((t, d), F32),
        scratch_shapes=[pltpu.VMEM((tm, d), BF16), pltpu.VMEM((tm, d), F32)],
        compiler_params=_cparams(("arbitrary", "arbitrary")),
        name="ffn_sublayer",
    )(x, g, scale, shift, wg, wu, wd, ng, gate)


def _router_kernel(x_ref, g_ref, sc_ref, sh_ref, wr_ref, br_ref, h_ref, meta_ref, cnt_ref):
    tm = x_ref.shape[0]

    h = _modulated_norm(x_ref[...], g_ref[...], sc_ref[0], sh_ref[0])
    h_hi = h.astype(BF16)
    h_ref[...] = h_hi
    h_lo = (h - h_hi.astype(F32)).astype(BF16)
    logits = (jnp.dot(h_hi, wr_ref[0], preferred_element_type=F32)
              + jnp.dot(h_lo, wr_ref[0], preferred_element_type=F32)
              + jnp.dot(h_hi, wr_ref[1], preferred_element_type=F32)) + br_ref[...]
    lane = lax.broadcasted_iota(jnp.int32, (tm, LANES), 1).astype(F32)
    m1 = jnp.max(logits, axis=-1, keepdims=True)
    i1 = jnp.min(jnp.where(logits == m1, lane, float(LANES)), axis=-1, keepdims=True)
    oh1 = lane == i1
    rest = jnp.where(oh1, NEG_BIG * 2.0, logits)
    m2 = jnp.max(rest, axis=-1, keepdims=True)
    i2 = jnp.min(jnp.where(rest == m2, lane, float(LANES)), axis=-1, keepdims=True)
    oh2 = lane == i2
    e = jnp.exp(m2 - m1)
    w1 = 1.0 / (1.0 + e)
    w2 = e / (1.0 + e)

    picks = jnp.where(oh1, 1.0, 0.0) + jnp.where(oh2, 1.0, 0.0)
    row = lax.broadcasted_iota(jnp.int32, (tm, tm), 0)
    col = lax.broadcasted_iota(jnp.int32, (tm, tm), 1)
    lower = jnp.where(col < row, 1.0, 0.0).astype(BF16)
    before = jnp.dot(lower, picks.astype(BF16), preferred_element_type=F32)
    tile_cnt = jnp.sum(picks, axis=0, keepdims=True)
    ea = lax.broadcasted_iota(jnp.int32, (LANES, LANES), 0)
    eb = lax.broadcasted_iota(jnp.int32, (LANES, LANES), 1)
    seg_rows = jnp.floor((tile_cnt + (SUBLANES - 1.0)) * (1.0 / SUBLANES)) * SUBLANES
    seg_off = jnp.dot(jnp.broadcast_to(seg_rows, (SUBLANES, LANES)), jnp.where(ea < eb, 1.0, 0.0),
                      preferred_element_type=F32, precision=lax.Precision.HIGHEST)[0:1, :]
    local = before + seg_off
    loc1 = jnp.sum(jnp.where(oh1, local, 0.0), axis=-1, keepdims=True)
    loc2 = jnp.sum(jnp.where(oh2, local, 0.0), axis=-1, keepdims=True)
    cnt_ref[0] = tile_cnt

    meta = jnp.where(lane == 0.0, i1, 0.0)
    meta = jnp.where(lane == 1.0, i2, meta)
    meta = jnp.where(lane == 2.0, w1, meta)
    meta = jnp.where(lane == 3.0, w2, meta)
    meta = jnp.where(lane == 4.0, loc1, meta)
    meta = jnp.where(lane == 5.0, loc2, meta)
    meta_ref[...] = meta


def _router(x, g, scale, shift, w_router, b_router, *, tm):
    t, d = x.shape
    tiles_per_batch = t // scale.shape[0] // tm
    wr = jnp.zeros((d, LANES), F32).at[:, :N_EXPERTS].set(w_router)
    wr_hi = wr.astype(BF16)
    wr = jnp.stack([wr_hi, (wr - wr_hi.astype(F32)).astype(BF16)])
    br = jnp.full((1, LANES), NEG_BIG, F32).at[0, :N_EXPERTS].set(b_router)
    bvec = pl.BlockSpec((1, 1, d), lambda i: (i // tiles_per_batch, 0, 0))
    return pl.pallas_call(
        _router_kernel,
        grid=(t // tm,),
        in_specs=[
            pl.BlockSpec((tm, d), lambda i: (i, 0)),
            pl.BlockSpec((1, d), lambda i: (0, 0)),
            bvec, bvec,
            pl.BlockSpec((2, d, LANES), lambda i: (0, 0, 0)),
            pl.BlockSpec((1, LANES), lambda i: (0, 0)),
        ],
        out_specs=[
            pl.BlockSpec((tm, d), lambda i: (i, 0)),
            pl.BlockSpec((tm, LANES), lambda i: (i, 0)),
            pl.BlockSpec((1, 1, LANES), lambda i: (i, 0, 0)),
        ],
        out_shape=[
            jax.ShapeDtypeStruct((t, d), BF16),
            jax.ShapeDtypeStruct((t, LANES), F32),
            jax.ShapeDtypeStruct((t // tm, 1, LANES), F32),
        ],
        compiler_params=_cparams(("arbitrary",)),
        name="moe_router",
    )(x, g, scale, shift, wr, br)


SEG_FIELDS = 3
SEG_PAD_ROWS = N_EXPERTS * SUBLANES


def _segment_copies(scal_ref, tile, tm, per_copy):
    base = tile * (SEG_FIELDS * N_EXPERTS)
    for e in range(N_EXPERTS):
        n = scal_ref[base + e]
        g0 = scal_ref[base + N_EXPERTS + e]
        l0 = scal_ref[base + 2 * N_EXPERTS + e]
        done = 0
        b = tm
        while b >= SUBLANES:
            take = n & b

            @pl.when(take != 0)
            def _(b=b, done=done, g0=g0, l0=l0):
                per_copy(pl.multiple_of(l0 + done, SUBLANES), pl.multiple_of(g0 + done, SUBLANES), b)

            done = done + take
            b //= 2


def _dispatch_kernel(scal_ref, tail_ref, h_ref, meta_ref, xs_ref, ws_ref,
                     seg_scr, wseg_scr, zero_scr, sem, zsem, *, n_tiles, max_unused):
    tile = pl.program_id(0)
    tm = h_ref.shape[0]
    ns = seg_scr.shape[1]
    par = tile % 2

    def segment_dmas(which, buf):
        def descriptors(local, glob, n):
            return (pltpu.make_async_copy(wseg_scr.at[buf, pl.ds(local, n), :],
                                          ws_ref.at[pl.ds(glob, n), :], sem.at[buf]),
                    pltpu.make_async_copy(seg_scr.at[buf, pl.ds(local, n), :],
                                          xs_ref.at[pl.ds(glob, n), :], sem.at[buf]))
        return descriptors

    def wait_segments(which, buf):
        descriptors = segment_dmas(which, buf)
        _segment_copies(scal_ref, which, tm, lambda l, g, n: [c.wait() for c in descriptors(l, g, n)])

    @pl.when(tile == 0)
    def _():
        zero_scr[...] = jnp.zeros_like(zero_scr)
        fills = []
        n_slot_tiles = xs_ref.shape[0] // tm
        tails = [(tail_ref[N_EXPERTS + e] > 0, pl.multiple_of(tail_ref[e], tm))
                 for e in range(N_EXPERTS)]
        tails += [(k >= tail_ref[2 * N_EXPERTS], k * tm)
                  for k in range(n_slot_tiles - max_unused, n_slot_tiles)]
        for nonempty, tail in tails:
            fills.append((nonempty,
                          pltpu.make_async_copy(zero_scr, xs_ref.at[pl.ds(tail, tm), :], zsem),
                          pltpu.make_async_copy(zero_scr.at[:, 0:LANES],
                                                ws_ref.at[pl.ds(tail, tm), :], zsem)))
        for nonempty, fill_x, fill_w in fills:
            @pl.when(nonempty)
            def _(fill_x=fill_x, fill_w=fill_w):
                fill_x.start()
                fill_w.start()
        for nonempty, fill_x, fill_w in fills:
            @pl.when(nonempty)
            def _(fill_x=fill_x, fill_w=fill_w):
                fill_x.wait()
                fill_w.wait()

    meta = meta_ref[...]
    meta_t = meta.T
    slot = lax.broadcasted_iota(jnp.int32, (ns, tm), 0).astype(F32)
    g1 = jnp.where(slot == meta_t[4:5, :], 1.0, 0.0).astype(BF16)
    g2 = jnp.where(slot == meta_t[5:6, :], 1.0, 0.0).astype(BF16)
    rows = jnp.dot(g1 + g2, h_ref[...], preferred_element_type=F32)
    m_hi = meta.astype(BF16)
    m_lo = (meta - m_hi.astype(F32)).astype(BF16)
    first = jnp.dot(g1, m_hi, preferred_element_type=F32) + jnp.dot(g1, m_lo, preferred_element_type=F32)
    second = jnp.dot(g2, m_hi, preferred_element_type=F32) + jnp.dot(g2, m_lo, preferred_element_type=F32)
    weight = first[:, 2:3] + second[:, 3:4]

    @pl.when(tile >= 2)
    def _():
        wait_segments(tile - 2, par)

    seg_scr[par] = rows
    wseg_scr[par] = jnp.broadcast_to(weight, (ns, LANES))
    descriptors = segment_dmas(tile, par)
    _segment_copies(scal_ref, tile, tm, lambda l, g, n: [c.start() for c in descriptors(l, g, n)])

    @pl.when(tile == n_tiles - 1)
    def _():
        wait_segments(tile, par)
        if n_tiles > 1:
            wait_segments(tile - 1, 1 - par)


def _dispatch(h, meta, seg_table, tail_table, n_slots, *, tm):
    t, d = h.shape
    grid_spec = pltpu.PrefetchScalarGridSpec(
        num_scalar_prefetch=2,
        grid=(t // tm,),
        in_specs=[
            pl.BlockSpec((tm, d), lambda i, sc, tl: (i, 0)),
            pl.BlockSpec((tm, LANES), lambda i, sc, tl: (i, 0)),
        ],
        out_specs=[pl.BlockSpec(memory_space=pl.ANY), pl.BlockSpec(memory_space=pl.ANY)],
        scratch_shapes=[
            pltpu.VMEM((2, 2 * tm + SEG_PAD_ROWS, d), F32),
            pltpu.VMEM((2, 2 * tm + SEG_PAD_ROWS, LANES), F32),
            pltpu.VMEM((tm, d), F32),
            pltpu.SemaphoreType.DMA((2,)), pltpu.SemaphoreType.DMA(()),
        ],
    )
    return pl.pallas_call(
        functools.partial(_dispatch_kernel, n_tiles=t // tm,
                          max_unused=n_slots // tm - 2 * t // tm),
        grid_spec=grid_spec,
        out_shape=[jax.ShapeDtypeStruct((n_slots, d), F32),
                   jax.ShapeDtypeStruct((n_slots, LANES), F32)],
        compiler_params=_cparams(("arbitrary",)),
        name="moe_dispatch",
    )(seg_table, tail_table, h, meta)


def _expert_kernel(te_ref, nu_ref, xs_ref, ws_ref, wg_ref, wu_ref, wd_ref, ys_ref,
                   xb_scr, acc_scr):
    i = pl.program_id(0)
    j = pl.program_id(1)
    last = pl.num_programs(1) - 1
    used = i < nu_ref[0]

    @pl.when(used)
    def _():
        @pl.when(j == 0)
        def _():
            xb_scr[...] = xs_ref[...].astype(BF16)
            acc_scr[...] = jnp.zeros_like(acc_scr)

        x = xb_scr[...]
        gt = jnp.dot(x, wg_ref[0], preferred_element_type=F32)
        up = jnp.dot(x, wu_ref[0], preferred_element_type=F32)
        act = (gt * jax.nn.sigmoid(gt) * up).astype(BF16)
        acc_scr[...] += jnp.dot(act, wd_ref[0], preferred_element_type=F32)

        @pl.when(j == last)
        def _():
            ys_ref[...] = (acc_scr[...] * ws_ref[:, 0:1]).astype(ys_ref.dtype)

    @pl.when(jnp.logical_and(jnp.logical_not(used), j == last))
    def _():
        ys_ref[...] = jnp.zeros_like(ys_ref)


def _experts(xs, ws, tile_expert, n_used, wg, wu, wd, *, tm, th):
    n_slots, d = xs.shape
    hid = wg.shape[2]
    nt = n_slots // tm

    def hidden_step(i, j, nu):
        return jnp.where(i < nu[0], j, 0)

    def slot_tile(i, nu):
        return jnp.where(i < nu[0], i, 0)

    grid_spec = pltpu.PrefetchScalarGridSpec(
        num_scalar_prefetch=2,
        grid=(nt, hid // th),
        in_specs=[
            pl.BlockSpec((tm, d), lambda i, j, te, nu: (slot_tile(i, nu), 0)),
            pl.BlockSpec((tm, LANES), lambda i, j, te, nu: (slot_tile(i, nu), 0)),
            pl.BlockSpec((1, d, th), lambda i, j, te, nu: (te[i], 0, hidden_step(i, j, nu))),
            pl.BlockSpec((1, d, th), lambda i, j, te, nu: (te[i], 0, hidden_step(i, j, nu))),
            pl.BlockSpec((1, th, d), lambda i, j, te, nu: (te[i], hidden_step(i, j, nu), 0)),
        ],
        out_specs=pl.BlockSpec((tm, d), lambda i, j, te, nu: (i, 0)),
        scratch_shapes=[pltpu.VMEM((tm, d), BF16), pltpu.VMEM((tm, d), F32)],
    )
    return pl.pallas_call(
        _expert_kernel,
        grid_spec=grid_spec,
        out_shape=jax.ShapeDtypeStruct((n_slots, d), F32),
        compiler_params=_cparams(("arbitrary", "arbitrary")),
        name="moe_experts",
    )(tile_expert, n_used, xs, ws, wg, wu, wd)


def _combine_kernel(scal_ref, ys_ref, meta_ref, x_ref, ng_ref, gate_ref, o_ref, buf, sem,
                    *, n_tiles):
    tile = pl.program_id(0)
    tm = x_ref.shape[0]
    ns = buf.shape[1]
    par = tile % 2

    def fetch(which, slot, wait):
        def copy_in(local, glob, n):
            dma = pltpu.make_async_copy(ys_ref.at[pl.ds(glob, n), :],
                                        buf.at[slot, pl.ds(local, n), :], sem.at[slot])
            dma.wait() if wait else dma.start()
        _segment_copies(scal_ref, which, tm, copy_in)

    @pl.when(tile == 0)
    def _():
        buf[...] = jnp.zeros_like(buf)
        fetch(tile, par, False)

    @pl.when(tile + 1 < n_tiles)
    def _():
        fetch(tile + 1, 1 - par, False)

    fetch(tile, par, True)
    meta = meta_ref[...]
    slot_id = lax.broadcasted_iota(jnp.int32, (tm, ns), 1).astype(F32)
    pick = jnp.where((slot_id == meta[:, 4:5]) | (slot_id == meta[:, 5:6]), 1.0, 0.0).astype(BF16)
    y = jnp.dot(pick, buf[par].astype(BF16), preferred_element_type=F32)
    o_ref[...] = _post_norm_residual(x_ref[...], y, ng_ref[...], gate_ref[0])


def _combine(ys, meta, seg_table, x, ng, gate, *, tm):
    t, d = x.shape
    tiles_per_batch = t // gate.shape[0] // tm
    grid_spec = pltpu.PrefetchScalarGridSpec(
        num_scalar_prefetch=1,
        grid=(t // tm,),
        in_specs=[
            pl.BlockSpec(memory_space=pl.ANY),
            pl.BlockSpec((tm, LANES), lambda i, sc: (i, 0)),
            pl.BlockSpec((tm, d), lambda i, sc: (i, 0)),
            pl.BlockSpec((1, d), lambda i, sc: (0, 0)),
            pl.BlockSpec((1, 1, d), lambda i, sc: (i // tiles_per_batch, 0, 0)),
        ],
        out_specs=pl.BlockSpec((tm, d), lambda i, sc: (i, 0)),
        scratch_shapes=[pltpu.VMEM((2, 2 * tm + SEG_PAD_ROWS, d), F32),
                        pltpu.SemaphoreType.DMA((2,))],
    )
    return pl.pallas_call(
        functools.partial(_combine_kernel, n_tiles=t // tm),
        grid_spec=grid_spec,
        out_shape=jax.ShapeDtypeStruct((t, d), F32),
        compiler_params=_cparams(("arbitrary",)),
        name="moe_combine",
    )(seg_table, ys, meta, x, ng, gate)


def _moe_sublayer(x, g, scale, shift, w_router, b_router, wg, wu, wd, ng, gate):
    t = x.shape[0]
    tm, tm_e, th = 512, 512, 1792
    h, meta, counts = _router(x, g, scale, shift, w_router, b_router, tm=tm)

    assert tm == tm_e
    tile_cnt = counts[:, 0, :N_EXPERTS].astype(jnp.int32)
    seg_rows = ((tile_cnt + SUBLANES - 1) // SUBLANES) * SUBLANES
    cnt = jnp.sum(seg_rows, axis=0)
    padded = ((cnt + tm_e - 1) // tm_e) * tm_e
    ends = jnp.cumsum(padded)
    offs = ends - padded
    glob_start = offs[None, :] + jnp.cumsum(seg_rows, axis=0) - seg_rows
    local_start = jnp.cumsum(seg_rows, axis=1) - seg_rows
    seg_table = jnp.concatenate([seg_rows, glob_start, local_start], axis=1).reshape(-1)
    tail_table = jnp.concatenate([ends - tm_e, padded, ends[-1:] // tm_e]).astype(jnp.int32)
    n_tok_tiles = t // tm
    nt = -(-(2 * t + n_tok_tiles * SEG_PAD_ROWS) // tm_e) + N_EXPERTS
    starts = jnp.arange(nt, dtype=jnp.int32) * tm_e
    tile_expert = jnp.minimum(jnp.sum(starts[:, None] >= ends[None, :], axis=1), N_EXPERTS - 1)
    tile_expert = tile_expert.astype(jnp.int32)
    n_used = (ends[-1:] // tm_e).astype(jnp.int32)

    xs, ws = _dispatch(h, meta, seg_table, tail_table, nt * tm_e, tm=tm)
    ys = _experts(xs, ws, tile_expert, n_used, wg, wu, wd, tm=tm_e, th=th)
    return _combine(ys, meta, seg_table, x, ng, gate, tm=tm)


def _s5_disc_kernel(lam_re_ref, lam_im_ref, step_ref, pow_ref, keep_ref, b_re_ref, b_im_ref,
                    apr_ref, api_ref, bbr_ref, bbi_ref):
    lr = jnp.minimum(lam_re_ref[...], -1e-4)
    li = lam_im_ref[...]
    step = step_ref[...]
    m = pow_ref[...]
    mag = jnp.exp(lr * step * m) * keep_ref[...]
    ang = li * step * m
    apr_ref[...] = mag * jnp.cos(ang)
    api_ref[...] = mag * jnp.sin(ang)
    mag1 = jnp.exp(lr * step)
    ar = mag1 * jnp.cos(li * step)
    ai = mag1 * jnp.sin(li * step)
    inv = 1.0 / (lr * lr + li * li)
    fr = ((ar - 1.0) * lr + ai * li) * inv
    fi = (ai * lr - (ar - 1.0) * li) * inv
    br = b_re_ref[...]
    bi = b_im_ref[...]
    bbr_ref[...] = fr * br - fi * bi
    bbi_ref[...] = fr * bi + fi * br


S5_LOCAL_STEPS = 3
S5_ROW_POW0 = S5_LOCAL_STEPS * SUBLANES
S5_TILE_POW0 = S5_ROW_POW0 + SUBLANES


def _s5_power_rows(n_tile_steps):
    exps, keep = [], []
    for k in range(S5_LOCAL_STEPS):
        for tau in range(SUBLANES):
            exps.append(float(1 << k))
            keep.append(1.0 if tau >= (1 << k) else 0.0)
    for tau in range(SUBLANES):
        exps.append(float(tau + 1))
        keep.append(1.0)
    for k in range(n_tile_steps):
        exps.append(float(SUBLANES << k))
        keep.append(1.0)
    while len(exps) % SUBLANES:
        exps.append(0.0)
        keep.append(0.0)
    return exps, keep


def _s5_discretise(lam_re, lam_im, log_step, b_re, b_im, n_tile_steps):
    gp = S5_GROUPS * S5_STATE
    step = jnp.repeat(jnp.exp(log_step), S5_STATE).reshape(1, gp)
    exps, keep = _s5_power_rows(n_tile_steps)
    rows = len(exps)
    b_re_t = jnp.transpose(b_re, (2, 0, 1)).reshape(S5_GROUP, gp)
    b_im_t = jnp.transpose(b_im, (2, 0, 1)).reshape(S5_GROUP, gp)
    return pl.pallas_call(
        _s5_disc_kernel,
        out_shape=[jax.ShapeDtypeStruct((rows, gp), F32), jax.ShapeDtypeStruct((rows, gp), F32),
                   jax.ShapeDtypeStruct((S5_GROUP, gp), F32), jax.ShapeDtypeStruct((S5_GROUP, gp), F32)],
        name="s5_discretise",
    )(lam_re.reshape(1, gp), lam_im.reshape(1, gp), step,
      jnp.asarray(exps, F32).reshape(rows, 1), jnp.asarray(keep, F32).reshape(rows, 1),
      b_re_t, b_im_t)


POOL_HALO = 16
S5_HALF = 256
S5_HALF_STATES = 1024


def _complex_axpy(xr, xi, cr, ci, sr, si):
    return xr + (cr * sr - ci * si), xi + (cr * si + ci * sr)


def _mix0_kernel(u_ref, pw_ref, ps_ref, bw_ref, cw_ref, apr_ref, api_ref, dsk_ref, wglu_ref,
                 side_ref, o_ref, side_out_ref, halo_scr, ext_scr, sr_scr, si_scr, xr_scr, xi_scr):
    side_out_ref[...] = side_ref[...].astype(BF16)
    tt = u_ref.shape[0]
    n_states = sr_scr.shape[1]
    blk = pl.program_id(1)

    @pl.when(blk == 0)
    def _():
        halo_scr[...] = jnp.zeros_like(halo_scr)
        sr_scr[...] = jnp.zeros_like(sr_scr)
        si_scr[...] = jnp.zeros_like(si_scr)

    up = u_ref[:, :POOL_WIDTH].astype(F32)
    ext_scr[0:POOL_HALO, :] = halo_scr[...]
    ext_scr[POOL_HALO:, :] = up
    halo_scr[...] = up[tt - POOL_HALO:, :]
    ext = ext_scr[...].astype(BF16)
    row = lax.broadcasted_iota(jnp.int32, (tt, tt + POOL_HALO), 0)
    col = lax.broadcasted_iota(jnp.int32, (tt, tt + POOL_HALO), 1)
    lag = row + POOL_HALO - col
    t_glob = (blk * tt + row + 1).astype(F32)
    pooled_out = []
    for gi, win in enumerate(POOL_WINDOWS):
        inv_count = 1.0 / jnp.minimum(t_glob, float(win))
        band = jnp.where((lag >= 0) & (lag < win), inv_count, 0.0) - jnp.where(lag == 0, 1.0, 0.0)
        pooled = jnp.dot(band.astype(BF16), ext[:, gi * POOL_GROUP:(gi + 1) * POOL_GROUP],
                         preferred_element_type=F32)
        pooled_out.append(jnp.dot(pooled.astype(BF16), pw_ref[gi], preferred_element_type=F32))
    y_pool = jnp.concatenate(pooled_out, axis=-1) * ps_ref[...]
    o_ref[:, :POOL_WIDTH] = y_pool.astype(o_ref.dtype)

    usb = u_ref[:, POOL_WIDTH:]
    us = usb.astype(F32)
    bu = [jnp.dot(usb[:, hf * S5_HALF:(hf + 1) * S5_HALF], bw_ref[hf], preferred_element_type=F32)
          for hf in range(2)]
    nt = tt // SUBLANES
    xr = jnp.concatenate([b[:, :S5_HALF_STATES] for b in bu], axis=-1).reshape(nt, SUBLANES, n_states)
    xi = jnp.concatenate([b[:, S5_HALF_STATES:] for b in bu], axis=-1).reshape(nt, SUBLANES, n_states)
    for k in range(S5_LOCAL_STEPS):
        cr = apr_ref[k * SUBLANES:(k + 1) * SUBLANES, :][None]
        ci = api_ref[k * SUBLANES:(k + 1) * SUBLANES, :][None]
        xr, xi = _complex_axpy(xr, xi, cr, ci, pltpu.roll(xr, 1 << k, 1), pltpu.roll(xi, 1 << k, 1))

    xr2 = xr.reshape(tt, n_states)
    xi2 = xi.reshape(tt, n_states)
    n_cb = n_states // LANES
    for cb in range(n_cb):
        xr_scr[cb] = xr2[:, cb * LANES:(cb + 1) * LANES]
        xi_scr[cb] = xi2[:, cb * LANES:(cb + 1) * LANES]
    tile_end = pl.ds(SUBLANES - 1, nt, stride=SUBLANES)
    er = jnp.concatenate([xr_scr[cb, tile_end, :] for cb in range(n_cb)], axis=-1)
    ei = jnp.concatenate([xi_scr[cb, tile_end, :] for cb in range(n_cb)], axis=-1)
    prev_r = sr_scr[...]
    prev_i = si_scr[...]
    tile_row = lax.broadcasted_iota(jnp.int32, (nt, n_states), 0)
    a8r = apr_ref[S5_TILE_POW0:S5_TILE_POW0 + 1, :]
    a8i = api_ref[S5_TILE_POW0:S5_TILE_POW0 + 1, :]
    er = er + jnp.where(tile_row == 0, a8r * prev_r - a8i * prev_i, 0.0)
    ei = ei + jnp.where(tile_row == 0, a8r * prev_i + a8i * prev_r, 0.0)
    k = 0
    while (1 << k) < nt:
        cr = apr_ref[S5_TILE_POW0 + k:S5_TILE_POW0 + k + 1, :]
        ci = api_ref[S5_TILE_POW0 + k:S5_TILE_POW0 + k + 1, :]
        pr = jnp.where(tile_row < (1 << k), 0.0, pltpu.roll(er, 1 << k, 0))
        pi = jnp.where(tile_row < (1 << k), 0.0, pltpu.roll(ei, 1 << k, 0))
        er, ei = _complex_axpy(er, ei, cr, ci, pr, pi)
        k += 1
    sr_scr[...] = er[nt - 1:nt, :]
    si_scr[...] = ei[nt - 1:nt, :]
    in_r = jnp.where(tile_row == 0, prev_r, pltpu.roll(er, 1, 0))
    in_i = jnp.where(tile_row == 0, prev_i, pltpu.roll(ei, 1, 0))

    rep_row = lax.broadcasted_iota(jnp.int32, (tt, nt), 0)
    rep_col = lax.broadcasted_iota(jnp.int32, (tt, nt), 1)
    rep = jnp.where(rep_row // SUBLANES == rep_col, 1.0, 0.0).astype(BF16)
    entering = jnp.concatenate([in_r, in_i], axis=-1)
    ent_hi = entering.astype(BF16)
    ent_lo = (entering - ent_hi.astype(F32)).astype(BF16)
    ent = (jnp.dot(rep, ent_hi, preferred_element_type=F32)
           + jnp.dot(rep, ent_lo, preferred_element_type=F32))
    cbr = ent[:, :n_states].reshape(nt, SUBLANES, n_states)
    cbi = ent[:, n_states:].reshape(nt, SUBLANES, n_states)
    pwr = apr_ref[S5_ROW_POW0:S5_ROW_POW0 + SUBLANES, :][None]
    pwi = api_ref[S5_ROW_POW0:S5_ROW_POW0 + SUBLANES, :][None]
    xr, xi = _complex_axpy(xr, xi, pwr, pwi, cbr, cbi)
    xr2 = xr.reshape(tt, n_states).astype(BF16)
    xi2 = xi.reshape(tt, n_states).astype(BF16)

    ys = []
    for hf in range(2):
        sl = slice(hf * S5_HALF_STATES, (hf + 1) * S5_HALF_STATES)
        xcat = jnp.concatenate([xr2[:, sl], xi2[:, sl]], axis=-1)
        ys.append(jnp.dot(xcat, cw_ref[hf], preferred_element_type=F32))
    y = jnp.concatenate(ys, axis=-1) + dsk_ref[...] * us
    y = jax.nn.gelu(y)
    gate = jnp.dot(y.astype(BF16), wglu_ref[...], preferred_element_type=F32)
    o_ref[:, POOL_WIDTH:] = (y * jax.nn.sigmoid(gate)).astype(o_ref.dtype)


def _side_spec(arr, n_steps, step_of):
    assert arr.shape[0] % n_steps == 0
    return pl.BlockSpec((arr.shape[0] // n_steps, arr.shape[1]), lambda *idx: (step_of(*idx), 0))


def _pool_s5_mixer(u, bsz, pool_w, pool_scale, lam_re, lam_im, log_step, b_re, b_im, c_re, c_im,
                   d_skip, w_glu, side, *, tt):
    t = u.shape[0]
    seq = t // bsz
    gp = S5_GROUPS * S5_STATE
    apr, api, bbr, bbi = _s5_discretise(lam_re, lam_im, log_step, b_re, b_im,
                                        int(math.log2(tt // SUBLANES)))
    n_pow = apr.shape[0]

    gh = S5_GROUPS // 2
    eye = jnp.eye(gh, dtype=F32)

    def in_map(bb):
        bb = bb.reshape(S5_GROUP, 2, gh, S5_STATE)
        return jnp.einsum("hxgp,gk->xghkp", bb, eye).reshape(2, gh * S5_GROUP, gh * S5_STATE)

    bw = jnp.concatenate([in_map(bbr), in_map(bbi)], axis=-1).astype(BF16)

    def out_map(cc):
        cc = cc.reshape(2, gh, S5_GROUP, S5_STATE)
        return jnp.einsum("xghp,gk->xgpkh", cc, eye).reshape(2, gh * S5_STATE, gh * S5_GROUP)

    cw = jnp.concatenate([out_map(c_re), -out_map(c_im)], axis=1).astype(BF16)

    const2 = lambda b, i: (0, 0)
    const3 = lambda b, i: (0, 0, 0)
    nblk = seq // tt
    side_spec = _side_spec(side, bsz * nblk, lambda b, i: b * nblk + i)
    return pl.pallas_call(
        _mix0_kernel,
        grid=(bsz, nblk),
        in_specs=[
            pl.BlockSpec((tt, D_MODEL), lambda b, i: (b * nblk + i, 0)),
            pl.BlockSpec((len(POOL_WINDOWS), POOL_GROUP, POOL_GROUP), const3),
            pl.BlockSpec((1, POOL_WIDTH), const2),
            pl.BlockSpec((2, S5_HALF, 2 * S5_HALF_STATES), const3),
            pl.BlockSpec((2, 2 * S5_HALF_STATES, S5_HALF), const3),
            pl.BlockSpec((n_pow, gp), const2),
            pl.BlockSpec((n_pow, gp), const2),
            pl.BlockSpec((1, S5_WIDTH), const2),
            pl.BlockSpec((S5_WIDTH, S5_WIDTH), const2),
            side_spec,
        ],
        out_specs=[pl.BlockSpec((tt, D_MODEL), lambda b, i: (b * nblk + i, 0)), side_spec],
        out_shape=[jax.ShapeDtypeStruct((t, D_MODEL), BF16),
                   jax.ShapeDtypeStruct(side.shape, BF16)],
        scratch_shapes=[
            pltpu.VMEM((POOL_HALO, POOL_WIDTH), F32),
            pltpu.VMEM((tt + POOL_HALO, POOL_WIDTH), F32),
            pltpu.VMEM((1, gp), F32), pltpu.VMEM((1, gp), F32),
            pltpu.VMEM((gp // LANES, tt, LANES), F32), pltpu.VMEM((gp // LANES, tt, LANES), F32),
        ],
        compiler_params=_cparams(("arbitrary", "arbitrary")),
        name="pool_s5_mixer",
    )(u, pool_w.astype(BF16), pool_scale.reshape(1, POOL_WIDTH), bw, cw, apr, api,
      d_skip.reshape(1, S5_WIDTH), w_glu.astype(BF16), side)


CONV_HALO = SUBLANES
M2_GROUP_WIDTH = M2_HPG * M2_HEADDIM


def _causal_conv_silu(x_ref, halo_scr, full_scr, w_ref, b_ref, col0):
    lc = x_ref.shape[0]
    width = x_ref.shape[1]
    x = x_ref[...].astype(F32)
    full_scr[0:CONV_HALO, :] = halo_scr[...]
    full_scr[CONV_HALO:, :] = x
    halo_scr[...] = x[lc - CONV_HALO:, :]
    cols = slice(col0, col0 + width)
    acc = b_ref[:, cols] + w_ref[M2_CONV - 1:M2_CONV, cols] * x
    for k in range(M2_CONV - 1):
        off = CONV_HALO - (M2_CONV - 1) + k
        acc = acc + w_ref[k:k + 1, cols] * full_scr[off:off + lc, :]
    return acc * jax.nn.sigmoid(acc)


def _split_dot(v, e):
    hi = v.astype(BF16)
    lo = (v - hi.astype(F32)).astype(BF16)
    return (jnp.dot(hi, e, preferred_element_type=F32) + jnp.dot(lo, e, preferred_element_type=F32))


def _ssd_kernel(z_ref, xs_ref, bc_ref, dtr_ref, cw_ref, cb_ref, dtb_ref, alog_ref, dx_ref, ng_ref,
                e_ref, side_ref, o_ref, side_out_ref, halo_x, halo_bc, full_x, full_bc, st_scr, y_scr):
    side_out_ref[...] = side_ref[...].astype(BF16)
    lc = z_ref.shape[0]

    @pl.when(pl.program_id(1) == 0)
    def _():
        halo_x[...] = jnp.zeros_like(halo_x)
        halo_bc[...] = jnp.zeros_like(halo_bc)
        st_scr[...] = jnp.zeros_like(st_scr)

    xs = _causal_conv_silu(xs_ref, halo_x, full_x, cw_ref, cb_ref, 0)
    bc = _causal_conv_silu(bc_ref, halo_bc, full_bc, cw_ref, cb_ref, M2_INNER)

    lane = lax.broadcasted_iota(jnp.int32, (1, LANES), 1)
    x_dt = dtr_ref[...].astype(F32) + dtb_ref[...]
    dt = jnp.maximum(x_dt, 0.0) + jnp.log(1.0 + jnp.exp(-jnp.abs(x_dt)))
    a = jnp.where(lane < M2_HEADS, -jnp.exp(alog_ref[...]) * math.log2(math.e), 0.0)
    da = dt * a
    row = lax.broadcasted_iota(jnp.int32, (lc, lc), 0)
    col = lax.broadcasted_iota(jnp.int32, (lc, lc), 1)
    causal = col <= row
    cs = jnp.dot(jnp.where(causal, 1.0, 0.0), da, preferred_element_type=F32,
                 precision=lax.Precision.HIGHEST)
    cs_last = cs[lc - 1:lc, :]
    ecs = jnp.exp2(cs)
    w_in = dt * jnp.exp2(cs_last - cs)
    cs_t = cs.T
    dt_t = dt.T
    e = e_ref[...]
    ecs_x = _split_dot(ecs, e)
    w_x = _split_dot(w_in, e)
    xsb = xs.astype(BF16)
    xw = (xs * w_x).astype(BF16)

    for g in range(M2_GROUPS):
        bm = bc[:, g * M2_STATE:(g + 1) * M2_STATE].astype(BF16)
        cm = bc[:, M2_BC + g * M2_STATE:M2_BC + (g + 1) * M2_STATE].astype(BF16)
        cbm = lax.dot_general(cm, bm, (((1,), (1,)), ((), ())), preferred_element_type=F32)
        gcols = slice(g * M2_GROUP_WIDTH, (g + 1) * M2_GROUP_WIDTH)
        st = st_scr[g]
        y_off = jnp.dot(cm, st.astype(BF16), preferred_element_type=F32) * ecs_x[:, gcols]
        y_heads = []
        for jj in range(M2_HPG):
            j = g * M2_HPG + jj
            seg = cs[:, j:j + 1] - cs_t[j:j + 1, :]
            dec = jnp.exp2(jnp.where(causal, seg, NEG_BIG))
            m = (cbm * dec * dt_t[j:j + 1, :]).astype(BF16)
            y_heads.append(jnp.dot(m, xsb[:, j * M2_HEADDIM:(j + 1) * M2_HEADDIM],
                                   preferred_element_type=F32))
        y_g = jnp.concatenate(y_heads, axis=-1) + y_off
        st_scr[g] = st * ecs_x[lc - 1:lc, gcols] + lax.dot_general(
            bm, xw[:, gcols], (((0,), (0,)), ((), ())), preferred_element_type=F32)
        zg = z_ref[:, gcols].astype(F32)
        y_g = (y_g + dx_ref[:, gcols] * xs[:, gcols]) * (zg * jax.nn.sigmoid(zg))
        y_scr[:, gcols] = y_g

    y = y_scr[...]
    ms = jnp.mean(y * y, axis=-1, keepdims=True)
    o_ref[...] = (y * lax.rsqrt(ms + EPS) * ng_ref[...]).astype(o_ref.dtype)


def _ssd_mixer(proj, bsz, conv_w, conv_b, dt_bias, a_log, d_skip, norm_g, side, *, lc):
    t = proj.shape[0]
    seq = t // bsz
    nblk = seq // lc
    half = M2_INNER
    pad = LANES - M2_HEADS
    dtb = jnp.pad(dt_bias, (0, pad)).reshape(1, LANES)
    alog = jnp.pad(a_log, (0, pad)).reshape(1, LANES)
    dx = jnp.repeat(d_skip, M2_HEADDIM).reshape(1, M2_INNER)
    heads = jnp.arange(LANES, dtype=jnp.int32)[:, None]
    chans = jnp.arange(M2_INNER, dtype=jnp.int32)[None, :] // M2_HEADDIM
    expand = (heads == chans).astype(BF16)
    const2 = lambda b, i: (0, 0)
    side_spec = _side_spec(side, bsz * nblk, lambda b, i: b * nblk + i)
    return pl.pallas_call(
        _ssd_kernel,
        grid=(bsz, nblk),
        in_specs=[
            pl.BlockSpec((lc, half), lambda b, i: (b * nblk + i, 0)),
            pl.BlockSpec((lc, half), lambda b, i: (b * nblk + i, 1)),
            pl.BlockSpec((lc, half), lambda b, i: (b * nblk + i, 2)),
            pl.BlockSpec((lc, LANES), lambda b, i: (b * nblk + i, 3 * half // LANES)),
            pl.BlockSpec((M2_CONV, M2_CONV_DIM), const2),
            pl.BlockSpec((1, M2_CONV_DIM), const2),
            pl.BlockSpec((1, LANES), const2),
            pl.BlockSpec((1, LANES), const2),
            pl.BlockSpec((1, M2_INNER), const2),
            pl.BlockSpec((1, M2_INNER), const2),
            pl.BlockSpec((LANES, M2_INNER), const2),
            side_spec,
        ],
        out_specs=[pl.BlockSpec((lc, M2_INNER), lambda b, i: (b * nblk + i, 0)), side_spec],
        out_shape=[jax.ShapeDtypeStruct((t, M2_INNER), BF16),
                   jax.ShapeDtypeStruct(side.shape, BF16)],
        scratch_shapes=[
            pltpu.VMEM((CONV_HALO, half), F32), pltpu.VMEM((CONV_HALO, half), F32),
            pltpu.VMEM((lc + CONV_HALO, half), F32), pltpu.VMEM((lc + CONV_HALO, half), F32),
            pltpu.VMEM((M2_GROUPS, M2_STATE, M2_GROUP_WIDTH), F32),
            pltpu.VMEM((lc, M2_INNER), F32),
        ],
        compiler_params=_cparams(("arbitrary", "arbitrary")),
        name="ssd_mixer",
    )(proj, proj, proj, proj, conv_w, conv_b.reshape(1, M2_CONV_DIM), dtb, alog, dx,
      norm_g.reshape(1, M2_INNER), expand, side)


def kernel(x, c, ada_w, ada_b, norm_g, mix_w_in, pool_w, pool_scale, s5_lam_re, s5_lam_im, s5_log_step, s5_b_re, s5_b_im, s5_c_re, s5_c_im, s5_d, s5_w_glu, mix_w_out, ffn_w_gate, ffn_w_up, ffn_w_down, m2_w_in, m2_conv_w, m2_conv_b, m2_dt_bias, m2_a_log, m2_d, m2_norm_g, m2_w_out, moe_w_router, moe_b_router, moe_w_gate, moe_w_up, moe_w_down):
    bsz, seq, d = x.shape
    t = bsz * seq
    xt = x.reshape(t, d)

    mod = _ada_modulation(c, ada_w, ada_b)

    def mod_vecs(layer):
        return [mod[layer, :, k * d:(k + 1) * d].reshape(bsz, 1, d) for k in range(6)]

    def gvec(layer, k):
        return norm_g[layer, k].reshape(1, d)

    sh_m, sc_m, g_m, sh_f, sc_f, g_f = mod_vecs(0)
    u = _norm_matmul(xt, gvec(0, 0), sc_m, sh_m, mix_w_in[0].astype(BF16),
                     tm=1024, tn=1024, out_dtype=BF16, name="mix_in_proj")
    ne, dh = N_EXPERTS, EXPERT_HIDDEN
    ycat, wu_b = _pool_s5_mixer(u, bsz, pool_w[0], pool_scale[0], s5_lam_re[0], s5_lam_im[0],
                                s5_log_step[0], s5_b_re[0], s5_b_im[0], s5_c_re[0], s5_c_im[0],
                                s5_d[0], s5_w_glu[0], moe_w_up[0].reshape(ne * d, dh), tt=256)
    xt = _matmul_post(ycat, mix_w_out[0].astype(BF16), xt, gvec(0, 1), g_m,
                      tm=1024, name="mix_out_proj")
    xt = _ffn_sublayer(xt, gvec(0, 2), sc_f, sh_f, ffn_w_gate[0].astype(BF16),
                       ffn_w_up[0].astype(BF16), ffn_w_down[0].astype(BF16), gvec(0, 3), g_f,
                       tm=1024, th=1408)

    sh_m, sc_m, g_m, sh_f, sc_f, g_f = mod_vecs(1)
    proj_pad = 6400
    w_in = jnp.zeros((d, proj_pad), BF16).at[:, :M2_PROJ].set(m2_w_in[0].astype(BF16))
    proj, (wg_b,) = _norm_matmul(
        xt, gvec(1, 0), sc_m, sh_m, w_in, tm=1024, tn=1280, out_dtype=BF16, name="m2_in_proj",
        side=(moe_w_gate[0].reshape(ne * d, dh),), side_steps=32)
    y, wd_b = _ssd_mixer(proj, bsz, m2_conv_w[0], m2_conv_b[0], m2_dt_bias[0], m2_a_log[0],
                         m2_d[0], m2_norm_g[0], moe_w_down[0].reshape(ne * dh, d), lc=128)
    xt = _matmul_post(y, m2_w_out[0].astype(BF16), xt, gvec(1, 1), g_m, tm=1024,
                      name="m2_out_proj")
    xt = _moe_sublayer(xt, gvec(1, 2), sc_f, sh_f, moe_w_router[0], moe_b_router[0],
                       wg_b.reshape(ne, d, dh), wu_b.reshape(ne, d, dh), wd_b.reshape(ne, dh, d),
                       gvec(1, 3), g_f)
    return xt.reshape(bsz, seq, d)
```

```python
import functools
import math

import jax
import jax.numpy as jnp
from jax import lax
from jax.experimental import pallas as pl
from jax.experimental.pallas import tpu as pltpu

F32 = jnp.float32
BF16 = jnp.bfloat16

D_MODEL = 1024
EPS = 1e-6
POOL_WIDTH = 512
POOL_WINDOWS = (2, 4, 8, 16)
POOL_GROUP = 128
S5_WIDTH = 512
S5_GROUP = 16
S5_GROUPS = 32
S5_STATE = 64
M2_INNER = 2048
M2_HEADDIM = 64
M2_HEADS = 32
M2_GROUPS = 8
M2_HPG = 4
M2_STATE = 128
M2_CONV = 4
M2_BC = 1024
M2_CONV_DIM = 4096
M2_PROJ = 6176
FFN_HIDDEN = 2816
N_EXPERTS = 8
EXPERT_HIDDEN = 3584

LANES = 128
SUBLANES = 8
VMEM_LIMIT_BYTES = 56 * 1024 * 1024

NEG_BIG = -1e30


def _cparams(sem):
    return pltpu.CompilerParams(dimension_semantics=sem, vmem_limit_bytes=VMEM_LIMIT_BYTES)


def _modulated_norm(x, g, scale, shift):
    ms = jnp.mean(x * x, axis=-1, keepdims=True)
    return (x * lax.rsqrt(ms + EPS) * g) * (1.0 + scale) + shift


def _post_norm_residual(x, y, ng, gate):
    ms = jnp.mean(y * y, axis=-1, keepdims=True)
    return x + gate * (y * lax.rsqrt(ms + EPS) * ng)


def _ada_kernel(c_ref, w_ref, b_ref, o_ref):
    c = c_ref[...]
    a = c * jax.nn.sigmoid(c)
    o_ref[0] = jnp.dot(a.astype(BF16), w_ref[0].astype(BF16),
                       preferred_element_type=F32) + b_ref[0]


def _ada_modulation(c, ada_w, ada_b):
    depth, d, n = ada_w.shape
    b = c.shape[0]
    c_pad = jnp.zeros((SUBLANES, d), F32).at[:b].set(c)
    tn = 1024
    out = pl.pallas_call(
        _ada_kernel,
        grid=(depth, n // tn),
        in_specs=[
            pl.BlockSpec((SUBLANES, d), lambda l, j: (0, 0)),
            pl.BlockSpec((1, d, tn), lambda l, j: (l, 0, j)),
            pl.BlockSpec((1, 1, tn), lambda l, j: (l, 0, j)),
        ],
        out_specs=pl.BlockSpec((1, SUBLANES, tn), lambda l, j: (l, 0, j)),
        out_shape=jax.ShapeDtypeStruct((depth, SUBLANES, n), F32),
        compiler_params=_cparams(("arbitrary", "arbitrary")),
        name="ada_modulation",
    )(c_pad, ada_w, ada_b.reshape(depth, 1, n))
    return out[:, :b]


def _norm_mm_kernel(x_ref, g_ref, sc_ref, sh_ref, w_ref, *rest, n_side, side_steps):
    side_in = rest[:n_side]
    o_ref = rest[n_side]
    side_out = rest[n_side + 1:2 * n_side + 1]
    h_scr = rest[2 * n_side + 1]

    @pl.when(pl.program_id(1) == 0)
    def _():
        h = _modulated_norm(x_ref[...], g_ref[...], sc_ref[0], sh_ref[0])
        h_scr[...] = h.astype(BF16)

    o_ref[...] = jnp.dot(h_scr[...], w_ref[...], preferred_element_type=F32).astype(o_ref.dtype)

    step = pl.program_id(0) * pl.num_programs(1) + pl.program_id(1)
    for k in range(n_side):
        @pl.when(jnp.logical_and(step >= k * side_steps, step < (k + 1) * side_steps))
        def _(k=k):
            side_out[k][...] = side_in[k][...].astype(BF16)


def _norm_matmul(x, g, scale, shift, w, *, tm, tn, out_dtype, name, side=(), side_steps=1):
    t, d = x.shape
    n = w.shape[1]
    n_j = n // tn
    tiles_per_batch = t // scale.shape[0] // tm
    assert len(side) * side_steps <= (t // tm) * n_j

    def side_spec(k, arr):
        rows = arr.shape[0] // side_steps
        return pl.BlockSpec(
            (rows, arr.shape[1]),
            lambda i, j: (jnp.clip(i * n_j + j - k * side_steps, 0, side_steps - 1), 0))

    side_specs = [side_spec(k, a) for k, a in enumerate(side)]
    outs = pl.pallas_call(
        functools.partial(_norm_mm_kernel, n_side=len(side), side_steps=side_steps),
        grid=(t // tm, n_j),
        in_specs=[
            pl.BlockSpec((tm, d), lambda i, j: (i, 0)),
            pl.BlockSpec((1, d), lambda i, j: (0, 0)),
            pl.BlockSpec((1, 1, d), lambda i, j: (i // tiles_per_batch, 0, 0)),
            pl.BlockSpec((1, 1, d), lambda i, j: (i // tiles_per_batch, 0, 0)),
            pl.BlockSpec((d, tn), lambda i, j: (0, j)),
        ] + side_specs,
        out_specs=[pl.BlockSpec((tm, tn), lambda i, j: (i, j))] + side_specs,
        out_shape=[jax.ShapeDtypeStruct((t, n), out_dtype)]
                  + [jax.ShapeDtypeStruct(a.shape, BF16) for a in side],
        scratch_shapes=[pltpu.VMEM((tm, d), BF16)],
        compiler_params=_cparams(("arbitrary", "arbitrary")),
        name=name,
    )(x, g, scale, shift, w, *side)
    return (outs[0], outs[1:]) if side else outs[0]


def _mm_post_kernel(y_ref, w_ref, x_ref, ng_ref, gate_ref, o_ref):
    y = jnp.dot(y_ref[...], w_ref[...], preferred_element_type=F32)
    o_ref[...] = _post_norm_residual(x_ref[...], y, ng_ref[...], gate_ref[0])


def _matmul_post(y, w, x, ng, gate, *, tm, name):
    t, k = y.shape
    d = w.shape[1]
    tiles_per_batch = t // gate.shape[0] // tm
    return pl.pallas_call(
        _mm_post_kernel,
        grid=(t // tm,),
        in_specs=[
            pl.BlockSpec((tm, k), lambda i: (i, 0)),
            pl.BlockSpec((k, d), lambda i: (0, 0)),
            pl.BlockSpec((tm, d), lambda i: (i, 0)),
            pl.BlockSpec((1, d), lambda i: (0, 0)),
            pl.BlockSpec((1, 1, d), lambda i: (i // tiles_per_batch, 0, 0)),
        ],
        out_specs=pl.BlockSpec((tm, d), lambda i: (i, 0)),
        out_shape=jax.ShapeDtypeStruct((t, d), F32),
        compiler_params=_cparams(("arbitrary",)),
        name=name,
    )(y, w, x, ng, gate)


def _ffn_kernel(x_ref, g_ref, sc_ref, sh_ref, wgu_ref, wd_ref, ng_ref, gate_ref,
                o_ref, h_scr, acc_scr):
    j = pl.program_id(1)

    @pl.when(j == 0)
    def _():
        h = _modulated_norm(x_ref[...], g_ref[...], sc_ref[0], sh_ref[0])
        h_scr[...] = h.astype(BF16)
        acc_scr[...] = jnp.zeros_like(acc_scr)

    th = wd_ref.shape[0]
    gu = jnp.dot(h_scr[...], wgu_ref[...], preferred_element_type=F32)
    gt = gu[:, :th]
    up = gu[:, th:]
    act = (gt * jax.nn.sigmoid(gt) * up).astype(BF16)
    acc_scr[...] += jnp.dot(act, wd_ref[...], preferred_element_type=F32)

    @pl.when(j == pl.num_programs(1) - 1)
    def _():
        o_ref[...] = _post_norm_residual(x_ref[...], acc_scr[...], ng_ref[...], gate_ref[0])


def _ffn_sublayer(x, g, scale, shift, wg, wu, wd, ng, gate, *, tm, th):
    t, d = x.shape
    hid = wg.shape[1]
    n_j = hid // th
    tiles_per_batch = t // scale.shape[0] // tm
    bvec = pl.BlockSpec((1, 1, d), lambda i, j: (i // tiles_per_batch, 0, 0))
    wgu = jnp.stack([wg.reshape(d, n_j, th), wu.reshape(d, n_j, th)], axis=2).reshape(d, 2 * hid)
    return pl.pallas_call(
        _ffn_kernel,
        grid=(t // tm, n_j),
        in_specs=[
            pl.BlockSpec((tm, d), lambda i, j: (i, 0)),
            pl.BlockSpec((1, d), lambda i, j: (0, 0)),
            bvec, bvec,
            pl.BlockSpec((d, 2 * th), lambda i, j: (0, j)),
            pl.BlockSpec((th, d), lambda i, j: (j, 0)),
            pl.BlockSpec((1, d), lambda i, j: (0, 0)),
            bvec,
        ],
        out_specs=pl.BlockSpec((tm, d), lambda i, j: (i, 0)),
        out_shape=jax.ShapeDtypeStruct((t, d), F32),
        scratch_shapes=[pltpu.VMEM((tm, d), BF16), pltpu.VMEM((tm, d), F32)],
        compiler_params=_cparams(("arbitrary", "arbitrary")),
        name="ffn_sublayer",
    )(x, g, scale, shift, wgu, wd, ng, gate)


def _router_kernel(x_ref, g_ref, sc_ref, sh_ref, wr_ref, br_ref, h_ref, meta_ref, cnt_ref):
    tm = x_ref.shape[0]

    h = _modulated_norm(x_ref[...], g_ref[...], sc_ref[0], sh_ref[0])
    h_hi = h.astype(BF16)
    h_ref[...] = h_hi
    h_lo = (h - h_hi.astype(F32)).astype(BF16)
    logits = (jnp.dot(h_hi, wr_ref[0], preferred_element_type=F32)
              + jnp.dot(h_lo, wr_ref[0], preferred_element_type=F32)
              + jnp.dot(h_hi, wr_ref[1], preferred_element_type=F32)) + br_ref[...]
    lane = lax.broadcasted_iota(jnp.int32, (tm, LANES), 1).astype(F32)
    m1 = jnp.max(logits, axis=-1, keepdims=True)
    i1 = jnp.min(jnp.where(logits == m1, lane, float(LANES)), axis=-1, keepdims=True)
    oh1 = lane == i1
    rest = jnp.where(oh1, NEG_BIG * 2.0, logits)
    m2 = jnp.max(rest, axis=-1, keepdims=True)
    i2 = jnp.min(jnp.where(rest == m2, lane, float(LANES)), axis=-1, keepdims=True)
    oh2 = lane == i2
    e = jnp.exp(m2 - m1)
    w1 = 1.0 / (1.0 + e)
    w2 = e / (1.0 + e)

    picks = jnp.where(oh1, 1.0, 0.0) + jnp.where(oh2, 1.0, 0.0)
    row = lax.broadcasted_iota(jnp.int32, (tm, tm), 0)
    col = lax.broadcasted_iota(jnp.int32, (tm, tm), 1)
    lower = jnp.where(col < row, 1.0, 0.0).astype(BF16)
    before = jnp.dot(lower, picks.astype(BF16), preferred_element_type=F32)
    tile_cnt = jnp.sum(picks, axis=0, keepdims=True)
    ea = lax.broadcasted_iota(jnp.int32, (LANES, LANES), 0)
    eb = lax.broadcasted_iota(jnp.int32, (LANES, LANES), 1)
    seg_rows = jnp.floor((tile_cnt + (SUBLANES - 1.0)) * (1.0 / SUBLANES)) * SUBLANES
    seg_off = jnp.dot(jnp.broadcast_to(seg_rows, (SUBLANES, LANES)), jnp.where(ea < eb, 1.0, 0.0),
                      preferred_element_type=F32, precision=lax.Precision.HIGHEST)[0:1, :]
    local = before + seg_off
    loc1 = jnp.sum(jnp.where(oh1, local, 0.0), axis=-1, keepdims=True)
    loc2 = jnp.sum(jnp.where(oh2, local, 0.0), axis=-1, keepdims=True)
    cnt_ref[0] = tile_cnt

    meta = jnp.where(lane == 0.0, i1, 0.0)
    meta = jnp.where(lane == 1.0, i2, meta)
    meta = jnp.where(lane == 2.0, w1, meta)
    meta = jnp.where(lane == 3.0, w2, meta)
    meta = jnp.where(lane == 4.0, loc1, meta)
    meta = jnp.where(lane == 5.0, loc2, meta)
    meta_ref[...] = meta


def _router(x, g, scale, shift, w_router, b_router, *, tm):
    t, d = x.shape
    tiles_per_batch = t // scale.shape[0] // tm
    wr = jnp.zeros((d, LANES), F32).at[:, :N_EXPERTS].set(w_router)
    wr_hi = wr.astype(BF16)
    wr = jnp.stack([wr_hi, (wr - wr_hi.astype(F32)).astype(BF16)])
    br = jnp.full((1, LANES), NEG_BIG, F32).at[0, :N_EXPERTS].set(b_router)
    bvec = pl.BlockSpec((1, 1, d), lambda i: (i // tiles_per_batch, 0, 0))
    return pl.pallas_call(
        _router_kernel,
        grid=(t // tm,),
        in_specs=[
            pl.BlockSpec((tm, d), lambda i: (i, 0)),
            pl.BlockSpec((1, d), lambda i: (0, 0)),
            bvec, bvec,
            pl.BlockSpec((2, d, LANES), lambda i: (0, 0, 0)),
            pl.BlockSpec((1, LANES), lambda i: (0, 0)),
        ],
        out_specs=[
            pl.BlockSpec((tm, d), lambda i: (i, 0)),
            pl.BlockSpec((tm, LANES), lambda i: (i, 0)),
            pl.BlockSpec((1, 1, LANES), lambda i: (i, 0, 0)),
        ],
        out_shape=[
            jax.ShapeDtypeStruct((t, d), BF16),
            jax.ShapeDtypeStruct((t, LANES), F32),
            jax.ShapeDtypeStruct((t // tm, 1, LANES), F32),
        ],
        compiler_params=_cparams(("arbitrary",)),
        name="moe_router",
    )(x, g, scale, shift, wr, br)


SEG_FIELDS = 3
SEG_PAD_ROWS = N_EXPERTS * SUBLANES


def _segment_copies(scal_ref, tile, tm, per_copy):
    base = tile * (SEG_FIELDS * N_EXPERTS)
    for e in range(N_EXPERTS):
        n = scal_ref[base + e]
        g0 = scal_ref[base + N_EXPERTS + e]
        l0 = scal_ref[base + 2 * N_EXPERTS + e]
        done = 0
        b = tm
        while b >= SUBLANES:
            take = n & b

            @pl.when(take != 0)
            def _(b=b, done=done, g0=g0, l0=l0):
                per_copy(pl.multiple_of(l0 + done, SUBLANES), pl.multiple_of(g0 + done, SUBLANES), b)

            done = done + take
            b //= 2


def _dispatch_kernel(scal_ref, tail_ref, h_ref, meta_ref, xs_ref, ws_ref,
                     seg_scr, wseg_scr, zero_scr, sem, zsem, *, n_tiles, max_unused):
    tile = pl.program_id(0)
    tm = h_ref.shape[0]
    ns = seg_scr.shape[1]
    par = tile % 2

    def segment_dmas(which, buf):
        def descriptors(local, glob, n):
            return (pltpu.make_async_copy(wseg_scr.at[buf, pl.ds(local, n), :],
                                          ws_ref.at[pl.ds(glob, n), :], sem.at[buf]),
                    pltpu.make_async_copy(seg_scr.at[buf, pl.ds(local, n), :],
                                          xs_ref.at[pl.ds(glob, n), :], sem.at[buf]))
        return descriptors

    def wait_segments(which, buf):
        descriptors = segment_dmas(which, buf)
        _segment_copies(scal_ref, which, tm, lambda l, g, n: [c.wait() for c in descriptors(l, g, n)])

    @pl.when(tile == 0)
    def _():
        zero_scr[...] = jnp.zeros_like(zero_scr)
        fills = []
        n_slot_tiles = xs_ref.shape[0] // tm
        tails = [(tail_ref[N_EXPERTS + e] > 0, pl.multiple_of(tail_ref[e], tm))
                 for e in range(N_EXPERTS)]
        tails += [(k >= tail_ref[2 * N_EXPERTS], k * tm)
                  for k in range(n_slot_tiles - max_unused, n_slot_tiles)]
        for nonempty, tail in tails:
            fills.append((nonempty,
                          pltpu.make_async_copy(zero_scr, xs_ref.at[pl.ds(tail, tm), :], zsem),
                          pltpu.make_async_copy(zero_scr.at[:, 0:LANES],
                                                ws_ref.at[pl.ds(tail, tm), :], zsem)))
        for nonempty, fill_x, fill_w in fills:
            @pl.when(nonempty)
            def _(fill_x=fill_x, fill_w=fill_w):
                fill_x.start()
                fill_w.start()
        for nonempty, fill_x, fill_w in fills:
            @pl.when(nonempty)
            def _(fill_x=fill_x, fill_w=fill_w):
                fill_x.wait()
                fill_w.wait()

    meta = meta_ref[...]
    meta_t = meta.T
    slot = lax.broadcasted_iota(jnp.int32, (ns, tm), 0).astype(F32)
    g1 = jnp.where(slot == meta_t[4:5, :], 1.0, 0.0).astype(BF16)
    g2 = jnp.where(slot == meta_t[5:6, :], 1.0, 0.0).astype(BF16)
    rows = jnp.dot(g1 + g2, h_ref[...], preferred_element_type=F32)
    lane = lax.broadcasted_iota(jnp.int32, (tm, LANES), 1)

    def split_lanes(w):
        hi = w.astype(BF16).astype(F32)
        return jnp.where(lane == 0, hi, jnp.where(lane == 1, w - hi, 0.0)).astype(BF16)

    w_split = jnp.concatenate([split_lanes(meta[:, 2:3]), split_lanes(meta[:, 3:4])], axis=0)
    w_slot = jnp.dot(jnp.concatenate([g1, g2], axis=1), w_split, preferred_element_type=F32)
    weight = w_slot[:, 0:1] + w_slot[:, 1:2]

    @pl.when(tile >= 2)
    def _():
        wait_segments(tile - 2, par)

    seg_scr[par] = rows
    wseg_scr[par] = jnp.broadcast_to(weight, (ns, LANES))
    descriptors = segment_dmas(tile, par)
    _segment_copies(scal_ref, tile, tm, lambda l, g, n: [c.start() for c in descriptors(l, g, n)])

    @pl.when(tile == n_tiles - 1)
    def _():
        wait_segments(tile, par)
        if n_tiles > 1:
            wait_segments(tile - 1, 1 - par)


def _dispatch(h, meta, seg_table, tail_table, n_slots, *, tm):
    t, d = h.shape
    grid_spec = pltpu.PrefetchScalarGridSpec(
        num_scalar_prefetch=2,
        grid=(t // tm,),
        in_specs=[
            pl.BlockSpec((tm, d), lambda i, sc, tl: (i, 0)),
            pl.BlockSpec((tm, LANES), lambda i, sc, tl: (i, 0)),
        ],
        out_specs=[pl.BlockSpec(memory_space=pl.ANY), pl.BlockSpec(memory_space=pl.ANY)],
        scratch_shapes=[
            pltpu.VMEM((2, 2 * tm + SEG_PAD_ROWS, d), F32),
            pltpu.VMEM((2, 2 * tm + SEG_PAD_ROWS, LANES), F32),
            pltpu.VMEM((tm, d), F32),
            pltpu.SemaphoreType.DMA((2,)), pltpu.SemaphoreType.DMA(()),
        ],
    )
    return pl.pallas_call(
        functools.partial(_dispatch_kernel, n_tiles=t // tm,
                          max_unused=n_slots // tm - 2 * t // tm),
        grid_spec=grid_spec,
        out_shape=[jax.ShapeDtypeStruct((n_slots, d), F32),
                   jax.ShapeDtypeStruct((n_slots, LANES), F32)],
        compiler_params=_cparams(("arbitrary",)),
        name="moe_dispatch",
    )(seg_table, tail_table, h, meta)


def _expert_kernel(te_ref, nu_ref, xs_ref, ws_ref, wg_ref, wu_ref, wd_ref, ys_ref,
                   xb_scr, acc_scr):
    i = pl.program_id(0)
    j = pl.program_id(1)
    last = pl.num_programs(1) - 1
    used = i < nu_ref[0]

    @pl.when(used)
    def _():
        @pl.when(j == 0)
        def _():
            xb_scr[...] = xs_ref[...].astype(BF16)
            acc_scr[...] = jnp.zeros_like(acc_scr)

        x = xb_scr[...]
        gt = jnp.dot(x, wg_ref[0], preferred_element_type=F32)
        up = jnp.dot(x, wu_ref[0], preferred_element_type=F32)
        act = (gt * jax.nn.sigmoid(gt) * up).astype(BF16)
        acc_scr[...] += jnp.dot(act, wd_ref[0], preferred_element_type=F32)

        @pl.when(j == last)
        def _():
            ys_ref[...] = (acc_scr[...] * ws_ref[:, 0:1]).astype(ys_ref.dtype)

    @pl.when(jnp.logical_and(jnp.logical_not(used), j == last))
    def _():
        ys_ref[...] = jnp.zeros_like(ys_ref)


def _experts(xs, ws, tile_expert, n_used, wg, wu, wd, *, tm, th):
    n_slots, d = xs.shape
    hid = wg.shape[2]
    nt = n_slots // tm

    def hidden_step(i, j, nu):
        return jnp.where(i < nu[0], j, 0)

    def slot_tile(i, nu):
        return jnp.where(i < nu[0], i, 0)

    grid_spec = pltpu.PrefetchScalarGridSpec(
        num_scalar_prefetch=2,
        grid=(nt, hid // th),
        in_specs=[
            pl.BlockSpec((tm, d), lambda i, j, te, nu: (slot_tile(i, nu), 0)),
            pl.BlockSpec((tm, LANES), lambda i, j, te, nu: (slot_tile(i, nu), 0)),
            pl.BlockSpec((1, d, th), lambda i, j, te, nu: (te[i], 0, hidden_step(i, j, nu))),
            pl.BlockSpec((1, d, th), lambda i, j, te, nu: (te[i], 0, hidden_step(i, j, nu))),
            pl.BlockSpec((1, th, d), lambda i, j, te, nu: (te[i], hidden_step(i, j, nu), 0)),
        ],
        out_specs=pl.BlockSpec((tm, d), lambda i, j, te, nu: (i, 0)),
        scratch_shapes=[pltpu.VMEM((tm, d), BF16), pltpu.VMEM((tm, d), F32)],
    )
    return pl.pallas_call(
        _expert_kernel,
        grid_spec=grid_spec,
        out_shape=jax.ShapeDtypeStruct((n_slots, d), F32),
        compiler_params=_cparams(("arbitrary", "arbitrary")),
        name="moe_experts",
    )(tile_expert, n_used, xs, ws, wg, wu, wd)


def _combine_kernel(scal_ref, ys_ref, meta_ref, x_ref, ng_ref, gate_ref, o_ref, buf, sem,
                    *, n_tiles):
    tile = pl.program_id(0)
    tm = x_ref.shape[0]
    ns = buf.shape[1]
    par = tile % 2

    def fetch(which, slot, wait):
        def copy_in(local, glob, n):
            dma = pltpu.make_async_copy(ys_ref.at[pl.ds(glob, n), :],
                                        buf.at[slot, pl.ds(local, n), :], sem.at[slot])
            dma.wait() if wait else dma.start()
        _segment_copies(scal_ref, which, tm, copy_in)

    @pl.when(tile == 0)
    def _():
        buf[...] = jnp.zeros_like(buf)
        fetch(tile, par, False)

    @pl.when(tile + 1 < n_tiles)
    def _():
        fetch(tile + 1, 1 - par, False)

    fetch(tile, par, True)
    meta = meta_ref[...]
    slot_id = lax.broadcasted_iota(jnp.int32, (tm, ns), 1).astype(F32)
    pick = jnp.where((slot_id == meta[:, 4:5]) | (slot_id == meta[:, 5:6]), 1.0, 0.0).astype(BF16)
    y = jnp.dot(pick, buf[par].astype(BF16), preferred_element_type=F32)
    o_ref[...] = _post_norm_residual(x_ref[...], y, ng_ref[...], gate_ref[0])


def _combine(ys, meta, seg_table, x, ng, gate, *, tm):
    t, d = x.shape
    tiles_per_batch = t // gate.shape[0] // tm
    grid_spec = pltpu.PrefetchScalarGridSpec(
        num_scalar_prefetch=1,
        grid=(t // tm,),
        in_specs=[
            pl.BlockSpec(memory_space=pl.ANY),
            pl.BlockSpec((tm, LANES), lambda i, sc: (i, 0)),
            pl.BlockSpec((tm, d), lambda i, sc: (i, 0)),
            pl.BlockSpec((1, d), lambda i, sc: (0, 0)),
            pl.BlockSpec((1, 1, d), lambda i, sc: (i // tiles_per_batch, 0, 0)),
        ],
        out_specs=pl.BlockSpec((tm, d), lambda i, sc: (i, 0)),
        scratch_shapes=[pltpu.VMEM((2, 2 * tm + SEG_PAD_ROWS, d), F32),
                        pltpu.SemaphoreType.DMA((2,))],
    )
    return pl.pallas_call(
        functools.partial(_combine_kernel, n_tiles=t // tm),
        grid_spec=grid_spec,
        out_shape=jax.ShapeDtypeStruct((t, d), F32),
        compiler_params=_cparams(("arbitrary",)),
        name="moe_combine",
    )(seg_table, ys, meta, x, ng, gate)


def _moe_sublayer(x, g, scale, shift, w_router, b_router, wg, wu, wd, ng, gate):
    t = x.shape[0]
    tm, tm_e, th = 512, 512, 1792
    h, meta, counts = _router(x, g, scale, shift, w_router, b_router, tm=tm)

    assert tm == tm_e
    tile_cnt = counts[:, 0, :N_EXPERTS].astype(jnp.int32)
    seg_rows = ((tile_cnt + SUBLANES - 1) // SUBLANES) * SUBLANES
    cnt = jnp.sum(seg_rows, axis=0)
    padded = ((cnt + tm_e - 1) // tm_e) * tm_e
    ends = jnp.cumsum(padded)
    offs = ends - padded
    glob_start = offs[None, :] + jnp.cumsum(seg_rows, axis=0) - seg_rows
    local_start = jnp.cumsum(seg_rows, axis=1) - seg_rows
    seg_table = jnp.concatenate([seg_rows, glob_start, local_start], axis=1).reshape(-1)
    tail_table = jnp.concatenate([ends - tm_e, padded, ends[-1:] // tm_e]).astype(jnp.int32)
    n_tok_tiles = t // tm
    nt = -(-(2 * t + n_tok_tiles * SEG_PAD_ROWS) // tm_e) + N_EXPERTS
    starts = jnp.arange(nt, dtype=jnp.int32) * tm_e
    tile_expert = jnp.minimum(jnp.sum(starts[:, None] >= ends[None, :], axis=1), N_EXPERTS - 1)
    tile_expert = tile_expert.astype(jnp.int32)
    n_used = (ends[-1:] // tm_e).astype(jnp.int32)

    xs, ws = _dispatch(h, meta, seg_table, tail_table, nt * tm_e, tm=tm)
    ys = _experts(xs, ws, tile_expert, n_used, wg, wu, wd, tm=tm_e, th=th)
    return _combine(ys, meta, seg_table, x, ng, gate, tm=tm)


def _s5_disc_kernel(lam_re_ref, lam_im_ref, step_ref, pow_ref, keep_ref, b_re_ref, b_im_ref,
                    apr_ref, api_ref, bbr_ref, bbi_ref):
    lr = jnp.minimum(lam_re_ref[...], -1e-4)
    li = lam_im_ref[...]
    step = step_ref[...]
    m = pow_ref[...]
    mag = jnp.exp(lr * step * m) * keep_ref[...]
    ang = li * step * m
    apr_ref[...] = mag * jnp.cos(ang)
    api_ref[...] = mag * jnp.sin(ang)
    mag1 = jnp.exp(lr * step)
    ar = mag1 * jnp.cos(li * step)
    ai = mag1 * jnp.sin(li * step)
    inv = 1.0 / (lr * lr + li * li)
    fr = ((ar - 1.0) * lr + ai * li) * inv
    fi = (ai * lr - (ar - 1.0) * li) * inv
    br = b_re_ref[...]
    bi = b_im_ref[...]
    bbr_ref[...] = fr * br - fi * bi
    bbi_ref[...] = fr * bi + fi * br


S5_LOCAL_STEPS = 3
S5_ROW_POW0 = S5_LOCAL_STEPS * SUBLANES
S5_TILE_POW0 = S5_ROW_POW0 + SUBLANES


def _s5_power_rows(n_tile_steps):
    exps, keep = [], []
    for k in range(S5_LOCAL_STEPS):
        for tau in range(SUBLANES):
            exps.append(float(1 << k))
            keep.append(1.0 if tau >= (1 << k) else 0.0)
    for tau in range(SUBLANES):
        exps.append(float(tau + 1))
        keep.append(1.0)
    for k in range(n_tile_steps):
        exps.append(float(SUBLANES << k))
        keep.append(1.0)
    while len(exps) % SUBLANES:
        exps.append(0.0)
        keep.append(0.0)
    return exps, keep


def _s5_discretise(lam_re, lam_im, log_step, b_re, b_im, n_tile_steps):
    gp = S5_GROUPS * S5_STATE
    step = jnp.repeat(jnp.exp(log_step), S5_STATE).reshape(1, gp)
    exps, keep = _s5_power_rows(n_tile_steps)
    rows = len(exps)
    b_re_t = jnp.transpose(b_re, (2, 0, 1)).reshape(S5_GROUP, gp)
    b_im_t = jnp.transpose(b_im, (2, 0, 1)).reshape(S5_GROUP, gp)
    return pl.pallas_call(
        _s5_disc_kernel,
        out_shape=[jax.ShapeDtypeStruct((rows, gp), F32), jax.ShapeDtypeStruct((rows, gp), F32),
                   jax.ShapeDtypeStruct((S5_GROUP, gp), F32), jax.ShapeDtypeStruct((S5_GROUP, gp), F32)],
        name="s5_discretise",
    )(lam_re.reshape(1, gp), lam_im.reshape(1, gp), step,
      jnp.asarray(exps, F32).reshape(rows, 1), jnp.asarray(keep, F32).reshape(rows, 1),
      b_re_t, b_im_t)


POOL_HALO = 16
S5_HALF = 256
S5_HALF_STATES = 1024


def _complex_axpy(xr, xi, cr, ci, sr, si):
    return xr + (cr * sr - ci * si), xi + (cr * si + ci * sr)


def _mix0_kernel(u_ref, pw_ref, ps_ref, bw_ref, cw_ref, apr_ref, api_ref, dsk_ref, wglu_ref,
                 wout_ref, x_ref, ng_ref, gate_ref, side_ref, o_ref, side_out_ref,
                 halo_scr, ext_scr, sr_scr, si_scr, xr_scr, xi_scr, ycat_scr):
    side_out_ref[...] = side_ref[...].astype(BF16)
    tt = u_ref.shape[0]
    n_states = sr_scr.shape[1]
    blk = pl.program_id(1)

    @pl.when(blk == 0)
    def _():
        halo_scr[...] = jnp.zeros_like(halo_scr)
        sr_scr[...] = jnp.zeros_like(sr_scr)
        si_scr[...] = jnp.zeros_like(si_scr)

    up = u_ref[:, :POOL_WIDTH].astype(F32)
    ext_scr[0:POOL_HALO, :] = halo_scr[...]
    ext_scr[POOL_HALO:, :] = up
    halo_scr[...] = up[tt - POOL_HALO:, :]
    ext = ext_scr[...].astype(BF16)
    row = lax.broadcasted_iota(jnp.int32, (tt, tt + POOL_HALO), 0)
    col = lax.broadcasted_iota(jnp.int32, (tt, tt + POOL_HALO), 1)
    lag = row + POOL_HALO - col
    t_glob = (blk * tt + row + 1).astype(F32)
    pooled_out = []
    for gi, win in enumerate(POOL_WINDOWS):
        inv_count = 1.0 / jnp.minimum(t_glob, float(win))
        band = jnp.where((lag >= 0) & (lag < win), inv_count, 0.0) - jnp.where(lag == 0, 1.0, 0.0)
        pooled = jnp.dot(band.astype(BF16), ext[:, gi * POOL_GROUP:(gi + 1) * POOL_GROUP],
                         preferred_element_type=F32)
        pooled_out.append(jnp.dot(pooled.astype(BF16), pw_ref[gi], preferred_element_type=F32))
    y_pool = jnp.concatenate(pooled_out, axis=-1) * ps_ref[...]
    ycat_scr[:, :POOL_WIDTH] = y_pool.astype(BF16)

    usb = u_ref[:, POOL_WIDTH:]
    us = usb.astype(F32)
    bu = [jnp.dot(usb[:, hf * S5_HALF:(hf + 1) * S5_HALF], bw_ref[hf], preferred_element_type=F32)
          for hf in range(2)]
    nt = tt // SUBLANES
    xr = jnp.concatenate([b[:, :S5_HALF_STATES] for b in bu], axis=-1).reshape(nt, SUBLANES, n_states)
    xi = jnp.concatenate([b[:, S5_HALF_STATES:] for b in bu], axis=-1).reshape(nt, SUBLANES, n_states)
    for k in range(S5_LOCAL_STEPS):
        cr = apr_ref[k * SUBLANES:(k + 1) * SUBLANES, :][None]
        ci = api_ref[k * SUBLANES:(k + 1) * SUBLANES, :][None]
        xr, xi = _complex_axpy(xr, xi, cr, ci, pltpu.roll(xr, 1 << k, 1), pltpu.roll(xi, 1 << k, 1))

    xr2 = xr.reshape(tt, n_states)
    xi2 = xi.reshape(tt, n_states)
    n_cb = n_states // LANES
    for cb in range(n_cb):
        xr_scr[cb] = xr2[:, cb * LANES:(cb + 1) * LANES]
        xi_scr[cb] = xi2[:, cb * LANES:(cb + 1) * LANES]
    tile_end = pl.ds(SUBLANES - 1, nt, stride=SUBLANES)
    er = jnp.concatenate([xr_scr[cb, tile_end, :] for cb in range(n_cb)], axis=-1)
    ei = jnp.concatenate([xi_scr[cb, tile_end, :] for cb in range(n_cb)], axis=-1)
    prev_r = sr_scr[...]
    prev_i = si_scr[...]
    tile_row = lax.broadcasted_iota(jnp.int32, (nt, n_states), 0)
    a8r = apr_ref[S5_TILE_POW0:S5_TILE_POW0 + 1, :]
    a8i = api_ref[S5_TILE_POW0:S5_TILE_POW0 + 1, :]
    er = er + jnp.where(tile_row == 0, a8r * prev_r - a8i * prev_i, 0.0)
    ei = ei + jnp.where(tile_row == 0, a8r * prev_i + a8i * prev_r, 0.0)
    k = 0
    while (1 << k) < nt:
        cr = apr_ref[S5_TILE_POW0 + k:S5_TILE_POW0 + k + 1, :]
        ci = api_ref[S5_TILE_POW0 + k:S5_TILE_POW0 + k + 1, :]
        pr = jnp.where(tile_row < (1 << k), 0.0, pltpu.roll(er, 1 << k, 0))
        pi = jnp.where(tile_row < (1 << k), 0.0, pltpu.roll(ei, 1 << k, 0))
        er, ei = _complex_axpy(er, ei, cr, ci, pr, pi)
        k += 1
    sr_scr[...] = er[nt - 1:nt, :]
    si_scr[...] = ei[nt - 1:nt, :]
    in_r = jnp.where(tile_row == 0, prev_r, pltpu.roll(er, 1, 0))
    in_i = jnp.where(tile_row == 0, prev_i, pltpu.roll(ei, 1, 0))

    rep_row = lax.broadcasted_iota(jnp.int32, (tt, nt), 0)
    rep_col = lax.broadcasted_iota(jnp.int32, (tt, nt), 1)
    rep = jnp.where(rep_row // SUBLANES == rep_col, 1.0, 0.0).astype(BF16)
    entering = jnp.concatenate([in_r, in_i], axis=-1)
    ent_hi = entering.astype(BF16)
    ent_lo = (entering - ent_hi.astype(F32)).astype(BF16)
    ent = (jnp.dot(rep, ent_hi, preferred_element_type=F32)
           + jnp.dot(rep, ent_lo, preferred_element_type=F32))
    cbr = ent[:, :n_states].reshape(nt, SUBLANES, n_states)
    cbi = ent[:, n_states:].reshape(nt, SUBLANES, n_states)
    pwr = apr_ref[S5_ROW_POW0:S5_ROW_POW0 + SUBLANES, :][None]
    pwi = api_ref[S5_ROW_POW0:S5_ROW_POW0 + SUBLANES, :][None]
    xr, xi = _complex_axpy(xr, xi, pwr, pwi, cbr, cbi)
    xr2 = xr.reshape(tt, n_states).astype(BF16)
    xi2 = xi.reshape(tt, n_states).astype(BF16)

    ys = []
    for hf in range(2):
        sl = slice(hf * S5_HALF_STATES, (hf + 1) * S5_HALF_STATES)
        xcat = jnp.concatenate([xr2[:, sl], xi2[:, sl]], axis=-1)
        ys.append(jnp.dot(xcat, cw_ref[hf], preferred_element_type=F32))
    y = jnp.concatenate(ys, axis=-1) + dsk_ref[...] * us
    y = jax.nn.gelu(y)
    glu = jnp.dot(y.astype(BF16), wglu_ref[...], preferred_element_type=F32)
    ycat_scr[:, POOL_WIDTH:] = (y * jax.nn.sigmoid(glu)).astype(BF16)

    y_out = jnp.dot(ycat_scr[...], wout_ref[...], preferred_element_type=F32)
    o_ref[...] = _post_norm_residual(x_ref[...], y_out, ng_ref[...], gate_ref[0])


def _side_spec(arr, n_steps, step_of):
    assert arr.shape[0] % n_steps == 0
    return pl.BlockSpec((arr.shape[0] // n_steps, arr.shape[1]), lambda *idx: (step_of(*idx), 0))


def _pool_s5_mixer(u, bsz, pool_w, pool_scale, lam_re, lam_im, log_step, b_re, b_im, c_re, c_im,
                   d_skip, w_glu, w_out, x, ng, gate, side, *, tt):
    t = u.shape[0]
    seq = t // bsz
    gp = S5_GROUPS * S5_STATE
    apr, api, bbr, bbi = _s5_discretise(lam_re, lam_im, log_step, b_re, b_im,
                                        int(math.log2(tt // SUBLANES)))
    n_pow = apr.shape[0]

    gh = S5_GROUPS // 2
    eye = jnp.eye(gh, dtype=F32)

    def in_map(bb):
        bb = bb.reshape(S5_GROUP, 2, gh, S5_STATE)
        return jnp.einsum("hxgp,gk->xghkp", bb, eye).reshape(2, gh * S5_GROUP, gh * S5_STATE)

    bw = jnp.concatenate([in_map(bbr), in_map(bbi)], axis=-1).astype(BF16)

    def out_map(cc):
        cc = cc.reshape(2, gh, S5_GROUP, S5_STATE)
        return jnp.einsum("xghp,gk->xgpkh", cc, eye).reshape(2, gh * S5_STATE, gh * S5_GROUP)

    cw = jnp.concatenate([out_map(c_re), -out_map(c_im)], axis=1).astype(BF16)

    const2 = lambda b, i: (0, 0)
    const3 = lambda b, i: (0, 0, 0)
    nblk = seq // tt
    side_spec = _side_spec(side, bsz * nblk, lambda b, i: b * nblk + i)
    return pl.pallas_call(
        _mix0_kernel,
        grid=(bsz, nblk),
        in_specs=[
            pl.BlockSpec((tt, D_MODEL), lambda b, i: (b * nblk + i, 0)),
            pl.BlockSpec((len(POOL_WINDOWS), POOL_GROUP, POOL_GROUP), const3),
            pl.BlockSpec((1, POOL_WIDTH), const2),
            pl.BlockSpec((2, S5_HALF, 2 * S5_HALF_STATES), const3),
            pl.BlockSpec((2, 2 * S5_HALF_STATES, S5_HALF), const3),
            pl.BlockSpec((n_pow, gp), const2),
            pl.BlockSpec((n_pow, gp), const2),
            pl.BlockSpec((1, S5_WIDTH), const2),
            pl.BlockSpec((S5_WIDTH, S5_WIDTH), const2),
            pl.BlockSpec((D_MODEL, D_MODEL), const2),
            pl.BlockSpec((tt, D_MODEL), lambda b, i: (b * nblk + i, 0)),
            pl.BlockSpec((1, D_MODEL), const2),
            pl.BlockSpec((1, 1, D_MODEL), lambda b, i: (b, 0, 0)),
            side_spec,
        ],
        out_specs=[pl.BlockSpec((tt, D_MODEL), lambda b, i: (b * nblk + i, 0)), side_spec],
        out_shape=[jax.ShapeDtypeStruct((t, D_MODEL), F32),
                   jax.ShapeDtypeStruct(side.shape, BF16)],
        scratch_shapes=[
            pltpu.VMEM((POOL_HALO, POOL_WIDTH), F32),
            pltpu.VMEM((tt + POOL_HALO, POOL_WIDTH), F32),
            pltpu.VMEM((1, gp), F32), pltpu.VMEM((1, gp), F32),
            pltpu.VMEM((gp // LANES, tt, LANES), F32), pltpu.VMEM((gp // LANES, tt, LANES), F32),
            pltpu.VMEM((tt, D_MODEL), BF16),
        ],
        compiler_params=_cparams(("arbitrary", "arbitrary")),
        name="pool_s5_mixer",
    )(u, pool_w.astype(BF16), pool_scale.reshape(1, POOL_WIDTH), bw, cw, apr, api,
      d_skip.reshape(1, S5_WIDTH), w_glu.astype(BF16), w_out, x, ng, gate, side)


CONV_HALO = SUBLANES
M2_GROUP_WIDTH = M2_HPG * M2_HEADDIM


def _causal_conv_silu(x_ref, halo_scr, full_scr, w_ref, b_ref, col0):
    lc = x_ref.shape[0]
    width = x_ref.shape[1]
    x = x_ref[...].astype(F32)
    full_scr[0:CONV_HALO, :] = halo_scr[...]
    full_scr[CONV_HALO:, :] = x
    halo_scr[...] = x[lc - CONV_HALO:, :]
    cols = slice(col0, col0 + width)
    acc = b_ref[:, cols] + w_ref[M2_CONV - 1:M2_CONV, cols] * x
    for k in range(M2_CONV - 1):
        off = CONV_HALO - (M2_CONV - 1) + k
        acc = acc + w_ref[k:k + 1, cols] * full_scr[off:off + lc, :]
    return acc * jax.nn.sigmoid(acc)


def _split_dot(v, e):
    hi = v.astype(BF16)
    lo = (v - hi.astype(F32)).astype(BF16)
    return (jnp.dot(hi, e, preferred_element_type=F32) + jnp.dot(lo, e, preferred_element_type=F32))


def _ssd_kernel(z_ref, xs_ref, bc_ref, dtr_ref, cw_ref, cb_ref, dtb_ref, alog_ref, dx_ref, ng_ref,
                e_ref, side_ref, o_ref, side_out_ref, halo_x, halo_bc, full_x, full_bc, st_scr, y_scr):
    side_out_ref[...] = side_ref[...].astype(BF16)
    lc = z_ref.shape[0]

    @pl.when(pl.program_id(1) == 0)
    def _():
        halo_x[...] = jnp.zeros_like(halo_x)
        halo_bc[...] = jnp.zeros_like(halo_bc)
        st_scr[...] = jnp.zeros_like(st_scr)

    xs = _causal_conv_silu(xs_ref, halo_x, full_x, cw_ref, cb_ref, 0)
    bc = _causal_conv_silu(bc_ref, halo_bc, full_bc, cw_ref, cb_ref, M2_INNER)

    lane = lax.broadcasted_iota(jnp.int32, (1, LANES), 1)
    x_dt = dtr_ref[...].astype(F32) + dtb_ref[...]
    dt = jnp.maximum(x_dt, 0.0) + jnp.log(1.0 + jnp.exp(-jnp.abs(x_dt)))
    a = jnp.where(lane < M2_HEADS, -jnp.exp(alog_ref[...]) * math.log2(math.e), 0.0)
    da = dt * a
    row = lax.broadcasted_iota(jnp.int32, (lc, lc), 0)
    col = lax.broadcasted_iota(jnp.int32, (lc, lc), 1)
    causal = col <= row
    cs = jnp.dot(jnp.where(causal, 1.0, 0.0), da, preferred_element_type=F32,
                 precision=lax.Precision.HIGHEST)
    cs_last = cs[lc - 1:lc, :]
    ecs = jnp.exp2(cs)
    w_in = dt * jnp.exp2(cs_last - cs)
    cs_t = cs.T
    dt_t = dt.T
    e = e_ref[...]
    ecs_x = _split_dot(ecs, e)
    w_x = _split_dot(w_in, e)
    xsb = xs.astype(BF16)
    xw = (xs * w_x).astype(BF16)

    for g in range(M2_GROUPS):
        bm = bc[:, g * M2_STATE:(g + 1) * M2_STATE].astype(BF16)
        cm = bc[:, M2_BC + g * M2_STATE:M2_BC + (g + 1) * M2_STATE].astype(BF16)
        cbm = lax.dot_general(cm, bm, (((1,), (1,)), ((), ())), preferred_element_type=F32)
        gcols = slice(g * M2_GROUP_WIDTH, (g + 1) * M2_GROUP_WIDTH)
        st = st_scr[g]
        y_off = jnp.dot(cm, st.astype(BF16), preferred_element_type=F32) * ecs_x[:, gcols]
        y_heads = []
        for jj in range(M2_HPG):
            j = g * M2_HPG + jj
            seg = cs[:, j:j + 1] - cs_t[j:j + 1, :]
            dec = jnp.exp2(jnp.where(causal, seg, NEG_BIG))
            m = (cbm * dec * dt_t[j:j + 1, :]).astype(BF16)
            y_heads.append(jnp.dot(m, xsb[:, j * M2_HEADDIM:(j + 1) * M2_HEADDIM],
                                   preferred_element_type=F32))
        y_g = jnp.concatenate(y_heads, axis=-1) + y_off
        st_scr[g] = st * ecs_x[lc - 1:lc, gcols] + lax.dot_general(
            bm, xw[:, gcols], (((0,), (0,)), ((), ())), preferred_element_type=F32)
        zg = z_ref[:, gcols].astype(F32)
        y_g = (y_g + dx_ref[:, gcols] * xs[:, gcols]) * (zg * jax.nn.sigmoid(zg))
        y_scr[:, gcols] = y_g

    y = y_scr[...]
    ms = jnp.mean(y * y, axis=-1, keepdims=True)
    o_ref[...] = (y * lax.rsqrt(ms + EPS) * ng_ref[...]).astype(o_ref.dtype)


def _ssd_mixer(proj, bsz, conv_w, conv_b, dt_bias, a_log, d_skip, norm_g, side, *, lc):
    t = proj.shape[0]
    seq = t // bsz
    nblk = seq // lc
    half = M2_INNER
    pad = LANES - M2_HEADS
    dtb = jnp.pad(dt_bias, (0, pad)).reshape(1, LANES)
    alog = jnp.pad(a_log, (0, pad)).reshape(1, LANES)
    dx = jnp.repeat(d_skip, M2_HEADDIM).reshape(1, M2_INNER)
    heads = jnp.arange(LANES, dtype=jnp.int32)[:, None]
    chans = jnp.arange(M2_INNER, dtype=jnp.int32)[None, :] // M2_HEADDIM
    expand = (heads == chans).astype(BF16)
    const2 = lambda b, i: (0, 0)
    side_spec = _side_spec(side, bsz * nblk, lambda b, i: b * nblk + i)
    return pl.pallas_call(
        _ssd_kernel,
        grid=(bsz, nblk),
        in_specs=[
            pl.BlockSpec((lc, half), lambda b, i: (b * nblk + i, 0)),
            pl.BlockSpec((lc, half), lambda b, i: (b * nblk + i, 1)),
            pl.BlockSpec((lc, half), lambda b, i: (b * nblk + i, 2)),
            pl.BlockSpec((lc, LANES), lambda b, i: (b * nblk + i, 3 * half // LANES)),
            pl.BlockSpec((M2_CONV, M2_CONV_DIM), const2),
            pl.BlockSpec((1, M2_CONV_DIM), const2),
            pl.BlockSpec((1, LANES), const2),
            pl.BlockSpec((1, LANES), const2),
            pl.BlockSpec((1, M2_INNER), const2),
            pl.BlockSpec((1, M2_INNER), const2),
            pl.BlockSpec((LANES, M2_INNER), const2),
            side_spec,
        ],
        out_specs=[pl.BlockSpec((lc, M2_INNER), lambda b, i: (b * nblk + i, 0)), side_spec],
        out_shape=[jax.ShapeDtypeStruct((t, M2_INNER), BF16),
                   jax.ShapeDtypeStruct(side.shape, BF16)],
        scratch_shapes=[
            pltpu.VMEM((CONV_HALO, half), F32), pltpu.VMEM((CONV_HALO, half), F32),
            pltpu.VMEM((lc + CONV_HALO, half), F32), pltpu.VMEM((lc + CONV_HALO, half), F32),
            pltpu.VMEM((M2_GROUPS, M2_STATE, M2_GROUP_WIDTH), F32),
            pltpu.VMEM((lc, M2_INNER), F32),
        ],
        compiler_params=_cparams(("arbitrary", "arbitrary")),
        name="ssd_mixer",
    )(proj, proj, proj, proj, conv_w, conv_b.reshape(1, M2_CONV_DIM), dtb, alog, dx,
      norm_g.reshape(1, M2_INNER), expand, side)


def kernel(x, c, ada_w, ada_b, norm_g, mix_w_in, pool_w, pool_scale, s5_lam_re, s5_lam_im, s5_log_step, s5_b_re, s5_b_im, s5_c_re, s5_c_im, s5_d, s5_w_glu, mix_w_out, ffn_w_gate, ffn_w_up, ffn_w_down, m2_w_in, m2_conv_w, m2_conv_b, m2_dt_bias, m2_a_log, m2_d, m2_norm_g, m2_w_out, moe_w_router, moe_b_router, moe_w_gate, moe_w_up, moe_w_down):
    bsz, seq, d = x.shape
    t = bsz * seq
    xt = x.reshape(t, d)

    mod = _ada_modulation(c, ada_w, ada_b)

    def mod_vecs(layer):
        return [mod[layer, :, k * d:(k + 1) * d].reshape(bsz, 1, d) for k in range(6)]

    def gvec(layer, k):
        return norm_g[layer, k].reshape(1, d)

    sh_m, sc_m, g_m, sh_f, sc_f, g_f = mod_vecs(0)
    u = _norm_matmul(xt, gvec(0, 0), sc_m, sh_m, mix_w_in[0].astype(BF16),
                     tm=1024, tn=1024, out_dtype=BF16, name="mix_in_proj")
    ne, dh = N_EXPERTS, EXPERT_HIDDEN
    xt, wu_b = _pool_s5_mixer(u, bsz, pool_w[0], pool_scale[0], s5_lam_re[0], s5_lam_im[0],
                              s5_log_step[0], s5_b_re[0], s5_b_im[0], s5_c_re[0], s5_c_im[0],
                              s5_d[0], s5_w_glu[0], mix_w_out[0].astype(BF16), xt, gvec(0, 1), g_m,
                              moe_w_up[0].reshape(ne * d, dh), tt=256)
    xt = _ffn_sublayer(xt, gvec(0, 2), sc_f, sh_f, ffn_w_gate[0].astype(BF16),
                       ffn_w_up[0].astype(BF16), ffn_w_down[0].astype(BF16), gvec(0, 3), g_f,
                       tm=1024, th=1408)

    sh_m, sc_m, g_m, sh_f, sc_f, g_f = mod_vecs(1)
    proj_pad = 6400
    w_in = jnp.zeros((d, proj_pad), BF16).at[:, :M2_PROJ].set(m2_w_in[0].astype(BF16))
    proj, (wg_b,) = _norm_matmul(
        xt, gvec(1, 0), sc_m, sh_m, w_in, tm=1024, tn=1280, out_dtype=BF16, name="m2_in_proj",
        side=(moe_w_gate[0].reshape(ne * d, dh),), side_steps=32)
    y, wd_b = _ssd_mixer(proj, bsz, m2_conv_w[0], m2_conv_b[0], m2_dt_bias[0], m2_a_log[0],
                         m2_d[0], m2_norm_g[0], moe_w_down[0].reshape(ne * dh, d), lc=128)
    xt = _matmul_post(y, m2_w_out[0].astype(BF16), xt, gvec(1, 1), g_m, tm=1024,
                      name="m2_out_proj")
    xt = _moe_sublayer(xt, gvec(1, 2), sc_f, sh_f, moe_w_router[0], moe_b_router[0],
                       wg_b.reshape(ne, d, dh), wu_b.reshape(ne, d, dh), wd_b.reshape(ne, dh, d),
                       gvec(1, 3), g_f)
    return xt.reshape(bsz, seq, d)
```

```python
import functools
import math

import jax
import jax.numpy as jnp
from jax import lax
from jax.experimental import pallas as pl
from jax.experimental.pallas import tpu as pltpu

F32 = jnp.float32
BF16 = jnp.bfloat16

D_MODEL = 1024
EPS = 1e-6
POOL_WIDTH = 512
POOL_WINDOWS = (2, 4, 8, 16)
POOL_GROUP = 128
S5_WIDTH = 512
S5_GROUP = 16
S5_GROUPS = 32
S5_STATE = 64
M2_INNER = 2048
M2_HEADDIM = 64
M2_HEADS = 32
M2_GROUPS = 8
M2_HPG = 4
M2_STATE = 128
M2_CONV = 4
M2_BC = 1024
M2_CONV_DIM = 4096
M2_PROJ = 6176
FFN_HIDDEN = 2816
N_EXPERTS = 8
EXPERT_HIDDEN = 3584

LANES = 128
SUBLANES = 8
VMEM_LIMIT_BYTES = 56 * 1024 * 1024

NEG_BIG = -1e30


def _cparams(sem):
    return pltpu.CompilerParams(dimension_semantics=sem, vmem_limit_bytes=VMEM_LIMIT_BYTES)


def _modulated_norm(x, g, scale, shift):
    ms = jnp.mean(x * x, axis=-1, keepdims=True)
    return (x * lax.rsqrt(ms + EPS) * g) * (1.0 + scale) + shift


def _post_norm_residual(x, y, ng, gate):
    ms = jnp.mean(y * y, axis=-1, keepdims=True)
    return x + gate * (y * lax.rsqrt(ms + EPS) * ng)


def _ada_kernel(c_ref, w_ref, b_ref, o_ref):
    c = c_ref[...]
    a = c * jax.nn.sigmoid(c)
    o_ref[0] = jnp.dot(a.astype(BF16), w_ref[0].astype(BF16),
                       preferred_element_type=F32) + b_ref[0]


def _ada_modulation(c, ada_w, ada_b):
    depth, d, n = ada_w.shape
    b = c.shape[0]
    c_pad = jnp.zeros((SUBLANES, d), F32).at[:b].set(c)
    tn = 1024
    out = pl.pallas_call(
        _ada_kernel,
        grid=(depth, n // tn),
        in_specs=[
            pl.BlockSpec((SUBLANES, d), lambda l, j: (0, 0)),
            pl.BlockSpec((1, d, tn), lambda l, j: (l, 0, j)),
            pl.BlockSpec((1, 1, tn), lambda l, j: (l, 0, j)),
        ],
        out_specs=pl.BlockSpec((1, SUBLANES, tn), lambda l, j: (l, 0, j)),
        out_shape=jax.ShapeDtypeStruct((depth, SUBLANES, n), F32),
        compiler_params=_cparams(("arbitrary", "arbitrary")),
        name="ada_modulation",
    )(c_pad, ada_w, ada_b.reshape(depth, 1, n))
    return out[:, :b]


def _norm_mm_kernel(x_ref, g_ref, sc_ref, sh_ref, w_ref, *rest, n_side, side_steps):
    side_in = rest[:n_side]
    o_ref = rest[n_side]
    side_out = rest[n_side + 1:2 * n_side + 1]
    h_scr = rest[2 * n_side + 1]

    @pl.when(pl.program_id(1) == 0)
    def _():
        h = _modulated_norm(x_ref[...], g_ref[...], sc_ref[0], sh_ref[0])
        h_scr[...] = h.astype(BF16)

    o_ref[...] = jnp.dot(h_scr[...], w_ref[...], preferred_element_type=F32).astype(o_ref.dtype)

    step = pl.program_id(0) * pl.num_programs(1) + pl.program_id(1)
    for k in range(n_side):
        @pl.when(jnp.logical_and(step >= k * side_steps, step < (k + 1) * side_steps))
        def _(k=k):
            side_out[k][...] = side_in[k][...].astype(BF16)


def _norm_matmul(x, g, scale, shift, w, *, tm, tn, out_dtype, name, side=(), side_steps=1):
    t, d = x.shape
    n = w.shape[1]
    n_j = n // tn
    tiles_per_batch = t // scale.shape[0] // tm
    assert len(side) * side_steps <= (t // tm) * n_j

    def side_spec(k, arr):
        rows = arr.shape[0] // side_steps
        return pl.BlockSpec(
            (rows, arr.shape[1]),
            lambda i, j: (jnp.clip(i * n_j + j - k * side_steps, 0, side_steps - 1), 0))

    side_specs = [side_spec(k, a) for k, a in enumerate(side)]
    outs = pl.pallas_call(
        functools.partial(_norm_mm_kernel, n_side=len(side), side_steps=side_steps),
        grid=(t // tm, n_j),
        in_specs=[
            pl.BlockSpec((tm, d), lambda i, j: (i, 0)),
            pl.BlockSpec((1, d), lambda i, j: (0, 0)),
            pl.BlockSpec((1, 1, d), lambda i, j: (i // tiles_per_batch, 0, 0)),
            pl.BlockSpec((1, 1, d), lambda i, j: (i // tiles_per_batch, 0, 0)),
            pl.BlockSpec((d, tn), lambda i, j: (0, j)),
        ] + side_specs,
        out_specs=[pl.BlockSpec((tm, tn), lambda i, j: (i, j))] + side_specs,
        out_shape=[jax.ShapeDtypeStruct((t, n), out_dtype)]
                  + [jax.ShapeDtypeStruct(a.shape, BF16) for a in side],
        scratch_shapes=[pltpu.VMEM((tm, d), BF16)],
        compiler_params=_cparams(("arbitrary", "arbitrary")),
        name=name,
    )(x, g, scale, shift, w, *side)
    return (outs[0], outs[1:]) if side else outs[0]


def _mm_post_kernel(y_ref, w_ref, x_ref, ng_ref, gate_ref, o_ref):
    y = jnp.dot(y_ref[...], w_ref[...], preferred_element_type=F32)
    o_ref[...] = _post_norm_residual(x_ref[...], y, ng_ref[...], gate_ref[0])


def _matmul_post(y, w, x, ng, gate, *, tm, name):
    t, k = y.shape
    d = w.shape[1]
    tiles_per_batch = t // gate.shape[0] // tm
    return pl.pallas_call(
        _mm_post_kernel,
        grid=(t // tm,),
        in_specs=[
            pl.BlockSpec((tm, k), lambda i: (i, 0)),
            pl.BlockSpec((k, d), lambda i: (0, 0)),
            pl.BlockSpec((tm, d), lambda i: (i, 0)),
            pl.BlockSpec((1, d), lambda i: (0, 0)),
            pl.BlockSpec((1, 1, d), lambda i: (i // tiles_per_batch, 0, 0)),
        ],
        out_specs=pl.BlockSpec((tm, d), lambda i: (i, 0)),
        out_shape=jax.ShapeDtypeStruct((t, d), F32),
        compiler_params=_cparams(("arbitrary",)),
        name=name,
    )(y, w, x, ng, gate)


def _ffn_kernel(x_ref, g_ref, sc_ref, sh_ref, wgu_ref, wd_ref, ng_ref, gate_ref,
                o_ref, h_scr, acc_scr):
    j = pl.program_id(1)

    @pl.when(j == 0)
    def _():
        h = _modulated_norm(x_ref[...], g_ref[...], sc_ref[0], sh_ref[0])
        h_scr[...] = h.astype(BF16)
        acc_scr[...] = jnp.zeros_like(acc_scr)

    th = wd_ref.shape[0]
    gu = jnp.dot(h_scr[...], wgu_ref[...], preferred_element_type=F32)
    gt = gu[:, :th]
    up = gu[:, th:]
    act = (gt * jax.nn.sigmoid(gt) * up).astype(BF16)
    acc_scr[...] += jnp.dot(act, wd_ref[...], preferred_element_type=F32)

    @pl.when(j == pl.num_programs(1) - 1)
    def _():
        o_ref[...] = _post_norm_residual(x_ref[...], acc_scr[...], ng_ref[...], gate_ref[0])


def _ffn_sublayer(x, g, scale, shift, wg, wu, wd, ng, gate, *, tm, th):
    t, d = x.shape
    hid = wg.shape[1]
    n_j = hid // th
    tiles_per_batch = t // scale.shape[0] // tm
    bvec = pl.BlockSpec((1, 1, d), lambda i, j: (i // tiles_per_batch, 0, 0))
    wgu = jnp.concatenate([w[:, j * th:(j + 1) * th] for j in range(n_j) for w in (wg, wu)], axis=1)
    return pl.pallas_call(
        _ffn_kernel,
        grid=(t // tm, n_j),
        in_specs=[
            pl.BlockSpec((tm, d), lambda i, j: (i, 0)),
            pl.BlockSpec((1, d), lambda i, j: (0, 0)),
            bvec, bvec,
            pl.BlockSpec((d, 2 * th), lambda i, j: (0, j)),
            pl.BlockSpec((th, d), lambda i, j: (j, 0)),
            pl.BlockSpec((1, d), lambda i, j: (0, 0)),
            bvec,
        ],
        out_specs=pl.BlockSpec((tm, d), lambda i, j: (i, 0)),
        out_shape=jax.ShapeDtypeStruct((t, d), F32),
        scratch_shapes=[pltpu.VMEM((tm, d), BF16), pltpu.VMEM((tm, d), F32)],
        compiler_params=_cparams(("arbitrary", "arbitrary")),
        name="ffn_sublayer",
    )(x, g, scale, shift, wgu, wd, ng, gate)


def _router_kernel(x_ref, g_ref, sc_ref, sh_ref, wr_ref, br_ref, h_ref, meta_ref, cnt_ref):
    tm = x_ref.shape[0]

    h = _modulated_norm(x_ref[...], g_ref[...], sc_ref[0], sh_ref[0])
    h_hi = h.astype(BF16)
    h_ref[...] = h_hi
    h_lo = (h - h_hi.astype(F32)).astype(BF16)
    logits = (jnp.dot(h_hi, wr_ref[0], preferred_element_type=F32)
              + jnp.dot(h_lo, wr_ref[0], preferred_element_type=F32)
              + jnp.dot(h_hi, wr_ref[1], preferred_element_type=F32)) + br_ref[...]
    lane = lax.broadcasted_iota(jnp.int32, (tm, LANES), 1).astype(F32)
    m1 = jnp.max(logits, axis=-1, keepdims=True)
    i1 = jnp.min(jnp.where(logits == m1, lane, float(LANES)), axis=-1, keepdims=True)
    oh1 = lane == i1
    rest = jnp.where(oh1, NEG_BIG * 2.0, logits)
    m2 = jnp.max(rest, axis=-1, keepdims=True)
    i2 = jnp.min(jnp.where(rest == m2, lane, float(LANES)), axis=-1, keepdims=True)
    oh2 = lane == i2
    e = jnp.exp(m2 - m1)
    w1 = 1.0 / (1.0 + e)
    w2 = e / (1.0 + e)

    picks = jnp.where(oh1, 1.0, 0.0) + jnp.where(oh2, 1.0, 0.0)
    row = lax.broadcasted_iota(jnp.int32, (tm, tm), 0)
    col = lax.broadcasted_iota(jnp.int32, (tm, tm), 1)
    lower = jnp.where(col < row, 1.0, 0.0).astype(BF16)
    before = jnp.dot(lower, picks.astype(BF16), preferred_element_type=F32)
    tile_cnt = jnp.sum(picks, axis=0, keepdims=True)
    ea = lax.broadcasted_iota(jnp.int32, (LANES, LANES), 0)
    eb = lax.broadcasted_iota(jnp.int32, (LANES, LANES), 1)
    seg_rows = jnp.floor((tile_cnt + (SUBLANES - 1.0)) * (1.0 / SUBLANES)) * SUBLANES
    seg_off = jnp.dot(jnp.broadcast_to(seg_rows, (SUBLANES, LANES)), jnp.where(ea < eb, 1.0, 0.0),
                      preferred_element_type=F32, precision=lax.Precision.HIGHEST)[0:1, :]
    local = before + seg_off
    loc1 = jnp.sum(jnp.where(oh1, local, 0.0), axis=-1, keepdims=True)
    loc2 = jnp.sum(jnp.where(oh2, local, 0.0), axis=-1, keepdims=True)
    cnt_ref[0] = tile_cnt

    meta = jnp.where(lane == 0.0, i1, 0.0)
    meta = jnp.where(lane == 1.0, i2, meta)
    meta = jnp.where(lane == 2.0, w1, meta)
    meta = jnp.where(lane == 3.0, w2, meta)
    meta = jnp.where(lane == 4.0, loc1, meta)
    meta = jnp.where(lane == 5.0, loc2, meta)
    meta_ref[...] = meta


def _router(x, g, scale, shift, w_router, b_router, *, tm):
    t, d = x.shape
    tiles_per_batch = t // scale.shape[0] // tm
    wr = jnp.zeros((d, LANES), F32).at[:, :N_EXPERTS].set(w_router)
    wr_hi = wr.astype(BF16)
    wr = jnp.stack([wr_hi, (wr - wr_hi.astype(F32)).astype(BF16)])
    br = jnp.full((1, LANES), NEG_BIG, F32).at[0, :N_EXPERTS].set(b_router)
    bvec = pl.BlockSpec((1, 1, d), lambda i: (i // tiles_per_batch, 0, 0))
    return pl.pallas_call(
        _router_kernel,
        grid=(t // tm,),
        in_specs=[
            pl.BlockSpec((tm, d), lambda i: (i, 0)),
            pl.BlockSpec((1, d), lambda i: (0, 0)),
            bvec, bvec,
            pl.BlockSpec((2, d, LANES), lambda i: (0, 0, 0)),
            pl.BlockSpec((1, LANES), lambda i: (0, 0)),
        ],
        out_specs=[
            pl.BlockSpec((tm, d), lambda i: (i, 0)),
            pl.BlockSpec((tm, LANES), lambda i: (i, 0)),
            pl.BlockSpec((1, 1, LANES), lambda i: (i, 0, 0)),
        ],
        out_shape=[
            jax.ShapeDtypeStruct((t, d), BF16),
            jax.ShapeDtypeStruct((t, LANES), F32),
            jax.ShapeDtypeStruct((t // tm, 1, LANES), F32),
        ],
        compiler_params=_cparams(("arbitrary",)),
        name="moe_router",
    )(x, g, scale, shift, wr, br)


SEG_FIELDS = 3
SEG_PAD_ROWS = N_EXPERTS * SUBLANES


def _segment_copies(scal_ref, tile, tm, per_copy):
    base = tile * (SEG_FIELDS * N_EXPERTS)
    for e in range(N_EXPERTS):
        n = scal_ref[base + e]
        g0 = scal_ref[base + N_EXPERTS + e]
        l0 = scal_ref[base + 2 * N_EXPERTS + e]
        done = 0
        b = tm
        while b >= SUBLANES:
            take = n & b

            @pl.when(take != 0)
            def _(b=b, done=done, g0=g0, l0=l0):
                per_copy(pl.multiple_of(l0 + done, SUBLANES), pl.multiple_of(g0 + done, SUBLANES), b)

            done = done + take
            b //= 2


def _dispatch_kernel(scal_ref, tail_ref, h_ref, meta_ref, xs_ref, ws_ref,
                     seg_scr, wseg_scr, zero_scr, sem, zsem, *, n_tiles, max_unused):
    tile = pl.program_id(0)
    tm = h_ref.shape[0]
    ns = seg_scr.shape[1]
    par = tile % 2

    def segment_dmas(which, buf):
        def descriptors(local, glob, n):
            return (pltpu.make_async_copy(wseg_scr.at[buf, pl.ds(local, n), :],
                                          ws_ref.at[pl.ds(glob, n), :], sem.at[buf]),
                    pltpu.make_async_copy(seg_scr.at[buf, pl.ds(local, n), :],
                                          xs_ref.at[pl.ds(glob, n), :], sem.at[buf]))
        return descriptors

    def wait_segments(which, buf):
        descriptors = segment_dmas(which, buf)
        _segment_copies(scal_ref, which, tm, lambda l, g, n: [c.wait() for c in descriptors(l, g, n)])

    @pl.when(tile == 0)
    def _():
        zero_scr[...] = jnp.zeros_like(zero_scr)
        fills = []
        n_slot_tiles = xs_ref.shape[0] // tm
        tails = [(tail_ref[N_EXPERTS + e] > 0, pl.multiple_of(tail_ref[e], tm))
                 for e in range(N_EXPERTS)]
        tails += [(k >= tail_ref[2 * N_EXPERTS], k * tm)
                  for k in range(n_slot_tiles - max_unused, n_slot_tiles)]
        for nonempty, tail in tails:
            fills.append((nonempty,
                          pltpu.make_async_copy(zero_scr, xs_ref.at[pl.ds(tail, tm), :], zsem),
                          pltpu.make_async_copy(zero_scr.at[:, 0:LANES],
                                                ws_ref.at[pl.ds(tail, tm), :], zsem)))
        for nonempty, fill_x, fill_w in fills:
            @pl.when(nonempty)
            def _(fill_x=fill_x, fill_w=fill_w):
                fill_x.start()
                fill_w.start()
        for nonempty, fill_x, fill_w in fills:
            @pl.when(nonempty)
            def _(fill_x=fill_x, fill_w=fill_w):
                fill_x.wait()
                fill_w.wait()

    meta = meta_ref[...]
    meta_t = meta.T
    slot = lax.broadcasted_iota(jnp.int32, (ns, tm), 0).astype(F32)
    g1 = jnp.where(slot == meta_t[4:5, :], 1.0, 0.0).astype(BF16)
    g2 = jnp.where(slot == meta_t[5:6, :], 1.0, 0.0).astype(BF16)
    rows = jnp.dot(g1 + g2, h_ref[...], preferred_element_type=F32)
    lane = lax.broadcasted_iota(jnp.int32, (tm, LANES), 1)

    def split_lanes(w):
        hi = w.astype(BF16).astype(F32)
        return jnp.where(lane == 0, hi, jnp.where(lane == 1, w - hi, 0.0)).astype(BF16)

    w_split = jnp.concatenate([split_lanes(meta[:, 2:3]), split_lanes(meta[:, 3:4])], axis=0)
    w_slot = jnp.dot(jnp.concatenate([g1, g2], axis=1), w_split, preferred_element_type=F32)
    weight = w_slot[:, 0:1] + w_slot[:, 1:2]

    @pl.when(tile >= 2)
    def _():
        wait_segments(tile - 2, par)

    seg_scr[par] = rows
    wseg_scr[par] = jnp.broadcast_to(weight, (ns, LANES))
    descriptors = segment_dmas(tile, par)
    _segment_copies(scal_ref, tile, tm, lambda l, g, n: [c.start() for c in descriptors(l, g, n)])

    @pl.when(tile == n_tiles - 1)
    def _():
        wait_segments(tile, par)
        if n_tiles > 1:
            wait_segments(tile - 1, 1 - par)


def _dispatch(h, meta, seg_table, tail_table, n_slots, *, tm):
    t, d = h.shape
    grid_spec = pltpu.PrefetchScalarGridSpec(
        num_scalar_prefetch=2,
        grid=(t // tm,),
        in_specs=[
            pl.BlockSpec((tm, d), lambda i, sc, tl: (i, 0)),
            pl.BlockSpec((tm, LANES), lambda i, sc, tl: (i, 0)),
        ],
        out_specs=[pl.BlockSpec(memory_space=pl.ANY), pl.BlockSpec(memory_space=pl.ANY)],
        scratch_shapes=[
            pltpu.VMEM((2, 2 * tm + SEG_PAD_ROWS, d), F32),
            pltpu.VMEM((2, 2 * tm + SEG_PAD_ROWS, LANES), F32),
            pltpu.VMEM((tm, d), F32),
            pltpu.SemaphoreType.DMA((2,)), pltpu.SemaphoreType.DMA(()),
        ],
    )
    return pl.pallas_call(
        functools.partial(_dispatch_kernel, n_tiles=t // tm,
                          max_unused=n_slots // tm - 2 * t // tm),
        grid_spec=grid_spec,
        out_shape=[jax.ShapeDtypeStruct((n_slots, d), F32),
                   jax.ShapeDtypeStruct((n_slots, LANES), F32)],
        compiler_params=_cparams(("arbitrary",)),
        name="moe_dispatch",
    )(seg_table, tail_table, h, meta)


def _expert_kernel(te_ref, nu_ref, xs_ref, ws_ref, wg_ref, wu_ref, wd_ref, ys_ref,
                   xb_scr, acc_scr):
    i = pl.program_id(0)
    j = pl.program_id(1)
    last = pl.num_programs(1) - 1
    used = i < nu_ref[0]

    @pl.when(used)
    def _():
        @pl.when(j == 0)
        def _():
            xb_scr[...] = xs_ref[...].astype(BF16)
            acc_scr[...] = jnp.zeros_like(acc_scr)

        x = xb_scr[...]
        gt = jnp.dot(x, wg_ref[0], preferred_element_type=F32)
        up = jnp.dot(x, wu_ref[0], preferred_element_type=F32)
        act = (gt * jax.nn.sigmoid(gt) * up).astype(BF16)
        acc_scr[...] += jnp.dot(act, wd_ref[0], preferred_element_type=F32)

        @pl.when(j == last)
        def _():
            ys_ref[...] = (acc_scr[...] * ws_ref[:, 0:1]).astype(ys_ref.dtype)

    @pl.when(jnp.logical_and(jnp.logical_not(used), j == last))
    def _():
        ys_ref[...] = jnp.zeros_like(ys_ref)


def _experts(xs, ws, tile_expert, n_used, wg, wu, wd, *, tm, th):
    n_slots, d = xs.shape
    hid = wg.shape[2]
    nt = n_slots // tm

    def hidden_step(i, j, nu):
        return jnp.where(i < nu[0], j, 0)

    def slot_tile(i, nu):
        return jnp.where(i < nu[0], i, 0)

    grid_spec = pltpu.PrefetchScalarGridSpec(
        num_scalar_prefetch=2,
        grid=(nt, hid // th),
        in_specs=[
            pl.BlockSpec((tm, d), lambda i, j, te, nu: (slot_tile(i, nu), 0)),
            pl.BlockSpec((tm, LANES), lambda i, j, te, nu: (slot_tile(i, nu), 0)),
            pl.BlockSpec((1, d, th), lambda i, j, te, nu: (te[i], 0, hidden_step(i, j, nu))),
            pl.BlockSpec((1, d, th), lambda i, j, te, nu: (te[i], 0, hidden_step(i, j, nu))),
            pl.BlockSpec((1, th, d), lambda i, j, te, nu: (te[i], hidden_step(i, j, nu), 0)),
        ],
        out_specs=pl.BlockSpec((tm, d), lambda i, j, te, nu: (i, 0)),
        scratch_shapes=[pltpu.VMEM((tm, d), BF16), pltpu.VMEM((tm, d), F32)],
    )
    return pl.pallas_call(
        _expert_kernel,
        grid_spec=grid_spec,
        out_shape=jax.ShapeDtypeStruct((n_slots, d), F32),
        compiler_params=_cparams(("arbitrary", "arbitrary")),
        name="moe_experts",
    )(tile_expert, n_used, xs, ws, wg, wu, wd)


def _combine_kernel(scal_ref, ys_ref, meta_ref, x_ref, ng_ref, gate_ref, o_ref, buf, sem,
                    *, n_tiles):
    tile = pl.program_id(0)
    tm = x_ref.shape[0]
    ns = buf.shape[1]
    par = tile % 2

    def fetch(which, slot, wait):
        def copy_in(local, glob, n):
            dma = pltpu.make_async_copy(ys_ref.at[pl.ds(glob, n), :],
                                        buf.at[slot, pl.ds(local, n), :], sem.at[slot])
            dma.wait() if wait else dma.start()
        _segment_copies(scal_ref, which, tm, copy_in)

    @pl.when(tile == 0)
    def _():
        buf[...] = jnp.zeros_like(buf)
        fetch(tile, par, False)

    @pl.when(tile + 1 < n_tiles)
    def _():
        fetch(tile + 1, 1 - par, False)

    fetch(tile, par, True)
    meta = meta_ref[...]
    slot_id = lax.broadcasted_iota(jnp.int32, (tm, ns), 1).astype(F32)
    pick = jnp.where((slot_id == meta[:, 4:5]) | (slot_id == meta[:, 5:6]), 1.0, 0.0).astype(BF16)
    y = jnp.dot(pick, buf[par].astype(BF16), preferred_element_type=F32)
    o_ref[...] = _post_norm_residual(x_ref[...], y, ng_ref[...], gate_ref[0])


def _combine(ys, meta, seg_table, x, ng, gate, *, tm):
    t, d = x.shape
    tiles_per_batch = t // gate.shape[0] // tm
    grid_spec = pltpu.PrefetchScalarGridSpec(
        num_scalar_prefetch=1,
        grid=(t // tm,),
        in_specs=[
            pl.BlockSpec(memory_space=pl.ANY),
            pl.BlockSpec((tm, LANES), lambda i, sc: (i, 0)),
            pl.BlockSpec((tm, d), lambda i, sc: (i, 0)),
            pl.BlockSpec((1, d), lambda i, sc: (0, 0)),
            pl.BlockSpec((1, 1, d), lambda i, sc: (i // tiles_per_batch, 0, 0)),
        ],
        out_specs=pl.BlockSpec((tm, d), lambda i, sc: (i, 0)),
        scratch_shapes=[pltpu.VMEM((2, 2 * tm + SEG_PAD_ROWS, d), F32),
                        pltpu.SemaphoreType.DMA((2,))],
    )
    return pl.pallas_call(
        functools.partial(_combine_kernel, n_tiles=t // tm),
        grid_spec=grid_spec,
        out_shape=jax.ShapeDtypeStruct((t, d), F32),
        compiler_params=_cparams(("arbitrary",)),
        name="moe_combine",
    )(seg_table, ys, meta, x, ng, gate)


def _moe_sublayer(x, g, scale, shift, w_router, b_router, wg, wu, wd, ng, gate):
    t = x.shape[0]
    tm, tm_e, th = 512, 512, 1792
    h, meta, counts = _router(x, g, scale, shift, w_router, b_router, tm=tm)

    assert tm == tm_e
    tile_cnt = counts[:, 0, :N_EXPERTS].astype(jnp.int32)
    seg_rows = ((tile_cnt + SUBLANES - 1) // SUBLANES) * SUBLANES
    cnt = jnp.sum(seg_rows, axis=0)
    padded = ((cnt + tm_e - 1) // tm_e) * tm_e
    ends = jnp.cumsum(padded)
    offs = ends - padded
    glob_start = offs[None, :] + jnp.cumsum(seg_rows, axis=0) - seg_rows
    local_start = jnp.cumsum(seg_rows, axis=1) - seg_rows
    seg_table = jnp.concatenate([seg_rows, glob_start, local_start], axis=1).reshape(-1)
    tail_table = jnp.concatenate([ends - tm_e, padded, ends[-1:] // tm_e]).astype(jnp.int32)
    n_tok_tiles = t // tm
    nt = -(-(2 * t + n_tok_tiles * SEG_PAD_ROWS) // tm_e) + N_EXPERTS
    starts = jnp.arange(nt, dtype=jnp.int32) * tm_e
    tile_expert = jnp.minimum(jnp.sum(starts[:, None] >= ends[None, :], axis=1), N_EXPERTS - 1)
    tile_expert = tile_expert.astype(jnp.int32)
    n_used = (ends[-1:] // tm_e).astype(jnp.int32)

    xs, ws = _dispatch(h, meta, seg_table, tail_table, nt * tm_e, tm=tm)
    ys = _experts(xs, ws, tile_expert, n_used, wg, wu, wd, tm=tm_e, th=th)
    return _combine(ys, meta, seg_table, x, ng, gate, tm=tm)


def _s5_disc_kernel(lam_re_ref, lam_im_ref, step_ref, pow_ref, keep_ref, b_re_ref, b_im_ref,
                    apr_ref, api_ref, bbr_ref, bbi_ref):
    lr = jnp.minimum(lam_re_ref[...], -1e-4)
    li = lam_im_ref[...]
    step = step_ref[...]
    m = pow_ref[...]
    mag = jnp.exp(lr * step * m) * keep_ref[...]
    ang = li * step * m
    apr_ref[...] = mag * jnp.cos(ang)
    api_ref[...] = mag * jnp.sin(ang)
    mag1 = jnp.exp(lr * step)
    ar = mag1 * jnp.cos(li * step)
    ai = mag1 * jnp.sin(li * step)
    inv = 1.0 / (lr * lr + li * li)
    fr = ((ar - 1.0) * lr + ai * li) * inv
    fi = (ai * lr - (ar - 1.0) * li) * inv
    br = b_re_ref[...]
    bi = b_im_ref[...]
    bbr_ref[...] = fr * br - fi * bi
    bbi_ref[...] = fr * bi + fi * br


S5_LOCAL_STEPS = 3
S5_ROW_POW0 = S5_LOCAL_STEPS * SUBLANES
S5_TILE_POW0 = S5_ROW_POW0 + SUBLANES


def _s5_power_rows(n_tile_steps):
    exps, keep = [], []
    for k in range(S5_LOCAL_STEPS):
        for tau in range(SUBLANES):
            exps.append(float(1 << k))
            keep.append(1.0 if tau >= (1 << k) else 0.0)
    for tau in range(SUBLANES):
        exps.append(float(tau + 1))
        keep.append(1.0)
    for k in range(n_tile_steps):
        exps.append(float(SUBLANES << k))
        keep.append(1.0)
    while len(exps) % SUBLANES:
        exps.append(0.0)
        keep.append(0.0)
    return exps, keep


def _s5_discretise(lam_re, lam_im, log_step, b_re, b_im, n_tile_steps):
    gp = S5_GROUPS * S5_STATE
    step = jnp.repeat(jnp.exp(log_step), S5_STATE).reshape(1, gp)
    exps, keep = _s5_power_rows(n_tile_steps)
    rows = len(exps)
    b_re_t = jnp.transpose(b_re, (2, 0, 1)).reshape(S5_GROUP, gp)
    b_im_t = jnp.transpose(b_im, (2, 0, 1)).reshape(S5_GROUP, gp)
    return pl.pallas_call(
        _s5_disc_kernel,
        out_shape=[jax.ShapeDtypeStruct((rows, gp), F32), jax.ShapeDtypeStruct((rows, gp), F32),
                   jax.ShapeDtypeStruct((S5_GROUP, gp), F32), jax.ShapeDtypeStruct((S5_GROUP, gp), F32)],
        name="s5_discretise",
    )(lam_re.reshape(1, gp), lam_im.reshape(1, gp), step,
      jnp.asarray(exps, F32).reshape(rows, 1), jnp.asarray(keep, F32).reshape(rows, 1),
      b_re_t, b_im_t)


POOL_HALO = 16
S5_HALF = 256
S5_HALF_STATES = 1024


def _complex_axpy(xr, xi, cr, ci, sr, si):
    return xr + (cr * sr - ci * si), xi + (cr * si + ci * sr)


def _mix0_kernel(u_ref, pw_ref, ps_ref, bw_ref, cw_ref, apr_ref, api_ref, dsk_ref, wglu_ref,
                 wout_ref, x_ref, ng_ref, gate_ref, side_ref, o_ref, side_out_ref,
                 halo_scr, ext_scr, sr_scr, si_scr, xr_scr, xi_scr, ycat_scr):
    side_out_ref[...] = side_ref[...].astype(BF16)
    tt = u_ref.shape[0]
    n_states = sr_scr.shape[1]
    blk = pl.program_id(1)

    @pl.when(blk == 0)
    def _():
        halo_scr[...] = jnp.zeros_like(halo_scr)
        sr_scr[...] = jnp.zeros_like(sr_scr)
        si_scr[...] = jnp.zeros_like(si_scr)

    up = u_ref[:, :POOL_WIDTH].astype(F32)
    ext_scr[0:POOL_HALO, :] = halo_scr[...]
    ext_scr[POOL_HALO:, :] = up
    halo_scr[...] = up[tt - POOL_HALO:, :]
    ext = ext_scr[...].astype(BF16)
    row = lax.broadcasted_iota(jnp.int32, (tt, tt + POOL_HALO), 0)
    col = lax.broadcasted_iota(jnp.int32, (tt, tt + POOL_HALO), 1)
    lag = row + POOL_HALO - col
    t_glob = (blk * tt + row + 1).astype(F32)
    pooled_out = []
    for gi, win in enumerate(POOL_WINDOWS):
        inv_count = 1.0 / jnp.minimum(t_glob, float(win))
        band = jnp.where((lag >= 0) & (lag < win), inv_count, 0.0) - jnp.where(lag == 0, 1.0, 0.0)
        pooled = jnp.dot(band.astype(BF16), ext[:, gi * POOL_GROUP:(gi + 1) * POOL_GROUP],
                         preferred_element_type=F32)
        pooled_out.append(jnp.dot(pooled.astype(BF16), pw_ref[gi], preferred_element_type=F32))
    y_pool = jnp.concatenate(pooled_out, axis=-1) * ps_ref[...]
    ycat_scr[:, :POOL_WIDTH] = y_pool.astype(BF16)

    usb = u_ref[:, POOL_WIDTH:]
    us = usb.astype(F32)
    bu = [jnp.dot(usb[:, hf * S5_HALF:(hf + 1) * S5_HALF], bw_ref[hf], preferred_element_type=F32)
          for hf in range(2)]
    nt = tt // SUBLANES
    xr = jnp.concatenate([b[:, :S5_HALF_STATES] for b in bu], axis=-1).reshape(nt, SUBLANES, n_states)
    xi = jnp.concatenate([b[:, S5_HALF_STATES:] for b in bu], axis=-1).reshape(nt, SUBLANES, n_states)
    for k in range(S5_LOCAL_STEPS):
        cr = apr_ref[k * SUBLANES:(k + 1) * SUBLANES, :][None]
        ci = api_ref[k * SUBLANES:(k + 1) * SUBLANES, :][None]
        xr, xi = _complex_axpy(xr, xi, cr, ci, pltpu.roll(xr, 1 << k, 1), pltpu.roll(xi, 1 << k, 1))

    xr2 = xr.reshape(tt, n_states)
    xi2 = xi.reshape(tt, n_states)
    n_cb = n_states // LANES
    for cb in range(n_cb):
        xr_scr[cb] = xr2[:, cb * LANES:(cb + 1) * LANES]
        xi_scr[cb] = xi2[:, cb * LANES:(cb + 1) * LANES]
    tile_end = pl.ds(SUBLANES - 1, nt, stride=SUBLANES)
    er = jnp.concatenate([xr_scr[cb, tile_end, :] for cb in range(n_cb)], axis=-1)
    ei = jnp.concatenate([xi_scr[cb, tile_end, :] for cb in range(n_cb)], axis=-1)
    prev_r = sr_scr[...]
    prev_i = si_scr[...]
    tile_row = lax.broadcasted_iota(jnp.int32, (nt, n_states), 0)
    a8r = apr_ref[S5_TILE_POW0:S5_TILE_POW0 + 1, :]
    a8i = api_ref[S5_TILE_POW0:S5_TILE_POW0 + 1, :]
    er = er + jnp.where(tile_row == 0, a8r * prev_r - a8i * prev_i, 0.0)
    ei = ei + jnp.where(tile_row == 0, a8r * prev_i + a8i * prev_r, 0.0)
    k = 0
    while (1 << k) < nt:
        cr = apr_ref[S5_TILE_POW0 + k:S5_TILE_POW0 + k + 1, :]
        ci = api_ref[S5_TILE_POW0 + k:S5_TILE_POW0 + k + 1, :]
        pr = jnp.where(tile_row < (1 << k), 0.0, pltpu.roll(er, 1 << k, 0))
        pi = jnp.where(tile_row < (1 << k), 0.0, pltpu.roll(ei, 1 << k, 0))
        er, ei = _complex_axpy(er, ei, cr, ci, pr, pi)
        k += 1
    sr_scr[...] = er[nt - 1:nt, :]
    si_scr[...] = ei[nt - 1:nt, :]
    in_r = jnp.where(tile_row == 0, prev_r, pltpu.roll(er, 1, 0))
    in_i = jnp.where(tile_row == 0, prev_i, pltpu.roll(ei, 1, 0))

    rep_row = lax.broadcasted_iota(jnp.int32, (tt, nt), 0)
    rep_col = lax.broadcasted_iota(jnp.int32, (tt, nt), 1)
    rep = jnp.where(rep_row // SUBLANES == rep_col, 1.0, 0.0).astype(BF16)
    entering = jnp.concatenate([in_r, in_i], axis=-1)
    ent_hi = entering.astype(BF16)
    ent_lo = (entering - ent_hi.astype(F32)).astype(BF16)
    ent = (jnp.dot(rep, ent_hi, preferred_element_type=F32)
           + jnp.dot(rep, ent_lo, preferred_element_type=F32))
    cbr = ent[:, :n_states].reshape(nt, SUBLANES, n_states)
    cbi = ent[:, n_states:].reshape(nt, SUBLANES, n_states)
    pwr = apr_ref[S5_ROW_POW0:S5_ROW_POW0 + SUBLANES, :][None]
    pwi = api_ref[S5_ROW_POW0:S5_ROW_POW0 + SUBLANES, :][None]
    xr, xi = _complex_axpy(xr, xi, pwr, pwi, cbr, cbi)
    xr2 = xr.reshape(tt, n_states).astype(BF16)
    xi2 = xi.reshape(tt, n_states).astype(BF16)

    ys = []
    for hf in range(2):
        sl = slice(hf * S5_HALF_STATES, (hf + 1) * S5_HALF_STATES)
        xcat = jnp.concatenate([xr2[:, sl], xi2[:, sl]], axis=-1)
        ys.append(jnp.dot(xcat, cw_ref[hf], preferred_element_type=F32))
    y = jnp.concatenate(ys, axis=-1) + dsk_ref[...] * us
    y = jax.nn.gelu(y)
    glu = jnp.dot(y.astype(BF16), wglu_ref[...], preferred_element_type=F32)
    ycat_scr[:, POOL_WIDTH:] = (y * jax.nn.sigmoid(glu)).astype(BF16)

    y_out = jnp.dot(ycat_scr[...], wout_ref[...], preferred_element_type=F32)
    o_ref[...] = _post_norm_residual(x_ref[...], y_out, ng_ref[...], gate_ref[0])


def _side_spec(arr, n_steps, step_of):
    assert arr.shape[0] % n_steps == 0
    return pl.BlockSpec((arr.shape[0] // n_steps, arr.shape[1]), lambda *idx: (step_of(*idx), 0))


def _pool_s5_mixer(u, bsz, pool_w, pool_scale, lam_re, lam_im, log_step, b_re, b_im, c_re, c_im,
                   d_skip, w_glu, w_out, x, ng, gate, side, *, tt):
    t = u.shape[0]
    seq = t // bsz
    gp = S5_GROUPS * S5_STATE
    apr, api, bbr, bbi = _s5_discretise(lam_re, lam_im, log_step, b_re, b_im,
                                        int(math.log2(tt // SUBLANES)))
    n_pow = apr.shape[0]

    gh = S5_GROUPS // 2
    eye = jnp.eye(gh, dtype=F32)

    def in_map(bb):
        bb = bb.reshape(S5_GROUP, 2, gh, S5_STATE)
        return jnp.einsum("hxgp,gk->xghkp", bb, eye).reshape(2, gh * S5_GROUP, gh * S5_STATE)

    bw = jnp.concatenate([in_map(bbr), in_map(bbi)], axis=-1).astype(BF16)

    def out_map(cc):
        cc = cc.reshape(2, gh, S5_GROUP, S5_STATE)
        return jnp.einsum("xghp,gk->xgpkh", cc, eye).reshape(2, gh * S5_STATE, gh * S5_GROUP)

    cw = jnp.concatenate([out_map(c_re), -out_map(c_im)], axis=1).astype(BF16)

    const2 = lambda b, i: (0, 0)
    const3 = lambda b, i: (0, 0, 0)
    nblk = seq // tt
    side_spec = _side_spec(side, bsz * nblk, lambda b, i: b * nblk + i)
    return pl.pallas_call(
        _mix0_kernel,
        grid=(bsz, nblk),
        in_specs=[
            pl.BlockSpec((tt, D_MODEL), lambda b, i: (b * nblk + i, 0)),
            pl.BlockSpec((len(POOL_WINDOWS), POOL_GROUP, POOL_GROUP), const3),
            pl.BlockSpec((1, POOL_WIDTH), const2),
            pl.BlockSpec((2, S5_HALF, 2 * S5_HALF_STATES), const3),
            pl.BlockSpec((2, 2 * S5_HALF_STATES, S5_HALF), const3),
            pl.BlockSpec((n_pow, gp), const2),
            pl.BlockSpec((n_pow, gp), const2),
            pl.BlockSpec((1, S5_WIDTH), const2),
            pl.BlockSpec((S5_WIDTH, S5_WIDTH), const2),
            pl.BlockSpec((D_MODEL, D_MODEL), const2),
            pl.BlockSpec((tt, D_MODEL), lambda b, i: (b * nblk + i, 0)),
            pl.BlockSpec((1, D_MODEL), const2),
            pl.BlockSpec((1, 1, D_MODEL), lambda b, i: (b, 0, 0)),
            side_spec,
        ],
        out_specs=[pl.BlockSpec((tt, D_MODEL), lambda b, i: (b * nblk + i, 0)), side_spec],
        out_shape=[jax.ShapeDtypeStruct((t, D_MODEL), F32),
                   jax.ShapeDtypeStruct(side.shape, BF16)],
        scratch_shapes=[
            pltpu.VMEM((POOL_HALO, POOL_WIDTH), F32),
            pltpu.VMEM((tt + POOL_HALO, POOL_WIDTH), F32),
            pltpu.VMEM((1, gp), F32), pltpu.VMEM((1, gp), F32),
            pltpu.VMEM((gp // LANES, tt, LANES), F32), pltpu.VMEM((gp // LANES, tt, LANES), F32),
            pltpu.VMEM((tt, D_MODEL), BF16),
        ],
        compiler_params=_cparams(("arbitrary", "arbitrary")),
        name="pool_s5_mixer",
    )(u, pool_w.astype(BF16), pool_scale.reshape(1, POOL_WIDTH), bw, cw, apr, api,
      d_skip.reshape(1, S5_WIDTH), w_glu.astype(BF16), w_out, x, ng, gate, side)


CONV_HALO = SUBLANES
M2_GROUP_WIDTH = M2_HPG * M2_HEADDIM


def _causal_conv_silu(x_ref, halo_scr, full_scr, w_ref, b_ref, col0):
    lc = x_ref.shape[0]
    width = x_ref.shape[1]
    x = x_ref[...].astype(F32)
    full_scr[0:CONV_HALO, :] = halo_scr[...]
    full_scr[CONV_HALO:, :] = x
    halo_scr[...] = x[lc - CONV_HALO:, :]
    cols = slice(col0, col0 + width)
    acc = b_ref[:, cols] + w_ref[M2_CONV - 1:M2_CONV, cols] * x
    for k in range(M2_CONV - 1):
        off = CONV_HALO - (M2_CONV - 1) + k
        acc = acc + w_ref[k:k + 1, cols] * full_scr[off:off + lc, :]
    return acc * jax.nn.sigmoid(acc)


def _split_dot(v, e):
    hi = v.astype(BF16)
    lo = (v - hi.astype(F32)).astype(BF16)
    return (jnp.dot(hi, e, preferred_element_type=F32) + jnp.dot(lo, e, preferred_element_type=F32))


def _ssd_kernel(z_ref, xs_ref, bc_ref, dtr_ref, cw_ref, cb_ref, dtb_ref, alog_ref, dx_ref, ng_ref,
                e_ref, side_ref, o_ref, side_out_ref, halo_x, halo_bc, full_x, full_bc, st_scr, y_scr):
    side_out_ref[...] = side_ref[...].astype(BF16)
    lc = z_ref.shape[0]

    @pl.when(pl.program_id(1) == 0)
    def _():
        halo_x[...] = jnp.zeros_like(halo_x)
        halo_bc[...] = jnp.zeros_like(halo_bc)
        st_scr[...] = jnp.zeros_like(st_scr)

    xs = _causal_conv_silu(xs_ref, halo_x, full_x, cw_ref, cb_ref, 0)
    bc = _causal_conv_silu(bc_ref, halo_bc, full_bc, cw_ref, cb_ref, M2_INNER)

    lane = lax.broadcasted_iota(jnp.int32, (1, LANES), 1)
    x_dt = dtr_ref[...].astype(F32) + dtb_ref[...]
    dt = jnp.maximum(x_dt, 0.0) + jnp.log(1.0 + jnp.exp(-jnp.abs(x_dt)))
    a = jnp.where(lane < M2_HEADS, -jnp.exp(alog_ref[...]) * math.log2(math.e), 0.0)
    da = dt * a
    row = lax.broadcasted_iota(jnp.int32, (lc, lc), 0)
    col = lax.broadcasted_iota(jnp.int32, (lc, lc), 1)
    causal = col <= row
    cs = jnp.dot(jnp.where(causal, 1.0, 0.0), da, preferred_element_type=F32,
                 precision=lax.Precision.HIGHEST)
    cs_last = cs[lc - 1:lc, :]
    ecs = jnp.exp2(cs)
    w_in = dt * jnp.exp2(cs_last - cs)
    cs_t = cs.T
    dt_t = dt.T
    e = e_ref[...]
    ecs_x = _split_dot(ecs, e)
    w_x = _split_dot(w_in, e)
    xsb = xs.astype(BF16)
    xw = (xs * w_x).astype(BF16)

    for g in range(M2_GROUPS):
        bm = bc[:, g * M2_STATE:(g + 1) * M2_STATE].astype(BF16)
        cm = bc[:, M2_BC + g * M2_STATE:M2_BC + (g + 1) * M2_STATE].astype(BF16)
        cbm = lax.dot_general(cm, bm, (((1,), (1,)), ((), ())), preferred_element_type=F32)
        gcols = slice(g * M2_GROUP_WIDTH, (g + 1) * M2_GROUP_WIDTH)
        st = st_scr[g]
        y_off = jnp.dot(cm, st.astype(BF16), preferred_element_type=F32) * ecs_x[:, gcols]
        y_heads = []
        for jj in range(M2_HPG):
            j = g * M2_HPG + jj
            seg = cs[:, j:j + 1] - cs_t[j:j + 1, :]
            dec = jnp.exp2(jnp.where(causal, seg, NEG_BIG))
            m = (cbm * dec * dt_t[j:j + 1, :]).astype(BF16)
            y_heads.append(jnp.dot(m, xsb[:, j * M2_HEADDIM:(j + 1) * M2_HEADDIM],
                                   preferred_element_type=F32))
        y_g = jnp.concatenate(y_heads, axis=-1) + y_off
        st_scr[g] = st * ecs_x[lc - 1:lc, gcols] + lax.dot_general(
            bm, xw[:, gcols], (((0,), (0,)), ((), ())), preferred_element_type=F32)
        zg = z_ref[:, gcols].astype(F32)
        y_g = (y_g + dx_ref[:, gcols] * xs[:, gcols]) * (zg * jax.nn.sigmoid(zg))
        y_scr[:, gcols] = y_g

    y = y_scr[...]
    ms = jnp.mean(y * y, axis=-1, keepdims=True)
    o_ref[...] = (y * lax.rsqrt(ms + EPS) * ng_ref[...]).astype(o_ref.dtype)


def _ssd_mixer(proj, bsz, conv_w, conv_b, dt_bias, a_log, d_skip, norm_g, side, *, lc):
    t = proj.shape[0]
    seq = t // bsz
    nblk = seq // lc
    half = M2_INNER
    pad = LANES - M2_HEADS
    dtb = jnp.pad(dt_bias, (0, pad)).reshape(1, LANES)
    alog = jnp.pad(a_log, (0, pad)).reshape(1, LANES)
    dx = jnp.repeat(d_skip, M2_HEADDIM).reshape(1, M2_INNER)
    heads = jnp.arange(LANES, dtype=jnp.int32)[:, None]
    chans = jnp.arange(M2_INNER, dtype=jnp.int32)[None, :] // M2_HEADDIM
    expand = (heads == chans).astype(BF16)
    const2 = lambda b, i: (0, 0)
    side_spec = _side_spec(side, bsz * nblk, lambda b, i: b * nblk + i)
    return pl.pallas_call(
        _ssd_kernel,
        grid=(bsz, nblk),
        in_specs=[
            pl.BlockSpec((lc, half), lambda b, i: (b * nblk + i, 0)),
            pl.BlockSpec((lc, half), lambda b, i: (b * nblk + i, 1)),
            pl.BlockSpec((lc, half), lambda b, i: (b * nblk + i, 2)),
            pl.BlockSpec((lc, LANES), lambda b, i: (b * nblk + i, 3 * half // LANES)),
            pl.BlockSpec((M2_CONV, M2_CONV_DIM), const2),
            pl.BlockSpec((1, M2_CONV_DIM), const2),
            pl.BlockSpec((1, LANES), const2),
            pl.BlockSpec((1, LANES), const2),
            pl.BlockSpec((1, M2_INNER), const2),
            pl.BlockSpec((1, M2_INNER), const2),
            pl.BlockSpec((LANES, M2_INNER), const2),
            side_spec,
        ],
        out_specs=[pl.BlockSpec((lc, M2_INNER), lambda b, i: (b * nblk + i, 0)), side_spec],
        out_shape=[jax.ShapeDtypeStruct((t, M2_INNER), BF16),
                   jax.ShapeDtypeStruct(side.shape, BF16)],
        scratch_shapes=[
            pltpu.VMEM((CONV_HALO, half), F32), pltpu.VMEM((CONV_HALO, half), F32),
            pltpu.VMEM((lc + CONV_HALO, half), F32), pltpu.VMEM((lc + CONV_HALO, half), F32),
            pltpu.VMEM((M2_GROUPS, M2_STATE, M2_GROUP_WIDTH), F32),
            pltpu.VMEM((lc, M2_INNER), F32),
        ],
        compiler_params=_cparams(("arbitrary", "arbitrary")),
        name="ssd_mixer",
    )(proj, proj, proj, proj, conv_w, conv_b.reshape(1, M2_CONV_DIM), dtb, alog, dx,
      norm_g.reshape(1, M2_INNER), expand, side)


def kernel(x, c, ada_w, ada_b, norm_g, mix_w_in, pool_w, pool_scale, s5_lam_re, s5_lam_im, s5_log_step, s5_b_re, s5_b_im, s5_c_re, s5_c_im, s5_d, s5_w_glu, mix_w_out, ffn_w_gate, ffn_w_up, ffn_w_down, m2_w_in, m2_conv_w, m2_conv_b, m2_dt_bias, m2_a_log, m2_d, m2_norm_g, m2_w_out, moe_w_router, moe_b_router, moe_w_gate, moe_w_up, moe_w_down):
    bsz, seq, d = x.shape
    t = bsz * seq
    xt = x.reshape(t, d)

    mod = _ada_modulation(c, ada_w, ada_b)

    def mod_vecs(layer):
        return [mod[layer, :, k * d:(k + 1) * d].reshape(bsz, 1, d) for k in range(6)]

    def gvec(layer, k):
        return norm_g[layer, k].reshape(1, d)

    sh_m, sc_m, g_m, sh_f, sc_f, g_f = mod_vecs(0)
    u = _norm_matmul(xt, gvec(0, 0), sc_m, sh_m, mix_w_in[0].astype(BF16),
                     tm=1024, tn=1024, out_dtype=BF16, name="mix_in_proj")
    ne, dh = N_EXPERTS, EXPERT_HIDDEN
    xt, wu_b = _pool_s5_mixer(u, bsz, pool_w[0], pool_scale[0], s5_lam_re[0], s5_lam_im[0],
                              s5_log_step[0], s5_b_re[0], s5_b_im[0], s5_c_re[0], s5_c_im[0],
                              s5_d[0], s5_w_glu[0], mix_w_out[0].astype(BF16), xt, gvec(0, 1), g_m,
                              moe_w_up[0].reshape(ne * d, dh), tt=256)
    xt = _ffn_sublayer(xt, gvec(0, 2), sc_f, sh_f, ffn_w_gate[0].astype(BF16),
                       ffn_w_up[0].astype(BF16), ffn_w_down[0].astype(BF16), gvec(0, 3), g_f,
                       tm=1024, th=1408)

    sh_m, sc_m, g_m, sh_f, sc_f, g_f = mod_vecs(1)
    proj_pad = 6400
    w_in = jnp.zeros((d, proj_pad), BF16).at[:, :M2_PROJ].set(m2_w_in[0].astype(BF16))
    proj, (wg_b,) = _norm_matmul(
        xt, gvec(1, 0), sc_m, sh_m, w_in, tm=1024, tn=1280, out_dtype=BF16, name="m2_in_proj",
        side=(moe_w_gate[0].reshape(ne * d, dh),), side_steps=32)
    y, wd_b = _ssd_mixer(proj, bsz, m2_conv_w[0], m2_conv_b[0], m2_dt_bias[0], m2_a_log[0],
                         m2_d[0], m2_norm_g[0], moe_w_down[0].reshape(ne * dh, d), lc=128)
    xt = _matmul_post(y, m2_w_out[0].astype(BF16), xt, gvec(1, 1), g_m, tm=1024,
                      name="m2_out_proj")
    xt = _moe_sublayer(xt, gvec(1, 2), sc_f, sh_f, moe_w_router[0], moe_b_router[0],
                       wg_b.reshape(ne, d, dh), wu_b.reshape(ne, d, dh), wd_b.reshape(ne, dh, d),
                       gvec(1, 3), g_f)
    return xt.reshape(bsz, seq, d)
```

```python
import functools
import math

import jax
import jax.numpy as jnp
from jax import lax
from jax.experimental import pallas as pl
from jax.experimental.pallas import tpu as pltpu

F32 = jnp.float32
BF16 = jnp.bfloat16

D_MODEL = 1024
EPS = 1e-6
POOL_WIDTH = 512
POOL_WINDOWS = (2, 4, 8, 16)
POOL_GROUP = 128
S5_WIDTH = 512
S5_GROUP = 16
S5_GROUPS = 32
S5_STATE = 64
M2_INNER = 2048
M2_HEADDIM = 64
M2_HEADS = 32
M2_GROUPS = 8
M2_HPG = 4
M2_STATE = 128
M2_CONV = 4
M2_BC = 1024
M2_CONV_DIM = 4096
M2_PROJ = 6176
FFN_HIDDEN = 2816
N_EXPERTS = 8
EXPERT_HIDDEN = 3584

LANES = 128
SUBLANES = 8
VMEM_LIMIT_BYTES = 56 * 1024 * 1024

NEG_BIG = -1e30


def _cparams(sem):
    return pltpu.CompilerParams(dimension_semantics=sem, vmem_limit_bytes=VMEM_LIMIT_BYTES)


def _modulated_norm(x, g, scale, shift):
    ms = jnp.mean(x * x, axis=-1, keepdims=True)
    return (x * lax.rsqrt(ms + EPS) * g) * (1.0 + scale) + shift


def _post_norm_residual(x, y, ng, gate):
    ms = jnp.mean(y * y, axis=-1, keepdims=True)
    return x + gate * (y * lax.rsqrt(ms + EPS) * ng)


def _ada_kernel(c_ref, w_ref, b_ref, o_ref):
    c = c_ref[...]
    a = c * jax.nn.sigmoid(c)
    o_ref[0] = jnp.dot(a.astype(BF16), w_ref[0].astype(BF16),
                       preferred_element_type=F32) + b_ref[0]


def _ada_modulation(c, ada_w, ada_b):
    depth, d, n = ada_w.shape
    b = c.shape[0]
    c_pad = jnp.zeros((SUBLANES, d), F32).at[:b].set(c)
    tn = 1024
    out = pl.pallas_call(
        _ada_kernel,
        grid=(depth, n // tn),
        in_specs=[
            pl.BlockSpec((SUBLANES, d), lambda l, j: (0, 0)),
            pl.BlockSpec((1, d, tn), lambda l, j: (l, 0, j)),
            pl.BlockSpec((1, 1, tn), lambda l, j: (l, 0, j)),
        ],
        out_specs=pl.BlockSpec((1, SUBLANES, tn), lambda l, j: (l, 0, j)),
        out_shape=jax.ShapeDtypeStruct((depth, SUBLANES, n), F32),
        compiler_params=_cparams(("arbitrary", "arbitrary")),
        name="ada_modulation",
    )(c_pad, ada_w, ada_b.reshape(depth, 1, n))
    return out[:, :b]


def _norm_mm_kernel(x_ref, g_ref, sc_ref, sh_ref, w_ref, *rest, n_side, side_steps):
    side_in = rest[:n_side]
    o_ref = rest[n_side]
    side_out = rest[n_side + 1:2 * n_side + 1]
    h_scr = rest[2 * n_side + 1]

    @pl.when(pl.program_id(1) == 0)
    def _():
        h = _modulated_norm(x_ref[...], g_ref[...], sc_ref[0], sh_ref[0])
        h_scr[...] = h.astype(BF16)

    o_ref[...] = jnp.dot(h_scr[...], w_ref[...], preferred_element_type=F32).astype(o_ref.dtype)

    step = pl.program_id(0) * pl.num_programs(1) + pl.program_id(1)
    for k in range(n_side):
        @pl.when(jnp.logical_and(step >= k * side_steps, step < (k + 1) * side_steps))
        def _(k=k):
            side_out[k][...] = side_in[k][...].astype(BF16)


def _norm_matmul(x, g, scale, shift, w, *, tm, tn, out_dtype, name, side=(), side_steps=1):
    t, d = x.shape
    n = w.shape[1]
    n_j = n // tn
    tiles_per_batch = t // scale.shape[0] // tm
    assert len(side) * side_steps <= (t // tm) * n_j

    def side_spec(k, arr):
        rows = arr.shape[0] // side_steps
        return pl.BlockSpec(
            (rows, arr.shape[1]),
            lambda i, j: (jnp.clip(i * n_j + j - k * side_steps, 0, side_steps - 1), 0))

    side_specs = [side_spec(k, a) for k, a in enumerate(side)]
    outs = pl.pallas_call(
        functools.partial(_norm_mm_kernel, n_side=len(side), side_steps=side_steps),
        grid=(t // tm, n_j),
        in_specs=[
            pl.BlockSpec((tm, d), lambda i, j: (i, 0)),
            pl.BlockSpec((1, d), lambda i, j: (0, 0)),
            pl.BlockSpec((1, 1, d), lambda i, j: (i // tiles_per_batch, 0, 0)),
            pl.BlockSpec((1, 1, d), lambda i, j: (i // tiles_per_batch, 0, 0)),
            pl.BlockSpec((d, tn), lambda i, j: (0, j)),
        ] + side_specs,
        out_specs=[pl.BlockSpec((tm, tn), lambda i, j: (i, j))] + side_specs,
        out_shape=[jax.ShapeDtypeStruct((t, n), out_dtype)]
                  + [jax.ShapeDtypeStruct(a.shape, BF16) for a in side],
        scratch_shapes=[pltpu.VMEM((tm, d), BF16)],
        compiler_params=_cparams(("arbitrary", "arbitrary")),
        name=name,
    )(x, g, scale, shift, w, *side)
    return (outs[0], outs[1:]) if side else outs[0]


def _mm_post_kernel(y_ref, w_ref, x_ref, ng_ref, gate_ref, o_ref):
    y = jnp.dot(y_ref[...], w_ref[...], preferred_element_type=F32)
    o_ref[...] = _post_norm_residual(x_ref[...], y, ng_ref[...], gate_ref[0])


def _matmul_post(y, w, x, ng, gate, *, tm, name):
    t, k = y.shape
    d = w.shape[1]
    tiles_per_batch = t // gate.shape[0] // tm
    return pl.pallas_call(
        _mm_post_kernel,
        grid=(t // tm,),
        in_specs=[
            pl.BlockSpec((tm, k), lambda i: (i, 0)),
            pl.BlockSpec((k, d), lambda i: (0, 0)),
            pl.BlockSpec((tm, d), lambda i: (i, 0)),
            pl.BlockSpec((1, d), lambda i: (0, 0)),
            pl.BlockSpec((1, 1, d), lambda i: (i // tiles_per_batch, 0, 0)),
        ],
        out_specs=pl.BlockSpec((tm, d), lambda i: (i, 0)),
        out_shape=jax.ShapeDtypeStruct((t, d), F32),
        compiler_params=_cparams(("arbitrary",)),
        name=name,
    )(y, w, x, ng, gate)


def _ffn_kernel(x_ref, g_ref, sc_ref, sh_ref, wgu_ref, wd_ref, ng_ref, gate_ref,
                o_ref, h_scr, acc_scr):
    j = pl.program_id(1)

    @pl.when(j == 0)
    def _():
        h = _modulated_norm(x_ref[...], g_ref[...], sc_ref[0], sh_ref[0])
        h_scr[...] = h.astype(BF16)
        acc_scr[...] = jnp.zeros_like(acc_scr)

    th = wd_ref.shape[0]
    gu = jnp.dot(h_scr[...], wgu_ref[...], preferred_element_type=F32)
    gt = gu[:, :th]
    up = gu[:, th:]
    act = (gt * jax.nn.sigmoid(gt) * up).astype(BF16)
    acc_scr[...] += jnp.dot(act, wd_ref[...], preferred_element_type=F32)

    @pl.when(j == pl.num_programs(1) - 1)
    def _():
        o_ref[...] = _post_norm_residual(x_ref[...], acc_scr[...], ng_ref[...], gate_ref[0])


def _ffn_sublayer(x, g, scale, shift, wg, wu, wd, ng, gate, *, tm, th):
    t, d = x.shape
    hid = wg.shape[1]
    n_j = hid // th
    tiles_per_batch = t // scale.shape[0] // tm
    bvec = pl.BlockSpec((1, 1, d), lambda i, j: (i // tiles_per_batch, 0, 0))
    wgu = jnp.concatenate([w[:, j * th:(j + 1) * th] for j in range(n_j) for w in (wg, wu)], axis=1)
    return pl.pallas_call(
        _ffn_kernel,
        grid=(t // tm, n_j),
        in_specs=[
            pl.BlockSpec((tm, d), lambda i, j: (i, 0)),
            pl.BlockSpec((1, d), lambda i, j: (0, 0)),
            bvec, bvec,
            pl.BlockSpec((d, 2 * th), lambda i, j: (0, j)),
            pl.BlockSpec((th, d), lambda i, j: (j, 0)),
            pl.BlockSpec((1, d), lambda i, j: (0, 0)),
            bvec,
        ],
        out_specs=pl.BlockSpec((tm, d), lambda i, j: (i, 0)),
        out_shape=jax.ShapeDtypeStruct((t, d), F32),
        scratch_shapes=[pltpu.VMEM((tm, d), BF16), pltpu.VMEM((tm, d), F32)],
        compiler_params=_cparams(("arbitrary", "arbitrary")),
        name="ffn_sublayer",
    )(x, g, scale, shift, wgu, wd, ng, gate)


def _router_kernel(x_ref, g_ref, sc_ref, sh_ref, wr_ref, br_ref, h_ref, meta_ref, cnt_ref):
    tm = x_ref.shape[0]

    h = _modulated_norm(x_ref[...], g_ref[...], sc_ref[0], sh_ref[0])
    h_hi = h.astype(BF16)
    h_ref[...] = h_hi
    h_lo = (h - h_hi.astype(F32)).astype(BF16)
    logits = (jnp.dot(h_hi, wr_ref[0], preferred_element_type=F32)
              + jnp.dot(h_lo, wr_ref[0], preferred_element_type=F32)
              + jnp.dot(h_hi, wr_ref[1], preferred_element_type=F32)) + br_ref[...]
    lane = lax.broadcasted_iota(jnp.int32, (tm, LANES), 1).astype(F32)
    m1 = jnp.max(logits, axis=-1, keepdims=True)
    i1 = jnp.min(jnp.where(logits == m1, lane, float(LANES)), axis=-1, keepdims=True)
    oh1 = lane == i1
    rest = jnp.where(oh1, NEG_BIG * 2.0, logits)
    m2 = jnp.max(rest, axis=-1, keepdims=True)
    i2 = jnp.min(jnp.where(rest == m2, lane, float(LANES)), axis=-1, keepdims=True)
    oh2 = lane == i2
    e = jnp.exp(m2 - m1)
    w1 = 1.0 / (1.0 + e)
    w2 = e / (1.0 + e)

    picks = jnp.where(oh1, 1.0, 0.0) + jnp.where(oh2, 1.0, 0.0)
    row = lax.broadcasted_iota(jnp.int32, (tm, tm), 0)
    col = lax.broadcasted_iota(jnp.int32, (tm, tm), 1)
    lower = jnp.where(col < row, 1.0, 0.0).astype(BF16)
    before = jnp.dot(lower, picks.astype(BF16), preferred_element_type=F32)
    tile_cnt = jnp.sum(picks, axis=0, keepdims=True)
    ea = lax.broadcasted_iota(jnp.int32, (LANES, LANES), 0)
    eb = lax.broadcasted_iota(jnp.int32, (LANES, LANES), 1)
    seg_rows = jnp.floor((tile_cnt + (SUBLANES - 1.0)) * (1.0 / SUBLANES)) * SUBLANES
    seg_off = jnp.dot(jnp.broadcast_to(seg_rows, (SUBLANES, LANES)), jnp.where(ea < eb, 1.0, 0.0),
                      preferred_element_type=F32, precision=lax.Precision.HIGHEST)[0:1, :]
    local = before + seg_off
    loc1 = jnp.sum(jnp.where(oh1, local, 0.0), axis=-1, keepdims=True)
    loc2 = jnp.sum(jnp.where(oh2, local, 0.0), axis=-1, keepdims=True)
    cnt_ref[0] = tile_cnt

    meta = jnp.where(lane == 0.0, i1, 0.0)
    meta = jnp.where(lane == 1.0, i2, meta)
    meta = jnp.where(lane == 2.0, w1, meta)
    meta = jnp.where(lane == 3.0, w2, meta)
    meta = jnp.where(lane == 4.0, loc1, meta)
    meta = jnp.where(lane == 5.0, loc2, meta)
    meta_ref[...] = meta


def _router(x, g, scale, shift, w_router, b_router, *, tm):
    t, d = x.shape
    tiles_per_batch = t // scale.shape[0] // tm
    wr = jnp.zeros((d, LANES), F32).at[:, :N_EXPERTS].set(w_router)
    wr_hi = wr.astype(BF16)
    wr = jnp.stack([wr_hi, (wr - wr_hi.astype(F32)).astype(BF16)])
    br = jnp.full((1, LANES), NEG_BIG, F32).at[0, :N_EXPERTS].set(b_router)
    bvec = pl.BlockSpec((1, 1, d), lambda i: (i // tiles_per_batch, 0, 0))
    return pl.pallas_call(
        _router_kernel,
        grid=(t // tm,),
        in_specs=[
            pl.BlockSpec((tm, d), lambda i: (i, 0)),
            pl.BlockSpec((1, d), lambda i: (0, 0)),
            bvec, bvec,
            pl.BlockSpec((2, d, LANES), lambda i: (0, 0, 0)),
            pl.BlockSpec((1, LANES), lambda i: (0, 0)),
        ],
        out_specs=[
            pl.BlockSpec((tm, d), lambda i: (i, 0)),
            pl.BlockSpec((tm, LANES), lambda i: (i, 0)),
            pl.BlockSpec((1, 1, LANES), lambda i: (i, 0, 0)),
        ],
        out_shape=[
            jax.ShapeDtypeStruct((t, d), BF16),
            jax.ShapeDtypeStruct((t, LANES), F32),
            jax.ShapeDtypeStruct((t // tm, 1, LANES), F32),
        ],
        compiler_params=_cparams(("arbitrary",)),
        name="moe_router",
    )(x, g, scale, shift, wr, br)


SEG_FIELDS = 3
SEG_PAD_ROWS = N_EXPERTS * SUBLANES


def _segment_copies(scal_ref, tile, tm, per_copy):
    base = tile * (SEG_FIELDS * N_EXPERTS)
    for e in range(N_EXPERTS):
        n = scal_ref[base + e]
        g0 = scal_ref[base + N_EXPERTS + e]
        l0 = scal_ref[base + 2 * N_EXPERTS + e]
        done = 0
        b = tm
        while b >= SUBLANES:
            take = n & b

            @pl.when(take != 0)
            def _(b=b, done=done, g0=g0, l0=l0):
                per_copy(pl.multiple_of(l0 + done, SUBLANES), pl.multiple_of(g0 + done, SUBLANES), b)

            done = done + take
            b //= 2


def _dispatch_kernel(scal_ref, tail_ref, h_ref, meta_ref, xs_ref, ws_ref,
                     seg_scr, wseg_scr, zero_scr, sem, zsem, *, n_tiles, max_unused):
    tile = pl.program_id(0)
    tm = h_ref.shape[0]
    ns = seg_scr.shape[1]
    par = tile % 2

    def segment_dmas(which, buf):
        def descriptors(local, glob, n):
            return (pltpu.make_async_copy(wseg_scr.at[buf, pl.ds(local, n), :],
                                          ws_ref.at[pl.ds(glob, n), :], sem.at[buf]),
                    pltpu.make_async_copy(seg_scr.at[buf, pl.ds(local, n), :],
                                          xs_ref.at[pl.ds(glob, n), :], sem.at[buf]))
        return descriptors

    def wait_segments(which, buf):
        descriptors = segment_dmas(which, buf)
        _segment_copies(scal_ref, which, tm, lambda l, g, n: [c.wait() for c in descriptors(l, g, n)])

    @pl.when(tile == 0)
    def _():
        zero_scr[...] = jnp.zeros_like(zero_scr)
        fills = []
        n_slot_tiles = xs_ref.shape[0] // tm
        tails = [(tail_ref[N_EXPERTS + e] > 0, pl.multiple_of(tail_ref[e], tm))
                 for e in range(N_EXPERTS)]
        tails += [(k >= tail_ref[2 * N_EXPERTS], k * tm)
                  for k in range(n_slot_tiles - max_unused, n_slot_tiles)]
        for nonempty, tail in tails:
            fills.append((nonempty,
                          pltpu.make_async_copy(zero_scr, xs_ref.at[pl.ds(tail, tm), :], zsem),
                          pltpu.make_async_copy(zero_scr.at[:, 0:LANES],
                                                ws_ref.at[pl.ds(tail, tm), :], zsem)))
        for nonempty, fill_x, fill_w in fills:
            @pl.when(nonempty)
            def _(fill_x=fill_x, fill_w=fill_w):
                fill_x.start()
                fill_w.start()
        for nonempty, fill_x, fill_w in fills:
            @pl.when(nonempty)
            def _(fill_x=fill_x, fill_w=fill_w):
                fill_x.wait()
                fill_w.wait()

    meta = meta_ref[...]
    meta_t = meta.T
    slot = lax.broadcasted_iota(jnp.int32, (ns, tm), 0).astype(F32)
    g1 = jnp.where(slot == meta_t[4:5, :], 1.0, 0.0).astype(BF16)
    g2 = jnp.where(slot == meta_t[5:6, :], 1.0, 0.0).astype(BF16)
    rows = jnp.dot(g1 + g2, h_ref[...], preferred_element_type=F32)
    lane = lax.broadcasted_iota(jnp.int32, (tm, LANES), 1)

    def split_lanes(w):
        hi = w.astype(BF16).astype(F32)
        return jnp.where(lane == 0, hi, jnp.where(lane == 1, w - hi, 0.0)).astype(BF16)

    w_split = jnp.concatenate([split_lanes(meta[:, 2:3]), split_lanes(meta[:, 3:4])], axis=0)
    w_slot = jnp.dot(jnp.concatenate([g1, g2], axis=1), w_split, preferred_element_type=F32)
    weight = w_slot[:, 0:1] + w_slot[:, 1:2]

    @pl.when(tile >= 2)
    def _():
        wait_segments(tile - 2, par)

    seg_scr[par] = rows
    wseg_scr[par] = jnp.broadcast_to(weight, (ns, LANES))
    descriptors = segment_dmas(tile, par)
    _segment_copies(scal_ref, tile, tm, lambda l, g, n: [c.start() for c in descriptors(l, g, n)])

    @pl.when(tile == n_tiles - 1)
    def _():
        wait_segments(tile, par)
        if n_tiles > 1:
            wait_segments(tile - 1, 1 - par)


def _dispatch(h, meta, seg_table, tail_table, n_slots, *, tm):
    t, d = h.shape
    grid_spec = pltpu.PrefetchScalarGridSpec(
        num_scalar_prefetch=2,
        grid=(t // tm,),
        in_specs=[
            pl.BlockSpec((tm, d), lambda i, sc, tl: (i, 0)),
            pl.BlockSpec((tm, LANES), lambda i, sc, tl: (i, 0)),
        ],
        out_specs=[pl.BlockSpec(memory_space=pl.ANY), pl.BlockSpec(memory_space=pl.ANY)],
        scratch_shapes=[
            pltpu.VMEM((2, 2 * tm + SEG_PAD_ROWS, d), F32),
            pltpu.VMEM((2, 2 * tm + SEG_PAD_ROWS, LANES), F32),
            pltpu.VMEM((tm, d), F32),
            pltpu.SemaphoreType.DMA((2,)), pltpu.SemaphoreType.DMA(()),
        ],
    )
    return pl.pallas_call(
        functools.partial(_dispatch_kernel, n_tiles=t // tm,
                          max_unused=n_slots // tm - 2 * t // tm),
        grid_spec=grid_spec,
        out_shape=[jax.ShapeDtypeStruct((n_slots, d), F32),
                   jax.ShapeDtypeStruct((n_slots, LANES), F32)],
        compiler_params=_cparams(("arbitrary",)),
        name="moe_dispatch",
    )(seg_table, tail_table, h, meta)


def _expert_kernel(te_ref, nu_ref, xs_ref, ws_ref, wg_ref, wu_ref, wd_ref, ys_ref,
                   xb_scr, acc_scr):
    i = pl.program_id(0)
    j = pl.program_id(1)
    last = pl.num_programs(1) - 1
    used = i < nu_ref[0]

    @pl.when(used)
    def _():
        @pl.when(j == 0)
        def _():
            xb_scr[...] = xs_ref[...].astype(BF16)
            acc_scr[...] = jnp.zeros_like(acc_scr)

        x = xb_scr[...]
        gt = jnp.dot(x, wg_ref[0], preferred_element_type=F32)
        up = jnp.dot(x, wu_ref[0], preferred_element_type=F32)
        act = (gt * jax.nn.sigmoid(gt) * up).astype(BF16)
        acc_scr[...] += jnp.dot(act, wd_ref[0], preferred_element_type=F32)

        @pl.when(j == last)
        def _():
            ys_ref[...] = (acc_scr[...] * ws_ref[:, 0:1]).astype(ys_ref.dtype)

    @pl.when(jnp.logical_and(jnp.logical_not(used), j == last))
    def _():
        ys_ref[...] = jnp.zeros_like(ys_ref)


def _experts(xs, ws, tile_expert, n_used, wg, wu, wd, *, tm, th):
    n_slots, d = xs.shape
    hid = wg.shape[2]
    nt = n_slots // tm

    def hidden_step(i, j, nu):
        return jnp.where(i < nu[0], j, 0)

    def slot_tile(i, nu):
        return jnp.where(i < nu[0], i, 0)

    grid_spec = pltpu.PrefetchScalarGridSpec(
        num_scalar_prefetch=2,
        grid=(nt, hid // th),
        in_specs=[
            pl.BlockSpec((tm, d), lambda i, j, te, nu: (slot_tile(i, nu), 0)),
            pl.BlockSpec((tm, LANES), lambda i, j, te, nu: (slot_tile(i, nu), 0)),
            pl.BlockSpec((1, d, th), lambda i, j, te, nu: (te[i], 0, hidden_step(i, j, nu))),
            pl.BlockSpec((1, d, th), lambda i, j, te, nu: (te[i], 0, hidden_step(i, j, nu))),
            pl.BlockSpec((1, th, d), lambda i, j, te, nu: (te[i], hidden_step(i, j, nu), 0)),
        ],
        out_specs=pl.BlockSpec((tm, d), lambda i, j, te, nu: (i, 0)),
        scratch_shapes=[pltpu.VMEM((tm, d), BF16), pltpu.VMEM((tm, d), F32)],
    )
    return pl.pallas_call(
        _expert_kernel,
        grid_spec=grid_spec,
        out_shape=jax.ShapeDtypeStruct((n_slots, d), F32),
        compiler_params=_cparams(("arbitrary", "arbitrary")),
        name="moe_experts",
    )(tile_expert, n_used, xs, ws, wg, wu, wd)


def _combine_kernel(scal_ref, ys_ref, meta_ref, x_ref, ng_ref, gate_ref, o_ref, buf, sem,
                    *, n_tiles):
    tile = pl.program_id(0)
    tm = x_ref.shape[0]
    ns = buf.shape[1]
    par = tile % 2

    def fetch(which, slot, wait):
        def copy_in(local, glob, n):
            dma = pltpu.make_async_copy(ys_ref.at[pl.ds(glob, n), :],
                                        buf.at[slot, pl.ds(local, n), :], sem.at[slot])
            dma.wait() if wait else dma.start()
        _segment_copies(scal_ref, which, tm, copy_in)

    @pl.when(tile == 0)
    def _():
        buf[...] = jnp.zeros_like(buf)
        fetch(tile, par, False)

    @pl.when(tile + 1 < n_tiles)
    def _():
        fetch(tile + 1, 1 - par, False)

    fetch(tile, par, True)
    meta = meta_ref[...]
    slot_id = lax.broadcasted_iota(jnp.int32, (tm, ns), 1).astype(F32)
    pick = jnp.where((slot_id == meta[:, 4:5]) | (slot_id == meta[:, 5:6]), 1.0, 0.0).astype(BF16)
    y = jnp.dot(pick, buf[par].astype(BF16), preferred_element_type=F32)
    o_ref[...] = _post_norm_residual(x_ref[...], y, ng_ref[...], gate_ref[0])


def _combine(ys, meta, seg_table, x, ng, gate, *, tm):
    t, d = x.shape
    tiles_per_batch = t // gate.shape[0] // tm
    grid_spec = pltpu.PrefetchScalarGridSpec(
        num_scalar_prefetch=1,
        grid=(t // tm,),
        in_specs=[
            pl.BlockSpec(memory_space=pl.ANY),
            pl.BlockSpec((tm, LANES), lambda i, sc: (i, 0)),
            pl.BlockSpec((tm, d), lambda i, sc: (i, 0)),
            pl.BlockSpec((1, d), lambda i, sc: (0, 0)),
            pl.BlockSpec((1, 1, d), lambda i, sc: (i // tiles_per_batch, 0, 0)),
        ],
        out_specs=pl.BlockSpec((tm, d), lambda i, sc: (i, 0)),
        scratch_shapes=[pltpu.VMEM((2, 2 * tm + SEG_PAD_ROWS, d), F32),
                        pltpu.SemaphoreType.DMA((2,))],
    )
    return pl.pallas_call(
        functools.partial(_combine_kernel, n_tiles=t // tm),
        grid_spec=grid_spec,
        out_shape=jax.ShapeDtypeStruct((t, d), F32),
        compiler_params=_cparams(("arbitrary",)),
        name="moe_combine",
    )(seg_table, ys, meta, x, ng, gate)


def _moe_sublayer(x, g, scale, shift, w_router, b_router, wg, wu, wd, ng, gate):
    t = x.shape[0]
    tm, tm_e, th = 512, 512, 1792
    h, meta, counts = _router(x, g, scale, shift, w_router, b_router, tm=tm)

    assert tm == tm_e
    tile_cnt = counts[:, 0, :N_EXPERTS].astype(jnp.int32)
    seg_rows = ((tile_cnt + SUBLANES - 1) // SUBLANES) * SUBLANES
    cnt = jnp.sum(seg_rows, axis=0)
    padded = ((cnt + tm_e - 1) // tm_e) * tm_e
    ends = jnp.cumsum(padded)
    offs = ends - padded
    glob_start = offs[None, :] + jnp.cumsum(seg_rows, axis=0) - seg_rows
    local_start = jnp.cumsum(seg_rows, axis=1) - seg_rows
    seg_table = jnp.concatenate([seg_rows, glob_start, local_start], axis=1).reshape(-1)
    tail_table = jnp.concatenate([ends - tm_e, padded, ends[-1:] // tm_e]).astype(jnp.int32)
    n_tok_tiles = t // tm
    nt = -(-(2 * t + n_tok_tiles * SEG_PAD_ROWS) // tm_e) + N_EXPERTS
    starts = jnp.arange(nt, dtype=jnp.int32) * tm_e
    tile_expert = jnp.minimum(jnp.sum(starts[:, None] >= ends[None, :], axis=1), N_EXPERTS - 1)
    tile_expert = tile_expert.astype(jnp.int32)
    n_used = (ends[-1:] // tm_e).astype(jnp.int32)

    xs, ws = _dispatch(h, meta, seg_table, tail_table, nt * tm_e, tm=tm)
    ys = _experts(xs, ws, tile_expert, n_used, wg, wu, wd, tm=tm_e, th=th)
    return _combine(ys, meta, seg_table, x, ng, gate, tm=tm)


def _s5_disc_kernel(lam_re_ref, lam_im_ref, step_ref, pow_ref, keep_ref, b_re_ref, b_im_ref,
                    apr_ref, api_ref, bbr_ref, bbi_ref):
    lr = jnp.minimum(lam_re_ref[...], -1e-4)
    li = lam_im_ref[...]
    step = step_ref[...]
    m = pow_ref[...]
    mag = jnp.exp(lr * step * m) * keep_ref[...]
    ang = li * step * m
    apr_ref[...] = mag * jnp.cos(ang)
    api_ref[...] = mag * jnp.sin(ang)
    mag1 = jnp.exp(lr * step)
    ar = mag1 * jnp.cos(li * step)
    ai = mag1 * jnp.sin(li * step)
    inv = 1.0 / (lr * lr + li * li)
    fr = ((ar - 1.0) * lr + ai * li) * inv
    fi = (ai * lr - (ar - 1.0) * li) * inv
    br = b_re_ref[...]
    bi = b_im_ref[...]
    bbr_ref[...] = fr * br - fi * bi
    bbi_ref[...] = fr * bi + fi * br


S5_LOCAL_STEPS = 3
S5_ROW_POW0 = S5_LOCAL_STEPS * SUBLANES
S5_TILE_POW0 = S5_ROW_POW0 + SUBLANES


def _s5_power_rows(n_tile_steps):
    exps, keep = [], []
    for k in range(S5_LOCAL_STEPS):
        for tau in range(SUBLANES):
            exps.append(float(1 << k))
            keep.append(1.0 if tau >= (1 << k) else 0.0)
    for tau in range(SUBLANES):
        exps.append(float(tau + 1))
        keep.append(1.0)
    for k in range(n_tile_steps):
        exps.append(float(SUBLANES << k))
        keep.append(1.0)
    while len(exps) % SUBLANES:
        exps.append(0.0)
        keep.append(0.0)
    return exps, keep


def _s5_discretise(lam_re, lam_im, log_step, b_re, b_im, n_tile_steps):
    gp = S5_GROUPS * S5_STATE
    step = jnp.repeat(jnp.exp(log_step), S5_STATE).reshape(1, gp)
    exps, keep = _s5_power_rows(n_tile_steps)
    rows = len(exps)
    b_re_t = jnp.transpose(b_re, (2, 0, 1)).reshape(S5_GROUP, gp)
    b_im_t = jnp.transpose(b_im, (2, 0, 1)).reshape(S5_GROUP, gp)
    return pl.pallas_call(
        _s5_disc_kernel,
        out_shape=[jax.ShapeDtypeStruct((rows, gp), F32), jax.ShapeDtypeStruct((rows, gp), F32),
                   jax.ShapeDtypeStruct((S5_GROUP, gp), F32), jax.ShapeDtypeStruct((S5_GROUP, gp), F32)],
        name="s5_discretise",
    )(lam_re.reshape(1, gp), lam_im.reshape(1, gp), step,
      jnp.asarray(exps, F32).reshape(rows, 1), jnp.asarray(keep, F32).reshape(rows, 1),
      b_re_t, b_im_t)


POOL_HALO = 16
S5_HALF = 256
S5_HALF_STATES = 1024


def _complex_axpy(xr, xi, cr, ci, sr, si):
    return xr + (cr * sr - ci * si), xi + (cr * si + ci * sr)


def _mix0_kernel(x_ref, g_ref, sc_ref, sh_ref, win_ref, pw_ref, ps_ref, bw_ref, cw_ref, apr_ref,
                 api_ref, dsk_ref, wglu_ref, wout_ref, ng_ref, gate_ref, side_ref, o_ref,
                 side_out_ref, halo_scr, ext_scr, sr_scr, si_scr, xr_scr, xi_scr, ycat_scr):
    side_out_ref[...] = side_ref[...].astype(BF16)
    tt = x_ref.shape[0]
    n_states = sr_scr.shape[1]
    blk = pl.program_id(1)

    h = _modulated_norm(x_ref[...], g_ref[...], sc_ref[0], sh_ref[0]).astype(BF16)
    u = jnp.dot(h, win_ref[...], preferred_element_type=F32)

    @pl.when(blk == 0)
    def _():
        halo_scr[...] = jnp.zeros_like(halo_scr)
        sr_scr[...] = jnp.zeros_like(sr_scr)
        si_scr[...] = jnp.zeros_like(si_scr)

    up = u[:, :POOL_WIDTH]
    ext_scr[0:POOL_HALO, :] = halo_scr[...]
    ext_scr[POOL_HALO:, :] = up
    halo_scr[...] = up[tt - POOL_HALO:, :]
    ext = ext_scr[...].astype(BF16)
    row = lax.broadcasted_iota(jnp.int32, (tt, tt + POOL_HALO), 0)
    col = lax.broadcasted_iota(jnp.int32, (tt, tt + POOL_HALO), 1)
    lag = row + POOL_HALO - col
    t_glob = (blk * tt + row + 1).astype(F32)
    pooled_out = []
    for gi, win in enumerate(POOL_WINDOWS):
        inv_count = 1.0 / jnp.minimum(t_glob, float(win))
        band = jnp.where((lag >= 0) & (lag < win), inv_count, 0.0) - jnp.where(lag == 0, 1.0, 0.0)
        pooled = jnp.dot(band.astype(BF16), ext[:, gi * POOL_GROUP:(gi + 1) * POOL_GROUP],
                         preferred_element_type=F32)
        pooled_out.append(jnp.dot(pooled.astype(BF16), pw_ref[gi], preferred_element_type=F32))
    y_pool = jnp.concatenate(pooled_out, axis=-1) * ps_ref[...]
    ycat_scr[:, :POOL_WIDTH] = y_pool.astype(BF16)

    us = u[:, POOL_WIDTH:]
    usb = us.astype(BF16)
    bu = [jnp.dot(usb[:, hf * S5_HALF:(hf + 1) * S5_HALF], bw_ref[hf], preferred_element_type=F32)
          for hf in range(2)]
    nt = tt // SUBLANES
    xr = jnp.concatenate([b[:, :S5_HALF_STATES] for b in bu], axis=-1).reshape(nt, SUBLANES, n_states)
    xi = jnp.concatenate([b[:, S5_HALF_STATES:] for b in bu], axis=-1).reshape(nt, SUBLANES, n_states)
    for k in range(S5_LOCAL_STEPS):
        cr = apr_ref[k * SUBLANES:(k + 1) * SUBLANES, :][None]
        ci = api_ref[k * SUBLANES:(k + 1) * SUBLANES, :][None]
        xr, xi = _complex_axpy(xr, xi, cr, ci, pltpu.roll(xr, 1 << k, 1), pltpu.roll(xi, 1 << k, 1))

    xr2 = xr.reshape(tt, n_states)
    xi2 = xi.reshape(tt, n_states)
    n_cb = n_states // LANES
    for cb in range(n_cb):
        xr_scr[cb] = xr2[:, cb * LANES:(cb + 1) * LANES]
        xi_scr[cb] = xi2[:, cb * LANES:(cb + 1) * LANES]
    tile_end = pl.ds(SUBLANES - 1, nt, stride=SUBLANES)
    er = jnp.concatenate([xr_scr[cb, tile_end, :] for cb in range(n_cb)], axis=-1)
    ei = jnp.concatenate([xi_scr[cb, tile_end, :] for cb in range(n_cb)], axis=-1)
    prev_r = sr_scr[...]
    prev_i = si_scr[...]
    tile_row = lax.broadcasted_iota(jnp.int32, (nt, n_states), 0)
    a8r = apr_ref[S5_TILE_POW0:S5_TILE_POW0 + 1, :]
    a8i = api_ref[S5_TILE_POW0:S5_TILE_POW0 + 1, :]
    er = er + jnp.where(tile_row == 0, a8r * prev_r - a8i * prev_i, 0.0)
    ei = ei + jnp.where(tile_row == 0, a8r * prev_i + a8i * prev_r, 0.0)
    k = 0
    while (1 << k) < nt:
        cr = apr_ref[S5_TILE_POW0 + k:S5_TILE_POW0 + k + 1, :]
        ci = api_ref[S5_TILE_POW0 + k:S5_TILE_POW0 + k + 1, :]
        pr = jnp.where(tile_row < (1 << k), 0.0, pltpu.roll(er, 1 << k, 0))
        pi = jnp.where(tile_row < (1 << k), 0.0, pltpu.roll(ei, 1 << k, 0))
        er, ei = _complex_axpy(er, ei, cr, ci, pr, pi)
        k += 1
    sr_scr[...] = er[nt - 1:nt, :]
    si_scr[...] = ei[nt - 1:nt, :]
    in_r = jnp.where(tile_row == 0, prev_r, pltpu.roll(er, 1, 0))
    in_i = jnp.where(tile_row == 0, prev_i, pltpu.roll(ei, 1, 0))

    rep_row = lax.broadcasted_iota(jnp.int32, (tt, nt), 0)
    rep_col = lax.broadcasted_iota(jnp.int32, (tt, nt), 1)
    rep = jnp.where(rep_row // SUBLANES == rep_col, 1.0, 0.0).astype(BF16)
    entering = jnp.concatenate([in_r, in_i], axis=-1)
    ent_hi = entering.astype(BF16)
    ent_lo = (entering - ent_hi.astype(F32)).astype(BF16)
    ent = (jnp.dot(rep, ent_hi, preferred_element_type=F32)
           + jnp.dot(rep, ent_lo, preferred_element_type=F32))
    cbr = ent[:, :n_states].reshape(nt, SUBLANES, n_states)
    cbi = ent[:, n_states:].reshape(nt, SUBLANES, n_states)
    pwr = apr_ref[S5_ROW_POW0:S5_ROW_POW0 + SUBLANES, :][None]
    pwi = api_ref[S5_ROW_POW0:S5_ROW_POW0 + SUBLANES, :][None]
    xr, xi = _complex_axpy(xr, xi, pwr, pwi, cbr, cbi)
    xr2 = xr.reshape(tt, n_states).astype(BF16)
    xi2 = xi.reshape(tt, n_states).astype(BF16)

    ys = []
    for hf in range(2):
        sl = slice(hf * S5_HALF_STATES, (hf + 1) * S5_HALF_STATES)
        xcat = jnp.concatenate([xr2[:, sl], xi2[:, sl]], axis=-1)
        ys.append(jnp.dot(xcat, cw_ref[hf], preferred_element_type=F32))
    y = jnp.concatenate(ys, axis=-1) + dsk_ref[...] * us
    y = jax.nn.gelu(y)
    glu = jnp.dot(y.astype(BF16), wglu_ref[...], preferred_element_type=F32)
    ycat_scr[:, POOL_WIDTH:] = (y * jax.nn.sigmoid(glu)).astype(BF16)

    y_out = jnp.dot(ycat_scr[...], wout_ref[...], preferred_element_type=F32)
    o_ref[...] = _post_norm_residual(x_ref[...], y_out, ng_ref[...], gate_ref[0])


def _side_spec(arr, n_steps, step_of):
    assert arr.shape[0] % n_steps == 0
    return pl.BlockSpec((arr.shape[0] // n_steps, arr.shape[1]), lambda *idx: (step_of(*idx), 0))


def _pool_s5_mixer(x, bsz, g, scale, shift, w_in, pool_w, pool_scale, lam_re, lam_im, log_step,
                   b_re, b_im, c_re, c_im, d_skip, w_glu, w_out, ng, gate, side, *, tt):
    t = x.shape[0]
    seq = t // bsz
    gp = S5_GROUPS * S5_STATE
    apr, api, bbr, bbi = _s5_discretise(lam_re, lam_im, log_step, b_re, b_im,
                                        int(math.log2(tt // SUBLANES)))
    n_pow = apr.shape[0]

    gh = S5_GROUPS // 2
    eye = jnp.eye(gh, dtype=F32)

    def in_map(bb):
        bb = bb.reshape(S5_GROUP, 2, gh, S5_STATE)
        return jnp.einsum("hxgp,gk->xghkp", bb, eye).reshape(2, gh * S5_GROUP, gh * S5_STATE)

    bw = jnp.concatenate([in_map(bbr), in_map(bbi)], axis=-1).astype(BF16)

    def out_map(cc):
        cc = cc.reshape(2, gh, S5_GROUP, S5_STATE)
        return jnp.einsum("xghp,gk->xgpkh", cc, eye).reshape(2, gh * S5_STATE, gh * S5_GROUP)

    cw = jnp.concatenate([out_map(c_re), -out_map(c_im)], axis=1).astype(BF16)

    const2 = lambda b, i: (0, 0)
    const3 = lambda b, i: (0, 0, 0)
    nblk = seq // tt
    side_spec = _side_spec(side, bsz * nblk, lambda b, i: b * nblk + i)
    bvec = pl.BlockSpec((1, 1, D_MODEL), lambda b, i: (b, 0, 0))
    return pl.pallas_call(
        _mix0_kernel,
        grid=(bsz, nblk),
        in_specs=[
            pl.BlockSpec((tt, D_MODEL), lambda b, i: (b * nblk + i, 0)),
            pl.BlockSpec((1, D_MODEL), const2),
            bvec, bvec,
            pl.BlockSpec((D_MODEL, D_MODEL), const2),
            pl.BlockSpec((len(POOL_WINDOWS), POOL_GROUP, POOL_GROUP), const3),
            pl.BlockSpec((1, POOL_WIDTH), const2),
            pl.BlockSpec((2, S5_HALF, 2 * S5_HALF_STATES), const3),
            pl.BlockSpec((2, 2 * S5_HALF_STATES, S5_HALF), const3),
            pl.BlockSpec((n_pow, gp), const2),
            pl.BlockSpec((n_pow, gp), const2),
            pl.BlockSpec((1, S5_WIDTH), const2),
            pl.BlockSpec((S5_WIDTH, S5_WIDTH), const2),
            pl.BlockSpec((D_MODEL, D_MODEL), const2),
            pl.BlockSpec((1, D_MODEL), const2),
            bvec,
            side_spec,
        ],
        out_specs=[pl.BlockSpec((tt, D_MODEL), lambda b, i: (b * nblk + i, 0)), side_spec],
        out_shape=[jax.ShapeDtypeStruct((t, D_MODEL), F32),
                   jax.ShapeDtypeStruct(side.shape, BF16)],
        scratch_shapes=[
            pltpu.VMEM((POOL_HALO, POOL_WIDTH), F32),
            pltpu.VMEM((tt + POOL_HALO, POOL_WIDTH), F32),
            pltpu.VMEM((1, gp), F32), pltpu.VMEM((1, gp), F32),
            pltpu.VMEM((gp // LANES, tt, LANES), F32), pltpu.VMEM((gp // LANES, tt, LANES), F32),
            pltpu.VMEM((tt, D_MODEL), BF16),
        ],
        compiler_params=_cparams(("arbitrary", "arbitrary")),
        name="pool_s5_mixer",
    )(x, g, scale, shift, w_in, pool_w.astype(BF16), pool_scale.reshape(1, POOL_WIDTH), bw, cw,
      apr, api, d_skip.reshape(1, S5_WIDTH), w_glu.astype(BF16), w_out, ng, gate, side)


CONV_HALO = SUBLANES
M2_GROUP_WIDTH = M2_HPG * M2_HEADDIM


def _causal_conv_silu(x_ref, halo_scr, full_scr, w_ref, b_ref, col0):
    lc = x_ref.shape[0]
    width = x_ref.shape[1]
    x = x_ref[...].astype(F32)
    full_scr[0:CONV_HALO, :] = halo_scr[...]
    full_scr[CONV_HALO:, :] = x
    halo_scr[...] = x[lc - CONV_HALO:, :]
    cols = slice(col0, col0 + width)
    acc = b_ref[:, cols] + w_ref[M2_CONV - 1:M2_CONV, cols] * x
    for k in range(M2_CONV - 1):
        off = CONV_HALO - (M2_CONV - 1) + k
        acc = acc + w_ref[k:k + 1, cols] * full_scr[off:off + lc, :]
    return acc * jax.nn.sigmoid(acc)


def _split_dot(v, e):
    hi = v.astype(BF16)
    lo = (v - hi.astype(F32)).astype(BF16)
    return (jnp.dot(hi, e, preferred_element_type=F32) + jnp.dot(lo, e, preferred_element_type=F32))


def _ssd_kernel(z_ref, xs_ref, bc_ref, dtr_ref, cw_ref, cb_ref, dtb_ref, alog_ref, dx_ref, ng_ref,
                e_ref, side_ref, o_ref, side_out_ref, halo_x, halo_bc, full_x, full_bc, st_scr, y_scr):
    side_out_ref[...] = side_ref[...].astype(BF16)
    lc = z_ref.shape[0]

    @pl.when(pl.program_id(1) == 0)
    def _():
        halo_x[...] = jnp.zeros_like(halo_x)
        halo_bc[...] = jnp.zeros_like(halo_bc)
        st_scr[...] = jnp.zeros_like(st_scr)

    xs = _causal_conv_silu(xs_ref, halo_x, full_x, cw_ref, cb_ref, 0)
    bc = _causal_conv_silu(bc_ref, halo_bc, full_bc, cw_ref, cb_ref, M2_INNER)

    lane = lax.broadcasted_iota(jnp.int32, (1, LANES), 1)
    x_dt = dtr_ref[...].astype(F32) + dtb_ref[...]
    dt = jnp.maximum(x_dt, 0.0) + jnp.log(1.0 + jnp.exp(-jnp.abs(x_dt)))
    a = jnp.where(lane < M2_HEADS, -jnp.exp(alog_ref[...]) * math.log2(math.e), 0.0)
    da = dt * a
    row = lax.broadcasted_iota(jnp.int32, (lc, lc), 0)
    col = lax.broadcasted_iota(jnp.int32, (lc, lc), 1)
    causal = col <= row
    cs = jnp.dot(jnp.where(causal, 1.0, 0.0), da, preferred_element_type=F32,
                 precision=lax.Precision.HIGHEST)
    cs_last = cs[lc - 1:lc, :]
    ecs = jnp.exp2(cs)
    w_in = dt * jnp.exp2(cs_last - cs)
    cs_t = cs.T
    dt_t = dt.T
    e = e_ref[...]
    ecs_x = _split_dot(ecs, e)
    w_x = _split_dot(w_in, e)
    xsb = xs.astype(BF16)
    xw = (xs * w_x).astype(BF16)

    for g in range(M2_GROUPS):
        bm = bc[:, g * M2_STATE:(g + 1) * M2_STATE].astype(BF16)
        cm = bc[:, M2_BC + g * M2_STATE:M2_BC + (g + 1) * M2_STATE].astype(BF16)
        cbm = lax.dot_general(cm, bm, (((1,), (1,)), ((), ())), preferred_element_type=F32)
        gcols = slice(g * M2_GROUP_WIDTH, (g + 1) * M2_GROUP_WIDTH)
        st = st_scr[g]
        y_off = jnp.dot(cm, st.astype(BF16), preferred_element_type=F32) * ecs_x[:, gcols]
        y_heads = []
        for jj in range(M2_HPG):
            j = g * M2_HPG + jj
            seg = cs[:, j:j + 1] - cs_t[j:j + 1, :]
            dec = jnp.exp2(jnp.where(causal, seg, NEG_BIG))
            m = (cbm * dec * dt_t[j:j + 1, :]).astype(BF16)
            y_heads.append(jnp.dot(m, xsb[:, j * M2_HEADDIM:(j + 1) * M2_HEADDIM],
                                   preferred_element_type=F32))
        y_g = jnp.concatenate(y_heads, axis=-1) + y_off
        st_scr[g] = st * ecs_x[lc - 1:lc, gcols] + lax.dot_general(
            bm, xw[:, gcols], (((0,), (0,)), ((), ())), preferred_element_type=F32)
        zg = z_ref[:, gcols].astype(F32)
        y_g = (y_g + dx_ref[:, gcols] * xs[:, gcols]) * (zg * jax.nn.sigmoid(zg))
        y_scr[:, gcols] = y_g

    y = y_scr[...]
    ms = jnp.mean(y * y, axis=-1, keepdims=True)
    o_ref[...] = (y * lax.rsqrt(ms + EPS) * ng_ref[...]).astype(o_ref.dtype)


def _ssd_mixer(proj, bsz, conv_w, conv_b, dt_bias, a_log, d_skip, norm_g, side, *, lc):
    t = proj.shape[0]
    seq = t // bsz
    nblk = seq // lc
    half = M2_INNER
    pad = LANES - M2_HEADS
    dtb = jnp.pad(dt_bias, (0, pad)).reshape(1, LANES)
    alog = jnp.pad(a_log, (0, pad)).reshape(1, LANES)
    dx = jnp.repeat(d_skip, M2_HEADDIM).reshape(1, M2_INNER)
    heads = jnp.arange(LANES, dtype=jnp.int32)[:, None]
    chans = jnp.arange(M2_INNER, dtype=jnp.int32)[None, :] // M2_HEADDIM
    expand = (heads == chans).astype(BF16)
    const2 = lambda b, i: (0, 0)
    side_spec = _side_spec(side, bsz * nblk, lambda b, i: b * nblk + i)
    return pl.pallas_call(
        _ssd_kernel,
        grid=(bsz, nblk),
        in_specs=[
            pl.BlockSpec((lc, half), lambda b, i: (b * nblk + i, 0)),
            pl.BlockSpec((lc, half), lambda b, i: (b * nblk + i, 1)),
            pl.BlockSpec((lc, half), lambda b, i: (b * nblk + i, 2)),
            pl.BlockSpec((lc, LANES), lambda b, i: (b * nblk + i, 3 * half // LANES)),
            pl.BlockSpec((M2_CONV, M2_CONV_DIM), const2),
            pl.BlockSpec((1, M2_CONV_DIM), const2),
            pl.BlockSpec((1, LANES), const2),
            pl.BlockSpec((1, LANES), const2),
            pl.BlockSpec((1, M2_INNER), const2),
            pl.BlockSpec((1, M2_INNER), const2),
            pl.BlockSpec((LANES, M2_INNER), const2),
            side_spec,
        ],
        out_specs=[pl.BlockSpec((lc, M2_INNER), lambda b, i: (b * nblk + i, 0)), side_spec],
        out_shape=[jax.ShapeDtypeStruct((t, M2_INNER), BF16),
                   jax.ShapeDtypeStruct(side.shape, BF16)],
        scratch_shapes=[
            pltpu.VMEM((CONV_HALO, half), F32), pltpu.VMEM((CONV_HALO, half), F32),
            pltpu.VMEM((lc + CONV_HALO, half), F32), pltpu.VMEM((lc + CONV_HALO, half), F32),
            pltpu.VMEM((M2_GROUPS, M2_STATE, M2_GROUP_WIDTH), F32),
            pltpu.VMEM((lc, M2_INNER), F32),
        ],
        compiler_params=_cparams(("arbitrary", "arbitrary")),
        name="ssd_mixer",
    )(proj, proj, proj, proj, conv_w, conv_b.reshape(1, M2_CONV_DIM), dtb, alog, dx,
      norm_g.reshape(1, M2_INNER), expand, side)


def kernel(x, c, ada_w, ada_b, norm_g, mix_w_in, pool_w, pool_scale, s5_lam_re, s5_lam_im, s5_log_step, s5_b_re, s5_b_im, s5_c_re, s5_c_im, s5_d, s5_w_glu, mix_w_out, ffn_w_gate, ffn_w_up, ffn_w_down, m2_w_in, m2_conv_w, m2_conv_b, m2_dt_bias, m2_a_log, m2_d, m2_norm_g, m2_w_out, moe_w_router, moe_b_router, moe_w_gate, moe_w_up, moe_w_down):
    bsz, seq, d = x.shape
    t = bsz * seq
    xt = x.reshape(t, d)

    mod = _ada_modulation(c, ada_w, ada_b)

    def mod_vecs(layer):
        return [mod[layer, :, k * d:(k + 1) * d].reshape(bsz, 1, d) for k in range(6)]

    def gvec(layer, k):
        return norm_g[layer, k].reshape(1, d)

    sh_m, sc_m, g_m, sh_f, sc_f, g_f = mod_vecs(0)
    ne, dh = N_EXPERTS, EXPERT_HIDDEN
    xt, wu_b = _pool_s5_mixer(xt, bsz, gvec(0, 0), sc_m, sh_m, mix_w_in[0].astype(BF16),
                              pool_w[0], pool_scale[0], s5_lam_re[0], s5_lam_im[0],
                              s5_log_step[0], s5_b_re[0], s5_b_im[0], s5_c_re[0], s5_c_im[0],
                              s5_d[0], s5_w_glu[0], mix_w_out[0].astype(BF16), gvec(0, 1), g_m,
                              moe_w_up[0].reshape(ne * d, dh), tt=256)
    xt = _ffn_sublayer(xt, gvec(0, 2), sc_f, sh_f, ffn_w_gate[0].astype(BF16),
                       ffn_w_up[0].astype(BF16), ffn_w_down[0].astype(BF16), gvec(0, 3), g_f,
                       tm=1024, th=1408)

    sh_m, sc_m, g_m, sh_f, sc_f, g_f = mod_vecs(1)
    proj_pad = 6400
    w_in = jnp.zeros((d, proj_pad), BF16).at[:, :M2_PROJ].set(m2_w_in[0].astype(BF16))
    proj, (wg_b,) = _norm_matmul(
        xt, gvec(1, 0), sc_m, sh_m, w_in, tm=1024, tn=1280, out_dtype=BF16, name="m2_in_proj",
        side=(moe_w_gate[0].reshape(ne * d, dh),), side_steps=32)
    y, wd_b = _ssd_mixer(proj, bsz, m2_conv_w[0], m2_conv_b[0], m2_dt_bias[0], m2_a_log[0],
                         m2_d[0], m2_norm_g[0], moe_w_down[0].reshape(ne * dh, d), lc=128)
    xt = _matmul_post(y, m2_w_out[0].astype(BF16), xt, gvec(1, 1), g_m, tm=1024,
                      name="m2_out_proj")
    xt = _moe_sublayer(xt, gvec(1, 2), sc_f, sh_f, moe_w_router[0], moe_b_router[0],
                       wg_b.reshape(ne, d, dh), wu_b.reshape(ne, d, dh), wd_b.reshape(ne, dh, d),
                       gvec(1, 3), g_f)
    return xt.reshape(bsz, seq, d)
```

```python
import functools
import math

import jax
import jax.numpy as jnp
from jax import lax
from jax.experimental import pallas as pl
from jax.experimental.pallas import tpu as pltpu

F32 = jnp.float32
BF16 = jnp.bfloat16

D_MODEL = 1024
EPS = 1e-6
POOL_WIDTH = 512
POOL_WINDOWS = (2, 4, 8, 16)
POOL_GROUP = 128
S5_WIDTH = 512
S5_GROUP = 16
S5_GROUPS = 32
S5_STATE = 64
M2_INNER = 2048
M2_HEADDIM = 64
M2_HEADS = 32
M2_GROUPS = 8
M2_HPG = 4
M2_STATE = 128
M2_CONV = 4
M2_BC = 1024
M2_CONV_DIM = 4096
M2_PROJ = 6176
N_EXPERTS = 8
EXPERT_HIDDEN = 3584

LANES = 128
SUBLANES = 8
VMEM_LIMIT_BYTES = 56 * 1024 * 1024

TOKEN_TILE = 1024
FFN_HIDDEN_TILE = 1408
M2_PROJ_PAD = 6400
M2_PROJ_TILE = 1280
M2_SIDE_STEPS = 32
S5_BLOCK = 256
SSD_CHUNK = 128
MOE_TOKEN_TILE = 512
MOE_SLOT_TILE = 512
MOE_HIDDEN_TILE = 1792

NEG_BIG = -1e30


def _cparams(sem):
    return pltpu.CompilerParams(dimension_semantics=sem, vmem_limit_bytes=VMEM_LIMIT_BYTES)


def _modulated_norm(x, g, scale, shift):
    ms = jnp.mean(x * x, axis=-1, keepdims=True)
    return (x * lax.rsqrt(ms + EPS) * g) * (1.0 + scale) + shift


def _post_norm_residual(x, y, ng, gate):
    ms = jnp.mean(y * y, axis=-1, keepdims=True)
    return x + gate * (y * lax.rsqrt(ms + EPS) * ng)


def _ada_kernel(c_ref, w_ref, b_ref, o_ref):
    c = c_ref[...]
    a = c * jax.nn.sigmoid(c)
    o_ref[0] = jnp.dot(a.astype(BF16), w_ref[0].astype(BF16),
                       preferred_element_type=F32) + b_ref[0]


def _ada_modulation(c, ada_w, ada_b):
    depth, d, n = ada_w.shape
    b = c.shape[0]
    c_pad = jnp.zeros((SUBLANES, d), F32).at[:b].set(c)
    tn = 1024
    out = pl.pallas_call(
        _ada_kernel,
        grid=(depth, n // tn),
        in_specs=[
            pl.BlockSpec((SUBLANES, d), lambda l, j: (0, 0)),
            pl.BlockSpec((1, d, tn), lambda l, j: (l, 0, j)),
            pl.BlockSpec((1, 1, tn), lambda l, j: (l, 0, j)),
        ],
        out_specs=pl.BlockSpec((1, SUBLANES, tn), lambda l, j: (l, 0, j)),
        out_shape=jax.ShapeDtypeStruct((depth, SUBLANES, n), F32),
        compiler_params=_cparams(("arbitrary", "arbitrary")),
        name="ada_modulation",
    )(c_pad, ada_w, ada_b.reshape(depth, 1, n))
    return out[:, :b]


def _norm_mm_kernel(x_ref, g_ref, sc_ref, sh_ref, w_ref, *rest, n_side, side_steps):
    side_in = rest[:n_side]
    o_ref = rest[n_side]
    side_out = rest[n_side + 1:2 * n_side + 1]
    h_scr = rest[2 * n_side + 1]

    @pl.when(pl.program_id(1) == 0)
    def _():
        h = _modulated_norm(x_ref[...], g_ref[...], sc_ref[0], sh_ref[0])
        h_scr[...] = h.astype(BF16)

    o_ref[...] = jnp.dot(h_scr[...], w_ref[...], preferred_element_type=F32).astype(o_ref.dtype)

    step = pl.program_id(0) * pl.num_programs(1) + pl.program_id(1)
    for k in range(n_side):
        @pl.when(jnp.logical_and(step >= k * side_steps, step < (k + 1) * side_steps))
        def _(k=k):
            side_out[k][...] = side_in[k][...].astype(BF16)


def _norm_matmul(x, g, scale, shift, w, *, tm, tn, out_dtype, name, side=(), side_steps=1):
    t, d = x.shape
    n = w.shape[1]
    n_j = n // tn
    tiles_per_batch = t // scale.shape[0] // tm
    assert len(side) * side_steps <= (t // tm) * n_j

    def side_spec(k, arr):
        rows = arr.shape[0] // side_steps
        return pl.BlockSpec(
            (rows, arr.shape[1]),
            lambda i, j: (jnp.clip(i * n_j + j - k * side_steps, 0, side_steps - 1), 0))

    side_specs = [side_spec(k, a) for k, a in enumerate(side)]
    outs = pl.pallas_call(
        functools.partial(_norm_mm_kernel, n_side=len(side), side_steps=side_steps),
        grid=(t // tm, n_j),
        in_specs=[
            pl.BlockSpec((tm, d), lambda i, j: (i, 0)),
            pl.BlockSpec((1, d), lambda i, j: (0, 0)),
            pl.BlockSpec((1, 1, d), lambda i, j: (i // tiles_per_batch, 0, 0)),
            pl.BlockSpec((1, 1, d), lambda i, j: (i // tiles_per_batch, 0, 0)),
            pl.BlockSpec((d, tn), lambda i, j: (0, j)),
        ] + side_specs,
        out_specs=[pl.BlockSpec((tm, tn), lambda i, j: (i, j))] + side_specs,
        out_shape=[jax.ShapeDtypeStruct((t, n), out_dtype)]
                  + [jax.ShapeDtypeStruct(a.shape, BF16) for a in side],
        scratch_shapes=[pltpu.VMEM((tm, d), BF16)],
        compiler_params=_cparams(("arbitrary", "arbitrary")),
        name=name,
    )(x, g, scale, shift, w, *side)
    return (outs[0], outs[1:]) if side else outs[0]


def _mm_post_kernel(y_ref, w_ref, x_ref, ng_ref, gate_ref, o_ref):
    y = jnp.dot(y_ref[...], w_ref[...], preferred_element_type=F32)
    o_ref[...] = _post_norm_residual(x_ref[...], y, ng_ref[...], gate_ref[0])


def _matmul_post(y, w, x, ng, gate, *, tm, name):
    t, k = y.shape
    d = w.shape[1]
    tiles_per_batch = t // gate.shape[0] // tm
    return pl.pallas_call(
        _mm_post_kernel,
        grid=(t // tm,),
        in_specs=[
            pl.BlockSpec((tm, k), lambda i: (i, 0)),
            pl.BlockSpec((k, d), lambda i: (0, 0)),
            pl.BlockSpec((tm, d), lambda i: (i, 0)),
            pl.BlockSpec((1, d), lambda i: (0, 0)),
            pl.BlockSpec((1, 1, d), lambda i: (i // tiles_per_batch, 0, 0)),
        ],
        out_specs=pl.BlockSpec((tm, d), lambda i: (i, 0)),
        out_shape=jax.ShapeDtypeStruct((t, d), F32),
        compiler_params=_cparams(("arbitrary",)),
        name=name,
    )(y, w, x, ng, gate)


def _ffn_kernel(x_ref, g_ref, sc_ref, sh_ref, wgu_ref, wd_ref, ng_ref, gate_ref,
                o_ref, h_scr, acc_scr):
    j = pl.program_id(1)

    @pl.when(j == 0)
    def _():
        h = _modulated_norm(x_ref[...], g_ref[...], sc_ref[0], sh_ref[0])
        h_scr[...] = h.astype(BF16)
        acc_scr[...] = jnp.zeros_like(acc_scr)

    th = wd_ref.shape[0]
    gu = jnp.dot(h_scr[...], wgu_ref[...], preferred_element_type=F32)
    gt = gu[:, :th]
    up = gu[:, th:]
    act = (gt * jax.nn.sigmoid(gt) * up).astype(BF16)
    acc_scr[...] += jnp.dot(act, wd_ref[...], preferred_element_type=F32)

    @pl.when(j == pl.num_programs(1) - 1)
    def _():
        o_ref[...] = _post_norm_residual(x_ref[...], acc_scr[...], ng_ref[...], gate_ref[0])


def _ffn_sublayer(x, g, scale, shift, wg, wu, wd, ng, gate, *, tm, th):
    t, d = x.shape
    hid = wg.shape[1]
    n_j = hid // th
    tiles_per_batch = t // scale.shape[0] // tm
    bvec = pl.BlockSpec((1, 1, d), lambda i, j: (i // tiles_per_batch, 0, 0))
    wgu = jnp.concatenate([w[:, j * th:(j + 1) * th] for j in range(n_j) for w in (wg, wu)], axis=1)
    return pl.pallas_call(
        _ffn_kernel,
        grid=(t // tm, n_j),
        in_specs=[
            pl.BlockSpec((tm, d), lambda i, j: (i, 0)),
            pl.BlockSpec((1, d), lambda i, j: (0, 0)),
            bvec, bvec,
            pl.BlockSpec((d, 2 * th), lambda i, j: (0, j)),
            pl.BlockSpec((th, d), lambda i, j: (j, 0)),
            pl.BlockSpec((1, d), lambda i, j: (0, 0)),
            bvec,
        ],
        out_specs=pl.BlockSpec((tm, d), lambda i, j: (i, 0)),
        out_shape=jax.ShapeDtypeStruct((t, d), F32),
        scratch_shapes=[pltpu.VMEM((tm, d), BF16), pltpu.VMEM((tm, d), F32)],
        compiler_params=_cparams(("arbitrary", "arbitrary")),
        name="ffn_sublayer",
    )(x, g, scale, shift, wgu, wd, ng, gate)


def _router_kernel(x_ref, g_ref, sc_ref, sh_ref, wr_ref, br_ref, h_ref, meta_ref, cnt_ref):
    tm = x_ref.shape[0]

    h = _modulated_norm(x_ref[...], g_ref[...], sc_ref[0], sh_ref[0])
    h_hi = h.astype(BF16)
    h_ref[...] = h_hi
    h_lo = (h - h_hi.astype(F32)).astype(BF16)
    logits = (jnp.dot(h_hi, wr_ref[0], preferred_element_type=F32)
              + jnp.dot(h_lo, wr_ref[0], preferred_element_type=F32)
              + jnp.dot(h_hi, wr_ref[1], preferred_element_type=F32)) + br_ref[...]
    lane = lax.broadcasted_iota(jnp.int32, (tm, LANES), 1).astype(F32)
    m1 = jnp.max(logits, axis=-1, keepdims=True)
    i1 = jnp.min(jnp.where(logits == m1, lane, float(LANES)), axis=-1, keepdims=True)
    oh1 = lane == i1
    rest = jnp.where(oh1, NEG_BIG * 2.0, logits)
    m2 = jnp.max(rest, axis=-1, keepdims=True)
    i2 = jnp.min(jnp.where(rest == m2, lane, float(LANES)), axis=-1, keepdims=True)
    oh2 = lane == i2
    e = jnp.exp(m2 - m1)
    w1 = 1.0 / (1.0 + e)
    w2 = e / (1.0 + e)

    picks = jnp.where(oh1, 1.0, 0.0) + jnp.where(oh2, 1.0, 0.0)
    row = lax.broadcasted_iota(jnp.int32, (tm, tm), 0)
    col = lax.broadcasted_iota(jnp.int32, (tm, tm), 1)
    lower = jnp.where(col < row, 1.0, 0.0).astype(BF16)
    before = jnp.dot(lower, picks.astype(BF16), preferred_element_type=F32)
    tile_cnt = jnp.sum(picks, axis=0, keepdims=True)
    ea = lax.broadcasted_iota(jnp.int32, (LANES, LANES), 0)
    eb = lax.broadcasted_iota(jnp.int32, (LANES, LANES), 1)
    seg_rows = jnp.floor((tile_cnt + (SUBLANES - 1.0)) * (1.0 / SUBLANES)) * SUBLANES
    seg_off = jnp.dot(jnp.broadcast_to(seg_rows, (SUBLANES, LANES)), jnp.where(ea < eb, 1.0, 0.0),
                      preferred_element_type=F32, precision=lax.Precision.HIGHEST)[0:1, :]
    local = before + seg_off
    loc1 = jnp.sum(jnp.where(oh1, local, 0.0), axis=-1, keepdims=True)
    loc2 = jnp.sum(jnp.where(oh2, local, 0.0), axis=-1, keepdims=True)
    cnt_ref[0] = tile_cnt

    meta = jnp.where(lane == 0.0, i1, 0.0)
    meta = jnp.where(lane == 1.0, i2, meta)
    meta = jnp.where(lane == 2.0, w1, meta)
    meta = jnp.where(lane == 3.0, w2, meta)
    meta = jnp.where(lane == 4.0, loc1, meta)
    meta = jnp.where(lane == 5.0, loc2, meta)
    meta_ref[...] = meta


def _router(x, g, scale, shift, w_router, b_router, *, tm):
    t, d = x.shape
    tiles_per_batch = t // scale.shape[0] // tm
    wr = jnp.zeros((d, LANES), F32).at[:, :N_EXPERTS].set(w_router)
    wr_hi = wr.astype(BF16)
    wr = jnp.stack([wr_hi, (wr - wr_hi.astype(F32)).astype(BF16)])
    br = jnp.full((1, LANES), NEG_BIG, F32).at[0, :N_EXPERTS].set(b_router)
    bvec = pl.BlockSpec((1, 1, d), lambda i: (i // tiles_per_batch, 0, 0))
    return pl.pallas_call(
        _router_kernel,
        grid=(t // tm,),
        in_specs=[
            pl.BlockSpec((tm, d), lambda i: (i, 0)),
            pl.BlockSpec((1, d), lambda i: (0, 0)),
            bvec, bvec,
            pl.BlockSpec((2, d, LANES), lambda i: (0, 0, 0)),
            pl.BlockSpec((1, LANES), lambda i: (0, 0)),
        ],
        out_specs=[
            pl.BlockSpec((tm, d), lambda i: (i, 0)),
            pl.BlockSpec((tm, LANES), lambda i: (i, 0)),
            pl.BlockSpec((1, 1, LANES), lambda i: (i, 0, 0)),
        ],
        out_shape=[
            jax.ShapeDtypeStruct((t, d), BF16),
            jax.ShapeDtypeStruct((t, LANES), F32),
            jax.ShapeDtypeStruct((t // tm, 1, LANES), F32),
        ],
        compiler_params=_cparams(("arbitrary",)),
        name="moe_router",
    )(x, g, scale, shift, wr, br)


SEG_FIELDS = 3
SEG_PAD_ROWS = N_EXPERTS * SUBLANES


def _segment_copies(scal_ref, tile, tm, per_copy):
    base = tile * (SEG_FIELDS * N_EXPERTS)
    for e in range(N_EXPERTS):
        n = scal_ref[base + e]
        g0 = scal_ref[base + N_EXPERTS + e]
        l0 = scal_ref[base + 2 * N_EXPERTS + e]
        done = 0
        b = tm
        while b >= SUBLANES:
            take = n & b

            @pl.when(take != 0)
            def _(b=b, done=done, g0=g0, l0=l0):
                per_copy(pl.multiple_of(l0 + done, SUBLANES), pl.multiple_of(g0 + done, SUBLANES), b)

            done = done + take
            b //= 2


def _dispatch_kernel(scal_ref, tail_ref, h_ref, meta_ref, xs_ref, ws_ref,
                     seg_scr, wseg_scr, zero_scr, sem, zsem, *, n_tiles, max_unused):
    tile = pl.program_id(0)
    tm = h_ref.shape[0]
    ns = seg_scr.shape[1]
    par = tile % 2

    def segment_dmas(which, buf):
        def descriptors(local, glob, n):
            return (pltpu.make_async_copy(wseg_scr.at[buf, pl.ds(local, n), :],
                                          ws_ref.at[pl.ds(glob, n), :], sem.at[buf]),
                    pltpu.make_async_copy(seg_scr.at[buf, pl.ds(local, n), :],
                                          xs_ref.at[pl.ds(glob, n), :], sem.at[buf]))
        return descriptors

    def wait_segments(which, buf):
        descriptors = segment_dmas(which, buf)
        _segment_copies(scal_ref, which, tm, lambda l, g, n: [c.wait() for c in descriptors(l, g, n)])

    @pl.when(tile == 0)
    def _():
        zero_scr[...] = jnp.zeros_like(zero_scr)
        fills = []
        n_slot_tiles = xs_ref.shape[0] // tm
        tails = [(tail_ref[N_EXPERTS + e] > 0, pl.multiple_of(tail_ref[e], tm))
                 for e in range(N_EXPERTS)]
        tails += [(k >= tail_ref[2 * N_EXPERTS], k * tm)
                  for k in range(n_slot_tiles - max_unused, n_slot_tiles)]
        for nonempty, tail in tails:
            fills.append((nonempty,
                          pltpu.make_async_copy(zero_scr, xs_ref.at[pl.ds(tail, tm), :], zsem),
                          pltpu.make_async_copy(zero_scr.at[:, 0:LANES],
                                                ws_ref.at[pl.ds(tail, tm), :], zsem)))
        for nonempty, fill_x, fill_w in fills:
            @pl.when(nonempty)
            def _(fill_x=fill_x, fill_w=fill_w):
                fill_x.start()
                fill_w.start()
        for nonempty, fill_x, fill_w in fills:
            @pl.when(nonempty)
            def _(fill_x=fill_x, fill_w=fill_w):
                fill_x.wait()
                fill_w.wait()

    meta = meta_ref[...]
    meta_t = meta.T
    slot = lax.broadcasted_iota(jnp.int32, (ns, tm), 0).astype(F32)
    g1 = jnp.where(slot == meta_t[4:5, :], 1.0, 0.0).astype(BF16)
    g2 = jnp.where(slot == meta_t[5:6, :], 1.0, 0.0).astype(BF16)
    rows = jnp.dot(g1 + g2, h_ref[...], preferred_element_type=F32)
    lane = lax.broadcasted_iota(jnp.int32, (tm, LANES), 1)

    def split_lanes(w):
        hi = w.astype(BF16).astype(F32)
        return jnp.where(lane == 0, hi, jnp.where(lane == 1, w - hi, 0.0)).astype(BF16)

    w_split = jnp.concatenate([split_lanes(meta[:, 2:3]), split_lanes(meta[:, 3:4])], axis=0)
    w_slot = jnp.dot(jnp.concatenate([g1, g2], axis=1), w_split, preferred_element_type=F32)
    weight = w_slot[:, 0:1] + w_slot[:, 1:2]

    @pl.when(tile >= 2)
    def _():
        wait_segments(tile - 2, par)

    seg_scr[par] = rows
    wseg_scr[par] = jnp.broadcast_to(weight, (ns, LANES))
    descriptors = segment_dmas(tile, par)
    _segment_copies(scal_ref, tile, tm, lambda l, g, n: [c.start() for c in descriptors(l, g, n)])

    @pl.when(tile == n_tiles - 1)
    def _():
        wait_segments(tile, par)
        if n_tiles > 1:
            wait_segments(tile - 1, 1 - par)


def _dispatch(h, meta, seg_table, tail_table, n_slots, *, tm):
    t, d = h.shape
    grid_spec = pltpu.PrefetchScalarGridSpec(
        num_scalar_prefetch=2,
        grid=(t // tm,),
        in_specs=[
            pl.BlockSpec((tm, d), lambda i, sc, tl: (i, 0)),
            pl.BlockSpec((tm, LANES), lambda i, sc, tl: (i, 0)),
        ],
        out_specs=[pl.BlockSpec(memory_space=pl.ANY), pl.BlockSpec(memory_space=pl.ANY)],
        scratch_shapes=[
            pltpu.VMEM((2, 2 * tm + SEG_PAD_ROWS, d), F32),
            pltpu.VMEM((2, 2 * tm + SEG_PAD_ROWS, LANES), F32),
            pltpu.VMEM((tm, d), F32),
            pltpu.SemaphoreType.DMA((2,)), pltpu.SemaphoreType.DMA(()),
        ],
    )
    return pl.pallas_call(
        functools.partial(_dispatch_kernel, n_tiles=t // tm,
                          max_unused=n_slots // tm - 2 * t // tm),
        grid_spec=grid_spec,
        out_shape=[jax.ShapeDtypeStruct((n_slots, d), F32),
                   jax.ShapeDtypeStruct((n_slots, LANES), F32)],
        compiler_params=_cparams(("arbitrary",)),
        name="moe_dispatch",
    )(seg_table, tail_table, h, meta)


def _expert_kernel(te_ref, nu_ref, xs_ref, ws_ref, wg_ref, wu_ref, wd_ref, ys_ref,
                   xb_scr, acc_scr):
    i = pl.program_id(0)
    j = pl.program_id(1)
    last = pl.num_programs(1) - 1
    used = i < nu_ref[0]

    @pl.when(used)
    def _():
        @pl.when(j == 0)
        def _():
            xb_scr[...] = xs_ref[...].astype(BF16)
            acc_scr[...] = jnp.zeros_like(acc_scr)

        x = xb_scr[...]
        gt = jnp.dot(x, wg_ref[0], preferred_element_type=F32)
        up = jnp.dot(x, wu_ref[0], preferred_element_type=F32)
        act = (gt * jax.nn.sigmoid(gt) * up).astype(BF16)
        acc_scr[...] += jnp.dot(act, wd_ref[0], preferred_element_type=F32)

        @pl.when(j == last)
        def _():
            ys_ref[...] = (acc_scr[...] * ws_ref[:, 0:1]).astype(ys_ref.dtype)

    @pl.when(jnp.logical_and(jnp.logical_not(used), j == last))
    def _():
        ys_ref[...] = jnp.zeros_like(ys_ref)


def _experts(xs, ws, tile_expert, n_used, wg, wu, wd, *, tm, th):
    n_slots, d = xs.shape
    hid = wg.shape[2]
    nt = n_slots // tm

    def hidden_step(i, j, nu):
        return jnp.where(i < nu[0], j, 0)

    def slot_tile(i, nu):
        return jnp.where(i < nu[0], i, 0)

    grid_spec = pltpu.PrefetchScalarGridSpec(
        num_scalar_prefetch=2,
        grid=(nt, hid // th),
        in_specs=[
            pl.BlockSpec((tm, d), lambda i, j, te, nu: (slot_tile(i, nu), 0)),
            pl.BlockSpec((tm, LANES), lambda i, j, te, nu: (slot_tile(i, nu), 0)),
            pl.BlockSpec((1, d, th), lambda i, j, te, nu: (te[i], 0, hidden_step(i, j, nu))),
            pl.BlockSpec((1, d, th), lambda i, j, te, nu: (te[i], 0, hidden_step(i, j, nu))),
            pl.BlockSpec((1, th, d), lambda i, j, te, nu: (te[i], hidden_step(i, j, nu), 0)),
        ],
        out_specs=pl.BlockSpec((tm, d), lambda i, j, te, nu: (i, 0)),
        scratch_shapes=[pltpu.VMEM((tm, d), BF16), pltpu.VMEM((tm, d), F32)],
    )
    return pl.pallas_call(
        _expert_kernel,
        grid_spec=grid_spec,
        out_shape=jax.ShapeDtypeStruct((n_slots, d), F32),
        compiler_params=_cparams(("arbitrary", "arbitrary")),
        name="moe_experts",
    )(tile_expert, n_used, xs, ws, wg, wu, wd)


def _combine_kernel(scal_ref, ys_ref, meta_ref, x_ref, ng_ref, gate_ref, o_ref, buf, sem,
                    *, n_tiles):
    tile = pl.program_id(0)
    tm = x_ref.shape[0]
    ns = buf.shape[1]
    par = tile % 2

    def fetch(which, slot, wait):
        def copy_in(local, glob, n):
            dma = pltpu.make_async_copy(ys_ref.at[pl.ds(glob, n), :],
                                        buf.at[slot, pl.ds(local, n), :], sem.at[slot])
            dma.wait() if wait else dma.start()
        _segment_copies(scal_ref, which, tm, copy_in)

    @pl.when(tile == 0)
    def _():
        buf[...] = jnp.zeros_like(buf)
        fetch(tile, par, False)

    @pl.when(tile + 1 < n_tiles)
    def _():
        fetch(tile + 1, 1 - par, False)

    fetch(tile, par, True)
    meta = meta_ref[...]
    slot_id = lax.broadcasted_iota(jnp.int32, (tm, ns), 1).astype(F32)
    pick = jnp.where((slot_id == meta[:, 4:5]) | (slot_id == meta[:, 5:6]), 1.0, 0.0).astype(BF16)
    y = jnp.dot(pick, buf[par].astype(BF16), preferred_element_type=F32)
    o_ref[...] = _post_norm_residual(x_ref[...], y, ng_ref[...], gate_ref[0])


def _combine(ys, meta, seg_table, x, ng, gate, *, tm):
    t, d = x.shape
    tiles_per_batch = t // gate.shape[0] // tm
    grid_spec = pltpu.PrefetchScalarGridSpec(
        num_scalar_prefetch=1,
        grid=(t // tm,),
        in_specs=[
            pl.BlockSpec(memory_space=pl.ANY),
            pl.BlockSpec((tm, LANES), lambda i, sc: (i, 0)),
            pl.BlockSpec((tm, d), lambda i, sc: (i, 0)),
            pl.BlockSpec((1, d), lambda i, sc: (0, 0)),
            pl.BlockSpec((1, 1, d), lambda i, sc: (i // tiles_per_batch, 0, 0)),
        ],
        out_specs=pl.BlockSpec((tm, d), lambda i, sc: (i, 0)),
        scratch_shapes=[pltpu.VMEM((2, 2 * tm + SEG_PAD_ROWS, d), F32),
                        pltpu.SemaphoreType.DMA((2,))],
    )
    return pl.pallas_call(
        functools.partial(_combine_kernel, n_tiles=t // tm),
        grid_spec=grid_spec,
        out_shape=jax.ShapeDtypeStruct((t, d), F32),
        compiler_params=_cparams(("arbitrary",)),
        name="moe_combine",
    )(seg_table, ys, meta, x, ng, gate)


def _moe_sublayer(x, g, scale, shift, w_router, b_router, wg, wu, wd, ng, gate):
    t = x.shape[0]
    tm, tm_e, th = MOE_TOKEN_TILE, MOE_SLOT_TILE, MOE_HIDDEN_TILE
    h, meta, counts = _router(x, g, scale, shift, w_router, b_router, tm=tm)

    assert tm == tm_e
    tile_cnt = counts[:, 0, :N_EXPERTS].astype(jnp.int32)
    seg_rows = ((tile_cnt + SUBLANES - 1) // SUBLANES) * SUBLANES
    cnt = jnp.sum(seg_rows, axis=0)
    padded = ((cnt + tm_e - 1) // tm_e) * tm_e
    ends = jnp.cumsum(padded)
    offs = ends - padded
    glob_start = offs[None, :] + jnp.cumsum(seg_rows, axis=0) - seg_rows
    local_start = jnp.cumsum(seg_rows, axis=1) - seg_rows
    seg_table = jnp.concatenate([seg_rows, glob_start, local_start], axis=1).reshape(-1)
    tail_table = jnp.concatenate([ends - tm_e, padded, ends[-1:] // tm_e]).astype(jnp.int32)
    n_tok_tiles = t // tm
    nt = -(-(2 * t + n_tok_tiles * SEG_PAD_ROWS) // tm_e) + N_EXPERTS
    starts = jnp.arange(nt, dtype=jnp.int32) * tm_e
    tile_expert = jnp.minimum(jnp.sum(starts[:, None] >= ends[None, :], axis=1), N_EXPERTS - 1)
    tile_expert = tile_expert.astype(jnp.int32)
    n_used = (ends[-1:] // tm_e).astype(jnp.int32)

    xs, ws = _dispatch(h, meta, seg_table, tail_table, nt * tm_e, tm=tm)
    ys = _experts(xs, ws, tile_expert, n_used, wg, wu, wd, tm=tm_e, th=th)
    return _combine(ys, meta, seg_table, x, ng, gate, tm=tm)


def _s5_disc_kernel(lam_re_ref, lam_im_ref, step_ref, pow_ref, keep_ref, b_re_ref, b_im_ref,
                    apr_ref, api_ref, bbr_ref, bbi_ref):
    lr = jnp.minimum(lam_re_ref[...], -1e-4)
    li = lam_im_ref[...]
    step = step_ref[...]
    m = pow_ref[...]
    mag = jnp.exp(lr * step * m) * keep_ref[...]
    ang = li * step * m
    apr_ref[...] = mag * jnp.cos(ang)
    api_ref[...] = mag * jnp.sin(ang)
    mag1 = jnp.exp(lr * step)
    ar = mag1 * jnp.cos(li * step)
    ai = mag1 * jnp.sin(li * step)
    inv = 1.0 / (lr * lr + li * li)
    fr = ((ar - 1.0) * lr + ai * li) * inv
    fi = (ai * lr - (ar - 1.0) * li) * inv
    br = b_re_ref[...]
    bi = b_im_ref[...]
    bbr_ref[...] = fr * br - fi * bi
    bbi_ref[...] = fr * bi + fi * br


S5_LOCAL_STEPS = 3
S5_ROW_POW0 = S5_LOCAL_STEPS * SUBLANES
S5_TILE_POW0 = S5_ROW_POW0 + SUBLANES


def _s5_power_rows(n_tile_steps):
    exps, keep = [], []
    for k in range(S5_LOCAL_STEPS):
        for tau in range(SUBLANES):
            exps.append(float(1 << k))
            keep.append(1.0 if tau >= (1 << k) else 0.0)
    for tau in range(SUBLANES):
        exps.append(float(tau + 1))
        keep.append(1.0)
    for k in range(n_tile_steps):
        exps.append(float(SUBLANES << k))
        keep.append(1.0)
    while len(exps) % SUBLANES:
        exps.append(0.0)
        keep.append(0.0)
    return exps, keep


def _s5_discretise(lam_re, lam_im, log_step, b_re, b_im, n_tile_steps):
    gp = S5_GROUPS * S5_STATE
    step = jnp.repeat(jnp.exp(log_step), S5_STATE).reshape(1, gp)
    exps, keep = _s5_power_rows(n_tile_steps)
    rows = len(exps)
    b_re_t = jnp.transpose(b_re, (2, 0, 1)).reshape(S5_GROUP, gp)
    b_im_t = jnp.transpose(b_im, (2, 0, 1)).reshape(S5_GROUP, gp)
    return pl.pallas_call(
        _s5_disc_kernel,
        out_shape=[jax.ShapeDtypeStruct((rows, gp), F32), jax.ShapeDtypeStruct((rows, gp), F32),
                   jax.ShapeDtypeStruct((S5_GROUP, gp), F32), jax.ShapeDtypeStruct((S5_GROUP, gp), F32)],
        name="s5_discretise",
    )(lam_re.reshape(1, gp), lam_im.reshape(1, gp), step,
      jnp.asarray(exps, F32).reshape(rows, 1), jnp.asarray(keep, F32).reshape(rows, 1),
      b_re_t, b_im_t)


POOL_HALO = 16
S5_HALF = 256
S5_HALF_STATES = 1024


def _complex_axpy(xr, xi, cr, ci, sr, si):
    return xr + (cr * sr - ci * si), xi + (cr * si + ci * sr)


def _mix0_kernel(x_ref, g_ref, sc_ref, sh_ref, win_ref, pw_ref, ps_ref, bw_ref, cw_ref, apr_ref,
                 api_ref, dsk_ref, wglu_ref, wout_ref, ng_ref, gate_ref, side_ref, o_ref,
                 side_out_ref, halo_scr, ext_scr, sr_scr, si_scr, xr_scr, xi_scr, ycat_scr):
    side_out_ref[...] = side_ref[...].astype(BF16)
    tt = x_ref.shape[0]
    n_states = sr_scr.shape[1]
    blk = pl.program_id(1)

    h = _modulated_norm(x_ref[...], g_ref[...], sc_ref[0], sh_ref[0]).astype(BF16)
    u = jnp.dot(h, win_ref[...], preferred_element_type=F32)

    @pl.when(blk == 0)
    def _():
        halo_scr[...] = jnp.zeros_like(halo_scr)
        sr_scr[...] = jnp.zeros_like(sr_scr)
        si_scr[...] = jnp.zeros_like(si_scr)

    up = u[:, :POOL_WIDTH]
    ext_scr[0:POOL_HALO, :] = halo_scr[...]
    ext_scr[POOL_HALO:, :] = up
    halo_scr[...] = up[tt - POOL_HALO:, :]
    ext = ext_scr[...].astype(BF16)
    row = lax.broadcasted_iota(jnp.int32, (tt, tt + POOL_HALO), 0)
    col = lax.broadcasted_iota(jnp.int32, (tt, tt + POOL_HALO), 1)
    lag = row + POOL_HALO - col
    t_glob = (blk * tt + row + 1).astype(F32)
    pooled_out = []
    for gi, win in enumerate(POOL_WINDOWS):
        inv_count = 1.0 / jnp.minimum(t_glob, float(win))
        band = jnp.where((lag >= 0) & (lag < win), inv_count, 0.0) - jnp.where(lag == 0, 1.0, 0.0)
        pooled = jnp.dot(band.astype(BF16), ext[:, gi * POOL_GROUP:(gi + 1) * POOL_GROUP],
                         preferred_element_type=F32)
        pooled_out.append(jnp.dot(pooled.astype(BF16), pw_ref[gi], preferred_element_type=F32))
    y_pool = jnp.concatenate(pooled_out, axis=-1) * ps_ref[...]
    ycat_scr[:, :POOL_WIDTH] = y_pool.astype(BF16)

    us = u[:, POOL_WIDTH:]
    usb = us.astype(BF16)
    bu = [jnp.dot(usb[:, hf * S5_HALF:(hf + 1) * S5_HALF], bw_ref[hf], preferred_element_type=F32)
          for hf in range(2)]
    nt = tt // SUBLANES
    xr = jnp.concatenate([b[:, :S5_HALF_STATES] for b in bu], axis=-1).reshape(nt, SUBLANES, n_states)
    xi = jnp.concatenate([b[:, S5_HALF_STATES:] for b in bu], axis=-1).reshape(nt, SUBLANES, n_states)
    for k in range(S5_LOCAL_STEPS):
        cr = apr_ref[k * SUBLANES:(k + 1) * SUBLANES, :][None]
        ci = api_ref[k * SUBLANES:(k + 1) * SUBLANES, :][None]
        xr, xi = _complex_axpy(xr, xi, cr, ci, pltpu.roll(xr, 1 << k, 1), pltpu.roll(xi, 1 << k, 1))

    xr2 = xr.reshape(tt, n_states)
    xi2 = xi.reshape(tt, n_states)
    n_cb = n_states // LANES
    for cb in range(n_cb):
        xr_scr[cb] = xr2[:, cb * LANES:(cb + 1) * LANES]
        xi_scr[cb] = xi2[:, cb * LANES:(cb + 1) * LANES]
    tile_end = pl.ds(SUBLANES - 1, nt, stride=SUBLANES)
    er = jnp.concatenate([xr_scr[cb, tile_end, :] for cb in range(n_cb)], axis=-1)
    ei = jnp.concatenate([xi_scr[cb, tile_end, :] for cb in range(n_cb)], axis=-1)
    prev_r = sr_scr[...]
    prev_i = si_scr[...]
    tile_row = lax.broadcasted_iota(jnp.int32, (nt, n_states), 0)
    a8r = apr_ref[S5_TILE_POW0:S5_TILE_POW0 + 1, :]
    a8i = api_ref[S5_TILE_POW0:S5_TILE_POW0 + 1, :]
    er = er + jnp.where(tile_row == 0, a8r * prev_r - a8i * prev_i, 0.0)
    ei = ei + jnp.where(tile_row == 0, a8r * prev_i + a8i * prev_r, 0.0)
    k = 0
    while (1 << k) < nt:
        cr = apr_ref[S5_TILE_POW0 + k:S5_TILE_POW0 + k + 1, :]
        ci = api_ref[S5_TILE_POW0 + k:S5_TILE_POW0 + k + 1, :]
        pr = jnp.where(tile_row < (1 << k), 0.0, pltpu.roll(er, 1 << k, 0))
        pi = jnp.where(tile_row < (1 << k), 0.0, pltpu.roll(ei, 1 << k, 0))
        er, ei = _complex_axpy(er, ei, cr, ci, pr, pi)
        k += 1
    sr_scr[...] = er[nt - 1:nt, :]
    si_scr[...] = ei[nt - 1:nt, :]
    in_r = jnp.where(tile_row == 0, prev_r, pltpu.roll(er, 1, 0))
    in_i = jnp.where(tile_row == 0, prev_i, pltpu.roll(ei, 1, 0))

    rep_row = lax.broadcasted_iota(jnp.int32, (tt, nt), 0)
    rep_col = lax.broadcasted_iota(jnp.int32, (tt, nt), 1)
    rep = jnp.where(rep_row // SUBLANES == rep_col, 1.0, 0.0).astype(BF16)
    entering = jnp.concatenate([in_r, in_i], axis=-1)
    ent_hi = entering.astype(BF16)
    ent_lo = (entering - ent_hi.astype(F32)).astype(BF16)
    ent = (jnp.dot(rep, ent_hi, preferred_element_type=F32)
           + jnp.dot(rep, ent_lo, preferred_element_type=F32))
    cbr = ent[:, :n_states].reshape(nt, SUBLANES, n_states)
    cbi = ent[:, n_states:].reshape(nt, SUBLANES, n_states)
    pwr = apr_ref[S5_ROW_POW0:S5_ROW_POW0 + SUBLANES, :][None]
    pwi = api_ref[S5_ROW_POW0:S5_ROW_POW0 + SUBLANES, :][None]
    xr, xi = _complex_axpy(xr, xi, pwr, pwi, cbr, cbi)
    xr2 = xr.reshape(tt, n_states).astype(BF16)
    xi2 = xi.reshape(tt, n_states).astype(BF16)

    ys = []
    for hf in range(2):
        sl = slice(hf * S5_HALF_STATES, (hf + 1) * S5_HALF_STATES)
        xcat = jnp.concatenate([xr2[:, sl], xi2[:, sl]], axis=-1)
        ys.append(jnp.dot(xcat, cw_ref[hf], preferred_element_type=F32))
    y = jnp.concatenate(ys, axis=-1) + dsk_ref[...] * us
    y = jax.nn.gelu(y)
    glu = jnp.dot(y.astype(BF16), wglu_ref[...], preferred_element_type=F32)
    ycat_scr[:, POOL_WIDTH:] = (y * jax.nn.sigmoid(glu)).astype(BF16)

    y_out = jnp.dot(ycat_scr[...], wout_ref[...], preferred_element_type=F32)
    o_ref[...] = _post_norm_residual(x_ref[...], y_out, ng_ref[...], gate_ref[0])


def _side_spec(arr, n_steps, step_of):
    assert arr.shape[0] % n_steps == 0
    return pl.BlockSpec((arr.shape[0] // n_steps, arr.shape[1]), lambda *idx: (step_of(*idx), 0))


def _pool_s5_mixer(x, bsz, g, scale, shift, w_in, pool_w, pool_scale, lam_re, lam_im, log_step,
                   b_re, b_im, c_re, c_im, d_skip, w_glu, w_out, ng, gate, side, *, tt):
    t = x.shape[0]
    seq = t // bsz
    gp = S5_GROUPS * S5_STATE
    apr, api, bbr, bbi = _s5_discretise(lam_re, lam_im, log_step, b_re, b_im,
                                        int(math.log2(tt // SUBLANES)))
    n_pow = apr.shape[0]

    gh = S5_GROUPS // 2
    eye = jnp.eye(gh, dtype=F32)

    def in_map(bb):
        bb = bb.reshape(S5_GROUP, 2, gh, S5_STATE)
        return jnp.einsum("hxgp,gk->xghkp", bb, eye).reshape(2, gh * S5_GROUP, gh * S5_STATE)

    bw = jnp.concatenate([in_map(bbr), in_map(bbi)], axis=-1).astype(BF16)

    def out_map(cc):
        cc = cc.reshape(2, gh, S5_GROUP, S5_STATE)
        return jnp.einsum("xghp,gk->xgpkh", cc, eye).reshape(2, gh * S5_STATE, gh * S5_GROUP)

    cw = jnp.concatenate([out_map(c_re), -out_map(c_im)], axis=1).astype(BF16)

    const2 = lambda b, i: (0, 0)
    const3 = lambda b, i: (0, 0, 0)
    nblk = seq // tt
    side_spec = _side_spec(side, bsz * nblk, lambda b, i: b * nblk + i)
    bvec = pl.BlockSpec((1, 1, D_MODEL), lambda b, i: (b, 0, 0))
    return pl.pallas_call(
        _mix0_kernel,
        grid=(bsz, nblk),
        in_specs=[
            pl.BlockSpec((tt, D_MODEL), lambda b, i: (b * nblk + i, 0)),
            pl.BlockSpec((1, D_MODEL), const2),
            bvec, bvec,
            pl.BlockSpec((D_MODEL, D_MODEL), const2),
            pl.BlockSpec((len(POOL_WINDOWS), POOL_GROUP, POOL_GROUP), const3),
            pl.BlockSpec((1, POOL_WIDTH), const2),
            pl.BlockSpec((2, S5_HALF, 2 * S5_HALF_STATES), const3),
            pl.BlockSpec((2, 2 * S5_HALF_STATES, S5_HALF), const3),
            pl.BlockSpec((n_pow, gp), const2),
            pl.BlockSpec((n_pow, gp), const2),
            pl.BlockSpec((1, S5_WIDTH), const2),
            pl.BlockSpec((S5_WIDTH, S5_WIDTH), const2),
            pl.BlockSpec((D_MODEL, D_MODEL), const2),
            pl.BlockSpec((1, D_MODEL), const2),
            bvec,
            side_spec,
        ],
        out_specs=[pl.BlockSpec((tt, D_MODEL), lambda b, i: (b * nblk + i, 0)), side_spec],
        out_shape=[jax.ShapeDtypeStruct((t, D_MODEL), F32),
                   jax.ShapeDtypeStruct(side.shape, BF16)],
        scratch_shapes=[
            pltpu.VMEM((POOL_HALO, POOL_WIDTH), F32),
            pltpu.VMEM((tt + POOL_HALO, POOL_WIDTH), F32),
            pltpu.VMEM((1, gp), F32), pltpu.VMEM((1, gp), F32),
            pltpu.VMEM((gp // LANES, tt, LANES), F32), pltpu.VMEM((gp // LANES, tt, LANES), F32),
            pltpu.VMEM((tt, D_MODEL), BF16),
        ],
        compiler_params=_cparams(("arbitrary", "arbitrary")),
        name="pool_s5_mixer",
    )(x, g, scale, shift, w_in, pool_w.astype(BF16), pool_scale.reshape(1, POOL_WIDTH), bw, cw,
      apr, api, d_skip.reshape(1, S5_WIDTH), w_glu.astype(BF16), w_out, ng, gate, side)


CONV_HALO = SUBLANES
M2_GROUP_WIDTH = M2_HPG * M2_HEADDIM


def _causal_conv_silu(x_ref, halo_scr, full_scr, w_ref, b_ref, col0):
    lc = x_ref.shape[0]
    width = x_ref.shape[1]
    x = x_ref[...].astype(F32)
    full_scr[0:CONV_HALO, :] = halo_scr[...]
    full_scr[CONV_HALO:, :] = x
    halo_scr[...] = x[lc - CONV_HALO:, :]
    cols = slice(col0, col0 + width)
    acc = b_ref[:, cols] + w_ref[M2_CONV - 1:M2_CONV, cols] * x
    for k in range(M2_CONV - 1):
        off = CONV_HALO - (M2_CONV - 1) + k
        acc = acc + w_ref[k:k + 1, cols] * full_scr[off:off + lc, :]
    return acc * jax.nn.sigmoid(acc)


def _split_dot(v, e):
    hi = v.astype(BF16)
    lo = (v - hi.astype(F32)).astype(BF16)
    return (jnp.dot(hi, e, preferred_element_type=F32) + jnp.dot(lo, e, preferred_element_type=F32))


def _ssd_kernel(z_ref, xs_ref, bc_ref, dtr_ref, cw_ref, cb_ref, dtb_ref, alog_ref, dx_ref, ng_ref,
                e_ref, side_ref, o_ref, side_out_ref, halo_x, halo_bc, full_x, full_bc, st_scr, y_scr):
    side_out_ref[...] = side_ref[...].astype(BF16)
    lc = z_ref.shape[0]

    @pl.when(pl.program_id(1) == 0)
    def _():
        halo_x[...] = jnp.zeros_like(halo_x)
        halo_bc[...] = jnp.zeros_like(halo_bc)
        st_scr[...] = jnp.zeros_like(st_scr)

    xs = _causal_conv_silu(xs_ref, halo_x, full_x, cw_ref, cb_ref, 0)
    bc = _causal_conv_silu(bc_ref, halo_bc, full_bc, cw_ref, cb_ref, M2_INNER)

    lane = lax.broadcasted_iota(jnp.int32, (1, LANES), 1)
    x_dt = dtr_ref[...].astype(F32) + dtb_ref[...]
    dt = jnp.maximum(x_dt, 0.0) + jnp.log(1.0 + jnp.exp(-jnp.abs(x_dt)))
    a = jnp.where(lane < M2_HEADS, -jnp.exp(alog_ref[...]) * math.log2(math.e), 0.0)
    da = dt * a
    row = lax.broadcasted_iota(jnp.int32, (lc, lc), 0)
    col = lax.broadcasted_iota(jnp.int32, (lc, lc), 1)
    causal = col <= row
    cs = jnp.dot(jnp.where(causal, 1.0, 0.0), da, preferred_element_type=F32,
                 precision=lax.Precision.HIGHEST)
    cs_last = cs[lc - 1:lc, :]
    ecs = jnp.exp2(cs)
    w_in = dt * jnp.exp2(cs_last - cs)
    cs_t = cs.T
    dt_t = dt.T
    e = e_ref[...]
    ecs_x = _split_dot(ecs, e)
    w_x = _split_dot(w_in, e)
    xsb = xs.astype(BF16)
    xw = (xs * w_x).astype(BF16)

    for g in range(M2_GROUPS):
        bm = bc[:, g * M2_STATE:(g + 1) * M2_STATE].astype(BF16)
        cm = bc[:, M2_BC + g * M2_STATE:M2_BC + (g + 1) * M2_STATE].astype(BF16)
        cbm = lax.dot_general(cm, bm, (((1,), (1,)), ((), ())), preferred_element_type=F32)
        gcols = slice(g * M2_GROUP_WIDTH, (g + 1) * M2_GROUP_WIDTH)
        st = st_scr[g]
        y_off = jnp.dot(cm, st.astype(BF16), preferred_element_type=F32) * ecs_x[:, gcols]
        y_heads = []
        for jj in range(M2_HPG):
            j = g * M2_HPG + jj
            seg = cs[:, j:j + 1] - cs_t[j:j + 1, :]
            dec = jnp.exp2(jnp.where(causal, seg, NEG_BIG))
            m = (cbm * dec * dt_t[j:j + 1, :]).astype(BF16)
            y_heads.append(jnp.dot(m, xsb[:, j * M2_HEADDIM:(j + 1) * M2_HEADDIM],
                                   preferred_element_type=F32))
        y_g = jnp.concatenate(y_heads, axis=-1) + y_off
        st_scr[g] = st * ecs_x[lc - 1:lc, gcols] + lax.dot_general(
            bm, xw[:, gcols], (((0,), (0,)), ((), ())), preferred_element_type=F32)
        zg = z_ref[:, gcols].astype(F32)
        y_g = (y_g + dx_ref[:, gcols] * xs[:, gcols]) * (zg * jax.nn.sigmoid(zg))
        y_scr[:, gcols] = y_g

    y = y_scr[...]
    ms = jnp.mean(y * y, axis=-1, keepdims=True)
    o_ref[...] = (y * lax.rsqrt(ms + EPS) * ng_ref[...]).astype(o_ref.dtype)


def _ssd_mixer(proj, bsz, conv_w, conv_b, dt_bias, a_log, d_skip, norm_g, side, *, lc):
    t = proj.shape[0]
    seq = t // bsz
    nblk = seq // lc
    half = M2_INNER
    pad = LANES - M2_HEADS
    dtb = jnp.pad(dt_bias, (0, pad)).reshape(1, LANES)
    alog = jnp.pad(a_log, (0, pad)).reshape(1, LANES)
    dx = jnp.repeat(d_skip, M2_HEADDIM).reshape(1, M2_INNER)
    heads = jnp.arange(LANES, dtype=jnp.int32)[:, None]
    chans = jnp.arange(M2_INNER, dtype=jnp.int32)[None, :] // M2_HEADDIM
    expand = (heads == chans).astype(BF16)
    const2 = lambda b, i: (0, 0)
    side_spec = _side_spec(side, bsz * nblk, lambda b, i: b * nblk + i)
    return pl.pallas_call(
        _ssd_kernel,
        grid=(bsz, nblk),
        in_specs=[
            pl.BlockSpec((lc, half), lambda b, i: (b * nblk + i, 0)),
            pl.BlockSpec((lc, half), lambda b, i: (b * nblk + i, 1)),
            pl.BlockSpec((lc, half), lambda b, i: (b * nblk + i, 2)),
            pl.BlockSpec((lc, LANES), lambda b, i: (b * nblk + i, 3 * half // LANES)),
            pl.BlockSpec((M2_CONV, M2_CONV_DIM), const2),
            pl.BlockSpec((1, M2_CONV_DIM), const2),
            pl.BlockSpec((1, LANES), const2),
            pl.BlockSpec((1, LANES), const2),
            pl.BlockSpec((1, M2_INNER), const2),
            pl.BlockSpec((1, M2_INNER), const2),
            pl.BlockSpec((LANES, M2_INNER), const2),
            side_spec,
        ],
        out_specs=[pl.BlockSpec((lc, M2_INNER), lambda b, i: (b * nblk + i, 0)), side_spec],
        out_shape=[jax.ShapeDtypeStruct((t, M2_INNER), BF16),
                   jax.ShapeDtypeStruct(side.shape, BF16)],
        scratch_shapes=[
            pltpu.VMEM((CONV_HALO, half), F32), pltpu.VMEM((CONV_HALO, half), F32),
            pltpu.VMEM((lc + CONV_HALO, half), F32), pltpu.VMEM((lc + CONV_HALO, half), F32),
            pltpu.VMEM((M2_GROUPS, M2_STATE, M2_GROUP_WIDTH), F32),
            pltpu.VMEM((lc, M2_INNER), F32),
        ],
        compiler_params=_cparams(("arbitrary", "arbitrary")),
        name="ssd_mixer",
    )(proj, proj, proj, proj, conv_w, conv_b.reshape(1, M2_CONV_DIM), dtb, alog, dx,
      norm_g.reshape(1, M2_INNER), expand, side)


def kernel(x, c, ada_w, ada_b, norm_g, mix_w_in, pool_w, pool_scale, s5_lam_re, s5_lam_im, s5_log_step, s5_b_re, s5_b_im, s5_c_re, s5_c_im, s5_d, s5_w_glu, mix_w_out, ffn_w_gate, ffn_w_up, ffn_w_down, m2_w_in, m2_conv_w, m2_conv_b, m2_dt_bias, m2_a_log, m2_d, m2_norm_g, m2_w_out, moe_w_router, moe_b_router, moe_w_gate, moe_w_up, moe_w_down):
    bsz, seq, d = x.shape
    t = bsz * seq
    xt = x.reshape(t, d)

    mod = _ada_modulation(c, ada_w, ada_b)

    def mod_vecs(layer):
        return [mod[layer, :, k * d:(k + 1) * d].reshape(bsz, 1, d) for k in range(6)]

    def gvec(layer, k):
        return norm_g[layer, k].reshape(1, d)

    sh_m, sc_m, g_m, sh_f, sc_f, g_f = mod_vecs(0)
    ne, dh = N_EXPERTS, EXPERT_HIDDEN
    xt, wu_b = _pool_s5_mixer(xt, bsz, gvec(0, 0), sc_m, sh_m, mix_w_in[0].astype(BF16),
                              pool_w[0], pool_scale[0], s5_lam_re[0], s5_lam_im[0],
                              s5_log_step[0], s5_b_re[0], s5_b_im[0], s5_c_re[0], s5_c_im[0],
                              s5_d[0], s5_w_glu[0], mix_w_out[0].astype(BF16), gvec(0, 1), g_m,
                              moe_w_up[0].reshape(ne * d, dh), tt=S5_BLOCK)
    xt = _ffn_sublayer(xt, gvec(0, 2), sc_f, sh_f, ffn_w_gate[0].astype(BF16),
                       ffn_w_up[0].astype(BF16), ffn_w_down[0].astype(BF16), gvec(0, 3), g_f,
                       tm=TOKEN_TILE, th=FFN_HIDDEN_TILE)

    sh_m, sc_m, g_m, sh_f, sc_f, g_f = mod_vecs(1)
    w_in = jnp.zeros((d, M2_PROJ_PAD), BF16).at[:, :M2_PROJ].set(m2_w_in[0].astype(BF16))
    proj, (wg_b,) = _norm_matmul(
        xt, gvec(1, 0), sc_m, sh_m, w_in, tm=TOKEN_TILE, tn=M2_PROJ_TILE, out_dtype=BF16,
        name="m2_in_proj", side=(moe_w_gate[0].reshape(ne * d, dh),), side_steps=M2_SIDE_STEPS)
    y, wd_b = _ssd_mixer(proj, bsz, m2_conv_w[0], m2_conv_b[0], m2_dt_bias[0], m2_a_log[0],
                         m2_d[0], m2_norm_g[0], moe_w_down[0].reshape(ne * dh, d), lc=SSD_CHUNK)
    xt = _matmul_post(y, m2_w_out[0].astype(BF16), xt, gvec(1, 1), g_m, tm=TOKEN_TILE,
                      name="m2_out_proj")
    xt = _moe_sublayer(xt, gvec(1, 2), sc_f, sh_f, moe_w_router[0], moe_b_router[0],
                       wg_b.reshape(ne, d, dh), wu_b.reshape(ne, d, dh), wd_b.reshape(ne, dh, d),
                       gvec(1, 3), g_f)
    return xt.reshape(bsz, seq, d)
```

```python
import functools
import math

import jax
import jax.numpy as jnp
from jax import lax
from jax.experimental import pallas as pl
from jax.experimental.pallas import tpu as pltpu

F32 = jnp.float32
BF16 = jnp.bfloat16

D_MODEL = 1024
EPS = 1e-6
POOL_WIDTH = 512
POOL_WINDOWS = (2, 4, 8, 16)
POOL_GROUP = 128
S5_WIDTH = 512
S5_GROUP = 16
S5_GROUPS = 32
S5_STATE = 64
M2_INNER = 2048
M2_HEADDIM = 64
M2_HEADS = 32
M2_GROUPS = 8
M2_HPG = 4
M2_STATE = 128
M2_CONV = 4
M2_BC = 1024
M2_CONV_DIM = 4096
M2_PROJ = 6176
N_EXPERTS = 8
EXPERT_HIDDEN = 3584

LANES = 128
SUBLANES = 8
VMEM_LIMIT_BYTES = 56 * 1024 * 1024

TOKEN_TILE = 1024
FFN_HIDDEN_TILE = 1408
M2_PROJ_PAD = 6400
M2_PROJ_TILE = 1280
M2_SIDE_STEPS = 32
S5_BLOCK = 256
SSD_CHUNK = 128
MOE_TOKEN_TILE = 512
MOE_SLOT_TILE = 512
MOE_HIDDEN_TILE = 1792

NEG_BIG = -1e30


def _cparams(sem):
    return pltpu.CompilerParams(dimension_semantics=sem, vmem_limit_bytes=VMEM_LIMIT_BYTES)


def _modulated_norm(x, g, scale, shift):
    ms = jnp.mean(x * x, axis=-1, keepdims=True)
    return (x * lax.rsqrt(ms + EPS) * g) * (1.0 + scale) + shift


def _post_norm_residual(x, y, ng, gate):
    ms = jnp.mean(y * y, axis=-1, keepdims=True)
    return x + gate * (y * lax.rsqrt(ms + EPS) * ng)


def _ada_kernel(c_ref, w_ref, b_ref, o_ref):
    c = c_ref[...]
    a = c * jax.nn.sigmoid(c)
    o_ref[0] = jnp.dot(a.astype(BF16), w_ref[0].astype(BF16),
                       preferred_element_type=F32) + b_ref[0]


def _ada_modulation(c, ada_w, ada_b):
    depth, d, n = ada_w.shape
    b = c.shape[0]
    c_pad = jnp.zeros((SUBLANES, d), F32).at[:b].set(c)
    tn = 1024
    out = pl.pallas_call(
        _ada_kernel,
        grid=(depth, n // tn),
        in_specs=[
            pl.BlockSpec((SUBLANES, d), lambda l, j: (0, 0)),
            pl.BlockSpec((1, d, tn), lambda l, j: (l, 0, j)),
            pl.BlockSpec((1, 1, tn), lambda l, j: (l, 0, j)),
        ],
        out_specs=pl.BlockSpec((1, SUBLANES, tn), lambda l, j: (l, 0, j)),
        out_shape=jax.ShapeDtypeStruct((depth, SUBLANES, n), F32),
        compiler_params=_cparams(("arbitrary", "arbitrary")),
        name="ada_modulation",
    )(c_pad, ada_w, ada_b.reshape(depth, 1, n))
    return out[:, :b]


def _norm_mm_kernel(x_ref, g_ref, sc_ref, sh_ref, w_ref, *rest, n_side, side_steps):
    side_in = rest[:n_side]
    o_ref = rest[n_side]
    side_out = rest[n_side + 1:2 * n_side + 1]
    h_scr = rest[2 * n_side + 1]

    @pl.when(pl.program_id(1) == 0)
    def _():
        h = _modulated_norm(x_ref[...], g_ref[...], sc_ref[0], sh_ref[0])
        h_scr[...] = h.astype(BF16)

    o_ref[...] = jnp.dot(h_scr[...], w_ref[...], preferred_element_type=F32).astype(o_ref.dtype)

    step = pl.program_id(0) * pl.num_programs(1) + pl.program_id(1)
    for k in range(n_side):
        @pl.when(jnp.logical_and(step >= k * side_steps, step < (k + 1) * side_steps))
        def _(k=k):
            side_out[k][...] = side_in[k][...].astype(BF16)


def _norm_matmul(x, g, scale, shift, w, *, tm, tn, out_dtype, name, side=(), side_steps=1):
    t, d = x.shape
    n = w.shape[1]
    n_j = n // tn
    tiles_per_batch = t // scale.shape[0] // tm
    assert len(side) * side_steps <= (t // tm) * n_j

    def side_spec(k, arr):
        rows = arr.shape[0] // side_steps
        return pl.BlockSpec(
            (rows, arr.shape[1]),
            lambda i, j: (jnp.clip(i * n_j + j - k * side_steps, 0, side_steps - 1), 0))

    side_specs = [side_spec(k, a) for k, a in enumerate(side)]
    outs = pl.pallas_call(
        functools.partial(_norm_mm_kernel, n_side=len(side), side_steps=side_steps),
        grid=(t // tm, n_j),
        in_specs=[
            pl.BlockSpec((tm, d), lambda i, j: (i, 0)),
            pl.BlockSpec((1, d), lambda i, j: (0, 0)),
            pl.BlockSpec((1, 1, d), lambda i, j: (i // tiles_per_batch, 0, 0)),
            pl.BlockSpec((1, 1, d), lambda i, j: (i // tiles_per_batch, 0, 0)),
            pl.BlockSpec((d, tn), lambda i, j: (0, j)),
        ] + side_specs,
        out_specs=[pl.BlockSpec((tm, tn), lambda i, j: (i, j))] + side_specs,
        out_shape=[jax.ShapeDtypeStruct((t, n), out_dtype)]
                  + [jax.ShapeDtypeStruct(a.shape, BF16) for a in side],
        scratch_shapes=[pltpu.VMEM((tm, d), BF16)],
        compiler_params=_cparams(("arbitrary", "arbitrary")),
        name=name,
    )(x, g, scale, shift, w, *side)
    return (outs[0], outs[1:]) if side else outs[0]


def _mm_post_kernel(y_ref, w_ref, x_ref, ng_ref, gate_ref, o_ref):
    y = jnp.dot(y_ref[...], w_ref[...], preferred_element_type=F32)
    o_ref[...] = _post_norm_residual(x_ref[...], y, ng_ref[...], gate_ref[0])


def _matmul_post(y, w, x, ng, gate, *, tm, name):
    t, k = y.shape
    d = w.shape[1]
    tiles_per_batch = t // gate.shape[0] // tm
    return pl.pallas_call(
        _mm_post_kernel,
        grid=(t // tm,),
        in_specs=[
            pl.BlockSpec((tm, k), lambda i: (i, 0)),
            pl.BlockSpec((k, d), lambda i: (0, 0)),
            pl.BlockSpec((tm, d), lambda i: (i, 0)),
            pl.BlockSpec((1, d), lambda i: (0, 0)),
            pl.BlockSpec((1, 1, d), lambda i: (i // tiles_per_batch, 0, 0)),
        ],
        out_specs=pl.BlockSpec((tm, d), lambda i: (i, 0)),
        out_shape=jax.ShapeDtypeStruct((t, d), F32),
        compiler_params=_cparams(("arbitrary",)),
        name=name,
    )(y, w, x, ng, gate)


def _ffn_kernel(x_ref, g_ref, sc_ref, sh_ref, wgu_ref, wd_ref, ng_ref, gate_ref,
                o_ref, h_scr, acc_scr):
    j = pl.program_id(1)

    @pl.when(j == 0)
    def _():
        h = _modulated_norm(x_ref[...], g_ref[...], sc_ref[0], sh_ref[0])
        h_scr[...] = h.astype(BF16)
        acc_scr[...] = jnp.zeros_like(acc_scr)

    th = wd_ref.shape[0]
    gu = jnp.dot(h_scr[...], wgu_ref[...], preferred_element_type=F32)
    gt = gu[:, :th]
    up = gu[:, th:]
    act = (gt * jax.nn.sigmoid(gt) * up).astype(BF16)
    acc_scr[...] += jnp.dot(act, wd_ref[...], preferred_element_type=F32)

    @pl.when(j == pl.num_programs(1) - 1)
    def _():
        o_ref[...] = _post_norm_residual(x_ref[...], acc_scr[...], ng_ref[...], gate_ref[0])


def _ffn_sublayer(x, g, scale, shift, wg, wu, wd, ng, gate, *, tm, th):
    t, d = x.shape
    hid = wg.shape[1]
    n_j = hid // th
    tiles_per_batch = t // scale.shape[0] // tm
    bvec = pl.BlockSpec((1, 1, d), lambda i, j: (i // tiles_per_batch, 0, 0))
    wgu = jnp.concatenate([w[:, j * th:(j + 1) * th] for j in range(n_j) for w in (wg, wu)], axis=1)
    return pl.pallas_call(
        _ffn_kernel,
        grid=(t // tm, n_j),
        in_specs=[
            pl.BlockSpec((tm, d), lambda i, j: (i, 0)),
            pl.BlockSpec((1, d), lambda i, j: (0, 0)),
            bvec, bvec,
            pl.BlockSpec((d, 2 * th), lambda i, j: (0, j)),
            pl.BlockSpec((th, d), lambda i, j: (j, 0)),
            pl.BlockSpec((1, d), lambda i, j: (0, 0)),
            bvec,
        ],
        out_specs=pl.BlockSpec((tm, d), lambda i, j: (i, 0)),
        out_shape=jax.ShapeDtypeStruct((t, d), F32),
        scratch_shapes=[pltpu.VMEM((tm, d), BF16), pltpu.VMEM((tm, d), F32)],
        compiler_params=_cparams(("arbitrary", "arbitrary")),
        name="ffn_sublayer",
    )(x, g, scale, shift, wgu, wd, ng, gate)


def _router_kernel(x_ref, g_ref, sc_ref, sh_ref, wr_ref, br_ref, h_ref, meta_ref, cnt_ref):
    tm = x_ref.shape[0]

    h = _modulated_norm(x_ref[...], g_ref[...], sc_ref[0], sh_ref[0])
    h_hi = h.astype(BF16)
    h_ref[...] = h_hi
    h_lo = (h - h_hi.astype(F32)).astype(BF16)
    logits = (jnp.dot(h_hi, wr_ref[0], preferred_element_type=F32)
              + jnp.dot(h_lo, wr_ref[0], preferred_element_type=F32)
              + jnp.dot(h_hi, wr_ref[1], preferred_element_type=F32)) + br_ref[...]
    lane = lax.broadcasted_iota(jnp.int32, (tm, LANES), 1).astype(F32)
    m1 = jnp.max(logits, axis=-1, keepdims=True)
    i1 = jnp.min(jnp.where(logits == m1, lane, float(LANES)), axis=-1, keepdims=True)
    oh1 = lane == i1
    rest = jnp.where(oh1, NEG_BIG * 2.0, logits)
    m2 = jnp.max(rest, axis=-1, keepdims=True)
    i2 = jnp.min(jnp.where(rest == m2, lane, float(LANES)), axis=-1, keepdims=True)
    oh2 = lane == i2
    e = jnp.exp(m2 - m1)
    w1 = 1.0 / (1.0 + e)
    w2 = e / (1.0 + e)

    picks = jnp.where(oh1, 1.0, 0.0) + jnp.where(oh2, 1.0, 0.0)
    row = lax.broadcasted_iota(jnp.int32, (tm, tm), 0)
    col = lax.broadcasted_iota(jnp.int32, (tm, tm), 1)
    lower = jnp.where(col < row, 1.0, 0.0).astype(BF16)
    before = jnp.dot(lower, picks.astype(BF16), preferred_element_type=F32)
    tile_cnt = jnp.sum(picks, axis=0, keepdims=True)
    ea = lax.broadcasted_iota(jnp.int32, (LANES, LANES), 0)
    eb = lax.broadcasted_iota(jnp.int32, (LANES, LANES), 1)
    seg_rows = jnp.floor((tile_cnt + (SUBLANES - 1.0)) * (1.0 / SUBLANES)) * SUBLANES
    seg_off = jnp.dot(jnp.broadcast_to(seg_rows, (SUBLANES, LANES)), jnp.where(ea < eb, 1.0, 0.0),
                      preferred_element_type=F32, precision=lax.Precision.HIGHEST)[0:1, :]
    local = before + seg_off
    loc1 = jnp.sum(jnp.where(oh1, local, 0.0), axis=-1, keepdims=True)
    loc2 = jnp.sum(jnp.where(oh2, local, 0.0), axis=-1, keepdims=True)
    cnt_ref[0] = tile_cnt

    meta = jnp.where(lane == 0.0, i1, 0.0)
    meta = jnp.where(lane == 1.0, i2, meta)
    meta = jnp.where(lane == 2.0, w1, meta)
    meta = jnp.where(lane == 3.0, w2, meta)
    meta = jnp.where(lane == 4.0, loc1, meta)
    meta = jnp.where(lane == 5.0, loc2, meta)
    meta_ref[...] = meta


def _router(x, g, scale, shift, w_router, b_router, *, tm):
    t, d = x.shape
    tiles_per_batch = t // scale.shape[0] // tm
    wr = jnp.zeros((d, LANES), F32).at[:, :N_EXPERTS].set(w_router)
    wr_hi = wr.astype(BF16)
    wr = jnp.stack([wr_hi, (wr - wr_hi.astype(F32)).astype(BF16)])
    br = jnp.full((1, LANES), NEG_BIG, F32).at[0, :N_EXPERTS].set(b_router)
    bvec = pl.BlockSpec((1, 1, d), lambda i: (i // tiles_per_batch, 0, 0))
    return pl.pallas_call(
        _router_kernel,
        grid=(t // tm,),
        in_specs=[
            pl.BlockSpec((tm, d), lambda i: (i, 0)),
            pl.BlockSpec((1, d), lambda i: (0, 0)),
            bvec, bvec,
            pl.BlockSpec((2, d, LANES), lambda i: (0, 0, 0)),
            pl.BlockSpec((1, LANES), lambda i: (0, 0)),
        ],
        out_specs=[
            pl.BlockSpec((tm, d), lambda i: (i, 0)),
            pl.BlockSpec((tm, LANES), lambda i: (i, 0)),
            pl.BlockSpec((1, 1, LANES), lambda i: (i, 0, 0)),
        ],
        out_shape=[
            jax.ShapeDtypeStruct((t, d), BF16),
            jax.ShapeDtypeStruct((t, LANES), F32),
            jax.ShapeDtypeStruct((t // tm, 1, LANES), F32),
        ],
        compiler_params=_cparams(("arbitrary",)),
        name="moe_router",
    )(x, g, scale, shift, wr, br)


SEG_FIELDS = 3
SEG_PAD_ROWS = N_EXPERTS * SUBLANES


def _segment_copies(scal_ref, tile, tm, per_copy):
    base = tile * (SEG_FIELDS * N_EXPERTS)
    for e in range(N_EXPERTS):
        n = scal_ref[base + e]
        g0 = scal_ref[base + N_EXPERTS + e]
        l0 = scal_ref[base + 2 * N_EXPERTS + e]
        done = 0
        b = tm
        while b >= SUBLANES:
            take = n & b

            @pl.when(take != 0)
            def _(b=b, done=done, g0=g0, l0=l0):
                per_copy(pl.multiple_of(l0 + done, SUBLANES), pl.multiple_of(g0 + done, SUBLANES), b)

            done = done + take
            b //= 2


def _dispatch_kernel(scal_ref, tail_ref, h_ref, meta_ref, xs_ref, ws_ref,
                     seg_scr, wseg_scr, zero_scr, sem, zsem, *, n_tiles, max_unused):
    tile = pl.program_id(0)
    tm = h_ref.shape[0]
    ns = seg_scr.shape[1]
    par = tile % 2

    def segment_dmas(which, buf):
        def descriptors(local, glob, n):
            return (pltpu.make_async_copy(wseg_scr.at[buf, pl.ds(local, n), :],
                                          ws_ref.at[pl.ds(glob, n), :], sem.at[buf]),
                    pltpu.make_async_copy(seg_scr.at[buf, pl.ds(local, n), :],
                                          xs_ref.at[pl.ds(glob, n), :], sem.at[buf]))
        return descriptors

    def wait_segments(which, buf):
        descriptors = segment_dmas(which, buf)
        _segment_copies(scal_ref, which, tm, lambda l, g, n: [c.wait() for c in descriptors(l, g, n)])

    @pl.when(tile == 0)
    def _():
        zero_scr[...] = jnp.zeros_like(zero_scr)
        fills = []
        n_slot_tiles = xs_ref.shape[0] // tm
        tails = [(tail_ref[N_EXPERTS + e] > 0, pl.multiple_of(tail_ref[e], tm))
                 for e in range(N_EXPERTS)]
        tails += [(k >= tail_ref[2 * N_EXPERTS], k * tm)
                  for k in range(n_slot_tiles - max_unused, n_slot_tiles)]
        for nonempty, tail in tails:
            fills.append((nonempty,
                          pltpu.make_async_copy(zero_scr, xs_ref.at[pl.ds(tail, tm), :], zsem),
                          pltpu.make_async_copy(zero_scr.at[:, 0:LANES],
                                                ws_ref.at[pl.ds(tail, tm), :], zsem)))
        for nonempty, fill_x, fill_w in fills:
            @pl.when(nonempty)
            def _(fill_x=fill_x, fill_w=fill_w):
                fill_x.start()
                fill_w.start()
        for nonempty, fill_x, fill_w in fills:
            @pl.when(nonempty)
            def _(fill_x=fill_x, fill_w=fill_w):
                fill_x.wait()
                fill_w.wait()

    meta = meta_ref[...]
    meta_t = meta.T
    slot = lax.broadcasted_iota(jnp.int32, (ns, tm), 0).astype(F32)
    g1 = jnp.where(slot == meta_t[4:5, :], 1.0, 0.0).astype(BF16)
    g2 = jnp.where(slot == meta_t[5:6, :], 1.0, 0.0).astype(BF16)
    rows = jnp.dot(g1 + g2, h_ref[...], preferred_element_type=F32)
    lane = lax.broadcasted_iota(jnp.int32, (tm, LANES), 1)

    def split_lanes(w):
        hi = w.astype(BF16).astype(F32)
        return jnp.where(lane == 0, hi, jnp.where(lane == 1, w - hi, 0.0)).astype(BF16)

    w_split = jnp.concatenate([split_lanes(meta[:, 2:3]), split_lanes(meta[:, 3:4])], axis=0)
    w_slot = jnp.dot(jnp.concatenate([g1, g2], axis=1), w_split, preferred_element_type=F32)
    weight = w_slot[:, 0:1] + w_slot[:, 1:2]

    @pl.when(tile >= 2)
    def _():
        wait_segments(tile - 2, par)

    seg_scr[par] = rows
    wseg_scr[par] = jnp.broadcast_to(weight, (ns, LANES))
    descriptors = segment_dmas(tile, par)
    _segment_copies(scal_ref, tile, tm, lambda l, g, n: [c.start() for c in descriptors(l, g, n)])

    @pl.when(tile == n_tiles - 1)
    def _():
        wait_segments(tile, par)
        if n_tiles > 1:
            wait_segments(tile - 1, 1 - par)


def _dispatch(h, meta, seg_table, tail_table, n_slots, *, tm):
    t, d = h.shape
    grid_spec = pltpu.PrefetchScalarGridSpec(
        num_scalar_prefetch=2,
        grid=(t // tm,),
        in_specs=[
            pl.BlockSpec((tm, d), lambda i, sc, tl: (i, 0)),
            pl.BlockSpec((tm, LANES), lambda i, sc, tl: (i, 0)),
        ],
        out_specs=[pl.BlockSpec(memory_space=pl.ANY), pl.BlockSpec(memory_space=pl.ANY)],
        scratch_shapes=[
            pltpu.VMEM((2, 2 * tm + SEG_PAD_ROWS, d), F32),
            pltpu.VMEM((2, 2 * tm + SEG_PAD_ROWS, LANES), F32),
            pltpu.VMEM((tm, d), F32),
            pltpu.SemaphoreType.DMA((2,)), pltpu.SemaphoreType.DMA(()),
        ],
    )
    return pl.pallas_call(
        functools.partial(_dispatch_kernel, n_tiles=t // tm,
                          max_unused=n_slots // tm - 2 * t // tm),
        grid_spec=grid_spec,
        out_shape=[jax.ShapeDtypeStruct((n_slots, d), F32),
                   jax.ShapeDtypeStruct((n_slots, LANES), F32)],
        compiler_params=_cparams(("arbitrary",)),
        name="moe_dispatch",
    )(seg_table, tail_table, h, meta)


def _expert_kernel(te_ref, nu_ref, xs_ref, ws_ref, wg_ref, wu_ref, wd_ref, ys_ref,
                   xb_scr, acc_scr):
    i = pl.program_id(0)
    j = pl.program_id(1)
    last = pl.num_programs(1) - 1
    used = i < nu_ref[0]

    @pl.when(used)
    def _():
        @pl.when(j == 0)
        def _():
            xb_scr[...] = xs_ref[...].astype(BF16)
            acc_scr[...] = jnp.zeros_like(acc_scr)

        x = xb_scr[...]
        gt = jnp.dot(x, wg_ref[0], preferred_element_type=F32)
        up = jnp.dot(x, wu_ref[0], preferred_element_type=F32)
        act = (gt * jax.nn.sigmoid(gt) * up).astype(BF16)
        acc_scr[...] += jnp.dot(act, wd_ref[0], preferred_element_type=F32)

        @pl.when(j == last)
        def _():
            ys_ref[...] = (acc_scr[...] * ws_ref[:, 0:1]).astype(ys_ref.dtype)

    @pl.when(jnp.logical_and(jnp.logical_not(used), j == last))
    def _():
        ys_ref[...] = jnp.zeros_like(ys_ref)


def _experts(xs, ws, tile_expert, n_used, wg, wu, wd, *, tm, th):
    n_slots, d = xs.shape
    hid = wg.shape[2]
    nt = n_slots // tm

    def hidden_step(i, j, nu):
        return jnp.where(i < nu[0], j, 0)

    def slot_tile(i, nu):
        return jnp.where(i < nu[0], i, 0)

    grid_spec = pltpu.PrefetchScalarGridSpec(
        num_scalar_prefetch=2,
        grid=(nt, hid // th),
        in_specs=[
            pl.BlockSpec((tm, d), lambda i, j, te, nu: (slot_tile(i, nu), 0)),
            pl.BlockSpec((tm, LANES), lambda i, j, te, nu: (slot_tile(i, nu), 0)),
            pl.BlockSpec((1, d, th), lambda i, j, te, nu: (te[i], 0, hidden_step(i, j, nu))),
            pl.BlockSpec((1, d, th), lambda i, j, te, nu: (te[i], 0, hidden_step(i, j, nu))),
            pl.BlockSpec((1, th, d), lambda i, j, te, nu: (te[i], hidden_step(i, j, nu), 0)),
        ],
        out_specs=pl.BlockSpec((tm, d), lambda i, j, te, nu: (i, 0)),
        scratch_shapes=[pltpu.VMEM((tm, d), BF16), pltpu.VMEM((tm, d), F32)],
    )
    return pl.pallas_call(
        _expert_kernel,
        grid_spec=grid_spec,
        out_shape=jax.ShapeDtypeStruct((n_slots, d), F32),
        compiler_params=_cparams(("arbitrary", "arbitrary")),
        name="moe_experts",
    )(tile_expert, n_used, xs, ws, wg, wu, wd)


def _combine_kernel(scal_ref, ys_ref, meta_ref, x_ref, ng_ref, gate_ref, o_ref, buf, sem,
                    *, n_tiles):
    tile = pl.program_id(0)
    tm = x_ref.shape[0]
    ns = buf.shape[1]
    par = tile % 2

    def fetch(which, slot, wait):
        def copy_in(local, glob, n):
            dma = pltpu.make_async_copy(ys_ref.at[pl.ds(glob, n), :],
                                        buf.at[slot, pl.ds(local, n), :], sem.at[slot])
            dma.wait() if wait else dma.start()
        _segment_copies(scal_ref, which, tm, copy_in)

    @pl.when(tile == 0)
    def _():
        buf[...] = jnp.zeros_like(buf)
        fetch(tile, par, False)

    @pl.when(tile + 1 < n_tiles)
    def _():
        fetch(tile + 1, 1 - par, False)

    fetch(tile, par, True)
    meta = meta_ref[...]
    slot_id = lax.broadcasted_iota(jnp.int32, (tm, ns), 1).astype(F32)
    pick = jnp.where((slot_id == meta[:, 4:5]) | (slot_id == meta[:, 5:6]), 1.0, 0.0).astype(BF16)
    y = jnp.dot(pick, buf[par].astype(BF16), preferred_element_type=F32)
    o_ref[...] = _post_norm_residual(x_ref[...], y, ng_ref[...], gate_ref[0])


def _combine(ys, meta, seg_table, x, ng, gate, *, tm):
    t, d = x.shape
    tiles_per_batch = t // gate.shape[0] // tm
    grid_spec = pltpu.PrefetchScalarGridSpec(
        num_scalar_prefetch=1,
        grid=(t // tm,),
        in_specs=[
            pl.BlockSpec(memory_space=pl.ANY),
            pl.BlockSpec((tm, LANES), lambda i, sc: (i, 0)),
            pl.BlockSpec((tm, d), lambda i, sc: (i, 0)),
            pl.BlockSpec((1, d), lambda i, sc: (0, 0)),
            pl.BlockSpec((1, 1, d), lambda i, sc: (i // tiles_per_batch, 0, 0)),
        ],
        out_specs=pl.BlockSpec((tm, d), lambda i, sc: (i, 0)),
        scratch_shapes=[pltpu.VMEM((2, 2 * tm + SEG_PAD_ROWS, d), F32),
                        pltpu.SemaphoreType.DMA((2,))],
    )
    return pl.pallas_call(
        functools.partial(_combine_kernel, n_tiles=t // tm),
        grid_spec=grid_spec,
        out_shape=jax.ShapeDtypeStruct((t, d), F32),
        compiler_params=_cparams(("arbitrary",)),
        name="moe_combine",
    )(seg_table, ys, meta, x, ng, gate)


def _moe_sublayer(x, g, scale, shift, w_router, b_router, wg, wu, wd, ng, gate):
    t = x.shape[0]
    tm, tm_e, th = MOE_TOKEN_TILE, MOE_SLOT_TILE, MOE_HIDDEN_TILE
    h, meta, counts = _router(x, g, scale, shift, w_router, b_router, tm=tm)

    assert tm == tm_e
    tile_cnt = counts[:, 0, :N_EXPERTS].astype(jnp.int32)
    seg_rows = ((tile_cnt + SUBLANES - 1) // SUBLANES) * SUBLANES
    cnt = jnp.sum(seg_rows, axis=0)
    padded = ((cnt + tm_e - 1) // tm_e) * tm_e
    ends = jnp.cumsum(padded)
    offs = ends - padded
    glob_start = offs[None, :] + jnp.cumsum(seg_rows, axis=0) - seg_rows
    local_start = jnp.cumsum(seg_rows, axis=1) - seg_rows
    seg_table = jnp.concatenate([seg_rows, glob_start, local_start], axis=1).reshape(-1)
    tail_table = jnp.concatenate([ends - tm_e, padded, ends[-1:] // tm_e]).astype(jnp.int32)
    n_tok_tiles = t // tm
    nt = -(-(2 * t + n_tok_tiles * SEG_PAD_ROWS) // tm_e) + N_EXPERTS
    starts = jnp.arange(nt, dtype=jnp.int32) * tm_e
    tile_expert = jnp.minimum(jnp.sum(starts[:, None] >= ends[None, :], axis=1), N_EXPERTS - 1)
    tile_expert = tile_expert.astype(jnp.int32)
    n_used = (ends[-1:] // tm_e).astype(jnp.int32)

    xs, ws = _dispatch(h, meta, seg_table, tail_table, nt * tm_e, tm=tm)
    ys = _experts(xs, ws, tile_expert, n_used, wg, wu, wd, tm=tm_e, th=th)
    return _combine(ys, meta, seg_table, x, ng, gate, tm=tm)


def _s5_disc_kernel(lam_re_ref, lam_im_ref, step_ref, pow_ref, keep_ref, b_re_ref, b_im_ref,
                    apr_ref, api_ref, bbr_ref, bbi_ref):
    lr = jnp.minimum(lam_re_ref[...], -1e-4)
    li = lam_im_ref[...]
    step = step_ref[...]
    m = pow_ref[...]
    mag = jnp.exp(lr * step * m) * keep_ref[...]
    ang = li * step * m
    apr_ref[...] = mag * jnp.cos(ang)
    api_ref[...] = mag * jnp.sin(ang)
    mag1 = jnp.exp(lr * step)
    ar = mag1 * jnp.cos(li * step)
    ai = mag1 * jnp.sin(li * step)
    inv = 1.0 / (lr * lr + li * li)
    fr = ((ar - 1.0) * lr + ai * li) * inv
    fi = (ai * lr - (ar - 1.0) * li) * inv
    br = b_re_ref[...]
    bi = b_im_ref[...]
    bbr_ref[...] = fr * br - fi * bi
    bbi_ref[...] = fr * bi + fi * br


S5_LOCAL_STEPS = 3
S5_ROW_POW0 = S5_LOCAL_STEPS * SUBLANES
S5_TILE_POW0 = S5_ROW_POW0 + SUBLANES


def _s5_power_rows(n_tile_steps):
    exps, keep = [], []
    for k in range(S5_LOCAL_STEPS):
        for tau in range(SUBLANES):
            exps.append(float(1 << k))
            keep.append(1.0 if tau >= (1 << k) else 0.0)
    for tau in range(SUBLANES):
        exps.append(float(tau + 1))
        keep.append(1.0)
    for k in range(n_tile_steps):
        exps.append(float(SUBLANES << k))
        keep.append(1.0)
    while len(exps) % SUBLANES:
        exps.append(0.0)
        keep.append(0.0)
    return exps, keep


def _s5_discretise(lam_re, lam_im, log_step, b_re, b_im, n_tile_steps):
    gp = S5_GROUPS * S5_STATE
    step = jnp.repeat(jnp.exp(log_step), S5_STATE).reshape(1, gp)
    exps, keep = _s5_power_rows(n_tile_steps)
    rows = len(exps)
    b_re_t = jnp.transpose(b_re, (2, 0, 1)).reshape(S5_GROUP, gp)
    b_im_t = jnp.transpose(b_im, (2, 0, 1)).reshape(S5_GROUP, gp)
    return pl.pallas_call(
        _s5_disc_kernel,
        out_shape=[jax.ShapeDtypeStruct((rows, gp), F32), jax.ShapeDtypeStruct((rows, gp), F32),
                   jax.ShapeDtypeStruct((S5_GROUP, gp), F32), jax.ShapeDtypeStruct((S5_GROUP, gp), F32)],
        name="s5_discretise",
    )(lam_re.reshape(1, gp), lam_im.reshape(1, gp), step,
      jnp.asarray(exps, F32).reshape(rows, 1), jnp.asarray(keep, F32).reshape(rows, 1),
      b_re_t, b_im_t)


POOL_HALO = 16
S5_HALF = 256
S5_HALF_STATES = 1024


def _complex_axpy(xr, xi, cr, ci, sr, si):
    return xr + (cr * sr - ci * si), xi + (cr * si + ci * sr)


def _mix0_kernel(x_ref, g_ref, sc_ref, sh_ref, win_ref, pw_ref, ps_ref, bw_ref, cw_ref, apr_ref,
                 api_ref, dsk_ref, wglu_ref, wout_ref, ng_ref, gate_ref, side_ref, o_ref,
                 side_out_ref, *scratch):
    side_out_ref[...] = side_ref[...].astype(BF16)
    halo_scr, _, sr_scr, si_scr, _, _, _ = scratch

    @pl.when(pl.program_id(0) == 0)
    def _():
        halo_scr[...] = jnp.zeros_like(halo_scr)
        sr_scr[...] = jnp.zeros_like(sr_scr)
        si_scr[...] = jnp.zeros_like(si_scr)

    for b in range(x_ref.shape[0]):
        one = pl.ds(b, 1)
        _mix0_block(x_ref.at[b], g_ref, sc_ref.at[one], sh_ref.at[one], win_ref, pw_ref, ps_ref,
                    bw_ref, cw_ref, apr_ref, api_ref, dsk_ref, wglu_ref, wout_ref, ng_ref,
                    gate_ref.at[one], o_ref.at[b], *[s.at[b] for s in scratch])


def _mix0_block(x_ref, g_ref, sc_ref, sh_ref, win_ref, pw_ref, ps_ref, bw_ref, cw_ref, apr_ref,
                api_ref, dsk_ref, wglu_ref, wout_ref, ng_ref, gate_ref, o_ref,
                halo_scr, ext_scr, sr_scr, si_scr, xr_scr, xi_scr, ycat_scr):
    tt = x_ref.shape[0]
    n_states = sr_scr.shape[1]
    blk = pl.program_id(0)

    h = _modulated_norm(x_ref[...], g_ref[...], sc_ref[0], sh_ref[0]).astype(BF16)
    u = jnp.dot(h, win_ref[...], preferred_element_type=F32)

    up = u[:, :POOL_WIDTH]
    ext_scr[0:POOL_HALO, :] = halo_scr[...]
    ext_scr[POOL_HALO:, :] = up
    halo_scr[...] = up[tt - POOL_HALO:, :]
    ext = ext_scr[...].astype(BF16)
    row = lax.broadcasted_iota(jnp.int32, (tt, tt + POOL_HALO), 0)
    col = lax.broadcasted_iota(jnp.int32, (tt, tt + POOL_HALO), 1)
    lag = row + POOL_HALO - col
    t_glob = (blk * tt + row + 1).astype(F32)
    pooled_out = []
    for gi, win in enumerate(POOL_WINDOWS):
        inv_count = 1.0 / jnp.minimum(t_glob, float(win))
        band = jnp.where((lag >= 0) & (lag < win), inv_count, 0.0) - jnp.where(lag == 0, 1.0, 0.0)
        pooled = jnp.dot(band.astype(BF16), ext[:, gi * POOL_GROUP:(gi + 1) * POOL_GROUP],
                         preferred_element_type=F32)
        pooled_out.append(jnp.dot(pooled.astype(BF16), pw_ref[gi], preferred_element_type=F32))
    y_pool = jnp.concatenate(pooled_out, axis=-1) * ps_ref[...]
    ycat_scr[:, :POOL_WIDTH] = y_pool.astype(BF16)

    us = u[:, POOL_WIDTH:]
    usb = us.astype(BF16)
    bu = [jnp.dot(usb[:, hf * S5_HALF:(hf + 1) * S5_HALF], bw_ref[hf], preferred_element_type=F32)
          for hf in range(2)]
    nt = tt // SUBLANES
    xr = jnp.concatenate([b[:, :S5_HALF_STATES] for b in bu], axis=-1).reshape(nt, SUBLANES, n_states)
    xi = jnp.concatenate([b[:, S5_HALF_STATES:] for b in bu], axis=-1).reshape(nt, SUBLANES, n_states)
    for k in range(S5_LOCAL_STEPS):
        cr = apr_ref[k * SUBLANES:(k + 1) * SUBLANES, :][None]
        ci = api_ref[k * SUBLANES:(k + 1) * SUBLANES, :][None]
        xr, xi = _complex_axpy(xr, xi, cr, ci, pltpu.roll(xr, 1 << k, 1), pltpu.roll(xi, 1 << k, 1))

    xr2 = xr.reshape(tt, n_states)
    xi2 = xi.reshape(tt, n_states)
    n_cb = n_states // LANES
    for cb in range(n_cb):
        xr_scr[cb] = xr2[:, cb * LANES:(cb + 1) * LANES]
        xi_scr[cb] = xi2[:, cb * LANES:(cb + 1) * LANES]
    tile_end = pl.ds(SUBLANES - 1, nt, stride=SUBLANES)
    er = jnp.concatenate([xr_scr[cb, tile_end, :] for cb in range(n_cb)], axis=-1)
    ei = jnp.concatenate([xi_scr[cb, tile_end, :] for cb in range(n_cb)], axis=-1)
    prev_r = sr_scr[...]
    prev_i = si_scr[...]
    tile_row = lax.broadcasted_iota(jnp.int32, (nt, n_states), 0)
    a8r = apr_ref[S5_TILE_POW0:S5_TILE_POW0 + 1, :]
    a8i = api_ref[S5_TILE_POW0:S5_TILE_POW0 + 1, :]
    er = er + jnp.where(tile_row == 0, a8r * prev_r - a8i * prev_i, 0.0)
    ei = ei + jnp.where(tile_row == 0, a8r * prev_i + a8i * prev_r, 0.0)
    k = 0
    while (1 << k) < nt:
        cr = apr_ref[S5_TILE_POW0 + k:S5_TILE_POW0 + k + 1, :]
        ci = api_ref[S5_TILE_POW0 + k:S5_TILE_POW0 + k + 1, :]
        pr = jnp.where(tile_row < (1 << k), 0.0, pltpu.roll(er, 1 << k, 0))
        pi = jnp.where(tile_row < (1 << k), 0.0, pltpu.roll(ei, 1 << k, 0))
        er, ei = _complex_axpy(er, ei, cr, ci, pr, pi)
        k += 1
    sr_scr[...] = er[nt - 1:nt, :]
    si_scr[...] = ei[nt - 1:nt, :]
    in_r = jnp.where(tile_row == 0, prev_r, pltpu.roll(er, 1, 0))
    in_i = jnp.where(tile_row == 0, prev_i, pltpu.roll(ei, 1, 0))

    rep_row = lax.broadcasted_iota(jnp.int32, (tt, nt), 0)
    rep_col = lax.broadcasted_iota(jnp.int32, (tt, nt), 1)
    rep = jnp.where(rep_row // SUBLANES == rep_col, 1.0, 0.0).astype(BF16)
    entering = jnp.concatenate([in_r, in_i], axis=-1)
    ent_hi = entering.astype(BF16)
    ent_lo = (entering - ent_hi.astype(F32)).astype(BF16)
    ent = (jnp.dot(rep, ent_hi, preferred_element_type=F32)
           + jnp.dot(rep, ent_lo, preferred_element_type=F32))
    cbr = ent[:, :n_states].reshape(nt, SUBLANES, n_states)
    cbi = ent[:, n_states:].reshape(nt, SUBLANES, n_states)
    pwr = apr_ref[S5_ROW_POW0:S5_ROW_POW0 + SUBLANES, :][None]
    pwi = api_ref[S5_ROW_POW0:S5_ROW_POW0 + SUBLANES, :][None]
    xr, xi = _complex_axpy(xr, xi, pwr, pwi, cbr, cbi)
    xr2 = xr.reshape(tt, n_states).astype(BF16)
    xi2 = xi.reshape(tt, n_states).astype(BF16)

    ys = []
    for hf in range(2):
        sl = slice(hf * S5_HALF_STATES, (hf + 1) * S5_HALF_STATES)
        xcat = jnp.concatenate([xr2[:, sl], xi2[:, sl]], axis=-1)
        ys.append(jnp.dot(xcat, cw_ref[hf], preferred_element_type=F32))
    y = jnp.concatenate(ys, axis=-1) + dsk_ref[...] * us
    y = jax.nn.gelu(y)
    glu = jnp.dot(y.astype(BF16), wglu_ref[...], preferred_element_type=F32)
    ycat_scr[:, POOL_WIDTH:] = (y * jax.nn.sigmoid(glu)).astype(BF16)

    y_out = jnp.dot(ycat_scr[...], wout_ref[...], preferred_element_type=F32)
    o_ref[...] = _post_norm_residual(x_ref[...], y_out, ng_ref[...], gate_ref[0])


def _side_spec(arr, n_steps, step_of):
    assert arr.shape[0] % n_steps == 0
    return pl.BlockSpec((arr.shape[0] // n_steps, arr.shape[1]), lambda *idx: (step_of(*idx), 0))


def _pool_s5_mixer(x, bsz, g, scale, shift, w_in, pool_w, pool_scale, lam_re, lam_im, log_step,
                   b_re, b_im, c_re, c_im, d_skip, w_glu, w_out, ng, gate, side, *, tt):
    t = x.shape[0]
    seq = t // bsz
    gp = S5_GROUPS * S5_STATE
    apr, api, bbr, bbi = _s5_discretise(lam_re, lam_im, log_step, b_re, b_im,
                                        int(math.log2(tt // SUBLANES)))
    n_pow = apr.shape[0]

    gh = S5_GROUPS // 2
    eye = jnp.eye(gh, dtype=F32)

    def in_map(bb):
        bb = bb.reshape(S5_GROUP, 2, gh, S5_STATE)
        return jnp.einsum("hxgp,gk->xghkp", bb, eye).reshape(2, gh * S5_GROUP, gh * S5_STATE)

    bw = jnp.concatenate([in_map(bbr), in_map(bbi)], axis=-1).astype(BF16)

    def out_map(cc):
        cc = cc.reshape(2, gh, S5_GROUP, S5_STATE)
        return jnp.einsum("xghp,gk->xgpkh", cc, eye).reshape(2, gh * S5_STATE, gh * S5_GROUP)

    cw = jnp.concatenate([out_map(c_re), -out_map(c_im)], axis=1).astype(BF16)

    const2 = lambda i: (0, 0)
    const3 = lambda i: (0, 0, 0)
    nblk = seq // tt
    side_spec = _side_spec(side, nblk, lambda i: i)
    bvec = pl.BlockSpec((bsz, 1, D_MODEL), const3)
    seq_block = pl.BlockSpec((bsz, tt, D_MODEL), lambda i: (0, i, 0))
    outs = pl.pallas_call(
        _mix0_kernel,
        grid=(nblk,),
        in_specs=[
            seq_block,
            pl.BlockSpec((1, D_MODEL), const2),
            bvec, bvec,
            pl.BlockSpec((D_MODEL, D_MODEL), const2),
            pl.BlockSpec((len(POOL_WINDOWS), POOL_GROUP, POOL_GROUP), const3),
            pl.BlockSpec((1, POOL_WIDTH), const2),
            pl.BlockSpec((2, S5_HALF, 2 * S5_HALF_STATES), const3),
            pl.BlockSpec((2, 2 * S5_HALF_STATES, S5_HALF), const3),
            pl.BlockSpec((n_pow, gp), const2),
            pl.BlockSpec((n_pow, gp), const2),
            pl.BlockSpec((1, S5_WIDTH), const2),
            pl.BlockSpec((S5_WIDTH, S5_WIDTH), const2),
            pl.BlockSpec((D_MODEL, D_MODEL), const2),
            pl.BlockSpec((1, D_MODEL), const2),
            bvec,
            side_spec,
        ],
        out_specs=[seq_block, side_spec],
        out_shape=[jax.ShapeDtypeStruct((bsz, seq, D_MODEL), F32),
                   jax.ShapeDtypeStruct(side.shape, BF16)],
        scratch_shapes=[
            pltpu.VMEM((bsz, POOL_HALO, POOL_WIDTH), F32),
            pltpu.VMEM((bsz, tt + POOL_HALO, POOL_WIDTH), F32),
            pltpu.VMEM((bsz, 1, gp), F32), pltpu.VMEM((bsz, 1, gp), F32),
            pltpu.VMEM((bsz, gp // LANES, tt, LANES), F32),
            pltpu.VMEM((bsz, gp // LANES, tt, LANES), F32),
            pltpu.VMEM((bsz, tt, D_MODEL), BF16),
        ],
        compiler_params=_cparams(("arbitrary",)),
        name="pool_s5_mixer",
    )(x.reshape(bsz, seq, D_MODEL), g, scale, shift, w_in, pool_w.astype(BF16),
      pool_scale.reshape(1, POOL_WIDTH), bw, cw, apr, api, d_skip.reshape(1, S5_WIDTH),
      w_glu.astype(BF16), w_out, ng, gate, side)
    return outs[0].reshape(t, D_MODEL), outs[1]


CONV_HALO = SUBLANES
M2_GROUP_WIDTH = M2_HPG * M2_HEADDIM


def _causal_conv_silu(x_ref, halo_scr, full_scr, w_ref, b_ref, col0):
    lc = x_ref.shape[0]
    width = x_ref.shape[1]
    x = x_ref[...].astype(F32)
    full_scr[0:CONV_HALO, :] = halo_scr[...]
    full_scr[CONV_HALO:, :] = x
    halo_scr[...] = x[lc - CONV_HALO:, :]
    cols = slice(col0, col0 + width)
    acc = b_ref[:, cols] + w_ref[M2_CONV - 1:M2_CONV, cols] * x
    for k in range(M2_CONV - 1):
        off = CONV_HALO - (M2_CONV - 1) + k
        acc = acc + w_ref[k:k + 1, cols] * full_scr[off:off + lc, :]
    return acc * jax.nn.sigmoid(acc)


def _split_dot(v, e):
    hi = v.astype(BF16)
    lo = (v - hi.astype(F32)).astype(BF16)
    return (jnp.dot(hi, e, preferred_element_type=F32) + jnp.dot(lo, e, preferred_element_type=F32))


def _ssd_kernel(z_ref, xs_ref, bc_ref, dtr_ref, cw_ref, cb_ref, dtb_ref, alog_ref, dx_ref, ng_ref,
                e_ref, side_ref, o_ref, side_out_ref, *scratch):
    side_out_ref[...] = side_ref[...].astype(BF16)
    halo_x, halo_bc, _, _, st_scr, _ = scratch

    @pl.when(pl.program_id(0) == 0)
    def _():
        halo_x[...] = jnp.zeros_like(halo_x)
        halo_bc[...] = jnp.zeros_like(halo_bc)
        st_scr[...] = jnp.zeros_like(st_scr)

    for b in range(z_ref.shape[0]):
        _ssd_chunk(z_ref.at[b], xs_ref.at[b], bc_ref.at[b], dtr_ref.at[b], cw_ref, cb_ref, dtb_ref,
                   alog_ref, dx_ref, ng_ref, e_ref, o_ref.at[b], *[s.at[b] for s in scratch])


def _ssd_chunk(z_ref, xs_ref, bc_ref, dtr_ref, cw_ref, cb_ref, dtb_ref, alog_ref, dx_ref, ng_ref,
               e_ref, o_ref, halo_x, halo_bc, full_x, full_bc, st_scr, y_scr):
    lc = z_ref.shape[0]

    xs = _causal_conv_silu(xs_ref, halo_x, full_x, cw_ref, cb_ref, 0)
    bc = _causal_conv_silu(bc_ref, halo_bc, full_bc, cw_ref, cb_ref, M2_INNER)

    lane = lax.broadcasted_iota(jnp.int32, (1, LANES), 1)
    x_dt = dtr_ref[...].astype(F32) + dtb_ref[...]
    dt = jnp.maximum(x_dt, 0.0) + jnp.log(1.0 + jnp.exp(-jnp.abs(x_dt)))
    a = jnp.where(lane < M2_HEADS, -jnp.exp(alog_ref[...]) * math.log2(math.e), 0.0)
    da = dt * a
    row = lax.broadcasted_iota(jnp.int32, (lc, lc), 0)
    col = lax.broadcasted_iota(jnp.int32, (lc, lc), 1)
    causal = col <= row
    cs = jnp.dot(jnp.where(causal, 1.0, 0.0), da, preferred_element_type=F32,
                 precision=lax.Precision.HIGHEST)
    cs_last = cs[lc - 1:lc, :]
    ecs = jnp.exp2(cs)
    w_in = dt * jnp.exp2(cs_last - cs)
    cs_t = cs.T
    dt_t = dt.T
    e = e_ref[...]
    ecs_x = _split_dot(ecs, e)
    w_x = _split_dot(w_in, e)
    xsb = xs.astype(BF16)
    xw = (xs * w_x).astype(BF16)

    for g in range(M2_GROUPS):
        bm = bc[:, g * M2_STATE:(g + 1) * M2_STATE].astype(BF16)
        cm = bc[:, M2_BC + g * M2_STATE:M2_BC + (g + 1) * M2_STATE].astype(BF16)
        cbm = lax.dot_general(cm, bm, (((1,), (1,)), ((), ())), preferred_element_type=F32)
        gcols = slice(g * M2_GROUP_WIDTH, (g + 1) * M2_GROUP_WIDTH)
        st = st_scr[g]
        y_off = jnp.dot(cm, st.astype(BF16), preferred_element_type=F32) * ecs_x[:, gcols]
        y_heads = []
        for jj in range(M2_HPG):
            j = g * M2_HPG + jj
            seg = cs[:, j:j + 1] - cs_t[j:j + 1, :]
            dec = jnp.exp2(jnp.where(causal, seg, NEG_BIG))
            m = (cbm * dec * dt_t[j:j + 1, :]).astype(BF16)
            y_heads.append(jnp.dot(m, xsb[:, j * M2_HEADDIM:(j + 1) * M2_HEADDIM],
                                   preferred_element_type=F32))
        y_g = jnp.concatenate(y_heads, axis=-1) + y_off
        st_scr[g] = st * ecs_x[lc - 1:lc, gcols] + lax.dot_general(
            bm, xw[:, gcols], (((0,), (0,)), ((), ())), preferred_element_type=F32)
        zg = z_ref[:, gcols].astype(F32)
        y_g = (y_g + dx_ref[:, gcols] * xs[:, gcols]) * (zg * jax.nn.sigmoid(zg))
        y_scr[:, gcols] = y_g

    y = y_scr[...]
    ms = jnp.mean(y * y, axis=-1, keepdims=True)
    o_ref[...] = (y * lax.rsqrt(ms + EPS) * ng_ref[...]).astype(o_ref.dtype)


def _ssd_mixer(proj, bsz, conv_w, conv_b, dt_bias, a_log, d_skip, norm_g, side, *, lc):
    t = proj.shape[0]
    seq = t // bsz
    nblk = seq // lc
    half = M2_INNER
    pad = LANES - M2_HEADS
    dtb = jnp.pad(dt_bias, (0, pad)).reshape(1, LANES)
    alog = jnp.pad(a_log, (0, pad)).reshape(1, LANES)
    dx = jnp.repeat(d_skip, M2_HEADDIM).reshape(1, M2_INNER)
    heads = jnp.arange(LANES, dtype=jnp.int32)[:, None]
    chans = jnp.arange(M2_INNER, dtype=jnp.int32)[None, :] // M2_HEADDIM
    expand = (heads == chans).astype(BF16)
    const2 = lambda i: (0, 0)
    side_spec = _side_spec(side, nblk, lambda i: i)
    proj3 = proj.reshape(bsz, seq, proj.shape[1])
    outs = pl.pallas_call(
        _ssd_kernel,
        grid=(nblk,),
        in_specs=[
            pl.BlockSpec((bsz, lc, half), lambda i: (0, i, 0)),
            pl.BlockSpec((bsz, lc, half), lambda i: (0, i, 1)),
            pl.BlockSpec((bsz, lc, half), lambda i: (0, i, 2)),
            pl.BlockSpec((bsz, lc, LANES), lambda i: (0, i, 3 * half // LANES)),
            pl.BlockSpec((M2_CONV, M2_CONV_DIM), const2),
            pl.BlockSpec((1, M2_CONV_DIM), const2),
            pl.BlockSpec((1, LANES), const2),
            pl.BlockSpec((1, LANES), const2),
            pl.BlockSpec((1, M2_INNER), const2),
            pl.BlockSpec((1, M2_INNER), const2),
            pl.BlockSpec((LANES, M2_INNER), const2),
            side_spec,
        ],
        out_specs=[pl.BlockSpec((bsz, lc, M2_INNER), lambda i: (0, i, 0)), side_spec],
        out_shape=[jax.ShapeDtypeStruct((bsz, seq, M2_INNER), BF16),
                   jax.ShapeDtypeStruct(side.shape, BF16)],
        scratch_shapes=[
            pltpu.VMEM((bsz, CONV_HALO, half), F32), pltpu.VMEM((bsz, CONV_HALO, half), F32),
            pltpu.VMEM((bsz, lc + CONV_HALO, half), F32),
            pltpu.VMEM((bsz, lc + CONV_HALO, half), F32),
            pltpu.VMEM((bsz, M2_GROUPS, M2_STATE, M2_GROUP_WIDTH), F32),
            pltpu.VMEM((bsz, lc, M2_INNER), F32),
        ],
        compiler_params=_cparams(("arbitrary",)),
        name="ssd_mixer",
    )(proj3, proj3, proj3, proj3, conv_w, conv_b.reshape(1, M2_CONV_DIM), dtb, alog, dx,
      norm_g.reshape(1, M2_INNER), expand, side)
    return outs[0].reshape(t, M2_INNER), outs[1]


def kernel(x, c, ada_w, ada_b, norm_g, mix_w_in, pool_w, pool_scale, s5_lam_re, s5_lam_im, s5_log_step, s5_b_re, s5_b_im, s5_c_re, s5_c_im, s5_d, s5_w_glu, mix_w_out, ffn_w_gate, ffn_w_up, ffn_w_down, m2_w_in, m2_conv_w, m2_conv_b, m2_dt_bias, m2_a_log, m2_d, m2_norm_g, m2_w_out, moe_w_router, moe_b_router, moe_w_gate, moe_w_up, moe_w_down):
    bsz, seq, d = x.shape
    t = bsz * seq
    xt = x.reshape(t, d)

    mod = _ada_modulation(c, ada_w, ada_b)

    def mod_vecs(layer):
        return [mod[layer, :, k * d:(k + 1) * d].reshape(bsz, 1, d) for k in range(6)]

    def gvec(layer, k):
        return norm_g[layer, k].reshape(1, d)

    sh_m, sc_m, g_m, sh_f, sc_f, g_f = mod_vecs(0)
    ne, dh = N_EXPERTS, EXPERT_HIDDEN
    xt, wu_b = _pool_s5_mixer(xt, bsz, gvec(0, 0), sc_m, sh_m, mix_w_in[0].astype(BF16),
                              pool_w[0], pool_scale[0], s5_lam_re[0], s5_lam_im[0],
                              s5_log_step[0], s5_b_re[0], s5_b_im[0], s5_c_re[0], s5_c_im[0],
                              s5_d[0], s5_w_glu[0], mix_w_out[0].astype(BF16), gvec(0, 1), g_m,
                              moe_w_up[0].reshape(ne * d, dh), tt=S5_BLOCK)
    xt = _ffn_sublayer(xt, gvec(0, 2), sc_f, sh_f, ffn_w_gate[0].astype(BF16),
                       ffn_w_up[0].astype(BF16), ffn_w_down[0].astype(BF16), gvec(0, 3), g_f,
                       tm=TOKEN_TILE, th=FFN_HIDDEN_TILE)

    sh_m, sc_m, g_m, sh_f, sc_f, g_f = mod_vecs(1)
    w_in = jnp.zeros((d, M2_PROJ_PAD), BF16).at[:, :M2_PROJ].set(m2_w_in[0].astype(BF16))
    proj, (wg_b,) = _norm_matmul(
        xt, gvec(1, 0), sc_m, sh_m, w_in, tm=TOKEN_TILE, tn=M2_PROJ_TILE, out_dtype=BF16,
        name="m2_in_proj", side=(moe_w_gate[0].reshape(ne * d, dh),), side_steps=M2_SIDE_STEPS)
    y, wd_b = _ssd_mixer(proj, bsz, m2_conv_w[0], m2_conv_b[0], m2_dt_bias[0], m2_a_log[0],
                         m2_d[0], m2_norm_g[0], moe_w_down[0].reshape(ne * dh, d), lc=SSD_CHUNK)
    xt = _matmul_post(y, m2_w_out[0].astype(BF16), xt, gvec(1, 1), g_m, tm=TOKEN_TILE,
                      name="m2_out_proj")
    xt = _moe_sublayer(xt, gvec(1, 2), sc_f, sh_f, moe_w_router[0], moe_b_router[0],
                       wg_b.reshape(ne, d, dh), wu_b.reshape(ne, d, dh), wd_b.reshape(ne, dh, d),
                       gvec(1, 3), g_f)
    return xt.reshape(bsz, seq, d)
```

```python
import functools
import math

import jax
import jax.numpy as jnp
from jax import lax
from jax.experimental import pallas as pl
from jax.experimental.pallas import tpu as pltpu

F32 = jnp.float32
BF16 = jnp.bfloat16

D_MODEL = 1024
EPS = 1e-6
POOL_WIDTH = 512
POOL_WINDOWS = (2, 4, 8, 16)
POOL_GROUP = 128
S5_WIDTH = 512
S5_GROUP = 16
S5_GROUPS = 32
S5_STATE = 64
M2_INNER = 2048
M2_HEADDIM = 64
M2_HEADS = 32
M2_GROUPS = 8
M2_HPG = 4
M2_STATE = 128
M2_CONV = 4
M2_BC = 1024
M2_CONV_DIM = 4096
M2_PROJ = 6176
N_EXPERTS = 8
EXPERT_HIDDEN = 3584

LANES = 128
SUBLANES = 8
VMEM_LIMIT_BYTES = 56 * 1024 * 1024

TOKEN_TILE = 1024
FFN_HIDDEN_TILE = 1408
M2_PROJ_PAD = 6400
M2_PROJ_TILE = 1280
M2_SIDE_STEPS = 64
S5_BLOCK = 256
SSD_CHUNK = 128
MOE_TOKEN_TILE = 512
MOE_SLOT_TILE = 512
MOE_HIDDEN_TILE = 1792

NEG_BIG = -1e30


def _cparams(sem):
    return pltpu.CompilerParams(dimension_semantics=sem, vmem_limit_bytes=VMEM_LIMIT_BYTES)


def _modulated_norm(x, g, scale, shift):
    ms = jnp.mean(x * x, axis=-1, keepdims=True)
    return (x * lax.rsqrt(ms + EPS) * g) * (1.0 + scale) + shift


def _post_norm_residual(x, y, ng, gate):
    ms = jnp.mean(y * y, axis=-1, keepdims=True)
    return x + gate * (y * lax.rsqrt(ms + EPS) * ng)


def _ada_kernel(c_ref, w_ref, b_ref, o_ref):
    c = c_ref[...]
    a = c * jax.nn.sigmoid(c)
    o_ref[0] = jnp.dot(a.astype(BF16), w_ref[0].astype(BF16),
                       preferred_element_type=F32) + b_ref[0]


def _ada_modulation(c, ada_w, ada_b):
    depth, d, n = ada_w.shape
    b = c.shape[0]
    c_pad = jnp.zeros((SUBLANES, d), F32).at[:b].set(c)
    tn = 1024
    out = pl.pallas_call(
        _ada_kernel,
        grid=(depth, n // tn),
        in_specs=[
            pl.BlockSpec((SUBLANES, d), lambda l, j: (0, 0)),
            pl.BlockSpec((1, d, tn), lambda l, j: (l, 0, j)),
            pl.BlockSpec((1, 1, tn), lambda l, j: (l, 0, j)),
        ],
        out_specs=pl.BlockSpec((1, SUBLANES, tn), lambda l, j: (l, 0, j)),
        out_shape=jax.ShapeDtypeStruct((depth, SUBLANES, n), F32),
        compiler_params=_cparams(("arbitrary", "arbitrary")),
        name="ada_modulation",
    )(c_pad, ada_w, ada_b.reshape(depth, 1, n))
    return out[:, :b]


def _norm_mm_kernel(x_ref, g_ref, sc_ref, sh_ref, w_ref, *rest, n_side, side_steps):
    side_in = rest[:n_side]
    o_ref = rest[n_side]
    side_out = rest[n_side + 1:2 * n_side + 1]
    h_scr = rest[2 * n_side + 1]

    @pl.when(pl.program_id(1) == 0)
    def _():
        h = _modulated_norm(x_ref[...], g_ref[...], sc_ref[0], sh_ref[0])
        h_scr[...] = h.astype(BF16)

    o_ref[...] = jnp.dot(h_scr[...], w_ref[...], preferred_element_type=F32).astype(o_ref.dtype)

    step = pl.program_id(0) * pl.num_programs(1) + pl.program_id(1)
    for k in range(n_side):
        @pl.when(jnp.logical_and(step >= k * side_steps, step < (k + 1) * side_steps))
        def _(k=k):
            side_out[k][...] = side_in[k][...].astype(BF16)


def _norm_matmul(x, g, scale, shift, w, *, tm, tn, out_dtype, name, side=(), side_steps=1):
    t, d = x.shape
    n = w.shape[1]
    n_j = n // tn
    tiles_per_batch = t // scale.shape[0] // tm
    assert len(side) * side_steps <= (t // tm) * n_j

    def side_spec(k, arr):
        rows = arr.shape[0] // side_steps
        return pl.BlockSpec(
            (rows, arr.shape[1]),
            lambda i, j: (jnp.clip(i * n_j + j - k * side_steps, 0, side_steps - 1), 0))

    side_specs = [side_spec(k, a) for k, a in enumerate(side)]
    outs = pl.pallas_call(
        functools.partial(_norm_mm_kernel, n_side=len(side), side_steps=side_steps),
        grid=(t // tm, n_j),
        in_specs=[
            pl.BlockSpec((tm, d), lambda i, j: (i, 0)),
            pl.BlockSpec((1, d), lambda i, j: (0, 0)),
            pl.BlockSpec((1, 1, d), lambda i, j: (i // tiles_per_batch, 0, 0)),
            pl.BlockSpec((1, 1, d), lambda i, j: (i // tiles_per_batch, 0, 0)),
            pl.BlockSpec((d, tn), lambda i, j: (0, j)),
        ] + side_specs,
        out_specs=[pl.BlockSpec((tm, tn), lambda i, j: (i, j))] + side_specs,
        out_shape=[jax.ShapeDtypeStruct((t, n), out_dtype)]
                  + [jax.ShapeDtypeStruct(a.shape, BF16) for a in side],
        scratch_shapes=[pltpu.VMEM((tm, d), BF16)],
        compiler_params=_cparams(("arbitrary", "arbitrary")),
        name=name,
    )(x, g, scale, shift, w, *side)
    return (outs[0], outs[1:]) if side else outs[0]


def _mm_post_kernel(y_ref, w_ref, x_ref, ng_ref, gate_ref, o_ref):
    y = jnp.dot(y_ref[...], w_ref[...], preferred_element_type=F32)
    o_ref[...] = _post_norm_residual(x_ref[...], y, ng_ref[...], gate_ref[0])


def _matmul_post(y, w, x, ng, gate, *, tm, name):
    t, k = y.shape
    d = w.shape[1]
    tiles_per_batch = t // gate.shape[0] // tm
    return pl.pallas_call(
        _mm_post_kernel,
        grid=(t // tm,),
        in_specs=[
            pl.BlockSpec((tm, k), lambda i: (i, 0)),
            pl.BlockSpec((k, d), lambda i: (0, 0)),
            pl.BlockSpec((tm, d), lambda i: (i, 0)),
            pl.BlockSpec((1, d), lambda i: (0, 0)),
            pl.BlockSpec((1, 1, d), lambda i: (i // tiles_per_batch, 0, 0)),
        ],
        out_specs=pl.BlockSpec((tm, d), lambda i: (i, 0)),
        out_shape=jax.ShapeDtypeStruct((t, d), F32),
        compiler_params=_cparams(("arbitrary",)),
        name=name,
    )(y, w, x, ng, gate)


def _ffn_kernel(x_ref, g_ref, sc_ref, sh_ref, wgu_ref, wd_ref, ng_ref, gate_ref,
                o_ref, h_scr, acc_scr):
    j = pl.program_id(1)

    @pl.when(j == 0)
    def _():
        h = _modulated_norm(x_ref[...], g_ref[...], sc_ref[0], sh_ref[0])
        h_scr[...] = h.astype(BF16)
        acc_scr[...] = jnp.zeros_like(acc_scr)

    th = wd_ref.shape[0]
    gu = jnp.dot(h_scr[...], wgu_ref[...], preferred_element_type=F32)
    gt = gu[:, :th]
    up = gu[:, th:]
    act = (gt * jax.nn.sigmoid(gt) * up).astype(BF16)
    acc_scr[...] += jnp.dot(act, wd_ref[...], preferred_element_type=F32)

    @pl.when(j == pl.num_programs(1) - 1)
    def _():
        o_ref[...] = _post_norm_residual(x_ref[...], acc_scr[...], ng_ref[...], gate_ref[0])


def _ffn_sublayer(x, g, scale, shift, wg, wu, wd, ng, gate, *, tm, th):
    t, d = x.shape
    hid = wg.shape[1]
    n_j = hid // th
    tiles_per_batch = t // scale.shape[0] // tm
    bvec = pl.BlockSpec((1, 1, d), lambda i, j: (i // tiles_per_batch, 0, 0))
    wgu = jnp.concatenate([w[:, j * th:(j + 1) * th] for j in range(n_j) for w in (wg, wu)], axis=1)
    return pl.pallas_call(
        _ffn_kernel,
        grid=(t // tm, n_j),
        in_specs=[
            pl.BlockSpec((tm, d), lambda i, j: (i, 0)),
            pl.BlockSpec((1, d), lambda i, j: (0, 0)),
            bvec, bvec,
            pl.BlockSpec((d, 2 * th), lambda i, j: (0, j)),
            pl.BlockSpec((th, d), lambda i, j: (j, 0)),
            pl.BlockSpec((1, d), lambda i, j: (0, 0)),
            bvec,
        ],
        out_specs=pl.BlockSpec((tm, d), lambda i, j: (i, 0)),
        out_shape=jax.ShapeDtypeStruct((t, d), F32),
        scratch_shapes=[pltpu.VMEM((tm, d), BF16), pltpu.VMEM((tm, d), F32)],
        compiler_params=_cparams(("arbitrary", "arbitrary")),
        name="ffn_sublayer",
    )(x, g, scale, shift, wgu, wd, ng, gate)


def _router_kernel(x_ref, g_ref, sc_ref, sh_ref, wr_ref, br_ref, h_ref, meta_ref, cnt_ref):
    tm = x_ref.shape[0]

    h = _modulated_norm(x_ref[...], g_ref[...], sc_ref[0], sh_ref[0])
    h_hi = h.astype(BF16)
    h_ref[...] = h_hi
    h_lo = (h - h_hi.astype(F32)).astype(BF16)
    logits = (jnp.dot(h_hi, wr_ref[0], preferred_element_type=F32)
              + jnp.dot(h_lo, wr_ref[0], preferred_element_type=F32)
              + jnp.dot(h_hi, wr_ref[1], preferred_element_type=F32)) + br_ref[...]
    lane = lax.broadcasted_iota(jnp.int32, (tm, LANES), 1).astype(F32)
    m1 = jnp.max(logits, axis=-1, keepdims=True)
    i1 = jnp.min(jnp.where(logits == m1, lane, float(LANES)), axis=-1, keepdims=True)
    oh1 = lane == i1
    rest = jnp.where(oh1, NEG_BIG * 2.0, logits)
    m2 = jnp.max(rest, axis=-1, keepdims=True)
    i2 = jnp.min(jnp.where(rest == m2, lane, float(LANES)), axis=-1, keepdims=True)
    oh2 = lane == i2
    e = jnp.exp(m2 - m1)
    w1 = 1.0 / (1.0 + e)
    w2 = e / (1.0 + e)

    picks = jnp.where(oh1, 1.0, 0.0) + jnp.where(oh2, 1.0, 0.0)
    row = lax.broadcasted_iota(jnp.int32, (tm, tm), 0)
    col = lax.broadcasted_iota(jnp.int32, (tm, tm), 1)
    lower = jnp.where(col < row, 1.0, 0.0).astype(BF16)
    before = jnp.dot(lower, picks.astype(BF16), preferred_element_type=F32)
    tile_cnt = jnp.sum(picks, axis=0, keepdims=True)
    ea = lax.broadcasted_iota(jnp.int32, (LANES, LANES), 0)
    eb = lax.broadcasted_iota(jnp.int32, (LANES, LANES), 1)
    seg_rows = jnp.floor((tile_cnt + (SUBLANES - 1.0)) * (1.0 / SUBLANES)) * SUBLANES
    seg_off = jnp.dot(jnp.broadcast_to(seg_rows, (SUBLANES, LANES)), jnp.where(ea < eb, 1.0, 0.0),
                      preferred_element_type=F32, precision=lax.Precision.HIGHEST)[0:1, :]
    local = before + seg_off
    loc1 = jnp.sum(jnp.where(oh1, local, 0.0), axis=-1, keepdims=True)
    loc2 = jnp.sum(jnp.where(oh2, local, 0.0), axis=-1, keepdims=True)
    cnt_ref[0] = tile_cnt

    meta = jnp.where(lane == 0.0, i1, 0.0)
    meta = jnp.where(lane == 1.0, i2, meta)
    meta = jnp.where(lane == 2.0, w1, meta)
    meta = jnp.where(lane == 3.0, w2, meta)
    meta = jnp.where(lane == 4.0, loc1, meta)
    meta = jnp.where(lane == 5.0, loc2, meta)
    meta_ref[...] = meta


def _router(x, g, scale, shift, w_router, b_router, *, tm):
    t, d = x.shape
    tiles_per_batch = t // scale.shape[0] // tm
    wr = jnp.zeros((d, LANES), F32).at[:, :N_EXPERTS].set(w_router)
    wr_hi = wr.astype(BF16)
    wr = jnp.stack([wr_hi, (wr - wr_hi.astype(F32)).astype(BF16)])
    br = jnp.full((1, LANES), NEG_BIG, F32).at[0, :N_EXPERTS].set(b_router)
    bvec = pl.BlockSpec((1, 1, d), lambda i: (i // tiles_per_batch, 0, 0))
    return pl.pallas_call(
        _router_kernel,
        grid=(t // tm,),
        in_specs=[
            pl.BlockSpec((tm, d), lambda i: (i, 0)),
            pl.BlockSpec((1, d), lambda i: (0, 0)),
            bvec, bvec,
            pl.BlockSpec((2, d, LANES), lambda i: (0, 0, 0)),
            pl.BlockSpec((1, LANES), lambda i: (0, 0)),
        ],
        out_specs=[
            pl.BlockSpec((tm, d), lambda i: (i, 0)),
            pl.BlockSpec((tm, LANES), lambda i: (i, 0)),
            pl.BlockSpec((1, 1, LANES), lambda i: (i, 0, 0)),
        ],
        out_shape=[
            jax.ShapeDtypeStruct((t, d), BF16),
            jax.ShapeDtypeStruct((t, LANES), F32),
            jax.ShapeDtypeStruct((t // tm, 1, LANES), F32),
        ],
        compiler_params=_cparams(("arbitrary",)),
        name="moe_router",
    )(x, g, scale, shift, wr, br)


SEG_FIELDS = 3
SEG_PAD_ROWS = N_EXPERTS * SUBLANES


def _segment_copies(scal_ref, tile, tm, per_copy):
    base = tile * (SEG_FIELDS * N_EXPERTS)
    for e in range(N_EXPERTS):
        n = scal_ref[base + e]
        g0 = scal_ref[base + N_EXPERTS + e]
        l0 = scal_ref[base + 2 * N_EXPERTS + e]
        done = 0
        b = tm
        while b >= SUBLANES:
            take = n & b

            @pl.when(take != 0)
            def _(b=b, done=done, g0=g0, l0=l0):
                per_copy(pl.multiple_of(l0 + done, SUBLANES), pl.multiple_of(g0 + done, SUBLANES), b)

            done = done + take
            b //= 2


def _dispatch_kernel(scal_ref, tail_ref, h_ref, meta_ref, xs_ref, ws_ref,
                     seg_scr, wseg_scr, zero_scr, sem, zsem, *, n_tiles, max_unused):
    tile = pl.program_id(0)
    tm = h_ref.shape[0]
    ns = seg_scr.shape[1]
    par = tile % 2

    def segment_dmas(which, buf):
        def descriptors(local, glob, n):
            return (pltpu.make_async_copy(wseg_scr.at[buf, pl.ds(local, n), :],
                                          ws_ref.at[pl.ds(glob, n), :], sem.at[buf]),
                    pltpu.make_async_copy(seg_scr.at[buf, pl.ds(local, n), :],
                                          xs_ref.at[pl.ds(glob, n), :], sem.at[buf]))
        return descriptors

    def wait_segments(which, buf):
        descriptors = segment_dmas(which, buf)
        _segment_copies(scal_ref, which, tm, lambda l, g, n: [c.wait() for c in descriptors(l, g, n)])

    @pl.when(tile == 0)
    def _():
        zero_scr[...] = jnp.zeros_like(zero_scr)
        fills = []
        n_slot_tiles = xs_ref.shape[0] // tm
        tails = [(tail_ref[N_EXPERTS + e] > 0, pl.multiple_of(tail_ref[e], tm))
                 for e in range(N_EXPERTS)]
        tails += [(k >= tail_ref[2 * N_EXPERTS], k * tm)
                  for k in range(n_slot_tiles - max_unused, n_slot_tiles)]
        for nonempty, tail in tails:
            fills.append((nonempty,
                          pltpu.make_async_copy(zero_scr, xs_ref.at[pl.ds(tail, tm), :], zsem),
                          pltpu.make_async_copy(zero_scr.at[:, 0:LANES],
                                                ws_ref.at[pl.ds(tail, tm), :], zsem)))
        for nonempty, fill_x, fill_w in fills:
            @pl.when(nonempty)
            def _(fill_x=fill_x, fill_w=fill_w):
                fill_x.start()
                fill_w.start()
        for nonempty, fill_x, fill_w in fills:
            @pl.when(nonempty)
            def _(fill_x=fill_x, fill_w=fill_w):
                fill_x.wait()
                fill_w.wait()

    meta = meta_ref[...]
    meta_t = meta.T
    slot = lax.broadcasted_iota(jnp.int32, (ns, tm), 0).astype(F32)
    g1 = jnp.where(slot == meta_t[4:5, :], 1.0, 0.0).astype(BF16)
    g2 = jnp.where(slot == meta_t[5:6, :], 1.0, 0.0).astype(BF16)
    rows = jnp.dot(g1 + g2, h_ref[...], preferred_element_type=F32)
    lane = lax.broadcasted_iota(jnp.int32, (tm, LANES), 1)

    def split_lanes(w):
        hi = w.astype(BF16).astype(F32)
        return jnp.where(lane == 0, hi, jnp.where(lane == 1, w - hi, 0.0)).astype(BF16)

    w_split = jnp.concatenate([split_lanes(meta[:, 2:3]), split_lanes(meta[:, 3:4])], axis=0)
    w_slot = jnp.dot(jnp.concatenate([g1, g2], axis=1), w_split, preferred_element_type=F32)
    weight = w_slot[:, 0:1] + w_slot[:, 1:2]

    @pl.when(tile >= 2)
    def _():
        wait_segments(tile - 2, par)

    seg_scr[par] = rows
    wseg_scr[par] = jnp.broadcast_to(weight, (ns, LANES))
    descriptors = segment_dmas(tile, par)
    _segment_copies(scal_ref, tile, tm, lambda l, g, n: [c.start() for c in descriptors(l, g, n)])

    @pl.when(tile == n_tiles - 1)
    def _():
        wait_segments(tile, par)
        if n_tiles > 1:
            wait_segments(tile - 1, 1 - par)


def _dispatch(h, meta, seg_table, tail_table, n_slots, *, tm):
    t, d = h.shape
    grid_spec = pltpu.PrefetchScalarGridSpec(
        num_scalar_prefetch=2,
        grid=(t // tm,),
        in_specs=[
            pl.BlockSpec((tm, d), lambda i, sc, tl: (i, 0)),
            pl.BlockSpec((tm, LANES), lambda i, sc, tl: (i, 0)),
        ],
        out_specs=[pl.BlockSpec(memory_space=pl.ANY), pl.BlockSpec(memory_space=pl.ANY)],
        scratch_shapes=[
            pltpu.VMEM((2, 2 * tm + SEG_PAD_ROWS, d), F32),
            pltpu.VMEM((2, 2 * tm + SEG_PAD_ROWS, LANES), F32),
            pltpu.VMEM((tm, d), F32),
            pltpu.SemaphoreType.DMA((2,)), pltpu.SemaphoreType.DMA(()),
        ],
    )
    return pl.pallas_call(
        functools.partial(_dispatch_kernel, n_tiles=t // tm,
                          max_unused=n_slots // tm - 2 * t // tm),
        grid_spec=grid_spec,
        out_shape=[jax.ShapeDtypeStruct((n_slots, d), F32),
                   jax.ShapeDtypeStruct((n_slots, LANES), F32)],
        compiler_params=_cparams(("arbitrary",)),
        name="moe_dispatch",
    )(seg_table, tail_table, h, meta)


def _expert_kernel(te_ref, nu_ref, xs_ref, ws_ref, wg_ref, wu_ref, wd_ref, ys_ref,
                   xb_scr, acc_scr):
    i = pl.program_id(0)
    j = pl.program_id(1)
    last = pl.num_programs(1) - 1
    used = i < nu_ref[0]

    @pl.when(used)
    def _():
        @pl.when(j == 0)
        def _():
            xb_scr[...] = xs_ref[...].astype(BF16)
            acc_scr[...] = jnp.zeros_like(acc_scr)

        x = xb_scr[...]
        gt = jnp.dot(x, wg_ref[0], preferred_element_type=F32)
        up = jnp.dot(x, wu_ref[0], preferred_element_type=F32)
        act = (gt * jax.nn.sigmoid(gt) * up).astype(BF16)
        acc_scr[...] += jnp.dot(act, wd_ref[0], preferred_element_type=F32)

        @pl.when(j == last)
        def _():
            ys_ref[...] = (acc_scr[...] * ws_ref[:, 0:1]).astype(ys_ref.dtype)

    @pl.when(jnp.logical_and(jnp.logical_not(used), j == last))
    def _():
        ys_ref[...] = jnp.zeros_like(ys_ref)


def _experts(xs, ws, tile_expert, n_used, wg, wu, wd, *, tm, th):
    n_slots, d = xs.shape
    hid = wg.shape[2]
    nt = n_slots // tm

    def hidden_step(i, j, nu):
        return jnp.where(i < nu[0], j, 0)

    def slot_tile(i, nu):
        return jnp.where(i < nu[0], i, 0)

    grid_spec = pltpu.PrefetchScalarGridSpec(
        num_scalar_prefetch=2,
        grid=(nt, hid // th),
        in_specs=[
            pl.BlockSpec((tm, d), lambda i, j, te, nu: (slot_tile(i, nu), 0)),
            pl.BlockSpec((tm, LANES), lambda i, j, te, nu: (slot_tile(i, nu), 0)),
            pl.BlockSpec((1, d, th), lambda i, j, te, nu: (te[i], 0, hidden_step(i, j, nu))),
            pl.BlockSpec((1, d, th), lambda i, j, te, nu: (te[i], 0, hidden_step(i, j, nu))),
            pl.BlockSpec((1, th, d), lambda i, j, te, nu: (te[i], hidden_step(i, j, nu), 0)),
        ],
        out_specs=pl.BlockSpec((tm, d), lambda i, j, te, nu: (i, 0)),
        scratch_shapes=[pltpu.VMEM((tm, d), BF16), pltpu.VMEM((tm, d), F32)],
    )
    return pl.pallas_call(
        _expert_kernel,
        grid_spec=grid_spec,
        out_shape=jax.ShapeDtypeStruct((n_slots, d), F32),
        compiler_params=_cparams(("arbitrary", "arbitrary")),
        name="moe_experts",
    )(tile_expert, n_used, xs, ws, wg, wu, wd)


def _combine_kernel(scal_ref, ys_ref, meta_ref, x_ref, ng_ref, gate_ref, o_ref, buf, sem,
                    *, n_tiles):
    tile = pl.program_id(0)
    tm = x_ref.shape[0]
    ns = buf.shape[1]
    par = tile % 2

    def fetch(which, slot, wait):
        def copy_in(local, glob, n):
            dma = pltpu.make_async_copy(ys_ref.at[pl.ds(glob, n), :],
                                        buf.at[slot, pl.ds(local, n), :], sem.at[slot])
            dma.wait() if wait else dma.start()
        _segment_copies(scal_ref, which, tm, copy_in)

    @pl.when(tile == 0)
    def _():
        buf[...] = jnp.zeros_like(buf)
        fetch(tile, par, False)

    @pl.when(tile + 1 < n_tiles)
    def _():
        fetch(tile + 1, 1 - par, False)

    fetch(tile, par, True)
    meta = meta_ref[...]
    slot_id = lax.broadcasted_iota(jnp.int32, (tm, ns), 1).astype(F32)
    pick = jnp.where((slot_id == meta[:, 4:5]) | (slot_id == meta[:, 5:6]), 1.0, 0.0).astype(BF16)
    y = jnp.dot(pick, buf[par].astype(BF16), preferred_element_type=F32)
    o_ref[...] = _post_norm_residual(x_ref[...], y, ng_ref[...], gate_ref[0])


def _combine(ys, meta, seg_table, x, ng, gate, *, tm):
    t, d = x.shape
    tiles_per_batch = t // gate.shape[0] // tm
    grid_spec = pltpu.PrefetchScalarGridSpec(
        num_scalar_prefetch=1,
        grid=(t // tm,),
        in_specs=[
            pl.BlockSpec(memory_space=pl.ANY),
            pl.BlockSpec((tm, LANES), lambda i, sc: (i, 0)),
            pl.BlockSpec((tm, d), lambda i, sc: (i, 0)),
            pl.BlockSpec((1, d), lambda i, sc: (0, 0)),
            pl.BlockSpec((1, 1, d), lambda i, sc: (i // tiles_per_batch, 0, 0)),
        ],
        out_specs=pl.BlockSpec((tm, d), lambda i, sc: (i, 0)),
        scratch_shapes=[pltpu.VMEM((2, 2 * tm + SEG_PAD_ROWS, d), F32),
                        pltpu.SemaphoreType.DMA((2,))],
    )
    return pl.pallas_call(
        functools.partial(_combine_kernel, n_tiles=t // tm),
        grid_spec=grid_spec,
        out_shape=jax.ShapeDtypeStruct((t, d), F32),
        compiler_params=_cparams(("arbitrary",)),
        name="moe_combine",
    )(seg_table, ys, meta, x, ng, gate)


def _moe_sublayer(x, g, scale, shift, w_router, b_router, wg, wu, wd, ng, gate):
    t = x.shape[0]
    tm, tm_e, th = MOE_TOKEN_TILE, MOE_SLOT_TILE, MOE_HIDDEN_TILE
    h, meta, counts = _router(x, g, scale, shift, w_router, b_router, tm=tm)

    assert tm == tm_e
    tile_cnt = counts[:, 0, :N_EXPERTS].astype(jnp.int32)
    seg_rows = ((tile_cnt + SUBLANES - 1) // SUBLANES) * SUBLANES
    cnt = jnp.sum(seg_rows, axis=0)
    padded = ((cnt + tm_e - 1) // tm_e) * tm_e
    ends = jnp.cumsum(padded)
    offs = ends - padded
    glob_start = offs[None, :] + jnp.cumsum(seg_rows, axis=0) - seg_rows
    local_start = jnp.cumsum(seg_rows, axis=1) - seg_rows
    seg_table = jnp.concatenate([seg_rows, glob_start, local_start], axis=1).reshape(-1)
    tail_table = jnp.concatenate([ends - tm_e, padded, ends[-1:] // tm_e]).astype(jnp.int32)
    n_tok_tiles = t // tm
    nt = -(-(2 * t + n_tok_tiles * SEG_PAD_ROWS) // tm_e) + N_EXPERTS
    starts = jnp.arange(nt, dtype=jnp.int32) * tm_e
    tile_expert = jnp.minimum(jnp.sum(starts[:, None] >= ends[None, :], axis=1), N_EXPERTS - 1)
    tile_expert = tile_expert.astype(jnp.int32)
    n_used = (ends[-1:] // tm_e).astype(jnp.int32)

    xs, ws = _dispatch(h, meta, seg_table, tail_table, nt * tm_e, tm=tm)
    ys = _experts(xs, ws, tile_expert, n_used, wg, wu, wd, tm=tm_e, th=th)
    return _combine(ys, meta, seg_table, x, ng, gate, tm=tm)


def _s5_disc_kernel(lam_re_ref, lam_im_ref, step_ref, pow_ref, keep_ref, b_re_ref, b_im_ref,
                    apr_ref, api_ref, bbr_ref, bbi_ref):
    lr = jnp.minimum(lam_re_ref[...], -1e-4)
    li = lam_im_ref[...]
    step = step_ref[...]
    m = pow_ref[...]
    mag = jnp.exp(lr * step * m) * keep_ref[...]
    ang = li * step * m
    apr_ref[...] = mag * jnp.cos(ang)
    api_ref[...] = mag * jnp.sin(ang)
    mag1 = jnp.exp(lr * step)
    ar = mag1 * jnp.cos(li * step)
    ai = mag1 * jnp.sin(li * step)
    inv = 1.0 / (lr * lr + li * li)
    fr = ((ar - 1.0) * lr + ai * li) * inv
    fi = (ai * lr - (ar - 1.0) * li) * inv
    br = b_re_ref[...]
    bi = b_im_ref[...]
    bbr_ref[...] = fr * br - fi * bi
    bbi_ref[...] = fr * bi + fi * br


S5_LOCAL_STEPS = 3
S5_ROW_POW0 = S5_LOCAL_STEPS * SUBLANES
S5_TILE_POW0 = S5_ROW_POW0 + SUBLANES


def _s5_power_rows(n_tile_steps):
    exps, keep = [], []
    for k in range(S5_LOCAL_STEPS):
        for tau in range(SUBLANES):
            exps.append(float(1 << k))
            keep.append(1.0 if tau >= (1 << k) else 0.0)
    for tau in range(SUBLANES):
        exps.append(float(tau + 1))
        keep.append(1.0)
    for k in range(n_tile_steps):
        exps.append(float(SUBLANES << k))
        keep.append(1.0)
    while len(exps) % SUBLANES:
        exps.append(0.0)
        keep.append(0.0)
    return exps, keep


def _s5_discretise(lam_re, lam_im, log_step, b_re, b_im, n_tile_steps):
    gp = S5_GROUPS * S5_STATE
    step = jnp.repeat(jnp.exp(log_step), S5_STATE).reshape(1, gp)
    exps, keep = _s5_power_rows(n_tile_steps)
    rows = len(exps)
    b_re_t = jnp.transpose(b_re, (2, 0, 1)).reshape(S5_GROUP, gp)
    b_im_t = jnp.transpose(b_im, (2, 0, 1)).reshape(S5_GROUP, gp)
    return pl.pallas_call(
        _s5_disc_kernel,
        out_shape=[jax.ShapeDtypeStruct((rows, gp), F32), jax.ShapeDtypeStruct((rows, gp), F32),
                   jax.ShapeDtypeStruct((S5_GROUP, gp), F32), jax.ShapeDtypeStruct((S5_GROUP, gp), F32)],
        name="s5_discretise",
    )(lam_re.reshape(1, gp), lam_im.reshape(1, gp), step,
      jnp.asarray(exps, F32).reshape(rows, 1), jnp.asarray(keep, F32).reshape(rows, 1),
      b_re_t, b_im_t)


POOL_HALO = 16
S5_HALF = 256
S5_HALF_STATES = 1024


def _complex_axpy(xr, xi, cr, ci, sr, si):
    return xr + (cr * sr - ci * si), xi + (cr * si + ci * sr)


def _mix0_kernel(x_ref, g_ref, sc_ref, sh_ref, win_ref, pw_ref, ps_ref, bw_ref, cw_ref, apr_ref,
                 api_ref, dsk_ref, wglu_ref, wout_ref, ng_ref, gate_ref, side_ref, o_ref,
                 side_out_ref, *scratch):
    side_out_ref[...] = side_ref[...].astype(BF16)
    halo_scr, _, sr_scr, si_scr, _, _, _ = scratch

    @pl.when(pl.program_id(0) == 0)
    def _():
        halo_scr[...] = jnp.zeros_like(halo_scr)
        sr_scr[...] = jnp.zeros_like(sr_scr)
        si_scr[...] = jnp.zeros_like(si_scr)

    for b in range(x_ref.shape[0]):
        one = pl.ds(b, 1)
        _mix0_block(x_ref.at[b], g_ref, sc_ref.at[one], sh_ref.at[one], win_ref, pw_ref, ps_ref,
                    bw_ref, cw_ref, apr_ref, api_ref, dsk_ref, wglu_ref, wout_ref, ng_ref,
                    gate_ref.at[one], o_ref.at[b], *[s.at[b] for s in scratch])


def _mix0_block(x_ref, g_ref, sc_ref, sh_ref, win_ref, pw_ref, ps_ref, bw_ref, cw_ref, apr_ref,
                api_ref, dsk_ref, wglu_ref, wout_ref, ng_ref, gate_ref, o_ref,
                halo_scr, ext_scr, sr_scr, si_scr, xr_scr, xi_scr, ycat_scr):
    tt = x_ref.shape[0]
    n_states = sr_scr.shape[1]
    blk = pl.program_id(0)

    h = _modulated_norm(x_ref[...], g_ref[...], sc_ref[0], sh_ref[0]).astype(BF16)
    u = jnp.dot(h, win_ref[...], preferred_element_type=F32)

    up = u[:, :POOL_WIDTH]
    ext_scr[0:POOL_HALO, :] = halo_scr[...]
    ext_scr[POOL_HALO:, :] = up
    halo_scr[...] = up[tt - POOL_HALO:, :]
    ext = ext_scr[...].astype(BF16)
    row = lax.broadcasted_iota(jnp.int32, (tt, tt + POOL_HALO), 0)
    col = lax.broadcasted_iota(jnp.int32, (tt, tt + POOL_HALO), 1)
    lag = row + POOL_HALO - col
    t_glob = (blk * tt + row + 1).astype(F32)
    pooled_out = []
    for gi, win in enumerate(POOL_WINDOWS):
        inv_count = 1.0 / jnp.minimum(t_glob, float(win))
        band = jnp.where((lag >= 0) & (lag < win), inv_count, 0.0) - jnp.where(lag == 0, 1.0, 0.0)
        pooled = jnp.dot(band.astype(BF16), ext[:, gi * POOL_GROUP:(gi + 1) * POOL_GROUP],
                         preferred_element_type=F32)
        pooled_out.append(jnp.dot(pooled.astype(BF16), pw_ref[gi], preferred_element_type=F32))
    y_pool = jnp.concatenate(pooled_out, axis=-1) * ps_ref[...]
    ycat_scr[:, :POOL_WIDTH] = y_pool.astype(BF16)

    us = u[:, POOL_WIDTH:]
    usb = us.astype(BF16)
    bu = [jnp.dot(usb[:, hf * S5_HALF:(hf + 1) * S5_HALF], bw_ref[hf], preferred_element_type=F32)
          for hf in range(2)]
    nt = tt // SUBLANES
    xr = jnp.concatenate([b[:, :S5_HALF_STATES] for b in bu], axis=-1).reshape(nt, SUBLANES, n_states)
    xi = jnp.concatenate([b[:, S5_HALF_STATES:] for b in bu], axis=-1).reshape(nt, SUBLANES, n_states)
    for k in range(S5_LOCAL_STEPS):
        cr = apr_ref[k * SUBLANES:(k + 1) * SUBLANES, :][None]
        ci = api_ref[k * SUBLANES:(k + 1) * SUBLANES, :][None]
        xr, xi = _complex_axpy(xr, xi, cr, ci, pltpu.roll(xr, 1 << k, 1), pltpu.roll(xi, 1 << k, 1))

    xr2 = xr.reshape(tt, n_states)
    xi2 = xi.reshape(tt, n_states)
    n_cb = n_states // LANES
    for cb in range(n_cb):
        xr_scr[cb] = xr2[:, cb * LANES:(cb + 1) * LANES]
        xi_scr[cb] = xi2[:, cb * LANES:(cb + 1) * LANES]
    tile_end = pl.ds(SUBLANES - 1, nt, stride=SUBLANES)
    er = jnp.concatenate([xr_scr[cb, tile_end, :] for cb in range(n_cb)], axis=-1)
    ei = jnp.concatenate([xi_scr[cb, tile_end, :] for cb in range(n_cb)], axis=-1)
    prev_r = sr_scr[...]
    prev_i = si_scr[...]
    tile_row = lax.broadcasted_iota(jnp.int32, (nt, n_states), 0)
    a8r = apr_ref[S5_TILE_POW0:S5_TILE_POW0 + 1, :]
    a8i = api_ref[S5_TILE_POW0:S5_TILE_POW0 + 1, :]
    er = er + jnp.where(tile_row == 0, a8r * prev_r - a8i * prev_i, 0.0)
    ei = ei + jnp.where(tile_row == 0, a8r * prev_i + a8i * prev_r, 0.0)
    k = 0
    while (1 << k) < nt:
        cr = apr_ref[S5_TILE_POW0 + k:S5_TILE_POW0 + k + 1, :]
        ci = api_ref[S5_TILE_POW0 + k:S5_TILE_POW0 + k + 1, :]
        pr = jnp.where(tile_row < (1 << k), 0.0, pltpu.roll(er, 1 << k, 0))
        pi = jnp.where(tile_row < (1 << k), 0.0, pltpu.roll(ei, 1 << k, 0))
        er, ei = _complex_axpy(er, ei, cr, ci, pr, pi)
        k += 1
    sr_scr[...] = er[nt - 1:nt, :]
    si_scr[...] = ei[nt - 1:nt, :]
    in_r = jnp.where(tile_row == 0, prev_r, pltpu.roll(er, 1, 0))
    in_i = jnp.where(tile_row == 0, prev_i, pltpu.roll(ei, 1, 0))

    rep_row = lax.broadcasted_iota(jnp.int32, (tt, nt), 0)
    rep_col = lax.broadcasted_iota(jnp.int32, (tt, nt), 1)
    rep = jnp.where(rep_row // SUBLANES == rep_col, 1.0, 0.0).astype(BF16)
    entering = jnp.concatenate([in_r, in_i], axis=-1)
    ent_hi = entering.astype(BF16)
    ent_lo = (entering - ent_hi.astype(F32)).astype(BF16)
    ent = (jnp.dot(rep, ent_hi, preferred_element_type=F32)
           + jnp.dot(rep, ent_lo, preferred_element_type=F32))
    cbr = ent[:, :n_states].reshape(nt, SUBLANES, n_states)
    cbi = ent[:, n_states:].reshape(nt, SUBLANES, n_states)
    pwr = apr_ref[S5_ROW_POW0:S5_ROW_POW0 + SUBLANES, :][None]
    pwi = api_ref[S5_ROW_POW0:S5_ROW_POW0 + SUBLANES, :][None]
    xr, xi = _complex_axpy(xr, xi, pwr, pwi, cbr, cbi)
    xr2 = xr.reshape(tt, n_states).astype(BF16)
    xi2 = xi.reshape(tt, n_states).astype(BF16)

    ys = []
    for hf in range(2):
        sl = slice(hf * S5_HALF_STATES, (hf + 1) * S5_HALF_STATES)
        xcat = jnp.concatenate([xr2[:, sl], xi2[:, sl]], axis=-1)
        ys.append(jnp.dot(xcat, cw_ref[hf], preferred_element_type=F32))
    y = jnp.concatenate(ys, axis=-1) + dsk_ref[...] * us
    y = jax.nn.gelu(y)
    glu = jnp.dot(y.astype(BF16), wglu_ref[...], preferred_element_type=F32)
    ycat_scr[:, POOL_WIDTH:] = (y * jax.nn.sigmoid(glu)).astype(BF16)

    y_out = jnp.dot(ycat_scr[...], wout_ref[...], preferred_element_type=F32)
    o_ref[...] = _post_norm_residual(x_ref[...], y_out, ng_ref[...], gate_ref[0])


def _side_spec(arr, n_steps, step_of):
    assert arr.shape[0] % n_steps == 0
    return pl.BlockSpec((arr.shape[0] // n_steps, arr.shape[1]), lambda *idx: (step_of(*idx), 0))


def _pool_s5_mixer(x, bsz, g, scale, shift, w_in, pool_w, pool_scale, lam_re, lam_im, log_step,
                   b_re, b_im, c_re, c_im, d_skip, w_glu, w_out, ng, gate, side, *, tt):
    t = x.shape[0]
    seq = t // bsz
    gp = S5_GROUPS * S5_STATE
    apr, api, bbr, bbi = _s5_discretise(lam_re, lam_im, log_step, b_re, b_im,
                                        int(math.log2(tt // SUBLANES)))
    n_pow = apr.shape[0]

    gh = S5_GROUPS // 2
    eye = jnp.eye(gh, dtype=F32)

    def in_map(bb):
        bb = bb.reshape(S5_GROUP, 2, gh, S5_STATE)
        return jnp.einsum("hxgp,gk->xghkp", bb, eye).reshape(2, gh * S5_GROUP, gh * S5_STATE)

    bw = jnp.concatenate([in_map(bbr), in_map(bbi)], axis=-1).astype(BF16)

    def out_map(cc):
        cc = cc.reshape(2, gh, S5_GROUP, S5_STATE)
        return jnp.einsum("xghp,gk->xgpkh", cc, eye).reshape(2, gh * S5_STATE, gh * S5_GROUP)

    cw = jnp.concatenate([out_map(c_re), -out_map(c_im)], axis=1).astype(BF16)

    const2 = lambda i: (0, 0)
    const3 = lambda i: (0, 0, 0)
    nblk = seq // tt
    side_spec = _side_spec(side, nblk, lambda i: i)
    bvec = pl.BlockSpec((bsz, 1, D_MODEL), const3)
    seq_block = pl.BlockSpec((bsz, tt, D_MODEL), lambda i: (0, i, 0))
    outs = pl.pallas_call(
        _mix0_kernel,
        grid=(nblk,),
        in_specs=[
            seq_block,
            pl.BlockSpec((1, D_MODEL), const2),
            bvec, bvec,
            pl.BlockSpec((D_MODEL, D_MODEL), const2),
            pl.BlockSpec((len(POOL_WINDOWS), POOL_GROUP, POOL_GROUP), const3),
            pl.BlockSpec((1, POOL_WIDTH), const2),
            pl.BlockSpec((2, S5_HALF, 2 * S5_HALF_STATES), const3),
            pl.BlockSpec((2, 2 * S5_HALF_STATES, S5_HALF), const3),
            pl.BlockSpec((n_pow, gp), const2),
            pl.BlockSpec((n_pow, gp), const2),
            pl.BlockSpec((1, S5_WIDTH), const2),
            pl.BlockSpec((S5_WIDTH, S5_WIDTH), const2),
            pl.BlockSpec((D_MODEL, D_MODEL), const2),
            pl.BlockSpec((1, D_MODEL), const2),
            bvec,
            side_spec,
        ],
        out_specs=[seq_block, side_spec],
        out_shape=[jax.ShapeDtypeStruct((bsz, seq, D_MODEL), F32),
                   jax.ShapeDtypeStruct(side.shape, BF16)],
        scratch_shapes=[
            pltpu.VMEM((bsz, POOL_HALO, POOL_WIDTH), F32),
            pltpu.VMEM((bsz, tt + POOL_HALO, POOL_WIDTH), F32),
            pltpu.VMEM((bsz, 1, gp), F32), pltpu.VMEM((bsz, 1, gp), F32),
            pltpu.VMEM((bsz, gp // LANES, tt, LANES), F32),
            pltpu.VMEM((bsz, gp // LANES, tt, LANES), F32),
            pltpu.VMEM((bsz, tt, D_MODEL), BF16),
        ],
        compiler_params=_cparams(("arbitrary",)),
        name="pool_s5_mixer",
    )(x.reshape(bsz, seq, D_MODEL), g, scale, shift, w_in, pool_w.astype(BF16),
      pool_scale.reshape(1, POOL_WIDTH), bw, cw, apr, api, d_skip.reshape(1, S5_WIDTH),
      w_glu.astype(BF16), w_out, ng, gate, side)
    return outs[0].reshape(t, D_MODEL), outs[1]


CONV_HALO = SUBLANES
M2_GROUP_WIDTH = M2_HPG * M2_HEADDIM


def _causal_conv_silu(x_ref, halo_scr, full_scr, w_ref, b_ref, col0):
    lc = x_ref.shape[0]
    width = x_ref.shape[1]
    x = x_ref[...].astype(F32)
    full_scr[0:CONV_HALO, :] = halo_scr[...]
    full_scr[CONV_HALO:, :] = x
    halo_scr[...] = x[lc - CONV_HALO:, :]
    cols = slice(col0, col0 + width)
    acc = b_ref[:, cols] + w_ref[M2_CONV - 1:M2_CONV, cols] * x
    for k in range(M2_CONV - 1):
        off = CONV_HALO - (M2_CONV - 1) + k
        acc = acc + w_ref[k:k + 1, cols] * full_scr[off:off + lc, :]
    return acc * jax.nn.sigmoid(acc)


def _split_dot(v, e):
    hi = v.astype(BF16)
    lo = (v - hi.astype(F32)).astype(BF16)
    return (jnp.dot(hi, e, preferred_element_type=F32) + jnp.dot(lo, e, preferred_element_type=F32))


def _ssd_kernel(z_ref, xs_ref, bc_ref, dtr_ref, cw_ref, cb_ref, dtb_ref, alog_ref, dx_ref, ng_ref,
                e_ref, side_ref, o_ref, side_out_ref, *scratch):
    side_out_ref[...] = side_ref[...].astype(BF16)
    halo_x, halo_bc, _, _, st_scr, _ = scratch

    @pl.when(pl.program_id(0) == 0)
    def _():
        halo_x[...] = jnp.zeros_like(halo_x)
        halo_bc[...] = jnp.zeros_like(halo_bc)
        st_scr[...] = jnp.zeros_like(st_scr)

    for b in range(z_ref.shape[0]):
        _ssd_chunk(z_ref.at[b], xs_ref.at[b], bc_ref.at[b], dtr_ref.at[b], cw_ref, cb_ref, dtb_ref,
                   alog_ref, dx_ref, ng_ref, e_ref, o_ref.at[b], *[s.at[b] for s in scratch])


def _ssd_chunk(z_ref, xs_ref, bc_ref, dtr_ref, cw_ref, cb_ref, dtb_ref, alog_ref, dx_ref, ng_ref,
               e_ref, o_ref, halo_x, halo_bc, full_x, full_bc, st_scr, y_scr):
    lc = z_ref.shape[0]

    xs = _causal_conv_silu(xs_ref, halo_x, full_x, cw_ref, cb_ref, 0)
    bc = _causal_conv_silu(bc_ref, halo_bc, full_bc, cw_ref, cb_ref, M2_INNER)

    lane = lax.broadcasted_iota(jnp.int32, (1, LANES), 1)
    x_dt = dtr_ref[...].astype(F32) + dtb_ref[...]
    dt = jnp.maximum(x_dt, 0.0) + jnp.log(1.0 + jnp.exp(-jnp.abs(x_dt)))
    a = jnp.where(lane < M2_HEADS, -jnp.exp(alog_ref[...]) * math.log2(math.e), 0.0)
    da = dt * a
    row = lax.broadcasted_iota(jnp.int32, (lc, lc), 0)
    col = lax.broadcasted_iota(jnp.int32, (lc, lc), 1)
    causal = col <= row
    cs = jnp.dot(jnp.where(causal, 1.0, 0.0), da, preferred_element_type=F32,
                 precision=lax.Precision.HIGHEST)
    cs_last = cs[lc - 1:lc, :]
    ecs = jnp.exp2(cs)
    w_in = dt * jnp.exp2(cs_last - cs)
    cs_t = cs.T
    dt_t = dt.T
    e = e_ref[...]
    ecs_x = _split_dot(ecs, e)
    w_x = _split_dot(w_in, e)
    xsb = xs.astype(BF16)
    xw = (xs * w_x).astype(BF16)

    for g in range(M2_GROUPS):
        bm = bc[:, g * M2_STATE:(g + 1) * M2_STATE].astype(BF16)
        cm = bc[:, M2_BC + g * M2_STATE:M2_BC + (g + 1) * M2_STATE].astype(BF16)
        cbm = lax.dot_general(cm, bm, (((1,), (1,)), ((), ())), preferred_element_type=F32)
        gcols = slice(g * M2_GROUP_WIDTH, (g + 1) * M2_GROUP_WIDTH)
        st = st_scr[g]
        y_off = jnp.dot(cm, st.astype(BF16), preferred_element_type=F32) * ecs_x[:, gcols]
        y_heads = []
        for jj in range(M2_HPG):
            j = g * M2_HPG + jj
            seg = cs[:, j:j + 1] - cs_t[j:j + 1, :]
            dec = jnp.exp2(jnp.where(causal, seg, NEG_BIG))
            m = (cbm * dec * dt_t[j:j + 1, :]).astype(BF16)
            y_heads.append(jnp.dot(m, xsb[:, j * M2_HEADDIM:(j + 1) * M2_HEADDIM],
                                   preferred_element_type=F32))
        y_g = jnp.concatenate(y_heads, axis=-1) + y_off
        st_scr[g] = st * ecs_x[lc - 1:lc, gcols] + lax.dot_general(
            bm, xw[:, gcols], (((0,), (0,)), ((), ())), preferred_element_type=F32)
        zg = z_ref[:, gcols].astype(F32)
        y_g = (y_g + dx_ref[:, gcols] * xs[:, gcols]) * (zg * jax.nn.sigmoid(zg))
        y_scr[:, gcols] = y_g

    y = y_scr[...]
    ms = jnp.mean(y * y, axis=-1, keepdims=True)
    o_ref[...] = (y * lax.rsqrt(ms + EPS) * ng_ref[...]).astype(o_ref.dtype)


def _ssd_mixer(proj, bsz, conv_w, conv_b, dt_bias, a_log, d_skip, norm_g, side, *, lc):
    t = proj.shape[0]
    seq = t // bsz
    nblk = seq // lc
    half = M2_INNER
    pad = LANES - M2_HEADS
    dtb = jnp.pad(dt_bias, (0, pad)).reshape(1, LANES)
    alog = jnp.pad(a_log, (0, pad)).reshape(1, LANES)
    dx = jnp.repeat(d_skip, M2_HEADDIM).reshape(1, M2_INNER)
    heads = jnp.arange(LANES, dtype=jnp.int32)[:, None]
    chans = jnp.arange(M2_INNER, dtype=jnp.int32)[None, :] // M2_HEADDIM
    expand = (heads == chans).astype(BF16)
    const2 = lambda i: (0, 0)
    side_spec = _side_spec(side, nblk, lambda i: i)
    proj3 = proj.reshape(bsz, seq, proj.shape[1])
    outs = pl.pallas_call(
        _ssd_kernel,
        grid=(nblk,),
        in_specs=[
            pl.BlockSpec((bsz, lc, half), lambda i: (0, i, 0)),
            pl.BlockSpec((bsz, lc, half), lambda i: (0, i, 1)),
            pl.BlockSpec((bsz, lc, half), lambda i: (0, i, 2)),
            pl.BlockSpec((bsz, lc, LANES), lambda i: (0, i, 3 * half // LANES)),
            pl.BlockSpec((M2_CONV, M2_CONV_DIM), const2),
            pl.BlockSpec((1, M2_CONV_DIM), const2),
            pl.BlockSpec((1, LANES), const2),
            pl.BlockSpec((1, LANES), const2),
            pl.BlockSpec((1, M2_INNER), const2),
            pl.BlockSpec((1, M2_INNER), const2),
            pl.BlockSpec((LANES, M2_INNER), const2),
            side_spec,
        ],
        out_specs=[pl.BlockSpec((bsz, lc, M2_INNER), lambda i: (0, i, 0)), side_spec],
        out_shape=[jax.ShapeDtypeStruct((bsz, seq, M2_INNER), BF16),
                   jax.ShapeDtypeStruct(side.shape, BF16)],
        scratch_shapes=[
            pltpu.VMEM((bsz, CONV_HALO, half), F32), pltpu.VMEM((bsz, CONV_HALO, half), F32),
            pltpu.VMEM((bsz, lc + CONV_HALO, half), F32),
            pltpu.VMEM((bsz, lc + CONV_HALO, half), F32),
            pltpu.VMEM((bsz, M2_GROUPS, M2_STATE, M2_GROUP_WIDTH), F32),
            pltpu.VMEM((bsz, lc, M2_INNER), F32),
        ],
        compiler_params=_cparams(("arbitrary",)),
        name="ssd_mixer",
    )(proj3, proj3, proj3, proj3, conv_w, conv_b.reshape(1, M2_CONV_DIM), dtb, alog, dx,
      norm_g.reshape(1, M2_INNER), expand, side)
    return outs[0].reshape(t, M2_INNER), outs[1]


def kernel(x, c, ada_w, ada_b, norm_g, mix_w_in, pool_w, pool_scale, s5_lam_re, s5_lam_im, s5_log_step, s5_b_re, s5_b_im, s5_c_re, s5_c_im, s5_d, s5_w_glu, mix_w_out, ffn_w_gate, ffn_w_up, ffn_w_down, m2_w_in, m2_conv_w, m2_conv_b, m2_dt_bias, m2_a_log, m2_d, m2_norm_g, m2_w_out, moe_w_router, moe_b_router, moe_w_gate, moe_w_up, moe_w_down):
    bsz, seq, d = x.shape
    t = bsz * seq
    xt = x.reshape(t, d)

    mod = _ada_modulation(c, ada_w, ada_b)

    def mod_vecs(layer):
        return [mod[layer, :, k * d:(k + 1) * d].reshape(bsz, 1, d) for k in range(6)]

    def gvec(layer, k):
        return norm_g[layer, k].reshape(1, d)

    sh_m, sc_m, g_m, sh_f, sc_f, g_f = mod_vecs(0)
    ne, dh = N_EXPERTS, EXPERT_HIDDEN
    xt, wu_b = _pool_s5_mixer(xt, bsz, gvec(0, 0), sc_m, sh_m, mix_w_in[0].astype(BF16),
                              pool_w[0], pool_scale[0], s5_lam_re[0], s5_lam_im[0],
                              s5_log_step[0], s5_b_re[0], s5_b_im[0], s5_c_re[0], s5_c_im[0],
                              s5_d[0], s5_w_glu[0], mix_w_out[0].astype(BF16), gvec(0, 1), g_m,
                              moe_w_up[0].reshape(ne * d, dh), tt=S5_BLOCK)
    xt = _ffn_sublayer(xt, gvec(0, 2), sc_f, sh_f, ffn_w_gate[0].astype(BF16),
                       ffn_w_up[0].astype(BF16), ffn_w_down[0].astype(BF16), gvec(0, 3), g_f,
                       tm=TOKEN_TILE, th=FFN_HIDDEN_TILE)

    sh_m, sc_m, g_m, sh_f, sc_f, g_f = mod_vecs(1)
    w_in = jnp.zeros((d, M2_PROJ_PAD), BF16).at[:, :M2_PROJ].set(m2_w_in[0].astype(BF16))
    proj, (wg_b,) = _norm_matmul(
        xt, gvec(1, 0), sc_m, sh_m, w_in, tm=TOKEN_TILE, tn=M2_PROJ_TILE, out_dtype=BF16,
        name="m2_in_proj", side=(moe_w_gate[0].reshape(ne * d, dh),), side_steps=M2_SIDE_STEPS)
    y, wd_b = _ssd_mixer(proj, bsz, m2_conv_w[0], m2_conv_b[0], m2_dt_bias[0], m2_a_log[0],
                         m2_d[0], m2_norm_g[0], moe_w_down[0].reshape(ne * dh, d), lc=SSD_CHUNK)
    xt = _matmul_post(y, m2_w_out[0].astype(BF16), xt, gvec(1, 1), g_m, tm=TOKEN_TILE,
                      name="m2_out_proj")
    xt = _moe_sublayer(xt, gvec(1, 2), sc_f, sh_f, moe_w_router[0], moe_b_router[0],
                       wg_b.reshape(ne, d, dh), wu_b.reshape(ne, d, dh), wd_b.reshape(ne, dh, d),
                       gvec(1, 3), g_f)
    return xt.reshape(bsz, seq, d)
```
